```python
import math
import jax
import jax.numpy as jnp
from jax import lax
import numpy as np

D_MODEL = 2048
BATCH = 4
SEQ = 2048
DEPTH = 2
DEC_BATCH = 8
DEC_SEQ = 4
PAST_LEN = 16384
PAGE_SIZE = 128

HD = 64
N_HEADS = 8
MIX_W = N_HEADS * HD
N_BRANCH = 4
MOBA_BLOCK = 256
MOBA_TOPK = 3
MOBA_QB = 32
NSA_KV_HEADS = 2
NSA_GROUP = N_HEADS // NSA_KV_HEADS
NSA_CMP_STRIDE = 16
NSA_CMP_LEN = 2 * NSA_CMP_STRIDE
NSA_SEL_BLOCK = 64
NSA_TOPN = 16
NSA_WINDOW = 512
NSA_QB = 64
WIN_QB = 128
DSA_IDX_HEADS = 4
DSA_IDX_DIM = 64
DSA_TOPK = 256
DSA_QB = 64
GM_CHUNK = 128
GM_GROUPS = 8
GM_W = 512
REL_BUCKETS = 32
REL_MAX_DIST = 128
N_REL_HEADS = 3 * N_HEADS
MEM_TOKENS = 256
MEM_HEADS = 4
MEM_HD = 128
MEM_W = MEM_HEADS * MEM_HD
PEER_HEADS = 8
PEER_NKEYS = 128
PEER_QDIM = 128
PEER_TOPK = 16
PEER_EXPERTS = PEER_NKEYS ** 2
PEER_BLOCK = 128
NORM_EPS = 1e-6
IN_SPLITS = (MIX_W, MIX_W, MIX_W, MIX_W, 6 * NSA_KV_HEADS * HD, 3 * N_HEADS, MIX_W, MIX_W, MIX_W, DSA_IDX_HEADS * DSA_IDX_DIM, DSA_IDX_DIM, DSA_IDX_HEADS, GM_W, GM_W)
IN_COLS = sum(IN_SPLITS)

kernel_name = 'hybrid_moba_nsa_dsa_gmlp_peer_step'


def rmsnorm(x, g):
    xf = x.astype(jnp.float32)
    y = xf * lax.rsqrt(jnp.mean(xf * xf, axis=-1, keepdims=True) + NORM_EPS)
    return (y * g.astype(jnp.float32)).astype(x.dtype)


def masked_softmax(logits, mask):
    l = jnp.where(mask, logits.astype(jnp.float32), -jnp.inf)
    m = jnp.max(l, axis=-1, keepdims=True)
    m = jnp.where(jnp.isfinite(m), m, 0.0)
    e = jnp.exp(l - m)
    return e / jnp.maximum(jnp.sum(e, axis=-1, keepdims=True), 1e-30)


def rel_bucket(dist):
    n = jnp.maximum(dist, 0)
    max_exact = REL_BUCKETS // 2
    nf = jnp.maximum(n, 1).astype(jnp.float32)
    large = max_exact + (jnp.log(nf / max_exact) / math.log(REL_MAX_DIST / max_exact) * (REL_BUCKETS - max_exact)).astype(jnp.int32)
    return jnp.where(n < max_exact, n, jnp.minimum(large, REL_BUCKETS - 1))


def rel_bias(dist, tab):
    H = tab.shape[1]
    return tab.T[jnp.arange(H)[:, None], rel_bucket(dist)].astype(jnp.float32)


def split_last(z, sizes):
    return jnp.split(z, np.cumsum(sizes)[:-1].tolist(), axis=-1)


def map_query_blocks(fn, qb, q_pos, *xs):
    T = q_pos.shape[0]
    qb = qb if T % qb == 0 else T
    nb = T // qb
    xb = tuple(jnp.moveaxis(x.reshape(x.shape[0], nb, qb, *x.shape[2:]), 1, 0) for x in xs)
    out = lax.map(lambda a: fn(*a), (q_pos.reshape(nb, qb),) + xb)
    return jnp.moveaxis(out, 0, 1).reshape(out.shape[1], T, *out.shape[3:])


def gather_pages(pool, page_table):
    g = pool[page_table]
    return g.reshape(g.shape[0], g.shape[1] * g.shape[2], *g.shape[3:])


def moba_attention(q, k, v, q_pos, tab):
    B, L, H, _ = k.shape
    nblk = -(-L // MOBA_BLOCK)
    pad = ((0, 0), (0, nblk * MOBA_BLOCK - L), (0, 0), (0, 0))
    kb = jnp.pad(k, pad).reshape(B, nblk, MOBA_BLOCK, H, HD)
    vb = jnp.pad(v, pad).reshape(B, nblk, MOBA_BLOCK, H, HD)
    k_mean = jnp.mean(kb.astype(jnp.float32), axis=2).astype(k.dtype)
    kb_t = jnp.transpose(kb, (0, 3, 1, 2, 4))
    vb_t = jnp.transpose(vb, (0, 3, 1, 2, 4))
    ntop = min(MOBA_TOPK, nblk)
    bi = jnp.arange(B)[:, None, None, None]
    hi = jnp.arange(H)[None, None, :, None]
    off = jnp.arange(MOBA_BLOCK)
    scale = HD ** -0.5

    def block_fn(qp, qq):
        qb = qp.shape[0]
        own = qp // MOBA_BLOCK
        gs = jnp.einsum('bqhd,bjhd->bqhj', qq, k_mean).astype(jnp.float32)
        fully_past = jnp.arange(nblk)[None, :] < own[:, None]
        gs = jnp.where(fully_past[None, :, None, :], gs, -jnp.inf)
        _, top = lax.top_k(gs, ntop)
        own_b = jnp.broadcast_to(own[None, :, None, None], (B, qb, H, 1))
        sel = jnp.concatenate([top, own_b], axis=-1)
        sel_ok = jnp.concatenate([top < own[None, :, None, None], jnp.ones(own_b.shape, bool)], axis=-1)
        kg = kb_t[bi, hi, sel].reshape(B, qb, H, (ntop + 1) * MOBA_BLOCK, HD)
        vg = vb_t[bi, hi, sel].reshape(B, qb, H, (ntop + 1) * MOBA_BLOCK, HD)
        pos = (sel[..., None] * MOBA_BLOCK + off).reshape(B, qb, H, -1)
        dist = qp[None, :, None, None] - pos
        ok = jnp.repeat(sel_ok, MOBA_BLOCK, axis=-1) & (dist >= 0)
        lg = jnp.einsum('bqhd,bqhkd->bqhk', qq, kg).astype(jnp.float32) * scale + rel_bias(dist, tab)
        p = masked_softmax(lg, ok)
        return jnp.einsum('bqhk,bqhkd->bqhd', p.astype(vg.dtype), vg)

    return map_query_blocks(block_fn, MOBA_QB, q_pos, q)


def nsa_attention(q, gates, kc, vc, ks, vs, kw, vw, kw_pos0, q_pos, w1, w2, pe, tab):
    B, L, G, _ = kc.shape
    T = q.shape[1]
    H = N_HEADS
    scale = HD ** -0.5
    nch = L // NSA_CMP_STRIDE
    ncmp = nch - 1

    def compress(x, i):
        c = x[:, :nch * NSA_CMP_STRIDE].reshape(B, nch, NSA_CMP_STRIDE, G, HD)
        blk = jnp.concatenate([c[:, :-1], c[:, 1:]], axis=2) + pe[i][None, None, :, None, :]
        blk = jnp.transpose(blk, (0, 1, 3, 2, 4)).reshape(B, ncmp, G, NSA_CMP_LEN * HD)
        return jax.nn.gelu(blk @ w1[i]) @ w2[i]

    kcmp = compress(kc, 0)
    vcmp = compress(vc, 1)
    cmp_end = jnp.arange(ncmp) * NSA_CMP_STRIDE + NSA_CMP_LEN - 1
    qg = q.reshape(B, T, G, NSA_GROUP, HD)
    dist_c = q_pos[:, None] - cmp_end[None, :]
    lg_c = jnp.einsum('btgrd,bngd->btgrn', qg, kcmp).astype(jnp.float32).reshape(B, T, H, ncmp) * scale
    lg_c = lg_c + rel_bias(dist_c[None, :, None, :], tab)
    p_c = masked_softmax(lg_c, (dist_c >= 0)[None, :, None, :])
    p_cg = p_c.reshape(B, T, G, NSA_GROUP, ncmp)
    o_cmp = jnp.einsum('btgrn,bngd->btgrd', p_cg.astype(vcmp.dtype), vcmp).reshape(B, T, H, HD)
    nsel = -(-L // NSA_SEL_BLOCK)
    ii = np.arange(ncmp)[:, None]
    jj_np = np.arange(nsel)[None, :]
    overlap = ((ii * NSA_CMP_STRIDE < (jj_np + 1) * NSA_SEL_BLOCK) & (ii * NSA_CMP_STRIDE + NSA_CMP_LEN > jj_np * NSA_SEL_BLOCK)).astype(np.float32)
    imp = jnp.einsum('btgrn,nj->btgj', p_cg, jnp.asarray(overlap))
    cur = q_pos // NSA_SEL_BLOCK
    jj = jnp.arange(nsel)[None, :]
    causal_blk = jj <= cur[:, None]
    forced = (jj == 0) | (jj == cur[:, None]) | (jj == cur[:, None] - 1)
    imp = jnp.where(forced[None, :, None, :], jnp.inf, jnp.where(causal_blk[None, :, None, :], imp, -jnp.inf))
    ntop = min(NSA_TOPN, nsel)
    _, sel = lax.top_k(imp, ntop)
    pad = ((0, 0), (0, nsel * NSA_SEL_BLOCK - L), (0, 0), (0, 0))
    ksb = jnp.transpose(jnp.pad(ks, pad).reshape(B, nsel, NSA_SEL_BLOCK, G, HD), (0, 3, 1, 2, 4))
    vsb = jnp.transpose(jnp.pad(vs, pad).reshape(B, nsel, NSA_SEL_BLOCK, G, HD), (0, 3, 1, 2, 4))
    bi = jnp.arange(B)[:, None, None, None]
    gi = jnp.arange(G)[None, None, :, None]
    off = jnp.arange(NSA_SEL_BLOCK)

    def sel_fn(qp, qq, ss):
        qb = qp.shape[0]
        kg = ksb[bi, gi, ss].reshape(B, qb, G, ntop * NSA_SEL_BLOCK, HD)
        vg = vsb[bi, gi, ss].reshape(B, qb, G, ntop * NSA_SEL_BLOCK, HD)
        pos = (ss[..., None] * NSA_SEL_BLOCK + off).reshape(B, qb, G, -1)
        dist = jnp.repeat(qp[None, :, None, None] - pos, NSA_GROUP, axis=2)
        lg = jnp.einsum('bqgrd,bqgkd->bqgrk', qq.reshape(B, qb, G, NSA_GROUP, HD), kg).astype(jnp.float32).reshape(B, qb, H, -1) * scale
        p = masked_softmax(lg + rel_bias(dist, tab), dist >= 0)
        o = jnp.einsum('bqgrk,bqgkd->bqgrd', p.reshape(B, qb, G, NSA_GROUP, -1).astype(vg.dtype), vg)
        return o.reshape(B, qb, H, HD)

    o_sel = map_query_blocks(sel_fn, NSA_QB, q_pos, q, sel)
    W = NSA_WINDOW
    wpad = ((0, 0), (W, 0), (0, 0), (0, 0))
    kwp = jnp.pad(kw, wpad)
    vwp = jnp.pad(vw, wpad)

    def win_fn(qp, qq):
        qb = qp.shape[0]
        span = W + qb - 1
        start = qp[0] - kw_pos0 + 1
        kk = lax.dynamic_slice_in_dim(kwp, start, span, axis=1)
        vv = lax.dynamic_slice_in_dim(vwp, start, span, axis=1)
        pos = qp[0] - W + 1 + jnp.arange(span)
        dist = qp[:, None] - pos[None, :]
        ok = (dist >= 0) & (dist < W) & (pos[None, :] >= 0)
        lg = jnp.einsum('bqgrd,bkgd->bqgrk', qq.reshape(B, qb, G, NSA_GROUP, HD), kk).astype(jnp.float32).reshape(B, qb, H, span) * scale
        p = masked_softmax(lg + rel_bias(dist[None, :, None, :], tab), ok[None, :, None, :])
        o = jnp.einsum('bqgrk,bkgd->bqgrd', p.reshape(B, qb, G, NSA_GROUP, span).astype(vv.dtype), vv)
        return o.reshape(B, qb, H, HD)

    o_win = map_query_blocks(win_fn, WIN_QB, q_pos, q)
    g = jax.nn.sigmoid(gates.astype(jnp.float32)).astype(q.dtype)
    return g[..., 0:1] * o_cmp + g[..., 1:2] * o_sel + g[..., 2:3] * o_win


def dsa_attention(q, qi, wi, k, v, ki, q_pos, tab):
    B, L, H, _ = k.shape
    kk = min(DSA_TOPK, L // 4)
    bi = jnp.arange(B)[:, None, None]
    scale = HD ** -0.5

    def block_fn(qp, qq, qqi, wwi):
        s = jax.nn.relu(jnp.einsum('bqhd,bld->bqhl', qqi, ki).astype(jnp.float32) * DSA_IDX_DIM ** -0.5)
        score = jnp.einsum('bqh,bqhl->bql', wwi.astype(jnp.float32), s)
        admissible = jnp.arange(L)[None, :] <= qp[:, None]
        score = jnp.where(admissible[None], score, -jnp.inf)
        _, sel = lax.top_k(score, kk)
        kg = k[bi, sel]
        vg = v[bi, sel]
        dist = (qp[None, :, None] - sel)[:, :, None, :]
        lg = jnp.einsum('bqhd,bqkhd->bqhk', qq, kg).astype(jnp.float32) * scale + rel_bias(dist, tab)
        p = masked_softmax(lg, dist >= 0)
        return jnp.einsum('bqhk,bqkhd->bqhd', p.astype(vg.dtype), vg)

    return map_query_blocks(block_fn, DSA_QB, q_pos, q, qi, wi)


def gmlp_sgu(u, v, ws, b, gnorm):
    B, T, _ = u.shape
    vn = rmsnorm(v, gnorm)
    nc = -(-T // GM_CHUNK)
    vc = jnp.pad(vn, ((0, 0), (0, nc * GM_CHUNK - T), (0, 0))).reshape(B, nc, GM_CHUNK, GM_GROUPS, GM_W // GM_GROUPS)
    wsm = ws * jnp.tril(jnp.ones((GM_CHUNK, GM_CHUNK), ws.dtype))
    mixed = jnp.einsum('gij,bcjgd->bcigd', wsm, vc) + b.T[None, None, :, :, None]
    mixed = mixed.reshape(B, nc * GM_CHUNK, GM_W)[:, :T]
    return u * mixed, vn


def mem_attention(h, mkv, wq, wo):
    B, T, _ = h.shape
    q = (h @ wq).reshape(B, T, MEM_HEADS, MEM_HD)
    lg = jnp.einsum('bthd,bmhd->bthm', q, mkv[:, :, 0]).astype(jnp.float32) * MEM_HD ** -0.5
    p = jax.nn.softmax(lg, axis=-1)
    o = jnp.einsum('bthm,bmhd->bthd', p.astype(mkv.dtype), mkv[:, :, 1]).reshape(B, T, MEM_W)
    return o @ wo


def peer_ffn(h, wq, keys, U, V):
    B, T, D = h.shape
    N = B * T
    nb = -(-N // PEER_BLOCK)
    hf = jnp.pad(h.reshape(N, D), ((0, nb * PEER_BLOCK - N), (0, 0))).reshape(nb, PEER_BLOCK, D)

    def block_fn(x):
        q = (x @ wq).reshape(PEER_BLOCK, PEER_HEADS, 2, PEER_QDIM // 2)
        s = jnp.einsum('nhcd,hckd->nhck', q, keys).astype(jnp.float32)
        v1, i1 = lax.top_k(s[:, :, 0], PEER_TOPK)
        v2, i2 = lax.top_k(s[:, :, 1], PEER_TOPK)
        cand = (v1[..., :, None] + v2[..., None, :]).reshape(PEER_BLOCK, PEER_HEADS, PEER_TOPK * PEER_TOPK)
        sv, ci = lax.top_k(cand, PEER_TOPK)
        e = jnp.take_along_axis(i1, ci // PEER_TOPK, axis=-1) * PEER_NKEYS + jnp.take_along_axis(i2, ci % PEER_TOPK, axis=-1)
        g = jax.nn.softmax(sv, axis=-1)
        ug = U[e]
        vg = V[e]
        act = jax.nn.gelu(jnp.einsum('nd,nhkd->nhk', x, ug).astype(jnp.float32))
        return jnp.einsum('nhk,nhkd->nd', (g * act).astype(x.dtype), vg)

    out = lax.map(block_fn, hf).reshape(nb * PEER_BLOCK, D)[:N]
    return out.reshape(B, T, D)


def token_mix(h, past, lp, tab):
    B, T, _ = h.shape
    P = 0 if past is None else past['moba'].shape[1]
    q_pos = P + jnp.arange(T, dtype=jnp.int32)
    qa, ka, va, q_nsa, nsa_kv, nsa_g, qc, kc, vc, qi, ki, wi, u, vg = split_last(h @ lp['w_in'], IN_SPLITS)
    moba_new = jnp.stack([ka, va], axis=2).reshape(B, T, 2, N_HEADS, HD)
    nsa_new = nsa_kv.reshape(B, T, 6, NSA_KV_HEADS, HD)
    nsa_main_new = nsa_new[:, :, :4]
    win_new = nsa_new[:, :, 4:]
    dsa_new = jnp.stack([kc, vc], axis=2).reshape(B, T, 2, N_HEADS, HD)
    if past is None:
        moba_all, nsa_all, win_all, dsa_all, idx_all = moba_new, nsa_main_new, win_new, dsa_new, ki
    else:
        moba_all = jnp.concatenate([past['moba'], moba_new], axis=1)
        nsa_all = jnp.concatenate([past['nsa'], nsa_main_new], axis=1)
        win_all = jnp.concatenate([past['win'], win_new], axis=1)
        dsa_all = jnp.concatenate([past['dsa'], dsa_new], axis=1)
        idx_all = jnp.concatenate([past['idx'], ki], axis=1)
    kw_pos0 = P + T - win_all.shape[1]
    y_a = moba_attention(qa.reshape(B, T, N_HEADS, HD), moba_all[:, :, 0], moba_all[:, :, 1], q_pos, tab[:, :N_HEADS])
    y_b = nsa_attention(q_nsa.reshape(B, T, N_HEADS, HD), nsa_g.reshape(B, T, N_HEADS, 3), nsa_all[:, :, 0], nsa_all[:, :, 1], nsa_all[:, :, 2], nsa_all[:, :, 3], win_all[:, :, 0], win_all[:, :, 1], kw_pos0, q_pos, lp['cmp_w1'], lp['cmp_w2'], lp['cmp_pe'], tab[:, N_HEADS:2 * N_HEADS])
    y_c = dsa_attention(qc.reshape(B, T, N_HEADS, HD), qi.reshape(B, T, DSA_IDX_HEADS, DSA_IDX_DIM), wi, dsa_all[:, :, 0], dsa_all[:, :, 1], idx_all, q_pos, tab[:, 2 * N_HEADS:])
    y_d, v_rows = gmlp_sgu(u, vg, lp['gm_ws'], lp['gm_b'], lp['gm_norm'])
    branches = jnp.stack([y_a.reshape(B, T, MIX_W), y_b.reshape(B, T, MIX_W), y_c.reshape(B, T, MIX_W), y_d], axis=2)
    proj = jnp.einsum('btnc,ncd->btnd', branches, lp['w_branch'])
    gate = jax.nn.sigmoid((h @ lp['w_gate']).reshape(B, T, N_BRANCH, D_MODEL) + lp['b_gate'])
    out = jnp.sum(gate * proj, axis=2) @ lp['w_out']
    win_keep = win_all[:, -min(NSA_WINDOW, P + T):]
    return out, (moba_new, nsa_main_new, win_keep, dsa_new, ki, v_rows)


def layer(x, past, mkv, lp, tab):
    mix, rows = token_mix(rmsnorm(x, lp['norm_mix']), past, lp, tab)
    x = x + mix
    x = x + mem_attention(rmsnorm(x, lp['norm_mem']), mkv, lp['w_mem_q'], lp['w_mem_o'])
    x = x + peer_ffn(rmsnorm(x, lp['norm_ffn']), lp['peer_wq'], lp['peer_keys'], lp['peer_u'], lp['peer_v'])
    return x, rows


def setup_inputs(seed: int = 0) -> dict:
    key = jax.random.key(seed)
    ks = iter(jax.random.split(key, 40))

    def nrm(shape, scale):
        return jax.random.normal(next(ks), shape, jnp.float32) * scale

    n_pages = PAST_LEN // PAGE_SIZE
    n_used = DEC_BATCH * n_pages
    n_pool = n_used + max(1, n_used // 4)
    win_buf = min(NSA_WINDOW, PAST_LEN)
    page_table = jax.random.permutation(next(ks), n_pool)[:n_used].reshape(DEC_BATCH, n_pages).astype(jnp.int32)
    return {
        'x_prompt': nrm((BATCH, SEQ, D_MODEL), 1.0),
        'x_sample': nrm((DEC_BATCH, DEC_SEQ, D_MODEL), 1.0),
        'cache_moba_kv': nrm((DEPTH, n_pool, PAGE_SIZE, 2, N_HEADS, HD), 1.0),
        'cache_nsa_kv': nrm((DEPTH, n_pool, PAGE_SIZE, 4, NSA_KV_HEADS, HD), 1.0),
        'cache_dsa_kv': nrm((DEPTH, n_pool, PAGE_SIZE, 2, N_HEADS, HD), 1.0),
        'cache_dsa_idx': nrm((DEPTH, n_pool, PAGE_SIZE, DSA_IDX_DIM), 1.0),
        'state_nsa_win': nrm((DEPTH, DEC_BATCH, win_buf, 2, NSA_KV_HEADS, HD), 1.0),
        'cache_mem_kv': nrm((DEPTH, DEC_BATCH, MEM_TOKENS, 2, MEM_HEADS, MEM_HD), 1.0),
        'page_table': page_table,
        'mem_prompt': nrm((BATCH, MEM_TOKENS, D_MODEL), 1.0),
        'rel_bias_table': nrm((REL_BUCKETS, N_REL_HEADS), 0.5),
        'w_in': nrm((DEPTH, D_MODEL, IN_COLS), D_MODEL ** -0.5),
        'nsa_cmp_w1': nrm((DEPTH, 2, NSA_CMP_LEN * HD, HD), (NSA_CMP_LEN * HD) ** -0.5),
        'nsa_cmp_w2': nrm((DEPTH, 2, HD, HD), HD ** -0.5),
        'nsa_cmp_pe': nrm((DEPTH, 2, NSA_CMP_LEN, HD), 0.1),
        'gm_ws': nrm((DEPTH, GM_GROUPS, GM_CHUNK, GM_CHUNK), GM_CHUNK ** -0.5),
        'gm_b': 1.0 + nrm((DEPTH, GM_GROUPS, GM_CHUNK), 0.1),
        'gm_norm': 1.0 + nrm((DEPTH, GM_W), 0.1),
        'w_branch': nrm((DEPTH, N_BRANCH, MIX_W, D_MODEL), MIX_W ** -0.5),
        'w_gate': nrm((DEPTH, D_MODEL, N_BRANCH * D_MODEL), D_MODEL ** -0.5),
        'b_gate': nrm((DEPTH, N_BRANCH, D_MODEL), 0.1),
        'w_out': nrm((DEPTH, D_MODEL, D_MODEL), D_MODEL ** -0.5),
        'w_mem_q': nrm((DEPTH, D_MODEL, MEM_W), D_MODEL ** -0.5),
        'w_mem_kv': nrm((DEPTH, D_MODEL, 2 * MEM_W), D_MODEL ** -0.5),
        'w_mem_o': nrm((DEPTH, MEM_W, D_MODEL), MEM_W ** -0.5),
        'peer_wq': nrm((DEPTH, D_MODEL, PEER_HEADS * PEER_QDIM), D_MODEL ** -0.5),
        'peer_keys': nrm((DEPTH, PEER_HEADS, 2, PEER_NKEYS, PEER_QDIM // 2), (PEER_QDIM // 2) ** -0.5),
        'peer_u': nrm((DEPTH, PEER_EXPERTS, D_MODEL), D_MODEL ** -0.5),
        'peer_v': nrm((DEPTH, PEER_EXPERTS, D_MODEL), 0.3),
        'norm_mix': 1.0 + nrm((DEPTH, D_MODEL), 0.1),
        'norm_mem': 1.0 + nrm((DEPTH, D_MODEL), 0.1),
        'norm_ffn': 1.0 + nrm((DEPTH, D_MODEL), 0.1),
        'norm_final': 1.0 + nrm((D_MODEL,), 0.1),
    }


def reference(x_prompt, x_sample, cache_moba_kv, cache_nsa_kv, cache_dsa_kv, cache_dsa_idx, state_nsa_win, cache_mem_kv, page_table, mem_prompt, rel_bias_table, w_in, nsa_cmp_w1, nsa_cmp_w2, nsa_cmp_pe, gm_ws, gm_b, gm_norm, w_branch, w_gate, b_gate, w_out, w_mem_q, w_mem_kv, w_mem_o, peer_wq, peer_keys, peer_u, peer_v, norm_mix, norm_mem, norm_ffn, norm_final):
    xp = x_prompt
    xs = x_sample
    rows_p = []
    rows_s = []
    mem_rows = []
    Bp = x_prompt.shape[0]
    for l in range(DEPTH):
        lp = {
            'w_in': w_in[l], 'cmp_w1': nsa_cmp_w1[l], 'cmp_w2': nsa_cmp_w2[l], 'cmp_pe': nsa_cmp_pe[l],
            'gm_ws': gm_ws[l], 'gm_b': gm_b[l], 'gm_norm': gm_norm[l],
            'w_branch': w_branch[l], 'w_gate': w_gate[l], 'b_gate': b_gate[l], 'w_out': w_out[l],
            'w_mem_q': w_mem_q[l], 'w_mem_o': w_mem_o[l],
            'peer_wq': peer_wq[l], 'peer_keys': peer_keys[l], 'peer_u': peer_u[l], 'peer_v': peer_v[l],
            'norm_mix': norm_mix[l], 'norm_mem': norm_mem[l], 'norm_ffn': norm_ffn[l],
        }
        mkv_p = (mem_prompt @ w_mem_kv[l]).reshape(Bp, MEM_TOKENS, 2, MEM_HEADS, MEM_HD)
        past = {
            'moba': gather_pages(cache_moba_kv[l], page_table),
            'nsa': gather_pages(cache_nsa_kv[l], page_table),
            'dsa': gather_pages(cache_dsa_kv[l], page_table),
            'idx': gather_pages(cache_dsa_idx[l], page_table),
            'win': state_nsa_win[l],
        }
        xp, rp = layer(xp, None, mkv_p, lp, rel_bias_table)
        xs, rs = layer(xs, past, cache_mem_kv[l], lp, rel_bias_table)
        rows_p.append(rp)
        rows_s.append(rs)
        mem_rows.append(mkv_p)
    y_prompt = rmsnorm(xp, norm_final)
    y_sample = rmsnorm(xs, norm_final)
    new_moba_kv_prompt = jnp.stack([r[0] for r in rows_p])
    new_moba_kv_sample = jnp.stack([r[0] for r in rows_s])
    new_nsa_kv_prompt = jnp.stack([r[1] for r in rows_p])
    new_nsa_kv_sample = jnp.stack([r[1] for r in rows_s])
    new_nsa_win_prompt = jnp.stack([r[2] for r in rows_p])
    new_nsa_win_sample = jnp.stack([r[2] for r in rows_s])
    new_dsa_kv_prompt = jnp.stack([r[3] for r in rows_p])
    new_dsa_kv_sample = jnp.stack([r[3] for r in rows_s])
    new_dsa_idx_prompt = jnp.stack([r[4] for r in rows_p])
    new_dsa_idx_sample = jnp.stack([r[4] for r in rows_s])
    new_mem_kv_prompt = jnp.stack(mem_rows)
    new_gmlp_v_sample = jnp.stack([r[5] for r in rows_s])
    return (y_prompt, y_sample, new_moba_kv_prompt, new_moba_kv_sample, new_nsa_kv_prompt, new_nsa_kv_sample, new_nsa_win_prompt, new_nsa_win_sample, new_dsa_kv_prompt, new_dsa_kv_sample, new_dsa_idx_prompt, new_dsa_idx_sample, new_mem_kv_prompt, new_gmlp_v_sample)
```

```python
import functools
import math
import jax
import jax.numpy as jnp
from jax import lax
import numpy as np
from jax.experimental import pallas as pl
from jax.experimental.pallas import tpu as pltpu

D_MODEL = 2048
BATCH = 4
SEQ = 2048
DEPTH = 2
DEC_BATCH = 8
DEC_SEQ = 4
PAST_LEN = 16384
PAGE_SIZE = 128

HD = 64
N_HEADS = 8
MIX_W = N_HEADS * HD
N_BRANCH = 4
MOBA_BLOCK = 256
MOBA_TOPK = 3
MOBA_QB = 32
NSA_KV_HEADS = 2
NSA_GROUP = N_HEADS // NSA_KV_HEADS
NSA_CMP_STRIDE = 16
NSA_CMP_LEN = 2 * NSA_CMP_STRIDE
NSA_SEL_BLOCK = 64
NSA_TOPN = 16
NSA_WINDOW = 512
NSA_QB = 64
WIN_QB = 128
DSA_IDX_HEADS = 4
DSA_IDX_DIM = 64
DSA_TOPK = 256
DSA_QB = 64
GM_CHUNK = 128
GM_GROUPS = 8
GM_W = 512
REL_BUCKETS = 32
REL_MAX_DIST = 128
N_REL_HEADS = 3 * N_HEADS
MEM_TOKENS = 256
MEM_HEADS = 4
MEM_HD = 128
MEM_W = MEM_HEADS * MEM_HD
PEER_HEADS = 8
PEER_NKEYS = 128
PEER_QDIM = 128
PEER_TOPK = 16
PEER_EXPERTS = PEER_NKEYS ** 2
PEER_BLOCK = 128
NORM_EPS = 1e-6
IN_SPLITS = (MIX_W, MIX_W, MIX_W, MIX_W, 6 * NSA_KV_HEADS * HD, 3 * N_HEADS, MIX_W, MIX_W, MIX_W, DSA_IDX_HEADS * DSA_IDX_DIM, DSA_IDX_DIM, DSA_IDX_HEADS, GM_W, GM_W)
IN_COLS = sum(IN_SPLITS)


def rmsnorm(x, g):
    xf = x.astype(jnp.float32)
    y = xf * lax.rsqrt(jnp.mean(xf * xf, axis=-1, keepdims=True) + NORM_EPS)
    return (y * g.astype(jnp.float32)).astype(x.dtype)


def masked_softmax(logits, mask):
    l = jnp.where(mask, logits.astype(jnp.float32), -jnp.inf)
    m = jnp.max(l, axis=-1, keepdims=True)
    m = jnp.where(jnp.isfinite(m), m, 0.0)
    e = jnp.exp(l - m)
    return e / jnp.maximum(jnp.sum(e, axis=-1, keepdims=True), 1e-30)


def rel_bucket(dist):
    n = jnp.maximum(dist, 0)
    max_exact = REL_BUCKETS // 2
    nf = jnp.maximum(n, 1).astype(jnp.float32)
    large = max_exact + (jnp.log(nf / max_exact) / math.log(REL_MAX_DIST / max_exact) * (REL_BUCKETS - max_exact)).astype(jnp.int32)
    return jnp.where(n < max_exact, n, jnp.minimum(large, REL_BUCKETS - 1))


def rel_bias(dist, tab):
    H = tab.shape[1]
    return tab.T[jnp.arange(H)[:, None], rel_bucket(dist)].astype(jnp.float32)


def split_last(z, sizes):
    return jnp.split(z, np.cumsum(sizes)[:-1].tolist(), axis=-1)


def map_query_blocks(fn, qb, q_pos, *xs):
    T = q_pos.shape[0]
    qb = qb if T % qb == 0 else T
    nb = T // qb
    xb = tuple(jnp.moveaxis(x.reshape(x.shape[0], nb, qb, *x.shape[2:]), 1, 0) for x in xs)
    out = lax.map(lambda a: fn(*a), (q_pos.reshape(nb, qb),) + xb)
    return jnp.moveaxis(out, 0, 1).reshape(out.shape[1], T, *out.shape[3:])


def gather_pages(pool, page_table):
    g = pool[page_table]
    return g.reshape(g.shape[0], g.shape[1] * g.shape[2], *g.shape[3:])


def moba_attention(q, k, v, q_pos, tab):
    B, L, H, _ = k.shape
    nblk = -(-L // MOBA_BLOCK)
    pad = ((0, 0), (0, nblk * MOBA_BLOCK - L), (0, 0), (0, 0))
    kb = jnp.pad(k, pad).reshape(B, nblk, MOBA_BLOCK, H, HD)
    vb = jnp.pad(v, pad).reshape(B, nblk, MOBA_BLOCK, H, HD)
    k_mean = jnp.mean(kb.astype(jnp.float32), axis=2).astype(k.dtype)
    kb_t = jnp.transpose(kb, (0, 3, 1, 2, 4))
    vb_t = jnp.transpose(vb, (0, 3, 1, 2, 4))
    ntop = min(MOBA_TOPK, nblk)
    bi = jnp.arange(B)[:, None, None, None]
    hi = jnp.arange(H)[None, None, :, None]
    off = jnp.arange(MOBA_BLOCK)
    scale = HD ** -0.5

    def block_fn(qp, qq):
        qb = qp.shape[0]
        own = qp // MOBA_BLOCK
        gs = jnp.einsum('bqhd,bjhd->bqhj', qq, k_mean).astype(jnp.float32)
        fully_past = jnp.arange(nblk)[None, :] < own[:, None]
        gs = jnp.where(fully_past[None, :, None, :], gs, -jnp.inf)
        _, top = lax.top_k(gs, ntop)
        own_b = jnp.broadcast_to(own[None, :, None, None], (B, qb, H, 1))
        sel = jnp.concatenate([top, own_b], axis=-1)
        sel_ok = jnp.concatenate([top < own[None, :, None, None], jnp.ones(own_b.shape, bool)], axis=-1)
        kg = kb_t[bi, hi, sel].reshape(B, qb, H, (ntop + 1) * MOBA_BLOCK, HD)
        vg = vb_t[bi, hi, sel].reshape(B, qb, H, (ntop + 1) * MOBA_BLOCK, HD)
        pos = (sel[..., None] * MOBA_BLOCK + off).reshape(B, qb, H, -1)
        dist = qp[None, :, None, None] - pos
        ok = jnp.repeat(sel_ok, MOBA_BLOCK, axis=-1) & (dist >= 0)
        lg = jnp.einsum('bqhd,bqhkd->bqhk', qq, kg).astype(jnp.float32) * scale + rel_bias(dist, tab)
        p = masked_softmax(lg, ok)
        return jnp.einsum('bqhk,bqhkd->bqhd', p.astype(vg.dtype), vg)

    return map_query_blocks(block_fn, MOBA_QB, q_pos, q)


def nsa_attention(q, gates, kc, vc, ks, vs, kw, vw, kw_pos0, q_pos, w1, w2, pe, tab):
    B, L, G, _ = kc.shape
    T = q.shape[1]
    H = N_HEADS
    scale = HD ** -0.5
    nch = L // NSA_CMP_STRIDE
    ncmp = nch - 1

    def compress(x, i):
        c = x[:, :nch * NSA_CMP_STRIDE].reshape(B, nch, NSA_CMP_STRIDE, G, HD)
        blk = jnp.concatenate([c[:, :-1], c[:, 1:]], axis=2) + pe[i][None, None, :, None, :]
        blk = jnp.transpose(blk, (0, 1, 3, 2, 4)).reshape(B, ncmp, G, NSA_CMP_LEN * HD)
        return jax.nn.gelu(blk @ w1[i]) @ w2[i]

    kcmp = compress(kc, 0)
    vcmp = compress(vc, 1)
    cmp_end = jnp.arange(ncmp) * NSA_CMP_STRIDE + NSA_CMP_LEN - 1
    qg = q.reshape(B, T, G, NSA_GROUP, HD)
    dist_c = q_pos[:, None] - cmp_end[None, :]
    lg_c = jnp.einsum('btgrd,bngd->btgrn', qg, kcmp).astype(jnp.float32).reshape(B, T, H, ncmp) * scale
    lg_c = lg_c + rel_bias(dist_c[None, :, None, :], tab)
    p_c = masked_softmax(lg_c, (dist_c >= 0)[None, :, None, :])
    p_cg = p_c.reshape(B, T, G, NSA_GROUP, ncmp)
    o_cmp = jnp.einsum('btgrn,bngd->btgrd', p_cg.astype(vcmp.dtype), vcmp).reshape(B, T, H, HD)
    nsel = -(-L // NSA_SEL_BLOCK)
    ii = np.arange(ncmp)[:, None]
    jj_np = np.arange(nsel)[None, :]
    overlap = ((ii * NSA_CMP_STRIDE < (jj_np + 1) * NSA_SEL_BLOCK) & (ii * NSA_CMP_STRIDE + NSA_CMP_LEN > jj_np * NSA_SEL_BLOCK)).astype(np.float32)
    imp = jnp.einsum('btgrn,nj->btgj', p_cg, jnp.asarray(overlap))
    cur = q_pos // NSA_SEL_BLOCK
    jj = jnp.arange(nsel)[None, :]
    causal_blk = jj <= cur[:, None]
    forced = (jj == 0) | (jj == cur[:, None]) | (jj == cur[:, None] - 1)
    imp = jnp.where(forced[None, :, None, :], jnp.inf, jnp.where(causal_blk[None, :, None, :], imp, -jnp.inf))
    ntop = min(NSA_TOPN, nsel)
    _, sel = lax.top_k(imp, ntop)
    pad = ((0, 0), (0, nsel * NSA_SEL_BLOCK - L), (0, 0), (0, 0))
    ksb = jnp.transpose(jnp.pad(ks, pad).reshape(B, nsel, NSA_SEL_BLOCK, G, HD), (0, 3, 1, 2, 4))
    vsb = jnp.transpose(jnp.pad(vs, pad).reshape(B, nsel, NSA_SEL_BLOCK, G, HD), (0, 3, 1, 2, 4))
    bi = jnp.arange(B)[:, None, None, None]
    gi = jnp.arange(G)[None, None, :, None]
    off = jnp.arange(NSA_SEL_BLOCK)

    def sel_fn(qp, qq, ss):
        qb = qp.shape[0]
        kg = ksb[bi, gi, ss].reshape(B, qb, G, ntop * NSA_SEL_BLOCK, HD)
        vg = vsb[bi, gi, ss].reshape(B, qb, G, ntop * NSA_SEL_BLOCK, HD)
        pos = (ss[..., None] * NSA_SEL_BLOCK + off).reshape(B, qb, G, -1)
        dist = jnp.repeat(qp[None, :, None, None] - pos, NSA_GROUP, axis=2)
        lg = jnp.einsum('bqgrd,bqgkd->bqgrk', qq.reshape(B, qb, G, NSA_GROUP, HD), kg).astype(jnp.float32).reshape(B, qb, H, -1) * scale
        p = masked_softmax(lg + rel_bias(dist, tab), dist >= 0)
        o = jnp.einsum('bqgrk,bqgkd->bqgrd', p.reshape(B, qb, G, NSA_GROUP, -1).astype(vg.dtype), vg)
        return o.reshape(B, qb, H, HD)

    o_sel = map_query_blocks(sel_fn, NSA_QB, q_pos, q, sel)
    W = NSA_WINDOW
    wpad = ((0, 0), (W, 0), (0, 0), (0, 0))
    kwp = jnp.pad(kw, wpad)
    vwp = jnp.pad(vw, wpad)

    def win_fn(qp, qq):
        qb = qp.shape[0]
        span = W + qb - 1
        start = qp[0] - kw_pos0 + 1
        kk = lax.dynamic_slice_in_dim(kwp, start, span, axis=1)
        vv = lax.dynamic_slice_in_dim(vwp, start, span, axis=1)
        pos = qp[0] - W + 1 + jnp.arange(span)
        dist = qp[:, None] - pos[None, :]
        ok = (dist >= 0) & (dist < W) & (pos[None, :] >= 0)
        lg = jnp.einsum('bqgrd,bkgd->bqgrk', qq.reshape(B, qb, G, NSA_GROUP, HD), kk).astype(jnp.float32).reshape(B, qb, H, span) * scale
        p = masked_softmax(lg + rel_bias(dist[None, :, None, :], tab), ok[None, :, None, :])
        o = jnp.einsum('bqgrk,bkgd->bqgrd', p.reshape(B, qb, G, NSA_GROUP, span).astype(vv.dtype), vv)
        return o.reshape(B, qb, H, HD)

    o_win = map_query_blocks(win_fn, WIN_QB, q_pos, q)
    g = jax.nn.sigmoid(gates.astype(jnp.float32)).astype(q.dtype)
    return g[..., 0:1] * o_cmp + g[..., 1:2] * o_sel + g[..., 2:3] * o_win


def dsa_attention(q, qi, wi, k, v, ki, q_pos, tab):
    B, L, H, _ = k.shape
    kk = min(DSA_TOPK, L // 4)
    bi = jnp.arange(B)[:, None, None]
    scale = HD ** -0.5

    def block_fn(qp, qq, qqi, wwi):
        s = jax.nn.relu(jnp.einsum('bqhd,bld->bqhl', qqi, ki).astype(jnp.float32) * DSA_IDX_DIM ** -0.5)
        score = jnp.einsum('bqh,bqhl->bql', wwi.astype(jnp.float32), s)
        admissible = jnp.arange(L)[None, :] <= qp[:, None]
        score = jnp.where(admissible[None], score, -jnp.inf)
        _, sel = lax.top_k(score, kk)
        kg = k[bi, sel]
        vg = v[bi, sel]
        dist = (qp[None, :, None] - sel)[:, :, None, :]
        lg = jnp.einsum('bqhd,bqkhd->bqhk', qq, kg).astype(jnp.float32) * scale + rel_bias(dist, tab)
        p = masked_softmax(lg, dist >= 0)
        return jnp.einsum('bqhk,bqkhd->bqhd', p.astype(vg.dtype), vg)

    return map_query_blocks(block_fn, DSA_QB, q_pos, q, qi, wi)


def gmlp_sgu(u, v, ws, b, gnorm):
    B, T, _ = u.shape
    vn = rmsnorm(v, gnorm)
    nc = -(-T // GM_CHUNK)
    vc = jnp.pad(vn, ((0, 0), (0, nc * GM_CHUNK - T), (0, 0))).reshape(B, nc, GM_CHUNK, GM_GROUPS, GM_W // GM_GROUPS)
    wsm = ws * jnp.tril(jnp.ones((GM_CHUNK, GM_CHUNK), ws.dtype))
    mixed = jnp.einsum('gij,bcjgd->bcigd', wsm, vc) + b.T[None, None, :, :, None]
    mixed = mixed.reshape(B, nc * GM_CHUNK, GM_W)[:, :T]
    return u * mixed, vn


def mem_attention(h, mkv, wq, wo):
    B, T, _ = h.shape
    q = (h @ wq).reshape(B, T, MEM_HEADS, MEM_HD)
    lg = jnp.einsum('bthd,bmhd->bthm', q, mkv[:, :, 0]).astype(jnp.float32) * MEM_HD ** -0.5
    p = jax.nn.softmax(lg, axis=-1)
    o = jnp.einsum('bthm,bmhd->bthd', p.astype(mkv.dtype), mkv[:, :, 1]).reshape(B, T, MEM_W)
    return o @ wo


def peer_ffn(h, wq, keys, U, V):
    B, T, D = h.shape
    N = B * T
    nb = -(-N // PEER_BLOCK)
    hf = jnp.pad(h.reshape(N, D), ((0, nb * PEER_BLOCK - N), (0, 0))).reshape(nb, PEER_BLOCK, D)

    def block_fn(x):
        q = (x @ wq).reshape(PEER_BLOCK, PEER_HEADS, 2, PEER_QDIM // 2)
        s = jnp.einsum('nhcd,hckd->nhck', q, keys).astype(jnp.float32)
        v1, i1 = lax.top_k(s[:, :, 0], PEER_TOPK)
        v2, i2 = lax.top_k(s[:, :, 1], PEER_TOPK)
        cand = (v1[..., :, None] + v2[..., None, :]).reshape(PEER_BLOCK, PEER_HEADS, PEER_TOPK * PEER_TOPK)
        sv, ci = lax.top_k(cand, PEER_TOPK)
        e = jnp.take_along_axis(i1, ci // PEER_TOPK, axis=-1) * PEER_NKEYS + jnp.take_along_axis(i2, ci % PEER_TOPK, axis=-1)
        g = jax.nn.softmax(sv, axis=-1)
        ug = U[e]
        vg = V[e]
        act = jax.nn.gelu(jnp.einsum('nd,nhkd->nhk', x, ug).astype(jnp.float32))
        return jnp.einsum('nhk,nhkd->nd', (g * act).astype(x.dtype), vg)

    out = lax.map(block_fn, hf).reshape(nb * PEER_BLOCK, D)[:N]
    return out.reshape(B, T, D)


def _norm_matmul_kernel(x_ref, g_ref, w_ref, o_ref, xn_ref):
    @pl.when(pl.program_id(1) == 0)
    def _():
        x = x_ref[...]
        y = x * lax.rsqrt(jnp.mean(x * x, axis=-1, keepdims=True) + NORM_EPS)
        xn_ref[...] = (y * g_ref[...]).astype(jnp.bfloat16)

    o_ref[...] = jnp.dot(xn_ref[...], w_ref[...].astype(jnp.bfloat16), preferred_element_type=jnp.float32)


def norm_matmul(x, g, w, tm=512, tn=512):
    N, D = x.shape
    C = w.shape[1]
    tm = min(tm, N)
    tn = min(tn, C)
    return pl.pallas_call(
        _norm_matmul_kernel,
        grid=(pl.cdiv(N, tm), pl.cdiv(C, tn)),
        in_specs=[
            pl.BlockSpec((tm, D), lambda i, j: (i, 0)),
            pl.BlockSpec((1, D), lambda i, j: (0, 0)),
            pl.BlockSpec((D, tn), lambda i, j: (0, j)),
        ],
        out_specs=pl.BlockSpec((tm, tn), lambda i, j: (i, j)),
        out_shape=jax.ShapeDtypeStruct((N, C), jnp.float32),
        scratch_shapes=[pltpu.VMEM((tm, D), jnp.bfloat16)],
        compiler_params=pltpu.CompilerParams(
            dimension_semantics=("arbitrary", "arbitrary"), vmem_limit_bytes=48 * 1024 * 1024),
        name="norm_matmul",
    )(x, g.reshape(1, D), w)


def token_mix(x, past, lp, tab):
    B, T, _ = x.shape
    P = 0 if past is None else past['moba'].shape[1]
    q_pos = P + jnp.arange(T, dtype=jnp.int32)
    x2 = x.reshape(B * T, D_MODEL)
    z = norm_matmul(x2, lp['norm_mix'], lp['w_in']).reshape(B, T, IN_COLS)
    qa, ka, va, q_nsa, nsa_kv, nsa_g, qc, kc, vc, qi, ki, wi, u, vg = split_last(z, IN_SPLITS)
    moba_new = jnp.stack([ka, va], axis=2).reshape(B, T, 2, N_HEADS, HD)
    nsa_new = nsa_kv.reshape(B, T, 6, NSA_KV_HEADS, HD)
    nsa_main_new = nsa_new[:, :, :4]
    win_new = nsa_new[:, :, 4:]
    dsa_new = jnp.stack([kc, vc], axis=2).reshape(B, T, 2, N_HEADS, HD)
    if past is None:
        moba_all, nsa_all, win_all, dsa_all, idx_all = moba_new, nsa_main_new, win_new, dsa_new, ki
    else:
        moba_all = jnp.concatenate([past['moba'], moba_new], axis=1)
        nsa_all = jnp.concatenate([past['nsa'], nsa_main_new], axis=1)
        win_all = jnp.concatenate([past['win'], win_new], axis=1)
        dsa_all = jnp.concatenate([past['dsa'], dsa_new], axis=1)
        idx_all = jnp.concatenate([past['idx'], ki], axis=1)
    kw_pos0 = P + T - win_all.shape[1]
    y_a = moba_attention(qa.reshape(B, T, N_HEADS, HD), moba_all[:, :, 0], moba_all[:, :, 1], q_pos, tab[:, :N_HEADS])
    y_b = nsa_attention(q_nsa.reshape(B, T, N_HEADS, HD), nsa_g.reshape(B, T, N_HEADS, 3), nsa_all[:, :, 0], nsa_all[:, :, 1], nsa_all[:, :, 2], nsa_all[:, :, 3], win_all[:, :, 0], win_all[:, :, 1], kw_pos0, q_pos, lp['cmp_w1'], lp['cmp_w2'], lp['cmp_pe'], tab[:, N_HEADS:2 * N_HEADS])
    y_c = dsa_attention(qc.reshape(B, T, N_HEADS, HD), qi.reshape(B, T, DSA_IDX_HEADS, DSA_IDX_DIM), wi, dsa_all[:, :, 0], dsa_all[:, :, 1], idx_all, q_pos, tab[:, 2 * N_HEADS:])
    y_d, v_rows = gmlp_sgu(u, vg, lp['gm_ws'], lp['gm_b'], lp['gm_norm'])
    branches = jnp.stack([y_a.reshape(B, T, MIX_W), y_b.reshape(B, T, MIX_W), y_c.reshape(B, T, MIX_W), y_d], axis=2)
    proj = jnp.einsum('btnc,ncd->btnd', branches, lp['w_branch'])
    gz = norm_matmul(x2, lp['norm_mix'], lp['w_gate']).reshape(B, T, N_BRANCH, D_MODEL)
    gate = jax.nn.sigmoid(gz + lp['b_gate'])
    out = jnp.sum(gate * proj, axis=2) @ lp['w_out']
    win_keep = win_all[:, -min(NSA_WINDOW, P + T):]
    return out, (moba_new, nsa_main_new, win_keep, dsa_new, ki, v_rows)


def layer(x, past, mkv, lp, tab):
    mix, rows = token_mix(x, past, lp, tab)
    x = x + mix
    x = x + mem_attention(rmsnorm(x, lp['norm_mem']), mkv, lp['w_mem_q'], lp['w_mem_o'])
    x = x + peer_ffn(rmsnorm(x, lp['norm_ffn']), lp['peer_wq'], lp['peer_keys'], lp['peer_u'], lp['peer_v'])
    return x, rows


def kernel(x_prompt, x_sample, cache_moba_kv, cache_nsa_kv, cache_dsa_kv, cache_dsa_idx, state_nsa_win, cache_mem_kv, page_table, mem_prompt, rel_bias_table, w_in, nsa_cmp_w1, nsa_cmp_w2, nsa_cmp_pe, gm_ws, gm_b, gm_norm, w_branch, w_gate, b_gate, w_out, w_mem_q, w_mem_kv, w_mem_o, peer_wq, peer_keys, peer_u, peer_v, norm_mix, norm_mem, norm_ffn, norm_final):
    xp = x_prompt
    xs = x_sample
    rows_p = []
    rows_s = []
    mem_rows = []
    Bp = x_prompt.shape[0]
    for l in range(DEPTH):
        lp = {
            'w_in': w_in[l], 'cmp_w1': nsa_cmp_w1[l], 'cmp_w2': nsa_cmp_w2[l], 'cmp_pe': nsa_cmp_pe[l],
            'gm_ws': gm_ws[l], 'gm_b': gm_b[l], 'gm_norm': gm_norm[l],
            'w_branch': w_branch[l], 'w_gate': w_gate[l], 'b_gate': b_gate[l], 'w_out': w_out[l],
            'w_mem_q': w_mem_q[l], 'w_mem_o': w_mem_o[l],
            'peer_wq': peer_wq[l], 'peer_keys': peer_keys[l], 'peer_u': peer_u[l], 'peer_v': peer_v[l],
            'norm_mix': norm_mix[l], 'norm_mem': norm_mem[l], 'norm_ffn': norm_ffn[l],
        }
        mkv_p = (mem_prompt @ w_mem_kv[l]).reshape(Bp, MEM_TOKENS, 2, MEM_HEADS, MEM_HD)
        past = {
            'moba': gather_pages(cache_moba_kv[l], page_table),
            'nsa': gather_pages(cache_nsa_kv[l], page_table),
            'dsa': gather_pages(cache_dsa_kv[l], page_table),
            'idx': gather_pages(cache_dsa_idx[l], page_table),
            'win': state_nsa_win[l],
        }
        xp, rp = layer(xp, None, mkv_p, lp, rel_bias_table)
        xs, rs = layer(xs, past, cache_mem_kv[l], lp, rel_bias_table)
        rows_p.append(rp)
        rows_s.append(rs)
        mem_rows.append(mkv_p)
    y_prompt = rmsnorm(xp, norm_final)
    y_sample = rmsnorm(xs, norm_final)
    new_moba_kv_prompt = jnp.stack([r[0] for r in rows_p])
    new_moba_kv_sample = jnp.stack([r[0] for r in rows_s])
    new_nsa_kv_prompt = jnp.stack([r[1] for r in rows_p])
    new_nsa_kv_sample = jnp.stack([r[1] for r in rows_s])
    new_nsa_win_prompt = jnp.stack([r[2] for r in rows_p])
    new_nsa_win_sample = jnp.stack([r[2] for r in rows_s])
    new_dsa_kv_prompt = jnp.stack([r[3] for r in rows_p])
    new_dsa_kv_sample = jnp.stack([r[3] for r in rows_s])
    new_dsa_idx_prompt = jnp.stack([r[4] for r in rows_p])
    new_dsa_idx_sample = jnp.stack([r[4] for r in rows_s])
    new_mem_kv_prompt = jnp.stack(mem_rows)
    new_gmlp_v_sample = jnp.stack([r[5] for r in rows_s])
    return (y_prompt, y_sample, new_moba_kv_prompt, new_moba_kv_sample, new_nsa_kv_prompt, new_nsa_kv_sample, new_nsa_win_prompt, new_nsa_win_sample, new_dsa_kv_prompt, new_dsa_kv_sample, new_dsa_idx_prompt, new_dsa_idx_sample, new_mem_kv_prompt, new_gmlp_v_sample)
```

```python
import functools
import math
import jax
import jax.numpy as jnp
from jax import lax
import numpy as np
from jax.experimental import pallas as pl
from jax.experimental.pallas import tpu as pltpu

D_MODEL = 2048
BATCH = 4
SEQ = 2048
DEPTH = 2
DEC_BATCH = 8
DEC_SEQ = 4
PAST_LEN = 16384
PAGE_SIZE = 128

HD = 64
N_HEADS = 8
MIX_W = N_HEADS * HD
N_BRANCH = 4
MOBA_BLOCK = 256
MOBA_TOPK = 3
MOBA_QB = 32
NSA_KV_HEADS = 2
NSA_GROUP = N_HEADS // NSA_KV_HEADS
NSA_CMP_STRIDE = 16
NSA_CMP_LEN = 2 * NSA_CMP_STRIDE
NSA_SEL_BLOCK = 64
NSA_TOPN = 16
NSA_WINDOW = 512
NSA_QB = 64
WIN_QB = 128
DSA_IDX_HEADS = 4
DSA_IDX_DIM = 64
DSA_TOPK = 256
DSA_QB = 64
GM_CHUNK = 128
GM_GROUPS = 8
GM_W = 512
REL_BUCKETS = 32
REL_MAX_DIST = 128
N_REL_HEADS = 3 * N_HEADS
MEM_TOKENS = 256
MEM_HEADS = 4
MEM_HD = 128
MEM_W = MEM_HEADS * MEM_HD
PEER_HEADS = 8
PEER_NKEYS = 128
PEER_QDIM = 128
PEER_TOPK = 16
PEER_EXPERTS = PEER_NKEYS ** 2
PEER_BLOCK = 128
NORM_EPS = 1e-6
IN_SPLITS = (MIX_W, MIX_W, MIX_W, MIX_W, 6 * NSA_KV_HEADS * HD, 3 * N_HEADS, MIX_W, MIX_W, MIX_W, DSA_IDX_HEADS * DSA_IDX_DIM, DSA_IDX_DIM, DSA_IDX_HEADS, GM_W, GM_W)
IN_COLS = sum(IN_SPLITS)


def rmsnorm(x, g):
    xf = x.astype(jnp.float32)
    y = xf * lax.rsqrt(jnp.mean(xf * xf, axis=-1, keepdims=True) + NORM_EPS)
    return (y * g.astype(jnp.float32)).astype(x.dtype)


def masked_softmax(logits, mask):
    l = jnp.where(mask, logits.astype(jnp.float32), -jnp.inf)
    m = jnp.max(l, axis=-1, keepdims=True)
    m = jnp.where(jnp.isfinite(m), m, 0.0)
    e = jnp.exp(l - m)
    return e / jnp.maximum(jnp.sum(e, axis=-1, keepdims=True), 1e-30)


def rel_bucket(dist):
    n = jnp.maximum(dist, 0)
    max_exact = REL_BUCKETS // 2
    nf = jnp.maximum(n, 1).astype(jnp.float32)
    large = max_exact + (jnp.log(nf / max_exact) / math.log(REL_MAX_DIST / max_exact) * (REL_BUCKETS - max_exact)).astype(jnp.int32)
    return jnp.where(n < max_exact, n, jnp.minimum(large, REL_BUCKETS - 1))


def rel_bias(dist, tab):
    H = tab.shape[1]
    return tab.T[jnp.arange(H)[:, None], rel_bucket(dist)].astype(jnp.float32)


def split_last(z, sizes):
    return jnp.split(z, np.cumsum(sizes)[:-1].tolist(), axis=-1)


def map_query_blocks(fn, qb, q_pos, *xs):
    T = q_pos.shape[0]
    qb = qb if T % qb == 0 else T
    nb = T // qb
    xb = tuple(jnp.moveaxis(x.reshape(x.shape[0], nb, qb, *x.shape[2:]), 1, 0) for x in xs)
    out = lax.map(lambda a: fn(*a), (q_pos.reshape(nb, qb),) + xb)
    return jnp.moveaxis(out, 0, 1).reshape(out.shape[1], T, *out.shape[3:])


def gather_pages(pool, page_table):
    g = pool[page_table]
    return g.reshape(g.shape[0], g.shape[1] * g.shape[2], *g.shape[3:])


def moba_attention(q, k, v, q_pos, tab):
    B, L, H, _ = k.shape
    nblk = -(-L // MOBA_BLOCK)
    pad = ((0, 0), (0, nblk * MOBA_BLOCK - L), (0, 0), (0, 0))
    kb = jnp.pad(k, pad).reshape(B, nblk, MOBA_BLOCK, H, HD)
    vb = jnp.pad(v, pad).reshape(B, nblk, MOBA_BLOCK, H, HD)
    k_mean = jnp.mean(kb.astype(jnp.float32), axis=2).astype(k.dtype)
    kb_t = jnp.transpose(kb, (0, 3, 1, 2, 4))
    vb_t = jnp.transpose(vb, (0, 3, 1, 2, 4))
    ntop = min(MOBA_TOPK, nblk)
    bi = jnp.arange(B)[:, None, None, None]
    hi = jnp.arange(H)[None, None, :, None]
    off = jnp.arange(MOBA_BLOCK)
    scale = HD ** -0.5

    def block_fn(qp, qq):
        qb = qp.shape[0]
        own = qp // MOBA_BLOCK
        gs = jnp.einsum('bqhd,bjhd->bqhj', qq, k_mean).astype(jnp.float32)
        fully_past = jnp.arange(nblk)[None, :] < own[:, None]
        gs = jnp.where(fully_past[None, :, None, :], gs, -jnp.inf)
        _, top = lax.top_k(gs, ntop)
        own_b = jnp.broadcast_to(own[None, :, None, None], (B, qb, H, 1))
        sel = jnp.concatenate([top, own_b], axis=-1)
        sel_ok = jnp.concatenate([top < own[None, :, None, None], jnp.ones(own_b.shape, bool)], axis=-1)
        kg = kb_t[bi, hi, sel].reshape(B, qb, H, (ntop + 1) * MOBA_BLOCK, HD)
        vg = vb_t[bi, hi, sel].reshape(B, qb, H, (ntop + 1) * MOBA_BLOCK, HD)
        pos = (sel[..., None] * MOBA_BLOCK + off).reshape(B, qb, H, -1)
        dist = qp[None, :, None, None] - pos
        ok = jnp.repeat(sel_ok, MOBA_BLOCK, axis=-1) & (dist >= 0)
        lg = jnp.einsum('bqhd,bqhkd->bqhk', qq, kg).astype(jnp.float32) * scale + rel_bias(dist, tab)
        p = masked_softmax(lg, ok)
        return jnp.einsum('bqhk,bqhkd->bqhd', p.astype(vg.dtype), vg)

    return map_query_blocks(block_fn, MOBA_QB, q_pos, q)


def nsa_attention(q, gates, kc, vc, ks, vs, kw, vw, kw_pos0, q_pos, w1, w2, pe, tab):
    B, L, G, _ = kc.shape
    T = q.shape[1]
    H = N_HEADS
    scale = HD ** -0.5
    nch = L // NSA_CMP_STRIDE
    ncmp = nch - 1

    def compress(x, i):
        c = x[:, :nch * NSA_CMP_STRIDE].reshape(B, nch, NSA_CMP_STRIDE, G, HD)
        blk = jnp.concatenate([c[:, :-1], c[:, 1:]], axis=2) + pe[i][None, None, :, None, :]
        blk = jnp.transpose(blk, (0, 1, 3, 2, 4)).reshape(B, ncmp, G, NSA_CMP_LEN * HD)
        return jax.nn.gelu(blk @ w1[i]) @ w2[i]

    kcmp = compress(kc, 0)
    vcmp = compress(vc, 1)
    cmp_end = jnp.arange(ncmp) * NSA_CMP_STRIDE + NSA_CMP_LEN - 1
    qg = q.reshape(B, T, G, NSA_GROUP, HD)
    dist_c = q_pos[:, None] - cmp_end[None, :]
    lg_c = jnp.einsum('btgrd,bngd->btgrn', qg, kcmp).astype(jnp.float32).reshape(B, T, H, ncmp) * scale
    lg_c = lg_c + rel_bias(dist_c[None, :, None, :], tab)
    p_c = masked_softmax(lg_c, (dist_c >= 0)[None, :, None, :])
    p_cg = p_c.reshape(B, T, G, NSA_GROUP, ncmp)
    o_cmp = jnp.einsum('btgrn,bngd->btgrd', p_cg.astype(vcmp.dtype), vcmp).reshape(B, T, H, HD)
    nsel = -(-L // NSA_SEL_BLOCK)
    ii = np.arange(ncmp)[:, None]
    jj_np = np.arange(nsel)[None, :]
    overlap = ((ii * NSA_CMP_STRIDE < (jj_np + 1) * NSA_SEL_BLOCK) & (ii * NSA_CMP_STRIDE + NSA_CMP_LEN > jj_np * NSA_SEL_BLOCK)).astype(np.float32)
    imp = jnp.einsum('btgrn,nj->btgj', p_cg, jnp.asarray(overlap))
    cur = q_pos // NSA_SEL_BLOCK
    jj = jnp.arange(nsel)[None, :]
    causal_blk = jj <= cur[:, None]
    forced = (jj == 0) | (jj == cur[:, None]) | (jj == cur[:, None] - 1)
    imp = jnp.where(forced[None, :, None, :], jnp.inf, jnp.where(causal_blk[None, :, None, :], imp, -jnp.inf))
    ntop = min(NSA_TOPN, nsel)
    _, sel = lax.top_k(imp, ntop)
    pad = ((0, 0), (0, nsel * NSA_SEL_BLOCK - L), (0, 0), (0, 0))
    ksb = jnp.transpose(jnp.pad(ks, pad).reshape(B, nsel, NSA_SEL_BLOCK, G, HD), (0, 3, 1, 2, 4))
    vsb = jnp.transpose(jnp.pad(vs, pad).reshape(B, nsel, NSA_SEL_BLOCK, G, HD), (0, 3, 1, 2, 4))
    bi = jnp.arange(B)[:, None, None, None]
    gi = jnp.arange(G)[None, None, :, None]
    off = jnp.arange(NSA_SEL_BLOCK)

    def sel_fn(qp, qq, ss):
        qb = qp.shape[0]
        kg = ksb[bi, gi, ss].reshape(B, qb, G, ntop * NSA_SEL_BLOCK, HD)
        vg = vsb[bi, gi, ss].reshape(B, qb, G, ntop * NSA_SEL_BLOCK, HD)
        pos = (ss[..., None] * NSA_SEL_BLOCK + off).reshape(B, qb, G, -1)
        dist = jnp.repeat(qp[None, :, None, None] - pos, NSA_GROUP, axis=2)
        lg = jnp.einsum('bqgrd,bqgkd->bqgrk', qq.reshape(B, qb, G, NSA_GROUP, HD), kg).astype(jnp.float32).reshape(B, qb, H, -1) * scale
        p = masked_softmax(lg + rel_bias(dist, tab), dist >= 0)
        o = jnp.einsum('bqgrk,bqgkd->bqgrd', p.reshape(B, qb, G, NSA_GROUP, -1).astype(vg.dtype), vg)
        return o.reshape(B, qb, H, HD)

    o_sel = map_query_blocks(sel_fn, NSA_QB, q_pos, q, sel)
    W = NSA_WINDOW
    wpad = ((0, 0), (W, 0), (0, 0), (0, 0))
    kwp = jnp.pad(kw, wpad)
    vwp = jnp.pad(vw, wpad)

    def win_fn(qp, qq):
        qb = qp.shape[0]
        span = W + qb - 1
        start = qp[0] - kw_pos0 + 1
        kk = lax.dynamic_slice_in_dim(kwp, start, span, axis=1)
        vv = lax.dynamic_slice_in_dim(vwp, start, span, axis=1)
        pos = qp[0] - W + 1 + jnp.arange(span)
        dist = qp[:, None] - pos[None, :]
        ok = (dist >= 0) & (dist < W) & (pos[None, :] >= 0)
        lg = jnp.einsum('bqgrd,bkgd->bqgrk', qq.reshape(B, qb, G, NSA_GROUP, HD), kk).astype(jnp.float32).reshape(B, qb, H, span) * scale
        p = masked_softmax(lg + rel_bias(dist[None, :, None, :], tab), ok[None, :, None, :])
        o = jnp.einsum('bqgrk,bkgd->bqgrd', p.reshape(B, qb, G, NSA_GROUP, span).astype(vv.dtype), vv)
        return o.reshape(B, qb, H, HD)

    o_win = map_query_blocks(win_fn, WIN_QB, q_pos, q)
    g = jax.nn.sigmoid(gates.astype(jnp.float32)).astype(q.dtype)
    return g[..., 0:1] * o_cmp + g[..., 1:2] * o_sel + g[..., 2:3] * o_win


def dsa_attention(q, qi, wi, k, v, ki, q_pos, tab):
    B, L, H, _ = k.shape
    kk = min(DSA_TOPK, L // 4)
    bi = jnp.arange(B)[:, None, None]
    scale = HD ** -0.5

    def block_fn(qp, qq, qqi, wwi):
        s = jax.nn.relu(jnp.einsum('bqhd,bld->bqhl', qqi, ki).astype(jnp.float32) * DSA_IDX_DIM ** -0.5)
        score = jnp.einsum('bqh,bqhl->bql', wwi.astype(jnp.float32), s)
        admissible = jnp.arange(L)[None, :] <= qp[:, None]
        score = jnp.where(admissible[None], score, -jnp.inf)
        _, sel = lax.top_k(score, kk)
        kg = k[bi, sel]
        vg = v[bi, sel]
        dist = (qp[None, :, None] - sel)[:, :, None, :]
        lg = jnp.einsum('bqhd,bqkhd->bqhk', qq, kg).astype(jnp.float32) * scale + rel_bias(dist, tab)
        p = masked_softmax(lg, dist >= 0)
        return jnp.einsum('bqhk,bqkhd->bqhd', p.astype(vg.dtype), vg)

    return map_query_blocks(block_fn, DSA_QB, q_pos, q, qi, wi)


def gmlp_sgu(u, v, ws, b, gnorm):
    B, T, _ = u.shape
    vn = rmsnorm(v, gnorm)
    nc = -(-T // GM_CHUNK)
    vc = jnp.pad(vn, ((0, 0), (0, nc * GM_CHUNK - T), (0, 0))).reshape(B, nc, GM_CHUNK, GM_GROUPS, GM_W // GM_GROUPS)
    wsm = ws * jnp.tril(jnp.ones((GM_CHUNK, GM_CHUNK), ws.dtype))
    mixed = jnp.einsum('gij,bcjgd->bcigd', wsm, vc) + b.T[None, None, :, :, None]
    mixed = mixed.reshape(B, nc * GM_CHUNK, GM_W)[:, :T]
    return u * mixed, vn


def mem_attention(h, mkv, wq, wo):
    B, T, _ = h.shape
    q = (h @ wq).reshape(B, T, MEM_HEADS, MEM_HD)
    lg = jnp.einsum('bthd,bmhd->bthm', q, mkv[:, :, 0]).astype(jnp.float32) * MEM_HD ** -0.5
    p = jax.nn.softmax(lg, axis=-1)
    o = jnp.einsum('bthm,bmhd->bthd', p.astype(mkv.dtype), mkv[:, :, 1]).reshape(B, T, MEM_W)
    return o @ wo


def peer_ffn(h, wq, keys, U, V):
    B, T, D = h.shape
    N = B * T
    nb = -(-N // PEER_BLOCK)
    hf = jnp.pad(h.reshape(N, D), ((0, nb * PEER_BLOCK - N), (0, 0))).reshape(nb, PEER_BLOCK, D)

    def block_fn(x):
        q = (x @ wq).reshape(PEER_BLOCK, PEER_HEADS, 2, PEER_QDIM // 2)
        s = jnp.einsum('nhcd,hckd->nhck', q, keys).astype(jnp.float32)
        v1, i1 = lax.top_k(s[:, :, 0], PEER_TOPK)
        v2, i2 = lax.top_k(s[:, :, 1], PEER_TOPK)
        cand = (v1[..., :, None] + v2[..., None, :]).reshape(PEER_BLOCK, PEER_HEADS, PEER_TOPK * PEER_TOPK)
        sv, ci = lax.top_k(cand, PEER_TOPK)
        e = jnp.take_along_axis(i1, ci // PEER_TOPK, axis=-1) * PEER_NKEYS + jnp.take_along_axis(i2, ci % PEER_TOPK, axis=-1)
        g = jax.nn.softmax(sv, axis=-1)
        ug = U[e]
        vg = V[e]
        act = jax.nn.gelu(jnp.einsum('nd,nhkd->nhk', x, ug).astype(jnp.float32))
        return jnp.einsum('nhk,nhkd->nd', (g * act).astype(x.dtype), vg)

    out = lax.map(block_fn, hf).reshape(nb * PEER_BLOCK, D)[:N]
    return out.reshape(B, T, D)


def _norm_matmul_kernel(x_ref, g_ref, w_ref, o_ref, xn_ref):
    @pl.when(pl.program_id(1) == 0)
    def _():
        x = x_ref[...]
        y = x * lax.rsqrt(jnp.mean(x * x, axis=-1, keepdims=True) + NORM_EPS)
        xn_ref[...] = (y * g_ref[...]).astype(jnp.bfloat16)

    o_ref[...] = jnp.dot(xn_ref[...], w_ref[...].astype(jnp.bfloat16), preferred_element_type=jnp.float32)


def norm_matmul(x, g, w, tm=512, tn=512):
    N, D = x.shape
    C = w.shape[1]
    tm = min(tm, N)
    tn = min(tn, C)
    return pl.pallas_call(
        _norm_matmul_kernel,
        grid=(pl.cdiv(N, tm), pl.cdiv(C, tn)),
        in_specs=[
            pl.BlockSpec((tm, D), lambda i, j: (i, 0)),
            pl.BlockSpec((1, D), lambda i, j: (0, 0)),
            pl.BlockSpec((D, tn), lambda i, j: (0, j)),
        ],
        out_specs=pl.BlockSpec((tm, tn), lambda i, j: (i, j)),
        out_shape=jax.ShapeDtypeStruct((N, C), jnp.float32),
        scratch_shapes=[pltpu.VMEM((tm, D), jnp.bfloat16)],
        compiler_params=pltpu.CompilerParams(
            dimension_semantics=("arbitrary", "arbitrary"), vmem_limit_bytes=48 * 1024 * 1024),
        name="norm_matmul",
    )(x, g.reshape(1, D), w)


TQ = 256
TK = 256
NEG_BIG = -1e30
VMEM_LIMIT = 48 * 1024 * 1024
_BF16 = jnp.bfloat16
_F32 = jnp.float32


def _attn_kernel(mode, qT_ref, k_ref, vT_ref, bias_ref, *rest):
    if mode == 'win':
        (o_ref,) = rest
        m_ref = None
    else:
        m_ref, o_ref = rest
    qi = pl.program_id(2)
    q = qT_ref[0, 0].astype(_BF16)
    lane = lax.broadcasted_iota(jnp.int32, (TK, TQ), 1)
    sub = lax.broadcasted_iota(jnp.int32, (TK, TQ), 0)
    rel0 = lane - sub

    def body(kj, carry):
        m, l, acc = carry
        k_t = k_ref[0, 0, pl.ds(pl.multiple_of(kj * TK, TK), TK), :].astype(_BF16)
        s = jnp.dot(k_t, q, preferred_element_type=_F32)
        s = s + bias_ref[0, jnp.minimum(qi - kj, 2)]
        dist = rel0 + (qi - kj) * TQ
        ok = dist >= 0
        if mode == 'moba':
            ok = ok & (m_ref[0, 0, kj] > 0.5)
        elif mode == 'sel':
            rows = m_ref[0, 0, kj]
            blk = sub // NSA_SEL_BLOCK
            r = jnp.where(blk == 0, rows[0:1], jnp.where(blk == 1, rows[1:2], jnp.where(blk == 2, rows[2:3], rows[3:4])))
            ok = ok & (r > 0.5)
        elif mode == 'dsa':
            ok = ok & (m_ref[0, pl.ds(pl.multiple_of(kj * TK, TK), TK), :].astype(_F32) > 0.5)
        elif mode == 'win':
            ok = ok & (dist < NSA_WINDOW)
        s = jnp.where(ok, s, NEG_BIG)
        m_new = jnp.maximum(m, jnp.max(s, axis=0, keepdims=True))
        p = jnp.where(ok, jnp.exp(s - m_new), 0.0)
        alpha = jnp.exp(m - m_new)
        l = alpha * l + jnp.sum(p, axis=0, keepdims=True)
        v_t = vT_ref[0, 0, kj].astype(_BF16)
        acc = alpha * acc + jnp.dot(v_t, p.astype(_BF16), preferred_element_type=_F32)
        return m_new, l, acc

    lo = jnp.maximum(qi - (NSA_WINDOW // TK), 0) if mode == 'win' else 0
    init = (jnp.full((1, TQ), NEG_BIG, _F32), jnp.zeros((1, TQ), _F32), jnp.zeros((HD, TQ), _F32))
    m, l, acc = lax.fori_loop(lo, qi + 1, body, init)
    o_ref[0, 0] = acc / jnp.maximum(l, 1e-30)


def attention_T(mode, qT, k, vT, bias_tiles, mask=None):
    B, H, _, T = qT.shape
    Hkv, L = k.shape[1], k.shape[2]
    grp = H // Hkv
    in_specs = [
        pl.BlockSpec((1, 1, HD, TQ), lambda b, h, i: (b, h, 0, i)),
        pl.BlockSpec((1, 1, L, HD), lambda b, h, i: (b, h // grp, 0, 0)),
        pl.BlockSpec((1, 1, L // TK, HD, TK), lambda b, h, i: (b, h // grp, 0, 0, 0)),
        pl.BlockSpec((1, 3, TK, TQ), lambda b, h, i: (h, 0, 0, 0)),
    ]
    args = [qT, k, vT, bias_tiles]
    if mode == 'moba':
        in_specs.append(pl.BlockSpec((1, 1, L // TK, 1, TQ), lambda b, h, i: (b, h, 0, 0, i)))
        args.append(mask.reshape(B, H, L // TK, 1, T))
    elif mode == 'sel':
        per = TK // NSA_SEL_BLOCK
        in_specs.append(pl.BlockSpec((1, 1, L // TK, per, TQ), lambda b, h, i: (b, h // grp, 0, 0, i)))
        args.append(mask.reshape(B, Hkv, L // TK, per, T))
    elif mode == 'dsa':
        in_specs.append(pl.BlockSpec((1, L, TQ), lambda b, h, i: (b, 0, i)))
        args.append(mask)
    return pl.pallas_call(
        functools.partial(_attn_kernel, mode),
        grid=(B, H, T // TQ),
        in_specs=in_specs,
        out_specs=pl.BlockSpec((1, 1, HD, TQ), lambda b, h, i: (b, h, 0, i)),
        out_shape=jax.ShapeDtypeStruct((B, H, HD, T), _F32),
        compiler_params=pltpu.CompilerParams(
            dimension_semantics=("arbitrary", "arbitrary", "arbitrary"), vmem_limit_bytes=VMEM_LIMIT),
        name="attn_" + mode,
    )(*args)


def _moba_gate_kernel(qT_ref, k_ref, sel_ref):
    nblk, T = sel_ref.shape[2], sel_ref.shape[3]
    q = qT_ref[0, 0].astype(_BF16)
    k = k_ref[0, 0]
    km = jnp.mean(k.reshape(nblk, MOBA_BLOCK, HD), axis=1)
    gs = jnp.dot(km.astype(_BF16), q, preferred_element_type=_F32)
    j = lax.broadcasted_iota(jnp.int32, (nblk, T), 0)
    own = lax.broadcasted_iota(jnp.int32, (nblk, T), 1) // MOBA_BLOCK
    fully_past = j < own
    gs = jnp.where(fully_past, gs, -jnp.inf)
    rank = jnp.zeros((nblk, T), jnp.int32)
    for jp in range(nblk):
        row = gs[jp:jp + 1, :]
        ahead = (row > gs) | ((row == gs) & (jp < j))
        rank = rank + ahead.astype(jnp.int32)
    sel = (fully_past & (rank < MOBA_TOPK)) | (j == own)
    sel_ref[0, 0] = sel.astype(_F32)


def moba_gate(qT, k):
    B, H, _, T = qT.shape
    L = k.shape[2]
    nblk = L // MOBA_BLOCK
    return pl.pallas_call(
        _moba_gate_kernel,
        grid=(B, H),
        in_specs=[pl.BlockSpec((1, 1, HD, T), lambda b, h: (b, h, 0, 0)),
                  pl.BlockSpec((1, 1, L, HD), lambda b, h: (b, h, 0, 0))],
        out_specs=pl.BlockSpec((1, 1, nblk, T), lambda b, h: (b, h, 0, 0)),
        out_shape=jax.ShapeDtypeStruct((B, H, nblk, T), _F32),
        compiler_params=pltpu.CompilerParams(dimension_semantics=("arbitrary", "arbitrary"), vmem_limit_bytes=VMEM_LIMIT),
        name="moba_gate",
    )(qT, k)


def _count_rows(pred):
    return jnp.sum(pred.astype(_F32), axis=0, keepdims=True)


def _dsa_mask_kernel(topk, ki_ref, qiT_ref, wiT_ref, mask_ref, key_ref):
    L = ki_ref.shape[1]
    qi = pl.program_id(1)
    ki = ki_ref[0].astype(_BF16)
    score = jnp.zeros((L, TQ), _F32)
    for hh in range(DSA_IDX_HEADS):
        s = jnp.dot(ki, qiT_ref[0, hh].astype(_BF16), preferred_element_type=_F32) * DSA_IDX_DIM ** -0.5
        score = score + wiT_ref[0, hh:hh + 1, :] * jnp.maximum(s, 0.0)
    idx = lax.broadcasted_iota(jnp.int32, (L, TQ), 0)
    q_pos = qi * TQ + lax.broadcasted_iota(jnp.int32, (L, TQ), 1)
    adm = idx <= q_pos
    score = jnp.where(adm, jnp.where(score == 0.0, 0.0, score), -jnp.inf)
    bits = pltpu.bitcast(score, jnp.int32)
    key_ref[...] = jnp.where(bits < 0, bits ^ jnp.int32(0x7FFFFFFF), bits)
    kf = jnp.float32(topk)
    int_min = jnp.int32(-2 ** 31)
    lo = jnp.where(_count_rows(key_ref[...] >= 0) >= kf, jnp.int32(0), int_min)

    def vstep(i, lo):
        cand = lo + jnp.left_shift(jnp.int32(1), 30 - i)
        return jnp.where(_count_rows(key_ref[...] >= cand) >= kf, cand, lo)

    thr = lax.fori_loop(0, 31, vstep, lo)
    need = kf - _count_rows(key_ref[...] > thr)
    nbits = max(1, (L - 1).bit_length())

    def istep(i, lo):
        cand = lo + jnp.left_shift(jnp.int32(1), nbits - 1 - i)
        c = _count_rows((key_ref[...] == thr) & (idx < cand))
        return jnp.where(c < need, cand, lo)

    last_tie = lax.fori_loop(0, nbits, istep, jnp.zeros((1, TQ), jnp.int32))
    key = key_ref[...]
    keep = ((key > thr) | ((key == thr) & (idx <= last_tie))) & adm
    mask_ref[0] = keep.astype(mask_ref.dtype)


def dsa_mask(ki, qiT, wiT, topk):
    B, L, _ = ki.shape
    T = qiT.shape[3]
    return pl.pallas_call(
        functools.partial(_dsa_mask_kernel, topk),
        grid=(B, T // TQ),
        in_specs=[pl.BlockSpec((1, L, DSA_IDX_DIM), lambda b, i: (b, 0, 0)),
                  pl.BlockSpec((1, DSA_IDX_HEADS, DSA_IDX_DIM, TQ), lambda b, i: (b, 0, 0, i)),
                  pl.BlockSpec((1, DSA_IDX_HEADS, TQ), lambda b, i: (b, 0, i))],
        out_specs=pl.BlockSpec((1, L, TQ), lambda b, i: (b, 0, i)),
        out_shape=jax.ShapeDtypeStruct((B, L, T), _BF16),
        scratch_shapes=[pltpu.VMEM((L, TQ), jnp.int32)],
        compiler_params=pltpu.CompilerParams(dimension_semantics=("arbitrary", "arbitrary"), vmem_limit_bytes=VMEM_LIMIT),
        name="dsa_mask",
    )(ki, qiT, wiT)


PEER_TN = 256
PEER_EB = 1024
PEER_IB = PEER_EB // PEER_NKEYS
RANK_OUT = 4096.0


def _extract_top(work, n):
    rows = work.shape[0]
    iota = lax.broadcasted_iota(jnp.int32, work.shape, 0)
    order = jnp.full(work.shape, RANK_OUT, _F32)
    vals = []
    idx = None
    for a in range(n):
        m = jnp.max(work, axis=0, keepdims=True)
        idx = jnp.min(jnp.where(work == m, iota, rows), axis=0, keepdims=True)
        hit = iota == idx
        vals.append(m)
        order = jnp.where(hit, jnp.float32(a), order)
        work = jnp.where(hit, -jnp.inf, work)
    return vals, order, idx


def _peer_select_kernel(xT_ref, g_ref, wqT_ref, keys_ref, hxT_ref, s1m_ref, s2m_ref, e1n_ref, e2_ref, r1_ref, r2_ref, tau_ref, phi_ref):
    x = xT_ref[...]
    y = x * lax.rsqrt(jnp.mean(x * x, axis=0, keepdims=True) + NORM_EPS) * g_ref[...]
    hx = y.astype(_BF16)
    hxT_ref[...] = hx
    qT = jnp.dot(wqT_ref[...], hx, preferred_element_type=_F32).astype(_BF16)
    half = PEER_QDIM // 2
    for h in range(PEER_HEADS):
        sc = []
        for c in range(2):
            r0 = (h * 2 + c) * half
            sc.append(jnp.dot(keys_ref[h * 2 + c], qT[r0:r0 + half, :], preferred_element_type=_F32))
        v1, o1, _ = _extract_top(sc[0], PEER_TOPK)
        v2, o2, _ = _extract_top(sc[1], PEER_TOPK)
        v2s = jnp.concatenate(v2, axis=0)
        cand = jnp.concatenate([v1[a] + v2s for a in range(PEER_TOPK)], axis=0)
        ex1 = [jnp.exp(v1[a] - v1[0]) for a in range(PEER_TOPK)]
        ex2s = jnp.exp(v2s - v2[0])
        ecand = jnp.concatenate([ex1[a] * ex2s for a in range(PEER_TOPK)], axis=0)
        cv, corder, clast = _extract_top(cand, PEER_TOPK)
        z = jnp.sum(jnp.where(corder < RANK_OUT, ecand, 0.0), axis=0, keepdims=True)
        s1m_ref[h] = jnp.where(o1 < RANK_OUT, sc[0], -jnp.inf)
        s2m_ref[h] = jnp.where(o2 < RANK_OUT, sc[1], -jnp.inf)
        e1n_ref[h] = jnp.exp(sc[0] - v1[0]) / z
        e2_ref[h] = jnp.exp(sc[1] - v2[0])
        r1_ref[h] = o1 * PEER_TOPK
        r2_ref[h] = o2
        tau_ref[h:h + 1, :] = cv[PEER_TOPK - 1]
        phi_ref[h:h + 1, :] = clast.astype(_F32)


def peer_select(xT, g, wqT, keys2):
    D, Np = xT.shape
    tn = PEER_TN
    row = lambda: pl.BlockSpec((PEER_HEADS, PEER_NKEYS, tn), lambda t: (0, 0, t))
    rshape = jax.ShapeDtypeStruct((PEER_HEADS, PEER_NKEYS, Np), _F32)
    return pl.pallas_call(
        _peer_select_kernel,
        grid=(Np // tn,),
        in_specs=[pl.BlockSpec((D, tn), lambda t: (0, t)),
                  pl.BlockSpec((D, 1), lambda t: (0, 0)),
                  pl.BlockSpec((PEER_HEADS * PEER_QDIM, D), lambda t: (0, 0)),
                  pl.BlockSpec((2 * PEER_HEADS, PEER_NKEYS, PEER_QDIM // 2), lambda t: (0, 0, 0))],
        out_specs=[pl.BlockSpec((D, tn), lambda t: (0, t)), row(), row(), row(), row(), row(), row(),
                   pl.BlockSpec((PEER_HEADS, tn), lambda t: (0, t)), pl.BlockSpec((PEER_HEADS, tn), lambda t: (0, t))],
        out_shape=[jax.ShapeDtypeStruct((D, Np), _BF16), rshape, rshape, rshape, rshape, rshape, rshape,
                   jax.ShapeDtypeStruct((PEER_HEADS, Np), _F32), jax.ShapeDtypeStruct((PEER_HEADS, Np), _F32)],
        compiler_params=pltpu.CompilerParams(dimension_semantics=("arbitrary",), vmem_limit_bytes=VMEM_LIMIT),
        name="peer_select",
    )(xT, g, wqT, keys2)


def _gelu_tanh(x):
    return 0.5 * x * (1.0 + jnp.tanh(math.sqrt(2.0 / math.pi) * (x + 0.044715 * (x * x * x))))


def _peer_dense_kernel(hxT_ref, u_ref, vT_ref, s1m_ref, e1n_ref, r1_ref, s2m_ref, e2_ref, r2_ref, tau_ref, phi_ref, o_ref, p_ref):
    @pl.when(pl.program_id(1) == 0)
    def _():
        o_ref[...] = jnp.zeros_like(o_ref)

    act = _gelu_tanh(jnp.dot(u_ref[...], hxT_ref[...], preferred_element_type=_F32))
    for ib in range(PEER_IB):
        w = jnp.zeros((PEER_NKEYS, o_ref.shape[1]), _F32)
        for h in range(PEER_HEADS):
            tau = tau_ref[h:h + 1, :]
            s = s1m_ref[h, ib:ib + 1, :] + s2m_ref[h]
            flat = r1_ref[h, ib:ib + 1, :] + r2_ref[h]
            sel = (s > tau) | ((s == tau) & (flat <= phi_ref[h:h + 1, :]))
            w = w + jnp.where(sel, e1n_ref[h, ib:ib + 1, :] * e2_ref[h], 0.0)
        p_ref[ib * PEER_NKEYS:(ib + 1) * PEER_NKEYS, :] = (w * act[ib * PEER_NKEYS:(ib + 1) * PEER_NKEYS, :]).astype(_BF16)
    o_ref[...] += jnp.dot(vT_ref[...], p_ref[...], preferred_element_type=_F32)


def peer_dense(hxT, u, vT, s1m, s2m, e1n, e2, r1, r2, tau, phi):
    D, Np = hxT.shape
    E = u.shape[0]
    tn = PEER_TN
    rowi = lambda: pl.BlockSpec((PEER_HEADS, PEER_IB, tn), lambda t, e: (0, e, t))
    rowj = lambda: pl.BlockSpec((PEER_HEADS, PEER_NKEYS, tn), lambda t, e: (0, 0, t))
    hd = lambda: pl.BlockSpec((PEER_HEADS, tn), lambda t, e: (0, t))
    return pl.pallas_call(
        _peer_dense_kernel,
        grid=(Np // tn, E // PEER_EB),
        in_specs=[pl.BlockSpec((D, tn), lambda t, e: (0, t)),
                  pl.BlockSpec((PEER_EB, D), lambda t, e: (e, 0)),
                  pl.BlockSpec((D, PEER_EB), lambda t, e: (0, e)),
                  rowi(), rowi(), rowi(), rowj(), rowj(), rowj(), hd(), hd()],
        out_specs=pl.BlockSpec((D, tn), lambda t, e: (0, t)),
        out_shape=jax.ShapeDtypeStruct((D, Np), _F32),
        scratch_shapes=[pltpu.VMEM((PEER_EB, tn), _BF16)],
        compiler_params=pltpu.CompilerParams(dimension_semantics=("arbitrary", "arbitrary"), vmem_limit_bytes=VMEM_LIMIT),
        name="peer_dense",
    )(hxT, u, vT, s1m, e1n, r1, s2m, e2, r2, tau, phi)


def peer_ffn_tokens(x_tokens, g, wq, keys, U, V):
    N, D = x_tokens.shape
    Np = -(-N // PEER_TN) * PEER_TN
    xT = jnp.pad(x_tokens, ((0, Np - N), (0, 0))).T
    outs = peer_select(xT, g.reshape(D, 1), wq.T.astype(_BF16), keys.reshape(2 * PEER_HEADS, PEER_NKEYS, PEER_QDIM // 2).astype(_BF16))
    hxT, s1m, s2m, e1n, e2, r1, r2, tau, phi = outs
    outT = peer_dense(hxT, U.astype(_BF16), V.T.astype(_BF16), s1m, s2m, e1n, e2, r1, r2, tau, phi)
    return outT.T[:N]


def rel_bias_tiles(tab):
    kk = jnp.arange(TK)[:, None]
    qq = jnp.arange(TQ)[None, :]
    dist = jnp.stack([d * TQ + qq - kk for d in range(3)])
    return jnp.transpose(tab[rel_bucket(dist)], (3, 0, 1, 2)).astype(_F32)


def heads_T(z, nh):
    B, T, _ = z.shape
    return jnp.transpose(z.reshape(B, T, nh, HD), (0, 2, 3, 1))


def heads_K(z, nh):
    B, L, _ = z.shape
    return jnp.transpose(z.reshape(B, L, nh, HD), (0, 2, 1, 3))


def heads_VT(z, nh):
    B, L, _ = z.shape
    return jnp.transpose(z.reshape(B, L // TK, TK, nh, HD), (0, 3, 1, 4, 2))


def from_heads_T(oT):
    B, H, _, T = oT.shape
    return jnp.transpose(oT, (0, 3, 1, 2)).reshape(B, T, H * HD)


def nsa_attention_prompt(q_flat, gates_flat, nsa_new, q_pos, w1, w2, pe, tab, bias_tiles):
    B, T, _ = q_flat.shape
    G, H, L = NSA_KV_HEADS, N_HEADS, T
    scale = HD ** -0.5
    q = q_flat.reshape(B, T, H, HD)
    kc, vc = nsa_new[:, :, 0], nsa_new[:, :, 1]
    nch = L // NSA_CMP_STRIDE
    ncmp = nch - 1

    def compress(x, i):
        c = x[:, :nch * NSA_CMP_STRIDE].reshape(B, nch, NSA_CMP_STRIDE, G, HD)
        blk = jnp.concatenate([c[:, :-1], c[:, 1:]], axis=2) + pe[i][None, None, :, None, :]
        blk = jnp.transpose(blk, (0, 1, 3, 2, 4)).reshape(B, ncmp, G, NSA_CMP_LEN * HD)
        return jax.nn.gelu(blk @ w1[i]) @ w2[i]

    kcmp = compress(kc, 0)
    vcmp = compress(vc, 1)
    cmp_end = jnp.arange(ncmp) * NSA_CMP_STRIDE + NSA_CMP_LEN - 1
    qg = q.reshape(B, T, G, NSA_GROUP, HD)
    dist_c = q_pos[:, None] - cmp_end[None, :]
    lg_c = jnp.einsum('btgrd,bngd->btgrn', qg, kcmp).astype(jnp.float32).reshape(B, T, H, ncmp) * scale
    lg_c = lg_c + rel_bias(dist_c[None, :, None, :], tab)
    p_c = masked_softmax(lg_c, (dist_c >= 0)[None, :, None, :])
    p_cg = p_c.reshape(B, T, G, NSA_GROUP, ncmp)
    o_cmp = jnp.einsum('btgrn,bngd->btgrd', p_cg.astype(vcmp.dtype), vcmp).reshape(B, T, H, HD)
    nsel = -(-L // NSA_SEL_BLOCK)
    ii = np.arange(ncmp)[:, None]
    jj_np = np.arange(nsel)[None, :]
    overlap = ((ii * NSA_CMP_STRIDE < (jj_np + 1) * NSA_SEL_BLOCK) & (ii * NSA_CMP_STRIDE + NSA_CMP_LEN > jj_np * NSA_SEL_BLOCK)).astype(np.float32)
    imp = jnp.einsum('btgrn,nj->btgj', p_cg, jnp.asarray(overlap))
    cur = q_pos // NSA_SEL_BLOCK
    jj = jnp.arange(nsel)[None, :]
    causal_blk = jj <= cur[:, None]
    forced = (jj == 0) | (jj == cur[:, None]) | (jj == cur[:, None] - 1)
    imp = jnp.where(forced[None, :, None, :], jnp.inf, jnp.where(causal_blk[None, :, None, :], imp, -jnp.inf))
    _, sel = lax.top_k(imp, min(NSA_TOPN, nsel))
    chosen = jnp.any(sel[..., None] == jnp.arange(nsel), axis=-2)
    selT = jnp.transpose(chosen, (0, 2, 3, 1)).astype(_F32)
    qT = heads_T(q_flat, H) * scale
    flat = lambda a: a.reshape(B, L, G * HD)
    o_sel = from_heads_T(attention_T('sel', qT, heads_K(flat(nsa_new[:, :, 2]), G), heads_VT(flat(nsa_new[:, :, 3]), G), bias_tiles, selT))
    o_win = from_heads_T(attention_T('win', qT, heads_K(flat(nsa_new[:, :, 4]), G), heads_VT(flat(nsa_new[:, :, 5]), G), bias_tiles))
    g = jax.nn.sigmoid(gates_flat.reshape(B, T, H, 3).astype(jnp.float32))
    out = g[..., 0:1] * o_cmp + g[..., 1:2] * o_sel.reshape(B, T, H, HD) + g[..., 2:3] * o_win.reshape(B, T, H, HD)
    return out.reshape(B, T, H * HD)


def token_mix(x, past, lp, tab):
    B, T, _ = x.shape
    P = 0 if past is None else past['moba'].shape[1]
    q_pos = P + jnp.arange(T, dtype=jnp.int32)
    x2 = x.reshape(B * T, D_MODEL)
    z = norm_matmul(x2, lp['norm_mix'], lp['w_in']).reshape(B, T, IN_COLS)
    qa, ka, va, q_nsa, nsa_kv, nsa_g, qc, kc, vc, qi, ki, wi, u, vg = split_last(z, IN_SPLITS)
    moba_new = jnp.stack([ka, va], axis=2).reshape(B, T, 2, N_HEADS, HD)
    nsa_new = nsa_kv.reshape(B, T, 6, NSA_KV_HEADS, HD)
    nsa_main_new = nsa_new[:, :, :4]
    win_new = nsa_new[:, :, 4:]
    dsa_new = jnp.stack([kc, vc], axis=2).reshape(B, T, 2, N_HEADS, HD)
    if past is None:
        moba_all, nsa_all, win_all, dsa_all, idx_all = moba_new, nsa_main_new, win_new, dsa_new, ki
    else:
        moba_all = jnp.concatenate([past['moba'], moba_new], axis=1)
        nsa_all = jnp.concatenate([past['nsa'], nsa_main_new], axis=1)
        win_all = jnp.concatenate([past['win'], win_new], axis=1)
        dsa_all = jnp.concatenate([past['dsa'], dsa_new], axis=1)
        idx_all = jnp.concatenate([past['idx'], ki], axis=1)
    kw_pos0 = P + T - win_all.shape[1]
    if past is None:
        scale = HD ** -0.5
        bt = lp['bias_tiles']
        qaT = heads_T(qa, N_HEADS) * scale
        y_a = from_heads_T(attention_T('moba', qaT, heads_K(ka, N_HEADS), heads_VT(va, N_HEADS), bt[0], moba_gate(qaT, heads_K(ka, N_HEADS))))
        y_b = nsa_attention_prompt(q_nsa, nsa_g, nsa_new, q_pos, lp['cmp_w1'], lp['cmp_w2'], lp['cmp_pe'], tab[:, N_HEADS:2 * N_HEADS], bt[1])
        qiT = jnp.transpose(qi.reshape(B, T, DSA_IDX_HEADS, DSA_IDX_DIM), (0, 2, 3, 1))
        keep = dsa_mask(ki, qiT, jnp.transpose(wi, (0, 2, 1)), min(DSA_TOPK, T // 4))
        y_c = from_heads_T(attention_T('dsa', heads_T(qc, N_HEADS) * scale, heads_K(kc, N_HEADS), heads_VT(vc, N_HEADS), bt[2], keep))
    else:
        y_a = moba_attention(qa.reshape(B, T, N_HEADS, HD), moba_all[:, :, 0], moba_all[:, :, 1], q_pos, tab[:, :N_HEADS])
        y_b = nsa_attention(q_nsa.reshape(B, T, N_HEADS, HD), nsa_g.reshape(B, T, N_HEADS, 3), nsa_all[:, :, 0], nsa_all[:, :, 1], nsa_all[:, :, 2], nsa_all[:, :, 3], win_all[:, :, 0], win_all[:, :, 1], kw_pos0, q_pos, lp['cmp_w1'], lp['cmp_w2'], lp['cmp_pe'], tab[:, N_HEADS:2 * N_HEADS])
        y_c = dsa_attention(qc.reshape(B, T, N_HEADS, HD), qi.reshape(B, T, DSA_IDX_HEADS, DSA_IDX_DIM), wi, dsa_all[:, :, 0], dsa_all[:, :, 1], idx_all, q_pos, tab[:, 2 * N_HEADS:])
    y_d, v_rows = gmlp_sgu(u, vg, lp['gm_ws'], lp['gm_b'], lp['gm_norm'])
    branches = jnp.stack([y_a.reshape(B, T, MIX_W), y_b.reshape(B, T, MIX_W), y_c.reshape(B, T, MIX_W), y_d], axis=2)
    proj = jnp.einsum('btnc,ncd->btnd', branches, lp['w_branch'])
    gz = norm_matmul(x2, lp['norm_mix'], lp['w_gate']).reshape(B, T, N_BRANCH, D_MODEL)
    gate = jax.nn.sigmoid(gz + lp['b_gate'])
    out = jnp.sum(gate * proj, axis=2) @ lp['w_out']
    win_keep = win_all[:, -min(NSA_WINDOW, P + T):]
    return out, (moba_new, nsa_main_new, win_keep, dsa_new, ki, v_rows)


def mix_and_mem(x, past, mkv, lp, tab):
    mix, rows = token_mix(x, past, lp, tab)
    x = x + mix
    x = x + mem_attention(rmsnorm(x, lp['norm_mem']), mkv, lp['w_mem_q'], lp['w_mem_o'])
    return x, rows


def layer_pair(xp, xs, past, mkv_p, mkv_s, lp, tab):
    xp, rp = mix_and_mem(xp, None, mkv_p, lp, tab)
    xs, rs = mix_and_mem(xs, past, mkv_s, lp, tab)
    rows = jnp.concatenate([xp.reshape(-1, D_MODEL), xs.reshape(-1, D_MODEL)], axis=0)
    rows = rows + peer_ffn_tokens(rows, lp['norm_ffn'], lp['peer_wq'], lp['peer_keys'], lp['peer_u'], lp['peer_v'])
    n_p = xp.shape[0] * xp.shape[1]
    return rows[:n_p].reshape(xp.shape), rows[n_p:].reshape(xs.shape), rp, rs


def kernel(x_prompt, x_sample, cache_moba_kv, cache_nsa_kv, cache_dsa_kv, cache_dsa_idx, state_nsa_win, cache_mem_kv, page_table, mem_prompt, rel_bias_table, w_in, nsa_cmp_w1, nsa_cmp_w2, nsa_cmp_pe, gm_ws, gm_b, gm_norm, w_branch, w_gate, b_gate, w_out, w_mem_q, w_mem_kv, w_mem_o, peer_wq, peer_keys, peer_u, peer_v, norm_mix, norm_mem, norm_ffn, norm_final):
    xp = x_prompt
    xs = x_sample
    rows_p = []
    rows_s = []
    mem_rows = []
    Bp = x_prompt.shape[0]
    bias_tiles = [rel_bias_tiles(rel_bias_table[:, n * N_HEADS:(n + 1) * N_HEADS]) for n in range(3)]
    for l in range(DEPTH):
        lp = {
            'w_in': w_in[l], 'cmp_w1': nsa_cmp_w1[l], 'cmp_w2': nsa_cmp_w2[l], 'cmp_pe': nsa_cmp_pe[l],
            'gm_ws': gm_ws[l], 'gm_b': gm_b[l], 'gm_norm': gm_norm[l],
            'w_branch': w_branch[l], 'w_gate': w_gate[l], 'b_gate': b_gate[l], 'w_out': w_out[l],
            'w_mem_q': w_mem_q[l], 'w_mem_o': w_mem_o[l],
            'peer_wq': peer_wq[l], 'peer_keys': peer_keys[l], 'peer_u': peer_u[l], 'peer_v': peer_v[l],
            'norm_mix': norm_mix[l], 'norm_mem': norm_mem[l], 'norm_ffn': norm_ffn[l],
        }
        mkv_p = (mem_prompt @ w_mem_kv[l]).reshape(Bp, MEM_TOKENS, 2, MEM_HEADS, MEM_HD)
        past = {
            'moba': gather_pages(cache_moba_kv[l], page_table),
            'nsa': gather_pages(cache_nsa_kv[l], page_table),
            'dsa': gather_pages(cache_dsa_kv[l], page_table),
            'idx': gather_pages(cache_dsa_idx[l], page_table),
            'win': state_nsa_win[l],
        }
        lp['bias_tiles'] = bias_tiles
        xp, xs, rp, rs = layer_pair(xp, xs, past, mkv_p, cache_mem_kv[l], lp, rel_bias_table)
        rows_p.append(rp)
        rows_s.append(rs)
        mem_rows.append(mkv_p)
    y_prompt = rmsnorm(xp, norm_final)
    y_sample = rmsnorm(xs, norm_final)
    new_moba_kv_prompt = jnp.stack([r[0] for r in rows_p])
    new_moba_kv_sample = jnp.stack([r[0] for r in rows_s])
    new_nsa_kv_prompt = jnp.stack([r[1] for r in rows_p])
    new_nsa_kv_sample = jnp.stack([r[1] for r in rows_s])
    new_nsa_win_prompt = jnp.stack([r[2] for r in rows_p])
    new_nsa_win_sample = jnp.stack([r[2] for r in rows_s])
    new_dsa_kv_prompt = jnp.stack([r[3] for r in rows_p])
    new_dsa_kv_sample = jnp.stack([r[3] for r in rows_s])
    new_dsa_idx_prompt = jnp.stack([r[4] for r in rows_p])
    new_dsa_idx_sample = jnp.stack([r[4] for r in rows_s])
    new_mem_kv_prompt = jnp.stack(mem_rows)
    new_gmlp_v_sample = jnp.stack([r[5] for r in rows_s])
    return (y_prompt, y_sample, new_moba_kv_prompt, new_moba_kv_sample, new_nsa_kv_prompt, new_nsa_kv_sample, new_nsa_win_prompt, new_nsa_win_sample, new_dsa_kv_prompt, new_dsa_kv_sample, new_dsa_idx_prompt, new_dsa_idx_sample, new_mem_kv_prompt, new_gmlp_v_sample)
```

```python
import functools
import math
import jax
import jax.numpy as jnp
from jax import lax
import numpy as np
from jax.experimental import pallas as pl
from jax.experimental.pallas import tpu as pltpu

D_MODEL = 2048
BATCH = 4
SEQ = 2048
DEPTH = 2
DEC_BATCH = 8
DEC_SEQ = 4
PAST_LEN = 16384
PAGE_SIZE = 128

HD = 64
N_HEADS = 8
MIX_W = N_HEADS * HD
N_BRANCH = 4
MOBA_BLOCK = 256
MOBA_TOPK = 3
MOBA_QB = 32
NSA_KV_HEADS = 2
NSA_GROUP = N_HEADS // NSA_KV_HEADS
NSA_CMP_STRIDE = 16
NSA_CMP_LEN = 2 * NSA_CMP_STRIDE
NSA_SEL_BLOCK = 64
NSA_TOPN = 16
NSA_WINDOW = 512
NSA_QB = 64
WIN_QB = 128
DSA_IDX_HEADS = 4
DSA_IDX_DIM = 64
DSA_TOPK = 256
DSA_QB = 64
GM_CHUNK = 128
GM_GROUPS = 8
GM_W = 512
REL_BUCKETS = 32
REL_MAX_DIST = 128
N_REL_HEADS = 3 * N_HEADS
MEM_TOKENS = 256
MEM_HEADS = 4
MEM_HD = 128
MEM_W = MEM_HEADS * MEM_HD
PEER_HEADS = 8
PEER_NKEYS = 128
PEER_QDIM = 128
PEER_TOPK = 16
PEER_EXPERTS = PEER_NKEYS ** 2
PEER_BLOCK = 128
NORM_EPS = 1e-6
IN_SPLITS = (MIX_W, MIX_W, MIX_W, MIX_W, 6 * NSA_KV_HEADS * HD, 3 * N_HEADS, MIX_W, MIX_W, MIX_W, DSA_IDX_HEADS * DSA_IDX_DIM, DSA_IDX_DIM, DSA_IDX_HEADS, GM_W, GM_W)
IN_COLS = sum(IN_SPLITS)


def rmsnorm(x, g):
    xf = x.astype(jnp.float32)
    y = xf * lax.rsqrt(jnp.mean(xf * xf, axis=-1, keepdims=True) + NORM_EPS)
    return (y * g.astype(jnp.float32)).astype(x.dtype)


def masked_softmax(logits, mask):
    l = jnp.where(mask, logits.astype(jnp.float32), -jnp.inf)
    m = jnp.max(l, axis=-1, keepdims=True)
    m = jnp.where(jnp.isfinite(m), m, 0.0)
    e = jnp.exp(l - m)
    return e / jnp.maximum(jnp.sum(e, axis=-1, keepdims=True), 1e-30)


def rel_bucket(dist):
    n = jnp.maximum(dist, 0)
    max_exact = REL_BUCKETS // 2
    nf = jnp.maximum(n, 1).astype(jnp.float32)
    large = max_exact + (jnp.log(nf / max_exact) / math.log(REL_MAX_DIST / max_exact) * (REL_BUCKETS - max_exact)).astype(jnp.int32)
    return jnp.where(n < max_exact, n, jnp.minimum(large, REL_BUCKETS - 1))


def rel_bias(dist, tab):
    H = tab.shape[1]
    return tab.T[jnp.arange(H)[:, None], rel_bucket(dist)].astype(jnp.float32)


def split_last(z, sizes):
    return jnp.split(z, np.cumsum(sizes)[:-1].tolist(), axis=-1)


def map_query_blocks(fn, qb, q_pos, *xs):
    T = q_pos.shape[0]
    qb = qb if T % qb == 0 else T
    nb = T // qb
    xb = tuple(jnp.moveaxis(x.reshape(x.shape[0], nb, qb, *x.shape[2:]), 1, 0) for x in xs)
    out = lax.map(lambda a: fn(*a), (q_pos.reshape(nb, qb),) + xb)
    return jnp.moveaxis(out, 0, 1).reshape(out.shape[1], T, *out.shape[3:])


def gather_pages(pool, page_table):
    g = pool[page_table]
    return g.reshape(g.shape[0], g.shape[1] * g.shape[2], *g.shape[3:])


def moba_attention(q, k, v, q_pos, tab):
    B, L, H, _ = k.shape
    nblk = -(-L // MOBA_BLOCK)
    pad = ((0, 0), (0, nblk * MOBA_BLOCK - L), (0, 0), (0, 0))
    kb = jnp.pad(k, pad).reshape(B, nblk, MOBA_BLOCK, H, HD)
    vb = jnp.pad(v, pad).reshape(B, nblk, MOBA_BLOCK, H, HD)
    k_mean = jnp.mean(kb.astype(jnp.float32), axis=2).astype(k.dtype)
    kb_t = jnp.transpose(kb, (0, 3, 1, 2, 4))
    vb_t = jnp.transpose(vb, (0, 3, 1, 2, 4))
    ntop = min(MOBA_TOPK, nblk)
    bi = jnp.arange(B)[:, None, None, None]
    hi = jnp.arange(H)[None, None, :, None]
    off = jnp.arange(MOBA_BLOCK)
    scale = HD ** -0.5

    def block_fn(qp, qq):
        qb = qp.shape[0]
        own = qp // MOBA_BLOCK
        gs = jnp.einsum('bqhd,bjhd->bqhj', qq, k_mean).astype(jnp.float32)
        fully_past = jnp.arange(nblk)[None, :] < own[:, None]
        gs = jnp.where(fully_past[None, :, None, :], gs, -jnp.inf)
        _, top = lax.top_k(gs, ntop)
        own_b = jnp.broadcast_to(own[None, :, None, None], (B, qb, H, 1))
        sel = jnp.concatenate([top, own_b], axis=-1)
        sel_ok = jnp.concatenate([top < own[None, :, None, None], jnp.ones(own_b.shape, bool)], axis=-1)
        kg = kb_t[bi, hi, sel].reshape(B, qb, H, (ntop + 1) * MOBA_BLOCK, HD)
        vg = vb_t[bi, hi, sel].reshape(B, qb, H, (ntop + 1) * MOBA_BLOCK, HD)
        pos = (sel[..., None] * MOBA_BLOCK + off).reshape(B, qb, H, -1)
        dist = qp[None, :, None, None] - pos
        ok = jnp.repeat(sel_ok, MOBA_BLOCK, axis=-1) & (dist >= 0)
        lg = jnp.einsum('bqhd,bqhkd->bqhk', qq, kg).astype(jnp.float32) * scale + rel_bias(dist, tab)
        p = masked_softmax(lg, ok)
        return jnp.einsum('bqhk,bqhkd->bqhd', p.astype(vg.dtype), vg)

    return map_query_blocks(block_fn, MOBA_QB, q_pos, q)


def nsa_attention(q, gates, kc, vc, ks, vs, kw, vw, kw_pos0, q_pos, w1, w2, pe, tab):
    B, L, G, _ = kc.shape
    T = q.shape[1]
    H = N_HEADS
    scale = HD ** -0.5
    nch = L // NSA_CMP_STRIDE
    ncmp = nch - 1

    def compress(x, i):
        c = x[:, :nch * NSA_CMP_STRIDE].reshape(B, nch, NSA_CMP_STRIDE, G, HD)
        blk = jnp.concatenate([c[:, :-1], c[:, 1:]], axis=2) + pe[i][None, None, :, None, :]
        blk = jnp.transpose(blk, (0, 1, 3, 2, 4)).reshape(B, ncmp, G, NSA_CMP_LEN * HD)
        return jax.nn.gelu(blk @ w1[i]) @ w2[i]

    kcmp = compress(kc, 0)
    vcmp = compress(vc, 1)
    cmp_end = jnp.arange(ncmp) * NSA_CMP_STRIDE + NSA_CMP_LEN - 1
    qg = q.reshape(B, T, G, NSA_GROUP, HD)
    dist_c = q_pos[:, None] - cmp_end[None, :]
    lg_c = jnp.einsum('btgrd,bngd->btgrn', qg, kcmp).astype(jnp.float32).reshape(B, T, H, ncmp) * scale
    lg_c = lg_c + rel_bias(dist_c[None, :, None, :], tab)
    p_c = masked_softmax(lg_c, (dist_c >= 0)[None, :, None, :])
    p_cg = p_c.reshape(B, T, G, NSA_GROUP, ncmp)
    o_cmp = jnp.einsum('btgrn,bngd->btgrd', p_cg.astype(vcmp.dtype), vcmp).reshape(B, T, H, HD)
    nsel = -(-L // NSA_SEL_BLOCK)
    ii = np.arange(ncmp)[:, None]
    jj_np = np.arange(nsel)[None, :]
    overlap = ((ii * NSA_CMP_STRIDE < (jj_np + 1) * NSA_SEL_BLOCK) & (ii * NSA_CMP_STRIDE + NSA_CMP_LEN > jj_np * NSA_SEL_BLOCK)).astype(np.float32)
    imp = jnp.einsum('btgrn,nj->btgj', p_cg, jnp.asarray(overlap))
    cur = q_pos // NSA_SEL_BLOCK
    jj = jnp.arange(nsel)[None, :]
    causal_blk = jj <= cur[:, None]
    forced = (jj == 0) | (jj == cur[:, None]) | (jj == cur[:, None] - 1)
    imp = jnp.where(forced[None, :, None, :], jnp.inf, jnp.where(causal_blk[None, :, None, :], imp, -jnp.inf))
    ntop = min(NSA_TOPN, nsel)
    _, sel = lax.top_k(imp, ntop)
    pad = ((0, 0), (0, nsel * NSA_SEL_BLOCK - L), (0, 0), (0, 0))
    ksb = jnp.transpose(jnp.pad(ks, pad).reshape(B, nsel, NSA_SEL_BLOCK, G, HD), (0, 3, 1, 2, 4))
    vsb = jnp.transpose(jnp.pad(vs, pad).reshape(B, nsel, NSA_SEL_BLOCK, G, HD), (0, 3, 1, 2, 4))
    bi = jnp.arange(B)[:, None, None, None]
    gi = jnp.arange(G)[None, None, :, None]
    off = jnp.arange(NSA_SEL_BLOCK)

    def sel_fn(qp, qq, ss):
        qb = qp.shape[0]
        kg = ksb[bi, gi, ss].reshape(B, qb, G, ntop * NSA_SEL_BLOCK, HD)
        vg = vsb[bi, gi, ss].reshape(B, qb, G, ntop * NSA_SEL_BLOCK, HD)
        pos = (ss[..., None] * NSA_SEL_BLOCK + off).reshape(B, qb, G, -1)
        dist = jnp.repeat(qp[None, :, None, None] - pos, NSA_GROUP, axis=2)
        lg = jnp.einsum('bqgrd,bqgkd->bqgrk', qq.reshape(B, qb, G, NSA_GROUP, HD), kg).astype(jnp.float32).reshape(B, qb, H, -1) * scale
        p = masked_softmax(lg + rel_bias(dist, tab), dist >= 0)
        o = jnp.einsum('bqgrk,bqgkd->bqgrd', p.reshape(B, qb, G, NSA_GROUP, -1).astype(vg.dtype), vg)
        return o.reshape(B, qb, H, HD)

    o_sel = map_query_blocks(sel_fn, NSA_QB, q_pos, q, sel)
    W = NSA_WINDOW
    wpad = ((0, 0), (W, 0), (0, 0), (0, 0))
    kwp = jnp.pad(kw, wpad)
    vwp = jnp.pad(vw, wpad)

    def win_fn(qp, qq):
        qb = qp.shape[0]
        span = W + qb - 1
        start = qp[0] - kw_pos0 + 1
        kk = lax.dynamic_slice_in_dim(kwp, start, span, axis=1)
        vv = lax.dynamic_slice_in_dim(vwp, start, span, axis=1)
        pos = qp[0] - W + 1 + jnp.arange(span)
        dist = qp[:, None] - pos[None, :]
        ok = (dist >= 0) & (dist < W) & (pos[None, :] >= 0)
        lg = jnp.einsum('bqgrd,bkgd->bqgrk', qq.reshape(B, qb, G, NSA_GROUP, HD), kk).astype(jnp.float32).reshape(B, qb, H, span) * scale
        p = masked_softmax(lg + rel_bias(dist[None, :, None, :], tab), ok[None, :, None, :])
        o = jnp.einsum('bqgrk,bkgd->bqgrd', p.reshape(B, qb, G, NSA_GROUP, span).astype(vv.dtype), vv)
        return o.reshape(B, qb, H, HD)

    o_win = map_query_blocks(win_fn, WIN_QB, q_pos, q)
    g = jax.nn.sigmoid(gates.astype(jnp.float32)).astype(q.dtype)
    return g[..., 0:1] * o_cmp + g[..., 1:2] * o_sel + g[..., 2:3] * o_win


def dsa_attention(q, qi, wi, k, v, ki, q_pos, tab):
    B, L, H, _ = k.shape
    kk = min(DSA_TOPK, L // 4)
    bi = jnp.arange(B)[:, None, None]
    scale = HD ** -0.5

    def block_fn(qp, qq, qqi, wwi):
        s = jax.nn.relu(jnp.einsum('bqhd,bld->bqhl', qqi, ki).astype(jnp.float32) * DSA_IDX_DIM ** -0.5)
        score = jnp.einsum('bqh,bqhl->bql', wwi.astype(jnp.float32), s)
        admissible = jnp.arange(L)[None, :] <= qp[:, None]
        score = jnp.where(admissible[None], score, -jnp.inf)
        _, sel = lax.top_k(score, kk)
        kg = k[bi, sel]
        vg = v[bi, sel]
        dist = (qp[None, :, None] - sel)[:, :, None, :]
        lg = jnp.einsum('bqhd,bqkhd->bqhk', qq, kg).astype(jnp.float32) * scale + rel_bias(dist, tab)
        p = masked_softmax(lg, dist >= 0)
        return jnp.einsum('bqhk,bqkhd->bqhd', p.astype(vg.dtype), vg)

    return map_query_blocks(block_fn, DSA_QB, q_pos, q, qi, wi)


def gmlp_sgu(u, v, ws, b, gnorm):
    B, T, _ = u.shape
    vn = rmsnorm(v, gnorm)
    nc = -(-T // GM_CHUNK)
    vc = jnp.pad(vn, ((0, 0), (0, nc * GM_CHUNK - T), (0, 0))).reshape(B, nc, GM_CHUNK, GM_GROUPS, GM_W // GM_GROUPS)
    wsm = ws * jnp.tril(jnp.ones((GM_CHUNK, GM_CHUNK), ws.dtype))
    mixed = jnp.einsum('gij,bcjgd->bcigd', wsm, vc) + b.T[None, None, :, :, None]
    mixed = mixed.reshape(B, nc * GM_CHUNK, GM_W)[:, :T]
    return u * mixed, vn


def mem_attention(h, mkv, wq, wo):
    B, T, _ = h.shape
    q = (h @ wq).reshape(B, T, MEM_HEADS, MEM_HD)
    lg = jnp.einsum('bthd,bmhd->bthm', q, mkv[:, :, 0]).astype(jnp.float32) * MEM_HD ** -0.5
    p = jax.nn.softmax(lg, axis=-1)
    o = jnp.einsum('bthm,bmhd->bthd', p.astype(mkv.dtype), mkv[:, :, 1]).reshape(B, T, MEM_W)
    return o @ wo


def peer_ffn(h, wq, keys, U, V):
    B, T, D = h.shape
    N = B * T
    nb = -(-N // PEER_BLOCK)
    hf = jnp.pad(h.reshape(N, D), ((0, nb * PEER_BLOCK - N), (0, 0))).reshape(nb, PEER_BLOCK, D)

    def block_fn(x):
        q = (x @ wq).reshape(PEER_BLOCK, PEER_HEADS, 2, PEER_QDIM // 2)
        s = jnp.einsum('nhcd,hckd->nhck', q, keys).astype(jnp.float32)
        v1, i1 = lax.top_k(s[:, :, 0], PEER_TOPK)
        v2, i2 = lax.top_k(s[:, :, 1], PEER_TOPK)
        cand = (v1[..., :, None] + v2[..., None, :]).reshape(PEER_BLOCK, PEER_HEADS, PEER_TOPK * PEER_TOPK)
        sv, ci = lax.top_k(cand, PEER_TOPK)
        e = jnp.take_along_axis(i1, ci // PEER_TOPK, axis=-1) * PEER_NKEYS + jnp.take_along_axis(i2, ci % PEER_TOPK, axis=-1)
        g = jax.nn.softmax(sv, axis=-1)
        ug = U[e]
        vg = V[e]
        act = jax.nn.gelu(jnp.einsum('nd,nhkd->nhk', x, ug).astype(jnp.float32))
        return jnp.einsum('nhk,nhkd->nd', (g * act).astype(x.dtype), vg)

    out = lax.map(block_fn, hf).reshape(nb * PEER_BLOCK, D)[:N]
    return out.reshape(B, T, D)


def _norm_matmul_kernel(x_ref, g_ref, w_ref, o_ref, xn_ref):
    @pl.when(pl.program_id(1) == 0)
    def _():
        x = x_ref[...]
        y = x * lax.rsqrt(jnp.mean(x * x, axis=-1, keepdims=True) + NORM_EPS)
        xn_ref[...] = (y * g_ref[...]).astype(jnp.bfloat16)

    o_ref[...] = jnp.dot(xn_ref[...], w_ref[...].astype(jnp.bfloat16), preferred_element_type=jnp.float32)


def norm_matmul(x, g, w, tm=512, tn=512):
    N, D = x.shape
    C = w.shape[1]
    tm = min(tm, N)
    tn = min(tn, C)
    return pl.pallas_call(
        _norm_matmul_kernel,
        grid=(pl.cdiv(N, tm), pl.cdiv(C, tn)),
        in_specs=[
            pl.BlockSpec((tm, D), lambda i, j: (i, 0)),
            pl.BlockSpec((1, D), lambda i, j: (0, 0)),
            pl.BlockSpec((D, tn), lambda i, j: (0, j)),
        ],
        out_specs=pl.BlockSpec((tm, tn), lambda i, j: (i, j)),
        out_shape=jax.ShapeDtypeStruct((N, C), jnp.float32),
        scratch_shapes=[pltpu.VMEM((tm, D), jnp.bfloat16)],
        compiler_params=pltpu.CompilerParams(
            dimension_semantics=("arbitrary", "arbitrary"), vmem_limit_bytes=48 * 1024 * 1024),
        name="norm_matmul",
    )(x, g.reshape(1, D), w)


TQ = 256
TK = 256
NEG_BIG = -1e30
VMEM_LIMIT = 48 * 1024 * 1024
_BF16 = jnp.bfloat16
_F32 = jnp.float32


def _attn_kernel(mode, qT_ref, k_ref, vT_ref, bias_ref, *rest):
    if mode == 'win':
        (o_ref,) = rest
        m_ref = None
    else:
        m_ref, o_ref = rest
    qi = pl.program_id(2)
    q = qT_ref[0, 0].astype(_BF16)
    lane = lax.broadcasted_iota(jnp.int32, (TK, TQ), 1)
    sub = lax.broadcasted_iota(jnp.int32, (TK, TQ), 0)
    rel0 = lane - sub

    def body(kj, carry):
        m, l, acc = carry
        k_t = k_ref[0, 0, pl.ds(pl.multiple_of(kj * TK, TK), TK), :].astype(_BF16)
        s = jnp.dot(k_t, q, preferred_element_type=_F32)
        s = s + bias_ref[0, jnp.minimum(qi - kj, 2)]
        dist = rel0 + (qi - kj) * TQ
        ok = dist >= 0
        if mode == 'moba':
            ok = ok & (m_ref[0, 0, kj] > 0.5)
        elif mode == 'sel':
            rows = m_ref[0, 0, kj]
            blk = sub // NSA_SEL_BLOCK
            r = jnp.where(blk == 0, rows[0:1], jnp.where(blk == 1, rows[1:2], jnp.where(blk == 2, rows[2:3], rows[3:4])))
            ok = ok & (r > 0.5)
        elif mode == 'dsa':
            ok = ok & (m_ref[0, pl.ds(pl.multiple_of(kj * TK, TK), TK), :].astype(_F32) > 0.5)
        elif mode == 'win':
            ok = ok & (dist < NSA_WINDOW)
        s = jnp.where(ok, s, NEG_BIG)
        m_new = jnp.maximum(m, jnp.max(s, axis=0, keepdims=True))
        p = jnp.where(ok, jnp.exp(s - m_new), 0.0)
        alpha = jnp.exp(m - m_new)
        l = alpha * l + jnp.sum(p, axis=0, keepdims=True)
        v_t = vT_ref[0, 0, kj].astype(_BF16)
        acc = alpha * acc + jnp.dot(v_t, p.astype(_BF16), preferred_element_type=_F32)
        return m_new, l, acc

    lo = jnp.maximum(qi - (NSA_WINDOW // TK), 0) if mode == 'win' else 0
    init = (jnp.full((1, TQ), NEG_BIG, _F32), jnp.zeros((1, TQ), _F32), jnp.zeros((HD, TQ), _F32))
    m, l, acc = lax.fori_loop(lo, qi + 1, body, init)
    o_ref[0, 0] = acc / jnp.maximum(l, 1e-30)


def attention_T(mode, qT, k, vT, bias_tiles, mask=None):
    B, H, _, T = qT.shape
    Hkv, L = k.shape[1], k.shape[2]
    grp = H // Hkv
    in_specs = [
        pl.BlockSpec((1, 1, HD, TQ), lambda b, h, i: (b, h, 0, i)),
        pl.BlockSpec((1, 1, L, HD), lambda b, h, i: (b, h // grp, 0, 0)),
        pl.BlockSpec((1, 1, L // TK, HD, TK), lambda b, h, i: (b, h // grp, 0, 0, 0)),
        pl.BlockSpec((1, 3, TK, TQ), lambda b, h, i: (h, 0, 0, 0)),
    ]
    args = [qT, k, vT, bias_tiles]
    if mode == 'moba':
        in_specs.append(pl.BlockSpec((1, 1, L // TK, 1, TQ), lambda b, h, i: (b, h, 0, 0, i)))
        args.append(mask.reshape(B, H, L // TK, 1, T))
    elif mode == 'sel':
        per = TK // NSA_SEL_BLOCK
        in_specs.append(pl.BlockSpec((1, 1, L // TK, per, TQ), lambda b, h, i: (b, h // grp, 0, 0, i)))
        args.append(mask.reshape(B, Hkv, L // TK, per, T))
    elif mode == 'dsa':
        in_specs.append(pl.BlockSpec((1, L, TQ), lambda b, h, i: (b, 0, i)))
        args.append(mask)
    return pl.pallas_call(
        functools.partial(_attn_kernel, mode),
        grid=(B, H, T // TQ),
        in_specs=in_specs,
        out_specs=pl.BlockSpec((1, 1, HD, TQ), lambda b, h, i: (b, h, 0, i)),
        out_shape=jax.ShapeDtypeStruct((B, H, HD, T), _F32),
        compiler_params=pltpu.CompilerParams(
            dimension_semantics=("arbitrary", "arbitrary", "arbitrary"), vmem_limit_bytes=VMEM_LIMIT),
        name="attn_" + mode,
    )(*args)


def _moba_gate_kernel(qT_ref, k_ref, sel_ref):
    nblk, T = sel_ref.shape[2], sel_ref.shape[3]
    q = qT_ref[0, 0].astype(_BF16)
    k = k_ref[0, 0]
    km = jnp.mean(k.reshape(nblk, MOBA_BLOCK, HD), axis=1)
    gs = jnp.dot(km.astype(_BF16), q, preferred_element_type=_F32)
    j = lax.broadcasted_iota(jnp.int32, (nblk, T), 0)
    own = lax.broadcasted_iota(jnp.int32, (nblk, T), 1) // MOBA_BLOCK
    fully_past = j < own
    gs = jnp.where(fully_past, gs, -jnp.inf)
    rank = jnp.zeros((nblk, T), jnp.int32)
    for jp in range(nblk):
        row = gs[jp:jp + 1, :]
        ahead = (row > gs) | ((row == gs) & (jp < j))
        rank = rank + ahead.astype(jnp.int32)
    sel = (fully_past & (rank < MOBA_TOPK)) | (j == own)
    sel_ref[0, 0] = sel.astype(_F32)


def moba_gate(qT, k):
    B, H, _, T = qT.shape
    L = k.shape[2]
    nblk = L // MOBA_BLOCK
    return pl.pallas_call(
        _moba_gate_kernel,
        grid=(B, H),
        in_specs=[pl.BlockSpec((1, 1, HD, T), lambda b, h: (b, h, 0, 0)),
                  pl.BlockSpec((1, 1, L, HD), lambda b, h: (b, h, 0, 0))],
        out_specs=pl.BlockSpec((1, 1, nblk, T), lambda b, h: (b, h, 0, 0)),
        out_shape=jax.ShapeDtypeStruct((B, H, nblk, T), _F32),
        compiler_params=pltpu.CompilerParams(dimension_semantics=("arbitrary", "arbitrary"), vmem_limit_bytes=VMEM_LIMIT),
        name="moba_gate",
    )(qT, k)


def _count_rows(pred):
    return jnp.sum(pred.astype(_F32), axis=0, keepdims=True)


def _dsa_mask_kernel(topk, ki_ref, qiT_ref, wiT_ref, mask_ref, key_ref):
    L = ki_ref.shape[1]
    qi = pl.program_id(1)
    ki = ki_ref[0].astype(_BF16)
    score = jnp.zeros((L, TQ), _F32)
    for hh in range(DSA_IDX_HEADS):
        s = jnp.dot(ki, qiT_ref[0, hh].astype(_BF16), preferred_element_type=_F32) * DSA_IDX_DIM ** -0.5
        score = score + wiT_ref[0, hh:hh + 1, :] * jnp.maximum(s, 0.0)
    idx = lax.broadcasted_iota(jnp.int32, (L, TQ), 0)
    q_pos = qi * TQ + lax.broadcasted_iota(jnp.int32, (L, TQ), 1)
    adm = idx <= q_pos
    score = jnp.where(adm, jnp.where(score == 0.0, 0.0, score), -jnp.inf)
    bits = pltpu.bitcast(score, jnp.int32)
    key_ref[...] = jnp.where(bits < 0, bits ^ jnp.int32(0x7FFFFFFF), bits)
    kf = jnp.float32(topk)
    int_min = jnp.int32(-2 ** 31)
    lo = jnp.where(_count_rows(key_ref[...] >= 0) >= kf, jnp.int32(0), int_min)

    def vstep(i, lo):
        cand = lo + jnp.left_shift(jnp.int32(1), 30 - i)
        return jnp.where(_count_rows(key_ref[...] >= cand) >= kf, cand, lo)

    thr = lax.fori_loop(0, 31, vstep, lo)
    need = kf - _count_rows(key_ref[...] > thr)
    nbits = max(1, (L - 1).bit_length())

    def istep(i, lo):
        cand = lo + jnp.left_shift(jnp.int32(1), nbits - 1 - i)
        c = _count_rows((key_ref[...] == thr) & (idx < cand))
        return jnp.where(c < need, cand, lo)

    last_tie = lax.fori_loop(0, nbits, istep, jnp.zeros((1, TQ), jnp.int32))
    key = key_ref[...]
    keep = ((key > thr) | ((key == thr) & (idx <= last_tie))) & adm
    mask_ref[0] = keep.astype(mask_ref.dtype)


def dsa_mask(ki, qiT, wiT, topk):
    B, L, _ = ki.shape
    T = qiT.shape[3]
    return pl.pallas_call(
        functools.partial(_dsa_mask_kernel, topk),
        grid=(B, T // TQ),
        in_specs=[pl.BlockSpec((1, L, DSA_IDX_DIM), lambda b, i: (b, 0, 0)),
                  pl.BlockSpec((1, DSA_IDX_HEADS, DSA_IDX_DIM, TQ), lambda b, i: (b, 0, 0, i)),
                  pl.BlockSpec((1, DSA_IDX_HEADS, TQ), lambda b, i: (b, 0, i))],
        out_specs=pl.BlockSpec((1, L, TQ), lambda b, i: (b, 0, i)),
        out_shape=jax.ShapeDtypeStruct((B, L, T), _BF16),
        scratch_shapes=[pltpu.VMEM((L, TQ), jnp.int32)],
        compiler_params=pltpu.CompilerParams(dimension_semantics=("arbitrary", "arbitrary"), vmem_limit_bytes=VMEM_LIMIT),
        name="dsa_mask",
    )(ki, qiT, wiT)


PEER_TN = 256
PEER_EB = 1024
PEER_IB = PEER_EB // PEER_NKEYS
RANK_OUT = 4096.0


def _extract_top(work, n):
    rows = work.shape[0]
    iota = lax.broadcasted_iota(jnp.int32, work.shape, 0)
    order = jnp.full(work.shape, RANK_OUT, _F32)
    vals = []
    idx = None
    for a in range(n):
        m = jnp.max(work, axis=0, keepdims=True)
        idx = jnp.min(jnp.where(work == m, iota, rows), axis=0, keepdims=True)
        hit = iota == idx
        vals.append(m)
        order = jnp.where(hit, jnp.float32(a), order)
        work = jnp.where(hit, -jnp.inf, work)
    return vals, order, idx


def _peer_select_kernel(xT_ref, g_ref, wqT_ref, keys_ref, hxT_ref, s1m_ref, s2m_ref, e1n_ref, e2_ref, r1_ref, r2_ref, tau_ref, phi_ref):
    x = xT_ref[...]
    y = x * lax.rsqrt(jnp.mean(x * x, axis=0, keepdims=True) + NORM_EPS) * g_ref[...]
    hx = y.astype(_BF16)
    hxT_ref[...] = hx
    qT = jnp.dot(wqT_ref[...], hx, preferred_element_type=_F32).astype(_BF16)
    half = PEER_QDIM // 2
    for h in range(PEER_HEADS):
        sc = []
        for c in range(2):
            r0 = (h * 2 + c) * half
            sc.append(jnp.dot(keys_ref[h * 2 + c], qT[r0:r0 + half, :], preferred_element_type=_F32))
        v1, o1, _ = _extract_top(sc[0], PEER_TOPK)
        v2, o2, _ = _extract_top(sc[1], PEER_TOPK)
        v2s = jnp.concatenate(v2, axis=0)
        cand = jnp.concatenate([v1[a] + v2s for a in range(PEER_TOPK)], axis=0)
        ex1 = [jnp.exp(v1[a] - v1[0]) for a in range(PEER_TOPK)]
        ex2s = jnp.exp(v2s - v2[0])
        ecand = jnp.concatenate([ex1[a] * ex2s for a in range(PEER_TOPK)], axis=0)
        cv, corder, clast = _extract_top(cand, PEER_TOPK)
        z = jnp.sum(jnp.where(corder < RANK_OUT, ecand, 0.0), axis=0, keepdims=True)
        s1m_ref[h] = jnp.where(o1 < RANK_OUT, sc[0], -jnp.inf)
        s2m_ref[h] = jnp.where(o2 < RANK_OUT, sc[1], -jnp.inf)
        e1n_ref[h] = jnp.exp(sc[0] - v1[0]) / z
        e2_ref[h] = jnp.exp(sc[1] - v2[0])
        r1_ref[h] = o1 * PEER_TOPK
        r2_ref[h] = o2
        tau_ref[h:h + 1, :] = cv[PEER_TOPK - 1]
        phi_ref[h:h + 1, :] = clast.astype(_F32)


def peer_select(xT, g, wqT, keys2):
    D, Np = xT.shape
    tn = PEER_TN
    row = lambda: pl.BlockSpec((PEER_HEADS, PEER_NKEYS, tn), lambda t: (0, 0, t))
    rshape = jax.ShapeDtypeStruct((PEER_HEADS, PEER_NKEYS, Np), _F32)
    return pl.pallas_call(
        _peer_select_kernel,
        grid=(Np // tn,),
        in_specs=[pl.BlockSpec((D, tn), lambda t: (0, t)),
                  pl.BlockSpec((D, 1), lambda t: (0, 0)),
                  pl.BlockSpec((PEER_HEADS * PEER_QDIM, D), lambda t: (0, 0)),
                  pl.BlockSpec((2 * PEER_HEADS, PEER_NKEYS, PEER_QDIM // 2), lambda t: (0, 0, 0))],
        out_specs=[pl.BlockSpec((D, tn), lambda t: (0, t)), row(), row(), row(), row(), row(), row(),
                   pl.BlockSpec((PEER_HEADS, tn), lambda t: (0, t)), pl.BlockSpec((PEER_HEADS, tn), lambda t: (0, t))],
        out_shape=[jax.ShapeDtypeStruct((D, Np), _BF16), rshape, rshape, rshape, rshape, rshape, rshape,
                   jax.ShapeDtypeStruct((PEER_HEADS, Np), _F32), jax.ShapeDtypeStruct((PEER_HEADS, Np), _F32)],
        compiler_params=pltpu.CompilerParams(dimension_semantics=("arbitrary",), vmem_limit_bytes=VMEM_LIMIT),
        name="peer_select",
    )(xT, g, wqT, keys2)


def _gelu_tanh(x):
    return 0.5 * x * (1.0 + jnp.tanh(math.sqrt(2.0 / math.pi) * (x + 0.044715 * (x * x * x))))


def _peer_dense_kernel(hxT_ref, u_ref, vT_ref, s1m_ref, e1n_ref, r1_ref, s2m_ref, e2_ref, r2_ref, tau_ref, phi_ref, o_ref, p_ref):
    @pl.when(pl.program_id(1) == 0)
    def _():
        o_ref[...] = jnp.zeros_like(o_ref)

    act = _gelu_tanh(jnp.dot(u_ref[...], hxT_ref[...], preferred_element_type=_F32))
    for ib in range(PEER_IB):
        w = jnp.zeros((PEER_NKEYS, o_ref.shape[1]), _F32)
        for h in range(PEER_HEADS):
            tau = tau_ref[h:h + 1, :]
            s = s1m_ref[h, ib:ib + 1, :] + s2m_ref[h]
            flat = r1_ref[h, ib:ib + 1, :] + r2_ref[h]
            sel = (s > tau) | ((s == tau) & (flat <= phi_ref[h:h + 1, :]))
            w = w + jnp.where(sel, e1n_ref[h, ib:ib + 1, :] * e2_ref[h], 0.0)
        p_ref[ib * PEER_NKEYS:(ib + 1) * PEER_NKEYS, :] = (w * act[ib * PEER_NKEYS:(ib + 1) * PEER_NKEYS, :]).astype(_BF16)
    o_ref[...] += jnp.dot(vT_ref[...], p_ref[...], preferred_element_type=_F32)


def peer_dense(hxT, u, vT, s1m, s2m, e1n, e2, r1, r2, tau, phi):
    D, Np = hxT.shape
    E = u.shape[0]
    tn = PEER_TN
    rowi = lambda: pl.BlockSpec((PEER_HEADS, PEER_IB, tn), lambda t, e: (0, e, t))
    rowj = lambda: pl.BlockSpec((PEER_HEADS, PEER_NKEYS, tn), lambda t, e: (0, 0, t))
    hd = lambda: pl.BlockSpec((PEER_HEADS, tn), lambda t, e: (0, t))
    return pl.pallas_call(
        _peer_dense_kernel,
        grid=(Np // tn, E // PEER_EB),
        in_specs=[pl.BlockSpec((D, tn), lambda t, e: (0, t)),
                  pl.BlockSpec((PEER_EB, D), lambda t, e: (e, 0)),
                  pl.BlockSpec((D, PEER_EB), lambda t, e: (0, e)),
                  rowi(), rowi(), rowi(), rowj(), rowj(), rowj(), hd(), hd()],
        out_specs=pl.BlockSpec((D, tn), lambda t, e: (0, t)),
        out_shape=jax.ShapeDtypeStruct((D, Np), _F32),
        scratch_shapes=[pltpu.VMEM((PEER_EB, tn), _BF16)],
        compiler_params=pltpu.CompilerParams(dimension_semantics=("arbitrary", "arbitrary"), vmem_limit_bytes=VMEM_LIMIT),
        name="peer_dense",
    )(hxT, u, vT, s1m, e1n, r1, s2m, e2, r2, tau, phi)


def peer_ffn_tokens(x_tokens, g, wq, keys, U, V):
    N, D = x_tokens.shape
    Np = -(-N // PEER_TN) * PEER_TN
    xT = jnp.pad(x_tokens, ((0, Np - N), (0, 0))).T
    outs = peer_select(xT, g.reshape(D, 1), wq.T.astype(_BF16), keys.reshape(2 * PEER_HEADS, PEER_NKEYS, PEER_QDIM // 2).astype(_BF16))
    hxT, s1m, s2m, e1n, e2, r1, r2, tau, phi = outs
    outT = peer_dense(hxT, U.astype(_BF16), V.T.astype(_BF16), s1m, s2m, e1n, e2, r1, r2, tau, phi)
    return outT.T[:N]


def bias_lookup(dist, tab):
    onehot = (rel_bucket(dist)[..., None] == jnp.arange(REL_BUCKETS)).astype(_F32)
    return jnp.einsum('...b,bh->h...', onehot, tab.astype(_F32), precision=lax.Precision.HIGHEST)


def rel_bias_tiles(tab):
    kk = jnp.arange(TK)[:, None]
    qq = jnp.arange(TQ)[None, :]
    dist = jnp.stack([d * TQ + qq - kk for d in range(3)])
    return bias_lookup(dist, tab)


ROWS_T = 8
DEC_ROWS = N_HEADS * ROWS_T
NEAR_TILES = 3


def _moba_decode_gate_kernel(pt_ref, q_ref, cache_ref, gs_ref, ksum_ref):
    p = pl.program_id(1)
    per = MOBA_BLOCK // PAGE_SIZE

    @pl.when(p == 0)
    def _():
        gs_ref[...] = jnp.zeros_like(gs_ref)

    @pl.when(p % per == 0)
    def _():
        ksum_ref[...] = jnp.zeros_like(ksum_ref)

    ksum_ref[...] += jnp.sum(cache_ref[0], axis=0, keepdims=True)

    @pl.when(p % per == per - 1)
    def _():
        km = (ksum_ref[...] * (1.0 / MOBA_BLOCK)).astype(_BF16).astype(_F32)
        col = jnp.sum(q_ref[0].astype(_F32) * km, axis=1, keepdims=True)
        lane = lax.broadcasted_iota(jnp.int32, gs_ref.shape[1:], 1)
        gs_ref[0] = jnp.where(lane == p // per, col, gs_ref[0])


def moba_decode_gate(pt, q_rows, cache_rows):
    B, n_pages = pt.shape
    W = N_HEADS * HD
    return pl.pallas_call(
        _moba_decode_gate_kernel,
        grid_spec=pltpu.PrefetchScalarGridSpec(
            num_scalar_prefetch=1,
            grid=(B, n_pages),
            in_specs=[pl.BlockSpec((1, DEC_ROWS, W), lambda b, p, pt: (b, 0, 0)),
                      pl.BlockSpec((1, PAGE_SIZE, W), lambda b, p, pt: (pt[b, p], 0, 0))],
            out_specs=pl.BlockSpec((1, DEC_ROWS, 128), lambda b, p, pt: (b, 0, 0)),
            scratch_shapes=[pltpu.VMEM((1, W), _F32)]),
        out_shape=jax.ShapeDtypeStruct((B, DEC_ROWS, 128), _F32),
        compiler_params=pltpu.CompilerParams(dimension_semantics=("arbitrary", "arbitrary"), vmem_limit_bytes=VMEM_LIMIT),
        name="moba_decode_gate",
    )(pt, q_rows, cache_rows)


def _paged_attn_kernel(mode, pt_ref, q_ref, cache_ref, new_ref, far_ref, near_ref, m_in_ref, o_ref, m_sc, l_sc, acc_sc, sel_sc):
    p = pl.program_id(1)
    n_pages = pl.num_programs(1) - 1
    W = N_HEADS * HD
    lane = lax.broadcasted_iota(jnp.int32, (DEC_ROWS, PAGE_SIZE), 1)
    row_t = lax.broadcasted_iota(jnp.int32, (DEC_ROWS, PAGE_SIZE), 0) % ROWS_T

    @pl.when(p == 0)
    def _():
        m_sc[...] = jnp.full_like(m_sc, NEG_BIG)
        l_sc[...] = jnp.zeros_like(l_sc)
        acc_sc[...] = jnp.zeros_like(acc_sc)
        if mode == 'moba':
            nblk = n_pages // (MOBA_BLOCK // PAGE_SIZE)
            valid = lane < nblk
            gs = jnp.where(valid, m_in_ref[0], -jnp.inf)
            rank = jnp.zeros(gs.shape, jnp.int32)
            for jp in range(PAST_LEN // MOBA_BLOCK):
                col = gs[:, jp:jp + 1]
                rank = rank + ((col > gs) | ((col == gs) & (jp < lane))).astype(jnp.int32)
            sel_sc[...] = (valid & (rank < MOBA_TOPK)).astype(_F32)

    is_new = p == n_pages
    blk = jnp.where(is_new, new_ref[0], cache_ref[0])
    k = blk[:, :W].astype(_BF16)
    v = blk[:, W:].astype(_BF16)
    s = lax.dot_general(q_ref[0], k, (((1,), (1,)), ((), ())), preferred_element_type=_F32)
    near = near_ref[jnp.clip(p - (n_pages + 1 - NEAR_TILES), 0, NEAR_TILES - 1)]
    s = s + jnp.where(p >= n_pages + 1 - NEAR_TILES, near, far_ref[...])
    if mode == 'moba':
        in_blk = jnp.sum(jnp.where(lane == p // (MOBA_BLOCK // PAGE_SIZE), sel_sc[...], 0.0), axis=1, keepdims=True)
        ok = jnp.where(is_new, (lane <= row_t).astype(_F32), in_blk) > 0.5
    else:
        keep = m_in_ref[0, 0]
        ok = jnp.concatenate([keep] * N_HEADS, axis=0) > 0.5
    s = jnp.where(ok, s, NEG_BIG)
    m_new = jnp.maximum(m_sc[...], jnp.max(s, axis=1, keepdims=True))
    pr = jnp.where(ok, jnp.exp(s - m_new), 0.0)
    alpha = jnp.exp(m_sc[...] - m_new)
    l_sc[...] = alpha * l_sc[...] + jnp.sum(pr, axis=1, keepdims=True)
    acc_sc[...] = alpha * acc_sc[...] + jnp.dot(pr.astype(_BF16), v, preferred_element_type=_F32)
    m_sc[...] = m_new

    @pl.when(is_new)
    def _():
        o_ref[0] = acc_sc[...] / jnp.maximum(l_sc[...], 1e-30)


def paged_attention(mode, pt, q_rows, cache_rows, new_rows, far, near, m_in):
    B, n_pages = pt.shape
    W = N_HEADS * HD
    clamp = lambda b, p, pt: (pt[b, jnp.minimum(p, n_pages - 1)], 0, 0)
    if mode == 'moba':
        m_spec = pl.BlockSpec((1, DEC_ROWS, 128), lambda b, p, pt: (b, 0, 0))
    else:
        m_spec = pl.BlockSpec((1, 1, ROWS_T, PAGE_SIZE), lambda b, p, pt: (b, p, 0, 0))
    return pl.pallas_call(
        functools.partial(_paged_attn_kernel, mode),
        grid_spec=pltpu.PrefetchScalarGridSpec(
            num_scalar_prefetch=1,
            grid=(B, n_pages + 1),
            in_specs=[pl.BlockSpec((1, DEC_ROWS, W), lambda b, p, pt: (b, 0, 0)),
                      pl.BlockSpec((1, PAGE_SIZE, 2 * W), clamp),
                      pl.BlockSpec((1, PAGE_SIZE, 2 * W), lambda b, p, pt: (b, 0, 0)),
                      pl.BlockSpec((DEC_ROWS, 1), lambda b, p, pt: (0, 0)),
                      pl.BlockSpec((NEAR_TILES, DEC_ROWS, PAGE_SIZE), lambda b, p, pt: (0, 0, 0)),
                      m_spec],
            out_specs=pl.BlockSpec((1, DEC_ROWS, W), lambda b, p, pt: (b, 0, 0)),
            scratch_shapes=[pltpu.VMEM((DEC_ROWS, 1), _F32), pltpu.VMEM((DEC_ROWS, 1), _F32),
                            pltpu.VMEM((DEC_ROWS, W), _F32), pltpu.VMEM((DEC_ROWS, 128), _F32)]),
        out_shape=jax.ShapeDtypeStruct((B, DEC_ROWS, W), _F32),
        compiler_params=pltpu.CompilerParams(dimension_semantics=("arbitrary", "arbitrary"), vmem_limit_bytes=VMEM_LIMIT),
        name="paged_attn_" + mode,
    )(pt, q_rows, cache_rows, new_rows, far, near, m_in)


def _dsa_decode_score_kernel(pt_ref, qi_ref, wi_ref, cache_ref, new_ref, score_ref):
    p = pl.program_id(1)
    n_pages = pl.num_programs(1) - 1
    ki = jnp.where(p == n_pages, new_ref[0], cache_ref[0]).astype(_BF16)
    acc = jnp.zeros((ROWS_T, PAGE_SIZE), _F32)
    for hh in range(DSA_IDX_HEADS):
        s = lax.dot_general(qi_ref[0, hh], ki, (((1,), (1,)), ((), ())), preferred_element_type=_F32) * DSA_IDX_DIM ** -0.5
        acc = acc + wi_ref[0, hh] * jnp.maximum(s, 0.0)
    score_ref[0] = acc


def dsa_decode_scores(pt, qi_rows, wi_rows, cache_rows, new_rows):
    B, n_pages = pt.shape
    return pl.pallas_call(
        _dsa_decode_score_kernel,
        grid_spec=pltpu.PrefetchScalarGridSpec(
            num_scalar_prefetch=1,
            grid=(B, n_pages + 1),
            in_specs=[pl.BlockSpec((1, DSA_IDX_HEADS, ROWS_T, DSA_IDX_DIM), lambda b, p, pt: (b, 0, 0, 0)),
                      pl.BlockSpec((1, DSA_IDX_HEADS, ROWS_T, 1), lambda b, p, pt: (b, 0, 0, 0)),
                      pl.BlockSpec((1, PAGE_SIZE, DSA_IDX_DIM), lambda b, p, pt: (pt[b, jnp.minimum(p, n_pages - 1)], 0, 0)),
                      pl.BlockSpec((1, PAGE_SIZE, DSA_IDX_DIM), lambda b, p, pt: (b, 0, 0))],
            out_specs=pl.BlockSpec((1, ROWS_T, PAGE_SIZE), lambda b, p, pt: (b, 0, p))),
        out_shape=jax.ShapeDtypeStruct((B, ROWS_T, (n_pages + 1) * PAGE_SIZE), _F32),
        compiler_params=pltpu.CompilerParams(dimension_semantics=("arbitrary", "arbitrary"), vmem_limit_bytes=VMEM_LIMIT),
        name="dsa_decode_scores",
    )(pt, qi_rows, wi_rows, cache_rows, new_rows)


def _count_lanes(pred):
    return jnp.sum(pred.astype(_F32), axis=1, keepdims=True)


def _dsa_decode_select_kernel(topk, past_len, score_ref, keep_ref, key_ref):
    Lp = score_ref.shape[2]
    idx = lax.broadcasted_iota(jnp.int32, (ROWS_T, Lp), 1)
    q_pos = past_len + lax.broadcasted_iota(jnp.int32, (ROWS_T, Lp), 0)
    adm = idx <= q_pos
    score = score_ref[0]
    score = jnp.where(adm, jnp.where(score == 0.0, 0.0, score), -jnp.inf)
    bits = pltpu.bitcast(score, jnp.int32)
    key_ref[...] = jnp.where(bits < 0, bits ^ jnp.int32(0x7FFFFFFF), bits)
    kf = jnp.float32(topk)
    lo = jnp.where(_count_lanes(key_ref[...] >= 0) >= kf, jnp.int32(0), jnp.int32(-2 ** 31))

    def vstep(i, lo):
        cand = lo + jnp.left_shift(jnp.int32(1), 30 - i)
        return jnp.where(_count_lanes(key_ref[...] >= cand) >= kf, cand, lo)

    thr = lax.fori_loop(0, 31, vstep, lo)
    need = kf - _count_lanes(key_ref[...] > thr)
    nbits = max(1, (Lp - 1).bit_length())

    def istep(i, lo):
        cand = lo + jnp.left_shift(jnp.int32(1), nbits - 1 - i)
        c = _count_lanes((key_ref[...] == thr) & (idx < cand))
        return jnp.where(c < need, cand, lo)

    last_tie = lax.fori_loop(0, nbits, istep, jnp.zeros((ROWS_T, 1), jnp.int32))
    key = key_ref[...]
    keep = ((key > thr) | ((key == thr) & (idx <= last_tie))) & adm
    keep_ref[0] = keep.astype(_F32)


def dsa_decode_select(score, topk, past_len):
    B, _, Lp = score.shape
    return pl.pallas_call(
        functools.partial(_dsa_decode_select_kernel, topk, past_len),
        grid=(B,),
        in_specs=[pl.BlockSpec((1, ROWS_T, Lp), lambda b: (b, 0, 0))],
        out_specs=pl.BlockSpec((1, ROWS_T, Lp), lambda b: (b, 0, 0)),
        out_shape=jax.ShapeDtypeStruct((B, ROWS_T, Lp), _F32),
        scratch_shapes=[pltpu.VMEM((ROWS_T, Lp), jnp.int32)],
        compiler_params=pltpu.CompilerParams(dimension_semantics=("arbitrary",), vmem_limit_bytes=VMEM_LIMIT),
        name="dsa_decode_select",
    )(score)


def decode_rows(q):
    B, T, H, _ = q.shape
    qp = jnp.pad(jnp.transpose(q, (0, 2, 1, 3)), ((0, 0), (0, 0), (0, ROWS_T - T), (0, 0)))
    rows = qp[:, :, :, None, :] * jnp.eye(H, dtype=q.dtype)[None, :, None, :, None]
    return rows.reshape(B, H * ROWS_T, H * HD)


def from_decode_rows(o, T):
    B = o.shape[0]
    o5 = o.reshape(B, N_HEADS, ROWS_T, N_HEADS, HD)
    own = jnp.sum(o5 * jnp.eye(N_HEADS, dtype=o.dtype)[None, :, None, :, None], axis=3)
    return jnp.transpose(own[:, :, :T], (0, 2, 1, 3)).reshape(B, T, N_HEADS * HD)


def decode_bias(tab, past_len, n_pages):
    t = jnp.arange(ROWS_T)[None, :, None]
    tile = (n_pages + 1 - NEAR_TILES + jnp.arange(NEAR_TILES))[:, None, None]
    dist = past_len + t - (tile * PAGE_SIZE + jnp.arange(PAGE_SIZE)[None, None, :])
    near = jnp.transpose(bias_lookup(dist, tab), (1, 0, 2, 3)).reshape(NEAR_TILES, DEC_ROWS, PAGE_SIZE)
    far = jnp.repeat(tab[REL_BUCKETS - 1].astype(_F32), ROWS_T).reshape(DEC_ROWS, 1)
    return far, near


def pad_rows(a, n):
    return jnp.pad(a, ((0, 0), (0, n - a.shape[1])) + ((0, 0),) * (a.ndim - 2))


def heads_T(z, nh):
    B, T, _ = z.shape
    return jnp.transpose(z.reshape(B, T, nh, HD), (0, 2, 3, 1))


def heads_K(z, nh):
    B, L, _ = z.shape
    return jnp.transpose(z.reshape(B, L, nh, HD), (0, 2, 1, 3))


def heads_VT(z, nh):
    B, L, _ = z.shape
    return jnp.transpose(z.reshape(B, L // TK, TK, nh, HD), (0, 3, 1, 4, 2))


def from_heads_T(oT):
    B, H, _, T = oT.shape
    return jnp.transpose(oT, (0, 3, 1, 2)).reshape(B, T, H * HD)


def nsa_attention_prompt(q_flat, gates_flat, nsa_new, q_pos, w1, w2, pe, tab, bias_tiles):
    B, T, _ = q_flat.shape
    G, H, L = NSA_KV_HEADS, N_HEADS, T
    scale = HD ** -0.5
    q = q_flat.reshape(B, T, H, HD)
    kc, vc = nsa_new[:, :, 0], nsa_new[:, :, 1]
    nch = L // NSA_CMP_STRIDE
    ncmp = nch - 1

    def compress(x, i):
        c = x[:, :nch * NSA_CMP_STRIDE].reshape(B, nch, NSA_CMP_STRIDE, G, HD)
        blk = jnp.concatenate([c[:, :-1], c[:, 1:]], axis=2) + pe[i][None, None, :, None, :]
        blk = jnp.transpose(blk, (0, 1, 3, 2, 4)).reshape(B, ncmp, G, NSA_CMP_LEN * HD)
        return jax.nn.gelu(blk @ w1[i]) @ w2[i]

    kcmp = compress(kc, 0)
    vcmp = compress(vc, 1)
    cmp_end = jnp.arange(ncmp) * NSA_CMP_STRIDE + NSA_CMP_LEN - 1
    qg = q.reshape(B, T, G, NSA_GROUP, HD)
    dist_c = q_pos[:, None] - cmp_end[None, :]
    lg_c = jnp.einsum('btgrd,bngd->btgrn', qg, kcmp).astype(jnp.float32).reshape(B, T, H, ncmp) * scale
    lg_c = lg_c + rel_bias(dist_c[None, :, None, :], tab)
    p_c = masked_softmax(lg_c, (dist_c >= 0)[None, :, None, :])
    p_cg = p_c.reshape(B, T, G, NSA_GROUP, ncmp)
    o_cmp = jnp.einsum('btgrn,bngd->btgrd', p_cg.astype(vcmp.dtype), vcmp).reshape(B, T, H, HD)
    nsel = -(-L // NSA_SEL_BLOCK)
    ii = np.arange(ncmp)[:, None]
    jj_np = np.arange(nsel)[None, :]
    overlap = ((ii * NSA_CMP_STRIDE < (jj_np + 1) * NSA_SEL_BLOCK) & (ii * NSA_CMP_STRIDE + NSA_CMP_LEN > jj_np * NSA_SEL_BLOCK)).astype(np.float32)
    imp = jnp.einsum('btgrn,nj->btgj', p_cg, jnp.asarray(overlap))
    cur = q_pos // NSA_SEL_BLOCK
    jj = jnp.arange(nsel)[None, :]
    causal_blk = jj <= cur[:, None]
    forced = (jj == 0) | (jj == cur[:, None]) | (jj == cur[:, None] - 1)
    imp = jnp.where(forced[None, :, None, :], jnp.inf, jnp.where(causal_blk[None, :, None, :], imp, -jnp.inf))
    _, sel = lax.top_k(imp, min(NSA_TOPN, nsel))
    chosen = jnp.any(sel[..., None] == jnp.arange(nsel), axis=-2)
    selT = jnp.transpose(chosen, (0, 2, 3, 1)).astype(_F32)
    qT = heads_T(q_flat, H) * scale
    flat = lambda a: a.reshape(B, L, G * HD)
    o_sel = from_heads_T(attention_T('sel', qT, heads_K(flat(nsa_new[:, :, 2]), G), heads_VT(flat(nsa_new[:, :, 3]), G), bias_tiles, selT))
    o_win = from_heads_T(attention_T('win', qT, heads_K(flat(nsa_new[:, :, 4]), G), heads_VT(flat(nsa_new[:, :, 5]), G), bias_tiles))
    g = jax.nn.sigmoid(gates_flat.reshape(B, T, H, 3).astype(jnp.float32))
    out = g[..., 0:1] * o_cmp + g[..., 1:2] * o_sel.reshape(B, T, H, HD) + g[..., 2:3] * o_win.reshape(B, T, H, HD)
    return out.reshape(B, T, H * HD)


def token_mix(x, past, lp, tab):
    B, T, _ = x.shape
    P = 0 if past is None else past['nsa'].shape[1]
    q_pos = P + jnp.arange(T, dtype=jnp.int32)
    x2 = x.reshape(B * T, D_MODEL)
    z = norm_matmul(x2, lp['norm_mix'], lp['w_in']).reshape(B, T, IN_COLS)
    qa, ka, va, q_nsa, nsa_kv, nsa_g, qc, kc, vc, qi, ki, wi, u, vg = split_last(z, IN_SPLITS)
    moba_new = jnp.stack([ka, va], axis=2).reshape(B, T, 2, N_HEADS, HD)
    nsa_new = nsa_kv.reshape(B, T, 6, NSA_KV_HEADS, HD)
    nsa_main_new = nsa_new[:, :, :4]
    win_new = nsa_new[:, :, 4:]
    dsa_new = jnp.stack([kc, vc], axis=2).reshape(B, T, 2, N_HEADS, HD)
    if past is None:
        nsa_all, win_all = nsa_main_new, win_new
    else:
        nsa_all = jnp.concatenate([past['nsa'], nsa_main_new], axis=1)
        win_all = jnp.concatenate([past['win'], win_new], axis=1)
    kw_pos0 = P + T - win_all.shape[1]
    scale = HD ** -0.5
    if past is None:
        bt = lp['bias_tiles']
        qaT = heads_T(qa, N_HEADS) * scale
        y_a = from_heads_T(attention_T('moba', qaT, heads_K(ka, N_HEADS), heads_VT(va, N_HEADS), bt[0], moba_gate(qaT, heads_K(ka, N_HEADS))))
        y_b = nsa_attention_prompt(q_nsa, nsa_g, nsa_new, q_pos, lp['cmp_w1'], lp['cmp_w2'], lp['cmp_pe'], tab[:, N_HEADS:2 * N_HEADS], bt[1])
        qiT = jnp.transpose(qi.reshape(B, T, DSA_IDX_HEADS, DSA_IDX_DIM), (0, 2, 3, 1))
        keep = dsa_mask(ki, qiT, jnp.transpose(wi, (0, 2, 1)), min(DSA_TOPK, T // 4))
        y_c = from_heads_T(attention_T('dsa', heads_T(qc, N_HEADS) * scale, heads_K(kc, N_HEADS), heads_VT(vc, N_HEADS), bt[2], keep))
    else:
        assert P % MOBA_BLOCK == 0 and T <= ROWS_T
        pt = past['pt']
        n_pages = pt.shape[1]
        far_a, near_a = decode_bias(tab[:, :N_HEADS], P, n_pages)
        qa_rows = (decode_rows(qa.reshape(B, T, N_HEADS, HD)) * scale).astype(_BF16)
        new_a = pad_rows(jnp.concatenate([ka, va], axis=-1), PAGE_SIZE)
        gs = moba_decode_gate(pt, qa_rows, past['moba_rows'])
        y_a = from_decode_rows(paged_attention('moba', pt, qa_rows, past['moba_rows'], new_a, far_a, near_a, gs), T)
        y_b = nsa_attention(q_nsa.reshape(B, T, N_HEADS, HD), nsa_g.reshape(B, T, N_HEADS, 3), nsa_all[:, :, 0], nsa_all[:, :, 1], nsa_all[:, :, 2], nsa_all[:, :, 3], win_all[:, :, 0], win_all[:, :, 1], kw_pos0, q_pos, lp['cmp_w1'], lp['cmp_w2'], lp['cmp_pe'], tab[:, N_HEADS:2 * N_HEADS])
        qi_rows = pad_rows(jnp.transpose(qi.reshape(B, T, DSA_IDX_HEADS, DSA_IDX_DIM), (0, 2, 1, 3)).reshape(B * DSA_IDX_HEADS, T, DSA_IDX_DIM), ROWS_T)
        qi_rows = qi_rows.reshape(B, DSA_IDX_HEADS, ROWS_T, DSA_IDX_DIM).astype(_BF16)
        wi_rows = pad_rows(jnp.transpose(wi, (0, 2, 1)).reshape(B * DSA_IDX_HEADS, T), ROWS_T).reshape(B, DSA_IDX_HEADS, ROWS_T, 1)
        score = dsa_decode_scores(pt, qi_rows, wi_rows, past['idx_rows'], pad_rows(ki, PAGE_SIZE))
        keep = dsa_decode_select(score, min(DSA_TOPK, (P + T) // 4), P)
        keep = jnp.transpose(keep.reshape(B, ROWS_T, n_pages + 1, PAGE_SIZE), (0, 2, 1, 3))
        far_c, near_c = decode_bias(tab[:, 2 * N_HEADS:], P, n_pages)
        qc_rows = (decode_rows(qc.reshape(B, T, N_HEADS, HD)) * scale).astype(_BF16)
        new_c = pad_rows(jnp.concatenate([kc, vc], axis=-1), PAGE_SIZE)
        y_c = from_decode_rows(paged_attention('dsa', pt, qc_rows, past['dsa_rows'], new_c, far_c, near_c, keep), T)
    y_d, v_rows = gmlp_sgu(u, vg, lp['gm_ws'], lp['gm_b'], lp['gm_norm'])
    branches = jnp.stack([y_a.reshape(B, T, MIX_W), y_b.reshape(B, T, MIX_W), y_c.reshape(B, T, MIX_W), y_d], axis=2)
    proj = jnp.einsum('btnc,ncd->btnd', branches, lp['w_branch'])
    gz = norm_matmul(x2, lp['norm_mix'], lp['w_gate']).reshape(B, T, N_BRANCH, D_MODEL)
    gate = jax.nn.sigmoid(gz + lp['b_gate'])
    out = jnp.sum(gate * proj, axis=2) @ lp['w_out']
    win_keep = win_all[:, -min(NSA_WINDOW, P + T):]
    return out, (moba_new, nsa_main_new, win_keep, dsa_new, ki, v_rows)


def mix_and_mem(x, past, mkv, lp, tab):
    mix, rows = token_mix(x, past, lp, tab)
    x = x + mix
    x = x + mem_attention(rmsnorm(x, lp['norm_mem']), mkv, lp['w_mem_q'], lp['w_mem_o'])
    return x, rows


def layer_pair(xp, xs, past, mkv_p, mkv_s, lp, tab):
    xp, rp = mix_and_mem(xp, None, mkv_p, lp, tab)
    xs, rs = mix_and_mem(xs, past, mkv_s, lp, tab)
    rows = jnp.concatenate([xp.reshape(-1, D_MODEL), xs.reshape(-1, D_MODEL)], axis=0)
    rows = rows + peer_ffn_tokens(rows, lp['norm_ffn'], lp['peer_wq'], lp['peer_keys'], lp['peer_u'], lp['peer_v'])
    n_p = xp.shape[0] * xp.shape[1]
    return rows[:n_p].reshape(xp.shape), rows[n_p:].reshape(xs.shape), rp, rs


def kernel(x_prompt, x_sample, cache_moba_kv, cache_nsa_kv, cache_dsa_kv, cache_dsa_idx, state_nsa_win, cache_mem_kv, page_table, mem_prompt, rel_bias_table, w_in, nsa_cmp_w1, nsa_cmp_w2, nsa_cmp_pe, gm_ws, gm_b, gm_norm, w_branch, w_gate, b_gate, w_out, w_mem_q, w_mem_kv, w_mem_o, peer_wq, peer_keys, peer_u, peer_v, norm_mix, norm_mem, norm_ffn, norm_final):
    xp = x_prompt
    xs = x_sample
    rows_p = []
    rows_s = []
    mem_rows = []
    Bp = x_prompt.shape[0]
    n_pool = cache_moba_kv.shape[1]
    bias_tiles =[rel_bias_tiles(rel_bias_table[:, n * N_HEADS:(n + 1) * N_HEADS]) for n in range(3)]
    for l in range(DEPTH):
        lp = {
            'w_in': w_in[l], 'cmp_w1': nsa_cmp_w1[l], 'cmp_w2': nsa_cmp_w2[l], 'cmp_pe': nsa_cmp_pe[l],
            'gm_ws': gm_ws[l], 'gm_b': gm_b[l], 'gm_norm': gm_norm[l],
            'w_branch': w_branch[l], 'w_gate': w_gate[l], 'b_gate': b_gate[l], 'w_out': w_out[l],
            'w_mem_q': w_mem_q[l], 'w_mem_o': w_mem_o[l],
            'peer_wq': peer_wq[l], 'peer_keys': peer_keys[l], 'peer_u': peer_u[l], 'peer_v': peer_v[l],
            'norm_mix': norm_mix[l], 'norm_mem': norm_mem[l], 'norm_ffn': norm_ffn[l],
        }
        mkv_p = (mem_prompt @ w_mem_kv[l]).reshape(Bp, MEM_TOKENS, 2, MEM_HEADS, MEM_HD)
        past = {
            'pt': page_table + l * n_pool,
            'moba_rows': cache_moba_kv.reshape(DEPTH * n_pool, PAGE_SIZE, 2 * MIX_W),
            'dsa_rows': cache_dsa_kv.reshape(DEPTH * n_pool, PAGE_SIZE, 2 * MIX_W),
            'idx_rows': cache_dsa_idx.reshape(DEPTH * n_pool, PAGE_SIZE, DSA_IDX_DIM),
            'nsa': gather_pages(cache_nsa_kv[l], page_table),
            'win': state_nsa_win[l],
        }
        lp['bias_tiles'] = bias_tiles
        xp, xs, rp, rs = layer_pair(xp, xs, past, mkv_p, cache_mem_kv[l], lp, rel_bias_table)
        rows_p.append(rp)
        rows_s.append(rs)
        mem_rows.append(mkv_p)
    y_prompt = rmsnorm(xp, norm_final)
    y_sample = rmsnorm(xs, norm_final)
    new_moba_kv_prompt = jnp.stack([r[0] for r in rows_p])
    new_moba_kv_sample = jnp.stack([r[0] for r in rows_s])
    new_nsa_kv_prompt = jnp.stack([r[1] for r in rows_p])
    new_nsa_kv_sample = jnp.stack([r[1] for r in rows_s])
    new_nsa_win_prompt = jnp.stack([r[2] for r in rows_p])
    new_nsa_win_sample = jnp.stack([r[2] for r in rows_s])
    new_dsa_kv_prompt = jnp.stack([r[3] for r in rows_p])
    new_dsa_kv_sample = jnp.stack([r[3] for r in rows_s])
    new_dsa_idx_prompt = jnp.stack([r[4] for r in rows_p])
    new_dsa_idx_sample = jnp.stack([r[4] for r in rows_s])
    new_mem_kv_prompt = jnp.stack(mem_rows)
    new_gmlp_v_sample = jnp.stack([r[5] for r in rows_s])
    return (y_prompt, y_sample, new_moba_kv_prompt, new_moba_kv_sample, new_nsa_kv_prompt, new_nsa_kv_sample, new_nsa_win_prompt, new_nsa_win_sample, new_dsa_kv_prompt, new_dsa_kv_sample, new_dsa_idx_prompt, new_dsa_idx_sample, new_mem_kv_prompt, new_gmlp_v_sample)
```

```python
import functools
import math
import jax
import jax.numpy as jnp
from jax import lax
import numpy as np
from jax.experimental import pallas as pl
from jax.experimental.pallas import tpu as pltpu

D_MODEL = 2048
BATCH = 4
SEQ = 2048
DEPTH = 2
DEC_BATCH = 8
DEC_SEQ = 4
PAST_LEN = 16384
PAGE_SIZE = 128

HD = 64
N_HEADS = 8
MIX_W = N_HEADS * HD
N_BRANCH = 4
MOBA_BLOCK = 256
MOBA_TOPK = 3
MOBA_QB = 32
NSA_KV_HEADS = 2
NSA_GROUP = N_HEADS // NSA_KV_HEADS
NSA_CMP_STRIDE = 16
NSA_CMP_LEN = 2 * NSA_CMP_STRIDE
NSA_SEL_BLOCK = 64
NSA_TOPN = 16
NSA_WINDOW = 512
NSA_QB = 64
WIN_QB = 128
DSA_IDX_HEADS = 4
DSA_IDX_DIM = 64
DSA_TOPK = 256
DSA_QB = 64
GM_CHUNK = 128
GM_GROUPS = 8
GM_W = 512
REL_BUCKETS = 32
REL_MAX_DIST = 128
N_REL_HEADS = 3 * N_HEADS
MEM_TOKENS = 256
MEM_HEADS = 4
MEM_HD = 128
MEM_W = MEM_HEADS * MEM_HD
PEER_HEADS = 8
PEER_NKEYS = 128
PEER_QDIM = 128
PEER_TOPK = 16
PEER_EXPERTS = PEER_NKEYS ** 2
PEER_BLOCK = 128
NORM_EPS = 1e-6
IN_SPLITS = (MIX_W, MIX_W, MIX_W, MIX_W, 6 * NSA_KV_HEADS * HD, 3 * N_HEADS, MIX_W, MIX_W, MIX_W, DSA_IDX_HEADS * DSA_IDX_DIM, DSA_IDX_DIM, DSA_IDX_HEADS, GM_W, GM_W)
IN_COLS = sum(IN_SPLITS)


def rmsnorm(x, g):
    xf = x.astype(jnp.float32)
    y = xf * lax.rsqrt(jnp.mean(xf * xf, axis=-1, keepdims=True) + NORM_EPS)
    return (y * g.astype(jnp.float32)).astype(x.dtype)


def masked_softmax(logits, mask):
    l = jnp.where(mask, logits.astype(jnp.float32), -jnp.inf)
    m = jnp.max(l, axis=-1, keepdims=True)
    m = jnp.where(jnp.isfinite(m), m, 0.0)
    e = jnp.exp(l - m)
    return e / jnp.maximum(jnp.sum(e, axis=-1, keepdims=True), 1e-30)


def rel_bucket(dist):
    n = jnp.maximum(dist, 0)
    max_exact = REL_BUCKETS // 2
    nf = jnp.maximum(n, 1).astype(jnp.float32)
    large = max_exact + (jnp.log(nf / max_exact) / math.log(REL_MAX_DIST / max_exact) * (REL_BUCKETS - max_exact)).astype(jnp.int32)
    return jnp.where(n < max_exact, n, jnp.minimum(large, REL_BUCKETS - 1))


def rel_bias(dist, tab):
    onehot = (rel_bucket(dist)[..., None] == jnp.arange(REL_BUCKETS)).astype(jnp.float32)
    spec = '...okb,bh->...hk' if dist.shape[-2] == 1 else '...hkb,bh->...hk'
    return jnp.einsum(spec, onehot, tab.astype(jnp.float32), precision=lax.Precision.HIGHEST)


def split_last(z, sizes):
    return jnp.split(z, np.cumsum(sizes)[:-1].tolist(), axis=-1)


def map_query_blocks(fn, qb, q_pos, *xs):
    T = q_pos.shape[0]
    qb = qb if T % qb == 0 else T
    nb = T // qb
    xb = tuple(jnp.moveaxis(x.reshape(x.shape[0], nb, qb, *x.shape[2:]), 1, 0) for x in xs)
    out = lax.map(lambda a: fn(*a), (q_pos.reshape(nb, qb),) + xb)
    return jnp.moveaxis(out, 0, 1).reshape(out.shape[1], T, *out.shape[3:])


def gather_pages(pool, page_table):
    g = pool[page_table]
    return g.reshape(g.shape[0], g.shape[1] * g.shape[2], *g.shape[3:])


def moba_attention(q, k, v, q_pos, tab):
    B, L, H, _ = k.shape
    nblk = -(-L // MOBA_BLOCK)
    pad = ((0, 0), (0, nblk * MOBA_BLOCK - L), (0, 0), (0, 0))
    kb = jnp.pad(k, pad).reshape(B, nblk, MOBA_BLOCK, H, HD)
    vb = jnp.pad(v, pad).reshape(B, nblk, MOBA_BLOCK, H, HD)
    k_mean = jnp.mean(kb.astype(jnp.float32), axis=2).astype(k.dtype)
    kb_t = jnp.transpose(kb, (0, 3, 1, 2, 4))
    vb_t = jnp.transpose(vb, (0, 3, 1, 2, 4))
    ntop = min(MOBA_TOPK, nblk)
    bi = jnp.arange(B)[:, None, None, None]
    hi = jnp.arange(H)[None, None, :, None]
    off = jnp.arange(MOBA_BLOCK)
    scale = HD ** -0.5

    def block_fn(qp, qq):
        qb = qp.shape[0]
        own = qp // MOBA_BLOCK
        gs = jnp.einsum('bqhd,bjhd->bqhj', qq, k_mean).astype(jnp.float32)
        fully_past = jnp.arange(nblk)[None, :] < own[:, None]
        gs = jnp.where(fully_past[None, :, None, :], gs, -jnp.inf)
        _, top = lax.top_k(gs, ntop)
        own_b = jnp.broadcast_to(own[None, :, None, None], (B, qb, H, 1))
        sel = jnp.concatenate([top, own_b], axis=-1)
        sel_ok = jnp.concatenate([top < own[None, :, None, None], jnp.ones(own_b.shape, bool)], axis=-1)
        kg = kb_t[bi, hi, sel].reshape(B, qb, H, (ntop + 1) * MOBA_BLOCK, HD)
        vg = vb_t[bi, hi, sel].reshape(B, qb, H, (ntop + 1) * MOBA_BLOCK, HD)
        pos = (sel[..., None] * MOBA_BLOCK + off).reshape(B, qb, H, -1)
        dist = qp[None, :, None, None] - pos
        ok = jnp.repeat(sel_ok, MOBA_BLOCK, axis=-1) & (dist >= 0)
        lg = jnp.einsum('bqhd,bqhkd->bqhk', qq, kg).astype(jnp.float32) * scale + rel_bias(dist, tab)
        p = masked_softmax(lg, ok)
        return jnp.einsum('bqhk,bqhkd->bqhd', p.astype(vg.dtype), vg)

    return map_query_blocks(block_fn, MOBA_QB, q_pos, q)


def nsa_attention(q, gates, kc, vc, ks, vs, kw, vw, kw_pos0, q_pos, w1, w2, pe, tab):
    B, L, G, _ = kc.shape
    T = q.shape[1]
    H = N_HEADS
    scale = HD ** -0.5
    nch = L // NSA_CMP_STRIDE
    ncmp = nch - 1

    def compress(x, i):
        c = x[:, :nch * NSA_CMP_STRIDE].reshape(B, nch, NSA_CMP_STRIDE, G, HD)
        blk = jnp.concatenate([c[:, :-1], c[:, 1:]], axis=2) + pe[i][None, None, :, None, :]
        blk = jnp.transpose(blk, (0, 1, 3, 2, 4)).reshape(B, ncmp, G, NSA_CMP_LEN * HD)
        return jax.nn.gelu(blk @ w1[i]) @ w2[i]

    kcmp = compress(kc, 0)
    vcmp = compress(vc, 1)
    cmp_end = jnp.arange(ncmp) * NSA_CMP_STRIDE + NSA_CMP_LEN - 1
    qg = q.reshape(B, T, G, NSA_GROUP, HD)
    dist_c = q_pos[:, None] - cmp_end[None, :]
    lg_c = jnp.einsum('btgrd,bngd->btgrn', qg, kcmp).astype(jnp.float32).reshape(B, T, H, ncmp) * scale
    lg_c = lg_c + rel_bias(dist_c[None, :, None, :], tab)
    p_c = masked_softmax(lg_c, (dist_c >= 0)[None, :, None, :])
    p_cg = p_c.reshape(B, T, G, NSA_GROUP, ncmp)
    o_cmp = jnp.einsum('btgrn,bngd->btgrd', p_cg.astype(vcmp.dtype), vcmp).reshape(B, T, H, HD)
    nsel = -(-L // NSA_SEL_BLOCK)
    ii = np.arange(ncmp)[:, None]
    jj_np = np.arange(nsel)[None, :]
    overlap = ((ii * NSA_CMP_STRIDE < (jj_np + 1) * NSA_SEL_BLOCK) & (ii * NSA_CMP_STRIDE + NSA_CMP_LEN > jj_np * NSA_SEL_BLOCK)).astype(np.float32)
    imp = jnp.einsum('btgrn,nj->btgj', p_cg, jnp.asarray(overlap))
    cur = q_pos // NSA_SEL_BLOCK
    jj = jnp.arange(nsel)[None, :]
    causal_blk = jj <= cur[:, None]
    forced = (jj == 0) | (jj == cur[:, None]) | (jj == cur[:, None] - 1)
    imp = jnp.where(forced[None, :, None, :], jnp.inf, jnp.where(causal_blk[None, :, None, :], imp, -jnp.inf))
    ntop = min(NSA_TOPN, nsel)
    _, sel = lax.top_k(imp, ntop)
    pad = ((0, 0), (0, nsel * NSA_SEL_BLOCK - L), (0, 0), (0, 0))
    ksb = jnp.transpose(jnp.pad(ks, pad).reshape(B, nsel, NSA_SEL_BLOCK, G, HD), (0, 3, 1, 2, 4))
    vsb = jnp.transpose(jnp.pad(vs, pad).reshape(B, nsel, NSA_SEL_BLOCK, G, HD), (0, 3, 1, 2, 4))
    bi = jnp.arange(B)[:, None, None, None]
    gi = jnp.arange(G)[None, None, :, None]
    off = jnp.arange(NSA_SEL_BLOCK)

    def sel_fn(qp, qq, ss):
        qb = qp.shape[0]
        kg = ksb[bi, gi, ss].reshape(B, qb, G, ntop * NSA_SEL_BLOCK, HD)
        vg = vsb[bi, gi, ss].reshape(B, qb, G, ntop * NSA_SEL_BLOCK, HD)
        pos = (ss[..., None] * NSA_SEL_BLOCK + off).reshape(B, qb, G, -1)
        dist = jnp.repeat(qp[None, :, None, None] - pos, NSA_GROUP, axis=2)
        lg = jnp.einsum('bqgrd,bqgkd->bqgrk', qq.reshape(B, qb, G, NSA_GROUP, HD), kg).astype(jnp.float32).reshape(B, qb, H, -1) * scale
        p = masked_softmax(lg + rel_bias(dist, tab), dist >= 0)
        o = jnp.einsum('bqgrk,bqgkd->bqgrd', p.reshape(B, qb, G, NSA_GROUP, -1).astype(vg.dtype), vg)
        return o.reshape(B, qb, H, HD)

    o_sel = map_query_blocks(sel_fn, NSA_QB, q_pos, q, sel)
    W = NSA_WINDOW
    wpad = ((0, 0), (W, 0), (0, 0), (0, 0))
    kwp = jnp.pad(kw, wpad)
    vwp = jnp.pad(vw, wpad)

    def win_fn(qp, qq):
        qb = qp.shape[0]
        span = W + qb - 1
        start = qp[0] - kw_pos0 + 1
        kk = lax.dynamic_slice_in_dim(kwp, start, span, axis=1)
        vv = lax.dynamic_slice_in_dim(vwp, start, span, axis=1)
        pos = qp[0] - W + 1 + jnp.arange(span)
        dist = qp[:, None] - pos[None, :]
        ok = (dist >= 0) & (dist < W) & (pos[None, :] >= 0)
        lg = jnp.einsum('bqgrd,bkgd->bqgrk', qq.reshape(B, qb, G, NSA_GROUP, HD), kk).astype(jnp.float32).reshape(B, qb, H, span) * scale
        p = masked_softmax(lg + rel_bias(dist[None, :, None, :], tab), ok[None, :, None, :])
        o = jnp.einsum('bqgrk,bkgd->bqgrd', p.reshape(B, qb, G, NSA_GROUP, span).astype(vv.dtype), vv)
        return o.reshape(B, qb, H, HD)

    o_win = map_query_blocks(win_fn, WIN_QB, q_pos, q)
    g = jax.nn.sigmoid(gates.astype(jnp.float32)).astype(q.dtype)
    return g[..., 0:1] * o_cmp + g[..., 1:2] * o_sel + g[..., 2:3] * o_win


def dsa_attention(q, qi, wi, k, v, ki, q_pos, tab):
    B, L, H, _ = k.shape
    kk = min(DSA_TOPK, L // 4)
    bi = jnp.arange(B)[:, None, None]
    scale = HD ** -0.5

    def block_fn(qp, qq, qqi, wwi):
        s = jax.nn.relu(jnp.einsum('bqhd,bld->bqhl', qqi, ki).astype(jnp.float32) * DSA_IDX_DIM ** -0.5)
        score = jnp.einsum('bqh,bqhl->bql', wwi.astype(jnp.float32), s)
        admissible = jnp.arange(L)[None, :] <= qp[:, None]
        score = jnp.where(admissible[None], score, -jnp.inf)
        _, sel = lax.top_k(score, kk)
        kg = k[bi, sel]
        vg = v[bi, sel]
        dist = (qp[None, :, None] - sel)[:, :, None, :]
        lg = jnp.einsum('bqhd,bqkhd->bqhk', qq, kg).astype(jnp.float32) * scale + rel_bias(dist, tab)
        p = masked_softmax(lg, dist >= 0)
        return jnp.einsum('bqhk,bqkhd->bqhd', p.astype(vg.dtype), vg)

    return map_query_blocks(block_fn, DSA_QB, q_pos, q, qi, wi)


def gmlp_sgu(u, v, ws, b, gnorm):
    B, T, _ = u.shape
    vn = rmsnorm(v, gnorm)
    nc = -(-T // GM_CHUNK)
    vc = jnp.pad(vn, ((0, 0), (0, nc * GM_CHUNK - T), (0, 0))).reshape(B, nc, GM_CHUNK, GM_GROUPS, GM_W // GM_GROUPS)
    wsm = ws * jnp.tril(jnp.ones((GM_CHUNK, GM_CHUNK), ws.dtype))
    mixed = jnp.einsum('gij,bcjgd->bcigd', wsm, vc) + b.T[None, None, :, :, None]
    mixed = mixed.reshape(B, nc * GM_CHUNK, GM_W)[:, :T]
    return u * mixed, vn


def mem_attention(h, mkv, wq, wo):
    B, T, _ = h.shape
    q = (h @ wq).reshape(B, T, MEM_HEADS, MEM_HD)
    lg = jnp.einsum('bthd,bmhd->bthm', q, mkv[:, :, 0]).astype(jnp.float32) * MEM_HD ** -0.5
    p = jax.nn.softmax(lg, axis=-1)
    o = jnp.einsum('bthm,bmhd->bthd', p.astype(mkv.dtype), mkv[:, :, 1]).reshape(B, T, MEM_W)
    return o @ wo


def peer_ffn(h, wq, keys, U, V):
    B, T, D = h.shape
    N = B * T
    nb = -(-N // PEER_BLOCK)
    hf = jnp.pad(h.reshape(N, D), ((0, nb * PEER_BLOCK - N), (0, 0))).reshape(nb, PEER_BLOCK, D)

    def block_fn(x):
        q = (x @ wq).reshape(PEER_BLOCK, PEER_HEADS, 2, PEER_QDIM // 2)
        s = jnp.einsum('nhcd,hckd->nhck', q, keys).astype(jnp.float32)
        v1, i1 = lax.top_k(s[:, :, 0], PEER_TOPK)
        v2, i2 = lax.top_k(s[:, :, 1], PEER_TOPK)
        cand = (v1[..., :, None] + v2[..., None, :]).reshape(PEER_BLOCK, PEER_HEADS, PEER_TOPK * PEER_TOPK)
        sv, ci = lax.top_k(cand, PEER_TOPK)
        e = jnp.take_along_axis(i1, ci // PEER_TOPK, axis=-1) * PEER_NKEYS + jnp.take_along_axis(i2, ci % PEER_TOPK, axis=-1)
        g = jax.nn.softmax(sv, axis=-1)
        ug = U[e]
        vg = V[e]
        act = jax.nn.gelu(jnp.einsum('nd,nhkd->nhk', x, ug).astype(jnp.float32))
        return jnp.einsum('nhk,nhkd->nd', (g * act).astype(x.dtype), vg)

    out = lax.map(block_fn, hf).reshape(nb * PEER_BLOCK, D)[:N]
    return out.reshape(B, T, D)


def _norm_matmul_kernel(x_ref, g_ref, w_ref, o_ref, xn_ref):
    @pl.when(pl.program_id(1) == 0)
    def _():
        x = x_ref[...]
        y = x * lax.rsqrt(jnp.mean(x * x, axis=-1, keepdims=True) + NORM_EPS)
        xn_ref[...] = (y * g_ref[...]).astype(jnp.bfloat16)

    o_ref[...] = jnp.dot(xn_ref[...], w_ref[...].astype(jnp.bfloat16), preferred_element_type=jnp.float32)


def norm_matmul(x, g, w, tm=512, tn=512):
    N, D = x.shape
    C = w.shape[1]
    tm = min(tm, N)
    tn = min(tn, C)
    return pl.pallas_call(
        _norm_matmul_kernel,
        grid=(pl.cdiv(N, tm), pl.cdiv(C, tn)),
        in_specs=[
            pl.BlockSpec((tm, D), lambda i, j: (i, 0)),
            pl.BlockSpec((1, D), lambda i, j: (0, 0)),
            pl.BlockSpec((D, tn), lambda i, j: (0, j)),
        ],
        out_specs=pl.BlockSpec((tm, tn), lambda i, j: (i, j)),
        out_shape=jax.ShapeDtypeStruct((N, C), jnp.float32),
        scratch_shapes=[pltpu.VMEM((tm, D), jnp.bfloat16)],
        compiler_params=pltpu.CompilerParams(
            dimension_semantics=("arbitrary", "arbitrary"), vmem_limit_bytes=48 * 1024 * 1024),
        name="norm_matmul",
    )(x, g.reshape(1, D), w)


TQ = 256
TK = 256
NEG_BIG = -1e30
VMEM_LIMIT = 48 * 1024 * 1024
_BF16 = jnp.bfloat16
_F32 = jnp.float32


def _attn_kernel(mode, qT_ref, k_ref, vT_ref, bias_ref, *rest):
    if mode == 'win':
        (o_ref,) = rest
        m_ref = None
    else:
        m_ref, o_ref = rest
    qi = pl.program_id(2)
    q = qT_ref[0, 0].astype(_BF16)
    lane = lax.broadcasted_iota(jnp.int32, (TK, TQ), 1)
    sub = lax.broadcasted_iota(jnp.int32, (TK, TQ), 0)
    rel0 = lane - sub

    def body(kj, carry):
        m, l, acc = carry
        k_t = k_ref[0, 0, pl.ds(pl.multiple_of(kj * TK, TK), TK), :].astype(_BF16)
        s = jnp.dot(k_t, q, preferred_element_type=_F32)
        s = s + bias_ref[0, jnp.minimum(qi - kj, 2)]
        dist = rel0 + (qi - kj) * TQ
        ok = dist >= 0
        if mode == 'moba':
            ok = ok & (m_ref[0, 0, kj] > 0.5)
        elif mode == 'sel':
            rows = m_ref[0, 0, kj]
            blk = sub // NSA_SEL_BLOCK
            r = jnp.where(blk == 0, rows[0:1], jnp.where(blk == 1, rows[1:2], jnp.where(blk == 2, rows[2:3], rows[3:4])))
            ok = ok & (r > 0.5)
        elif mode == 'dsa':
            ok = ok & (m_ref[0, pl.ds(pl.multiple_of(kj * TK, TK), TK), :].astype(_F32) > 0.5)
        elif mode == 'win':
            ok = ok & (dist < NSA_WINDOW)
        s = jnp.where(ok, s, NEG_BIG)
        m_new = jnp.maximum(m, jnp.max(s, axis=0, keepdims=True))
        p = jnp.where(ok, jnp.exp(s - m_new), 0.0)
        alpha = jnp.exp(m - m_new)
        l = alpha * l + jnp.sum(p, axis=0, keepdims=True)
        v_t = vT_ref[0, 0, kj].astype(_BF16)
        acc = alpha * acc + jnp.dot(v_t, p.astype(_BF16), preferred_element_type=_F32)
        return m_new, l, acc

    lo = jnp.maximum(qi - (NSA_WINDOW // TK), 0) if mode == 'win' else 0
    init = (jnp.full((1, TQ), NEG_BIG, _F32), jnp.zeros((1, TQ), _F32), jnp.zeros((HD, TQ), _F32))
    m, l, acc = lax.fori_loop(lo, qi + 1, body, init)
    o_ref[0, 0] = acc / jnp.maximum(l, 1e-30)


def attention_T(mode, qT, k, vT, bias_tiles, mask=None):
    B, H, _, T = qT.shape
    Hkv, L = k.shape[1], k.shape[2]
    grp = H // Hkv
    in_specs = [
        pl.BlockSpec((1, 1, HD, TQ), lambda b, h, i: (b, h, 0, i)),
        pl.BlockSpec((1, 1, L, HD), lambda b, h, i: (b, h // grp, 0, 0)),
        pl.BlockSpec((1, 1, L // TK, HD, TK), lambda b, h, i: (b, h // grp, 0, 0, 0)),
        pl.BlockSpec((1, 3, TK, TQ), lambda b, h, i: (h, 0, 0, 0)),
    ]
    args = [qT, k, vT, bias_tiles]
    if mode == 'moba':
        in_specs.append(pl.BlockSpec((1, 1, L // TK, 1, TQ), lambda b, h, i: (b, h, 0, 0, i)))
        args.append(mask.reshape(B, H, L // TK, 1, T))
    elif mode == 'sel':
        per = TK // NSA_SEL_BLOCK
        in_specs.append(pl.BlockSpec((1, 1, L // TK, per, TQ), lambda b, h, i: (b, h // grp, 0, 0, i)))
        args.append(mask.reshape(B, Hkv, L // TK, per, T))
    elif mode == 'dsa':
        in_specs.append(pl.BlockSpec((1, L, TQ), lambda b, h, i: (b, 0, i)))
        args.append(mask)
    return pl.pallas_call(
        functools.partial(_attn_kernel, mode),
        grid=(B, H, T // TQ),
        in_specs=in_specs,
        out_specs=pl.BlockSpec((1, 1, HD, TQ), lambda b, h, i: (b, h, 0, i)),
        out_shape=jax.ShapeDtypeStruct((B, H, HD, T), _F32),
        compiler_params=pltpu.CompilerParams(
            dimension_semantics=("arbitrary", "arbitrary", "arbitrary"), vmem_limit_bytes=VMEM_LIMIT),
        name="attn_" + mode,
    )(*args)


def _moba_gate_kernel(qT_ref, k_ref, sel_ref):
    nblk, T = sel_ref.shape[2], sel_ref.shape[3]
    q = qT_ref[0, 0].astype(_BF16)
    k = k_ref[0, 0]
    km = jnp.mean(k.reshape(nblk, MOBA_BLOCK, HD), axis=1)
    gs = jnp.dot(km.astype(_BF16), q, preferred_element_type=_F32)
    j = lax.broadcasted_iota(jnp.int32, (nblk, T), 0)
    own = lax.broadcasted_iota(jnp.int32, (nblk, T), 1) // MOBA_BLOCK
    fully_past = j < own
    gs = jnp.where(fully_past, gs, -jnp.inf)
    rank = jnp.zeros((nblk, T), jnp.int32)
    for jp in range(nblk):
        row = gs[jp:jp + 1, :]
        ahead = (row > gs) | ((row == gs) & (jp < j))
        rank = rank + ahead.astype(jnp.int32)
    sel = (fully_past & (rank < MOBA_TOPK)) | (j == own)
    sel_ref[0, 0] = sel.astype(_F32)


def moba_gate(qT, k):
    B, H, _, T = qT.shape
    L = k.shape[2]
    nblk = L // MOBA_BLOCK
    return pl.pallas_call(
        _moba_gate_kernel,
        grid=(B, H),
        in_specs=[pl.BlockSpec((1, 1, HD, T), lambda b, h: (b, h, 0, 0)),
                  pl.BlockSpec((1, 1, L, HD), lambda b, h: (b, h, 0, 0))],
        out_specs=pl.BlockSpec((1, 1, nblk, T), lambda b, h: (b, h, 0, 0)),
        out_shape=jax.ShapeDtypeStruct((B, H, nblk, T), _F32),
        compiler_params=pltpu.CompilerParams(dimension_semantics=("arbitrary", "arbitrary"), vmem_limit_bytes=VMEM_LIMIT),
        name="moba_gate",
    )(qT, k)


def _count_rows(pred):
    return jnp.sum(pred.astype(_F32), axis=0, keepdims=True)


def _dsa_mask_kernel(topk, ki_ref, qiT_ref, wiT_ref, mask_ref, key_ref):
    L = ki_ref.shape[1]
    qi = pl.program_id(1)
    ki = ki_ref[0].astype(_BF16)
    score = jnp.zeros((L, TQ), _F32)
    for hh in range(DSA_IDX_HEADS):
        s = jnp.dot(ki, qiT_ref[0, hh].astype(_BF16), preferred_element_type=_F32) * DSA_IDX_DIM ** -0.5
        score = score + wiT_ref[0, hh:hh + 1, :] * jnp.maximum(s, 0.0)
    idx = lax.broadcasted_iota(jnp.int32, (L, TQ), 0)
    q_pos = qi * TQ + lax.broadcasted_iota(jnp.int32, (L, TQ), 1)
    adm = idx <= q_pos
    score = jnp.where(adm, jnp.where(score == 0.0, 0.0, score), -jnp.inf)
    bits = pltpu.bitcast(score, jnp.int32)
    key_ref[...] = jnp.where(bits < 0, bits ^ jnp.int32(0x7FFFFFFF), bits)
    kf = jnp.float32(topk)
    int_min = jnp.int32(-2 ** 31)
    lo = jnp.where(_count_rows(key_ref[...] >= 0) >= kf, jnp.int32(0), int_min)

    def vstep(i, lo):
        cand = lo + jnp.left_shift(jnp.int32(1), 30 - i)
        return jnp.where(_count_rows(key_ref[...] >= cand) >= kf, cand, lo)

    thr = lax.fori_loop(0, 31, vstep, lo)
    need = kf - _count_rows(key_ref[...] > thr)
    nbits = max(1, (L - 1).bit_length())

    def istep(i, lo):
        cand = lo + jnp.left_shift(jnp.int32(1), nbits - 1 - i)
        c = _count_rows((key_ref[...] == thr) & (idx < cand))
        return jnp.where(c < need, cand, lo)

    last_tie = lax.fori_loop(0, nbits, istep, jnp.zeros((1, TQ), jnp.int32))
    key = key_ref[...]
    keep = ((key > thr) | ((key == thr) & (idx <= last_tie))) & adm
    mask_ref[0] = keep.astype(mask_ref.dtype)


def dsa_mask(ki, qiT, wiT, topk):
    B, L, _ = ki.shape
    T = qiT.shape[3]
    return pl.pallas_call(
        functools.partial(_dsa_mask_kernel, topk),
        grid=(B, T // TQ),
        in_specs=[pl.BlockSpec((1, L, DSA_IDX_DIM), lambda b, i: (b, 0, 0)),
                  pl.BlockSpec((1, DSA_IDX_HEADS, DSA_IDX_DIM, TQ), lambda b, i: (b, 0, 0, i)),
                  pl.BlockSpec((1, DSA_IDX_HEADS, TQ), lambda b, i: (b, 0, i))],
        out_specs=pl.BlockSpec((1, L, TQ), lambda b, i: (b, 0, i)),
        out_shape=jax.ShapeDtypeStruct((B, L, T), _BF16),
        scratch_shapes=[pltpu.VMEM((L, TQ), jnp.int32)],
        compiler_params=pltpu.CompilerParams(dimension_semantics=("arbitrary", "arbitrary"), vmem_limit_bytes=VMEM_LIMIT),
        name="dsa_mask",
    )(ki, qiT, wiT)


PEER_TN = 256
PEER_EB = 1024
PEER_IB = PEER_EB // PEER_NKEYS
RANK_OUT = 4096.0


def _extract_top(work, n):
    rows = work.shape[0]
    iota = lax.broadcasted_iota(jnp.int32, work.shape, 0)
    order = jnp.full(work.shape, RANK_OUT, _F32)
    vals = []
    idx = None
    for a in range(n):
        m = jnp.max(work, axis=0, keepdims=True)
        idx = jnp.min(jnp.where(work == m, iota, rows), axis=0, keepdims=True)
        hit = iota == idx
        vals.append(m)
        order = jnp.where(hit, jnp.float32(a), order)
        work = jnp.where(hit, -jnp.inf, work)
    return vals, order, idx


def _peer_select_kernel(x_ref, g_ref, wqT_ref, keys_ref, hx_ref, c1_ref, e1n_ref, r2_ref, e2_ref):
    x = x_ref[...]
    y = x * lax.rsqrt(jnp.mean(x * x, axis=1, keepdims=True) + NORM_EPS) * g_ref[...]
    hx = y.astype(_BF16)
    hx_ref[...] = hx
    qT = lax.dot_general(wqT_ref[...], hx, (((1,), (1,)), ((), ())), preferred_element_type=_F32).astype(_BF16)
    half = PEER_QDIM // 2
    for h in range(PEER_HEADS):
        sc = []
        for c in range(2):
            r0 = (h * 2 + c) * half
            sc.append(jnp.dot(keys_ref[h * 2 + c], qT[r0:r0 + half, :], preferred_element_type=_F32))
        v1, o1, _ = _extract_top(sc[0], PEER_TOPK)
        v2, o2, _ = _extract_top(sc[1], PEER_TOPK)
        v2s = jnp.concatenate(v2, axis=0)
        cand = jnp.concatenate([v1[a] + v2s for a in range(PEER_TOPK)], axis=0)
        ex1 = [jnp.exp(v1[a] - v1[0]) for a in range(PEER_TOPK)]
        ex2s = jnp.exp(v2s - v2[0])
        ecand = jnp.concatenate([ex1[a] * ex2s for a in range(PEER_TOPK)], axis=0)
        _, corder, _ = _extract_top(cand, PEER_TOPK)
        chosen = corder < RANK_OUT
        z = jnp.sum(jnp.where(chosen, ecand, 0.0), axis=0, keepdims=True)
        c1 = jnp.zeros(sc[0].shape, _F32)
        for a in range(PEER_TOPK):
            cnt = jnp.sum(chosen[a * PEER_TOPK:(a + 1) * PEER_TOPK].astype(_F32), axis=0, keepdims=True)
            c1 = jnp.where(o1 == jnp.float32(a), cnt, c1)
        c1_ref[h] = c1
        e1n_ref[h] = jnp.exp(sc[0] - v1[0]) / z
        r2_ref[h] = o2
        e2_ref[h] = jnp.exp(sc[1] - v2[0])


def peer_select(x, g, wqT, keys2):
    Np, D = x.shape
    tn = PEER_TN
    row = lambda: pl.BlockSpec((PEER_HEADS, PEER_NKEYS, tn), lambda t: (0, 0, t))
    rshape = jax.ShapeDtypeStruct((PEER_HEADS, PEER_NKEYS, Np), _F32)
    return pl.pallas_call(
        _peer_select_kernel,
        grid=(Np // tn,),
        in_specs=[pl.BlockSpec((tn, D), lambda t: (t, 0)),
                  pl.BlockSpec((1, D), lambda t: (0, 0)),
                  pl.BlockSpec((PEER_HEADS * PEER_QDIM, D), lambda t: (0, 0)),
                  pl.BlockSpec((2 * PEER_HEADS, PEER_NKEYS, PEER_QDIM // 2), lambda t: (0, 0, 0))],
        out_specs=[pl.BlockSpec((tn, D), lambda t: (t, 0)), row(), row(), row(), row()],
        out_shape=[jax.ShapeDtypeStruct((Np, D), _BF16), rshape, rshape, rshape, rshape],
        compiler_params=pltpu.CompilerParams(dimension_semantics=("arbitrary",), vmem_limit_bytes=VMEM_LIMIT),
        name="peer_select",
    )(x, g, wqT, keys2)


def _gelu_tanh(x):
    return 0.5 * x * (1.0 + jnp.tanh(math.sqrt(2.0 / math.pi) * (x + 0.044715 * (x * x * x))))


def _peer_dense_kernel(x_ref, hx_ref, u_ref, v_ref, c1_ref, e1n_ref, r2_ref, e2_ref, o_ref, p_ref):
    @pl.when(pl.program_id(1) == 0)
    def _():
        o_ref[...] = x_ref[...]

    tn = x_ref.shape[0]
    act = _gelu_tanh(lax.dot_general(u_ref[...], hx_ref[...], (((1,), (1,)), ((), ())), preferred_element_type=_F32))
    for ib in range(PEER_IB):
        w = jnp.zeros((PEER_NKEYS, tn), _F32)
        for h in range(PEER_HEADS):
            w = w + jnp.where(r2_ref[h] < c1_ref[h, ib:ib + 1, :], e1n_ref[h, ib:ib + 1, :] * e2_ref[h], 0.0)
        p_ref[ib * PEER_NKEYS:(ib + 1) * PEER_NKEYS, :] = (w * act[ib * PEER_NKEYS:(ib + 1) * PEER_NKEYS, :]).astype(_BF16)
    o_ref[...] += lax.dot_general(p_ref[...], v_ref[...], (((0,), (0,)), ((), ())), preferred_element_type=_F32)


def peer_dense(x, hx, u, v, c1, e1n, r2, e2):
    Np, D = x.shape
    E = u.shape[0]
    tn = PEER_TN
    rowi = lambda: pl.BlockSpec((PEER_HEADS, PEER_IB, tn), lambda t, e: (0, e, t))
    rowj = lambda: pl.BlockSpec((PEER_HEADS, PEER_NKEYS, tn), lambda t, e: (0, 0, t))
    return pl.pallas_call(
        _peer_dense_kernel,
        grid=(Np // tn, E // PEER_EB),
        in_specs=[pl.BlockSpec((tn, D), lambda t, e: (t, 0)),
                  pl.BlockSpec((tn, D), lambda t, e: (t, 0)),
                  pl.BlockSpec((PEER_EB, D), lambda t, e: (e, 0)),
                  pl.BlockSpec((PEER_EB, D), lambda t, e: (e, 0)),
                  rowi(), rowi(), rowj(), rowj()],
        out_specs=pl.BlockSpec((tn, D), lambda t, e: (t, 0)),
        out_shape=jax.ShapeDtypeStruct((Np, D), _F32),
        scratch_shapes=[pltpu.VMEM((PEER_EB, tn), _BF16)],
        compiler_params=pltpu.CompilerParams(dimension_semantics=("arbitrary", "arbitrary"), vmem_limit_bytes=VMEM_LIMIT),
        name="peer_dense",
    )(x, hx, u, v, c1, e1n, r2, e2)


def peer_ffn_tokens(x_tokens, g, wq, keys, U, V):
    N, D = x_tokens.shape
    Np = -(-N // PEER_TN) * PEER_TN
    x = jnp.pad(x_tokens, ((0, Np - N), (0, 0)))
    keys2 = keys.reshape(2 * PEER_HEADS, PEER_NKEYS, PEER_QDIM // 2).astype(_BF16)
    hx, c1, e1n, r2, e2 = peer_select(x, g.reshape(1, D), wq.T.astype(_BF16), keys2)
    return peer_dense(x, hx, U.astype(_BF16), V.astype(_BF16), c1, e1n, r2, e2)[:N]


def bias_lookup(dist, tab):
    onehot = (rel_bucket(dist)[..., None] == jnp.arange(REL_BUCKETS)).astype(_F32)
    return jnp.einsum('...b,bh->h...', onehot, tab.astype(_F32), precision=lax.Precision.HIGHEST)


def rel_bias_tiles(tab):
    kk = jnp.arange(TK)[:, None]
    qq = jnp.arange(TQ)[None, :]
    dist = jnp.stack([d * TQ + qq - kk for d in range(3)])
    return bias_lookup(dist, tab)


ROWS_T = 8
DEC_ROWS = N_HEADS * ROWS_T
NEAR_TILES = 3


PP = 4


def _page_specs(block, n_pages):
    def spec(i):
        return pl.BlockSpec(block, lambda b, p, pt: (pt[b, jnp.minimum(PP * p + i, n_pages - 1)], 0, 0))
    return [spec(i) for i in range(PP)]


def _moba_decode_gate_kernel(pt_ref, q_ref, *rest):
    caches, gs_ref = rest[:PP], rest[PP]
    p = pl.program_id(1)
    per = MOBA_BLOCK // PAGE_SIZE

    @pl.when(p == 0)
    def _():
        gs_ref[...] = jnp.zeros_like(gs_ref)

    lane = lax.broadcasted_iota(jnp.int32, gs_ref.shape[1:], 1)
    for g in range(PP // per):
        ksum = sum(jnp.sum(caches[g * per + i][0], axis=0, keepdims=True) for i in range(per))
        km = (ksum * (1.0 / MOBA_BLOCK)).astype(_BF16).astype(_F32)
        col = jnp.sum(q_ref[0].astype(_F32) * km, axis=1, keepdims=True)
        gs_ref[0] = jnp.where(lane == p * (PP // per) + g, col, gs_ref[0])


def moba_decode_gate(pt, q_rows, cache_rows):
    B, n_pages = pt.shape
    W = N_HEADS * HD
    assert n_pages % PP == 0 and PP % (MOBA_BLOCK // PAGE_SIZE) == 0
    return pl.pallas_call(
        _moba_decode_gate_kernel,
        grid_spec=pltpu.PrefetchScalarGridSpec(
            num_scalar_prefetch=1,
            grid=(B, n_pages // PP),
            in_specs=[pl.BlockSpec((1, DEC_ROWS, W), lambda b, p, pt: (b, 0, 0))] + _page_specs((1, PAGE_SIZE, W), n_pages),
            out_specs=pl.BlockSpec((1, DEC_ROWS, 128), lambda b, p, pt: (b, 0, 0))),
        out_shape=jax.ShapeDtypeStruct((B, DEC_ROWS, 128), _F32),
        compiler_params=pltpu.CompilerParams(dimension_semantics=("arbitrary", "arbitrary"), vmem_limit_bytes=VMEM_LIMIT),
        name="moba_decode_gate",
    )(pt, q_rows, *([cache_rows] * PP))


def _paged_attn_kernel(mode, pt_ref, q_ref, *rest):
    caches = rest[:PP]
    new_ref, far_ref, near_ref, m_in_ref, o_ref, m_sc, l_sc, acc_sc, sel_sc = rest[PP:]
    p = pl.program_id(1)
    last = pl.num_programs(1) - 1
    n_pages = last * PP
    W = N_HEADS * HD
    lane = lax.broadcasted_iota(jnp.int32, (DEC_ROWS, PAGE_SIZE), 1)
    row_t = lax.broadcasted_iota(jnp.int32, (DEC_ROWS, PAGE_SIZE), 0) % ROWS_T

    @pl.when(p == 0)
    def _():
        m_sc[...] = jnp.full_like(m_sc, NEG_BIG)
        l_sc[...] = jnp.zeros_like(l_sc)
        acc_sc[...] = jnp.zeros_like(acc_sc)
        if mode == 'moba':
            nblk = n_pages // (MOBA_BLOCK // PAGE_SIZE)
            valid = lane < nblk
            gs = jnp.where(valid, m_in_ref[0], -jnp.inf)
            rank = jnp.zeros(gs.shape, jnp.int32)
            for jp in range(PAST_LEN // MOBA_BLOCK):
                col = gs[:, jp:jp + 1]
                rank = rank + ((col > gs) | ((col == gs) & (jp < lane))).astype(jnp.int32)
            sel_sc[...] = (valid & (rank < MOBA_TOPK)).astype(_F32)

    def tile(blk, g, i, is_new):
        k = blk[:, :W].astype(_BF16)
        v = blk[:, W:].astype(_BF16)
        s = lax.dot_general(q_ref[0], k, (((1,), (1,)), ((), ())), preferred_element_type=_F32)
        near = near_ref[jnp.clip(g - (n_pages + 1 - NEAR_TILES), 0, NEAR_TILES - 1)]
        s = s + jnp.where(g >= n_pages + 1 - NEAR_TILES, near, far_ref[...])
        if mode == 'moba':
            if is_new:
                ok = lane <= row_t
            else:
                ok = jnp.sum(jnp.where(lane == g // (MOBA_BLOCK // PAGE_SIZE), sel_sc[...], 0.0), axis=1, keepdims=True) > 0.5
        else:
            ok = jnp.concatenate([m_in_ref[0, i]] * N_HEADS, axis=0) > 0.5
        s = jnp.where(ok, s, NEG_BIG)
        m_new = jnp.maximum(m_sc[...], jnp.max(s, axis=1, keepdims=True))
        pr = jnp.where(ok, jnp.exp(s - m_new), 0.0)
        alpha = jnp.exp(m_sc[...] - m_new)
        l_sc[...] = alpha * l_sc[...] + jnp.sum(pr, axis=1, keepdims=True)
        acc_sc[...] = alpha * acc_sc[...] + jnp.dot(pr.astype(_BF16), v, preferred_element_type=_F32)
        m_sc[...] = m_new

    @pl.when(p < last)
    def _():
        for i in range(PP):
            tile(caches[i][0], PP * p + i, i, False)

    @pl.when(p == last)
    def _():
        tile(new_ref[0], n_pages, 0, True)
        o_ref[0] = acc_sc[...] / jnp.maximum(l_sc[...], 1e-30)


def paged_attention(mode, pt, q_rows, cache_rows, new_rows, far, near, m_in):
    B, n_pages = pt.shape
    W = N_HEADS * HD
    assert n_pages % PP == 0
    if mode == 'moba':
        m_spec = pl.BlockSpec((1, DEC_ROWS, 128), lambda b, p, pt: (b, 0, 0))
    else:
        m_spec = pl.BlockSpec((1, PP, ROWS_T, PAGE_SIZE), lambda b, p, pt: (b, p, 0, 0))
    return pl.pallas_call(
        functools.partial(_paged_attn_kernel, mode),
        grid_spec=pltpu.PrefetchScalarGridSpec(
            num_scalar_prefetch=1,
            grid=(B, n_pages // PP + 1),
            in_specs=[pl.BlockSpec((1, DEC_ROWS, W), lambda b, p, pt: (b, 0, 0))] + _page_specs((1, PAGE_SIZE, 2 * W), n_pages) + [
                      pl.BlockSpec((1, PAGE_SIZE, 2 * W), lambda b, p, pt: (b, 0, 0)),
                      pl.BlockSpec((DEC_ROWS, 1), lambda b, p, pt: (0, 0)),
                      pl.BlockSpec((NEAR_TILES, DEC_ROWS, PAGE_SIZE), lambda b, p, pt: (0, 0, 0)),
                      m_spec],
            out_specs=pl.BlockSpec((1, DEC_ROWS, W), lambda b, p, pt: (b, 0, 0)),
            scratch_shapes=[pltpu.VMEM((DEC_ROWS, 1), _F32), pltpu.VMEM((DEC_ROWS, 1), _F32),
                            pltpu.VMEM((DEC_ROWS, W), _F32), pltpu.VMEM((DEC_ROWS, 128), _F32)]),
        out_shape=jax.ShapeDtypeStruct((B, DEC_ROWS, W), _F32),
        compiler_params=pltpu.CompilerParams(dimension_semantics=("arbitrary", "arbitrary"), vmem_limit_bytes=VMEM_LIMIT),
        name="paged_attn_" + mode,
    )(pt, q_rows, *([cache_rows] * PP), new_rows, far, near, m_in)


def _dsa_decode_score_kernel(pt_ref, qi_ref, wi_ref, *rest):
    caches, new_ref, score_ref = rest[:PP], rest[PP], rest[PP + 1]
    p = pl.program_id(1)
    last = pl.num_programs(1) - 1

    def tile_scores(ki):
        ki = ki.astype(_BF16)
        acc = jnp.zeros((ROWS_T, PAGE_SIZE), _F32)
        for hh in range(DSA_IDX_HEADS):
            s = lax.dot_general(qi_ref[0, hh], ki, (((1,), (1,)), ((), ())), preferred_element_type=_F32) * DSA_IDX_DIM ** -0.5
            acc = acc + wi_ref[0, hh] * jnp.maximum(s, 0.0)
        return acc

    @pl.when(p < last)
    def _():
        for i in range(PP):
            score_ref[0, :, i * PAGE_SIZE:(i + 1) * PAGE_SIZE] = tile_scores(caches[i][0])

    @pl.when(p == last)
    def _():
        score_ref[0] = jnp.zeros(score_ref.shape[1:], _F32)
        score_ref[0, :, 0:PAGE_SIZE] = tile_scores(new_ref[0])


def dsa_decode_scores(pt, qi_rows, wi_rows, cache_rows, new_rows):
    B, n_pages = pt.shape
    assert n_pages % PP == 0
    return pl.pallas_call(
        _dsa_decode_score_kernel,
        grid_spec=pltpu.PrefetchScalarGridSpec(
            num_scalar_prefetch=1,
            grid=(B, n_pages // PP + 1),
            in_specs=[pl.BlockSpec((1, DSA_IDX_HEADS, ROWS_T, DSA_IDX_DIM), lambda b, p, pt: (b, 0, 0, 0)),
                      pl.BlockSpec((1, DSA_IDX_HEADS, ROWS_T, 1), lambda b, p, pt: (b, 0, 0, 0))]
            + _page_specs((1, PAGE_SIZE, DSA_IDX_DIM), n_pages)
            + [pl.BlockSpec((1, PAGE_SIZE, DSA_IDX_DIM), lambda b, p, pt: (b, 0, 0))],
            out_specs=pl.BlockSpec((1, ROWS_T, PP * PAGE_SIZE), lambda b, p, pt: (b, 0, p))),
        out_shape=jax.ShapeDtypeStruct((B, ROWS_T, (n_pages + PP) * PAGE_SIZE), _F32),
        compiler_params=pltpu.CompilerParams(dimension_semantics=("arbitrary", "arbitrary"), vmem_limit_bytes=VMEM_LIMIT),
        name="dsa_decode_scores",
    )(pt, qi_rows, wi_rows, *([cache_rows] * PP), new_rows)


def _count_lanes(pred):
    return jnp.sum(pred.astype(_F32), axis=1, keepdims=True)


def _dsa_decode_select_kernel(topk, past_len, score_ref, keep_ref, key_ref):
    Lp = score_ref.shape[2]
    idx = lax.broadcasted_iota(jnp.int32, (ROWS_T, Lp), 1)
    q_pos = past_len + lax.broadcasted_iota(jnp.int32, (ROWS_T, Lp), 0)
    adm = idx <= q_pos
    score = score_ref[0]
    score = jnp.where(adm, jnp.where(score == 0.0, 0.0, score), -jnp.inf)
    bits = pltpu.bitcast(score, jnp.int32)
    key_ref[...] = jnp.where(bits < 0, bits ^ jnp.int32(0x7FFFFFFF), bits)
    kf = jnp.float32(topk)
    lo = jnp.where(_count_lanes(key_ref[...] >= 0) >= kf, jnp.int32(0), jnp.int32(-2 ** 31))

    def vstep(i, lo):
        cand = lo + jnp.left_shift(jnp.int32(1), 30 - i)
        return jnp.where(_count_lanes(key_ref[...] >= cand) >= kf, cand, lo)

    thr = lax.fori_loop(0, 31, vstep, lo)
    need = kf - _count_lanes(key_ref[...] > thr)
    nbits = max(1, (Lp - 1).bit_length())

    def istep(i, lo):
        cand = lo + jnp.left_shift(jnp.int32(1), nbits - 1 - i)
        c = _count_lanes((key_ref[...] == thr) & (idx < cand))
        return jnp.where(c < need, cand, lo)

    last_tie = lax.fori_loop(0, nbits, istep, jnp.zeros((ROWS_T, 1), jnp.int32))
    key = key_ref[...]
    keep = ((key > thr) | ((key == thr) & (idx <= last_tie))) & adm
    keep_ref[0] = keep.astype(_F32)


def dsa_decode_select(score, topk, past_len):
    B, _, Lp = score.shape
    return pl.pallas_call(
        functools.partial(_dsa_decode_select_kernel, topk, past_len),
        grid=(B,),
        in_specs=[pl.BlockSpec((1, ROWS_T, Lp), lambda b: (b, 0, 0))],
        out_specs=pl.BlockSpec((1, ROWS_T, Lp), lambda b: (b, 0, 0)),
        out_shape=jax.ShapeDtypeStruct((B, ROWS_T, Lp), _F32),
        scratch_shapes=[pltpu.VMEM((ROWS_T, Lp), jnp.int32)],
        compiler_params=pltpu.CompilerParams(dimension_semantics=("arbitrary",), vmem_limit_bytes=VMEM_LIMIT),
        name="dsa_decode_select",
    )(score)


def decode_rows(q):
    B, T, H, _ = q.shape
    qp = jnp.pad(jnp.transpose(q, (0, 2, 1, 3)), ((0, 0), (0, 0), (0, ROWS_T - T), (0, 0)))
    rows = qp[:, :, :, None, :] * jnp.eye(H, dtype=q.dtype)[None, :, None, :, None]
    return rows.reshape(B, H * ROWS_T, H * HD)


def from_decode_rows(o, T):
    B = o.shape[0]
    o5 = o.reshape(B, N_HEADS, ROWS_T, N_HEADS, HD)
    own = jnp.sum(o5 * jnp.eye(N_HEADS, dtype=o.dtype)[None, :, None, :, None], axis=3)
    return jnp.transpose(own[:, :, :T], (0, 2, 1, 3)).reshape(B, T, N_HEADS * HD)


def decode_bias(tab, past_len, n_pages):
    t = jnp.arange(ROWS_T)[None, :, None]
    tile = (n_pages + 1 - NEAR_TILES + jnp.arange(NEAR_TILES))[:, None, None]
    dist = past_len + t - (tile * PAGE_SIZE + jnp.arange(PAGE_SIZE)[None, None, :])
    near = jnp.transpose(bias_lookup(dist, tab), (1, 0, 2, 3)).reshape(NEAR_TILES, DEC_ROWS, PAGE_SIZE)
    far = jnp.repeat(tab[REL_BUCKETS - 1].astype(_F32), ROWS_T).reshape(DEC_ROWS, 1)
    return far, near


def pad_rows(a, n):
    return jnp.pad(a, ((0, 0), (0, n - a.shape[1])) + ((0, 0),) * (a.ndim - 2))


def heads_T(z, nh):
    B, T, _ = z.shape
    return jnp.transpose(z.reshape(B, T, nh, HD), (0, 2, 3, 1))


def heads_K(z, nh):
    B, L, _ = z.shape
    return jnp.transpose(z.reshape(B, L, nh, HD), (0, 2, 1, 3))


def heads_VT(z, nh):
    B, L, _ = z.shape
    return jnp.transpose(z.reshape(B, L // TK, TK, nh, HD), (0, 3, 1, 4, 2))


def from_heads_T(oT):
    B, H, _, T = oT.shape
    return jnp.transpose(oT, (0, 3, 1, 2)).reshape(B, T, H * HD)


def nsa_attention_prompt(q_flat, gates_flat, nsa_new, q_pos, w1, w2, pe, tab, bias_tiles):
    B, T, _ = q_flat.shape
    G, H, L = NSA_KV_HEADS, N_HEADS, T
    scale = HD ** -0.5
    q = q_flat.reshape(B, T, H, HD)
    kc, vc = nsa_new[:, :, 0], nsa_new[:, :, 1]
    nch = L // NSA_CMP_STRIDE
    ncmp = nch - 1

    def compress(x, i):
        c = x[:, :nch * NSA_CMP_STRIDE].reshape(B, nch, NSA_CMP_STRIDE, G, HD)
        blk = jnp.concatenate([c[:, :-1], c[:, 1:]], axis=2) + pe[i][None, None, :, None, :]
        blk = jnp.transpose(blk, (0, 1, 3, 2, 4)).reshape(B, ncmp, G, NSA_CMP_LEN * HD)
        return jax.nn.gelu(blk @ w1[i]) @ w2[i]

    kcmp = compress(kc, 0)
    vcmp = compress(vc, 1)
    cmp_end = jnp.arange(ncmp) * NSA_CMP_STRIDE + NSA_CMP_LEN - 1
    qg = q.reshape(B, T, G, NSA_GROUP, HD)
    dist_c = q_pos[:, None] - cmp_end[None, :]
    lg_c = jnp.einsum('btgrd,bngd->btgrn', qg, kcmp).astype(jnp.float32).reshape(B, T, H, ncmp) * scale
    lg_c = lg_c + rel_bias(dist_c[None, :, None, :], tab)
    p_c = masked_softmax(lg_c, (dist_c >= 0)[None, :, None, :])
    p_cg = p_c.reshape(B, T, G, NSA_GROUP, ncmp)
    o_cmp = jnp.einsum('btgrn,bngd->btgrd', p_cg.astype(vcmp.dtype), vcmp).reshape(B, T, H, HD)
    nsel = -(-L // NSA_SEL_BLOCK)
    ii = np.arange(ncmp)[:, None]
    jj_np = np.arange(nsel)[None, :]
    overlap = ((ii * NSA_CMP_STRIDE < (jj_np + 1) * NSA_SEL_BLOCK) & (ii * NSA_CMP_STRIDE + NSA_CMP_LEN > jj_np * NSA_SEL_BLOCK)).astype(np.float32)
    imp = jnp.einsum('btgrn,nj->btgj', p_cg, jnp.asarray(overlap))
    cur = q_pos // NSA_SEL_BLOCK
    jj = jnp.arange(nsel)[None, :]
    causal_blk = jj <= cur[:, None]
    forced = (jj == 0) | (jj == cur[:, None]) | (jj == cur[:, None] - 1)
    imp = jnp.where(forced[None, :, None, :], jnp.inf, jnp.where(causal_blk[None, :, None, :], imp, -jnp.inf))
    _, sel = lax.top_k(imp, min(NSA_TOPN, nsel))
    chosen = jnp.any(sel[..., None] == jnp.arange(nsel), axis=-2)
    selT = jnp.transpose(chosen, (0, 2, 3, 1)).astype(_F32)
    qT = heads_T(q_flat, H) * scale
    flat = lambda a: a.reshape(B, L, G * HD)
    o_sel = from_heads_T(attention_T('sel', qT, heads_K(flat(nsa_new[:, :, 2]), G), heads_VT(flat(nsa_new[:, :, 3]), G), bias_tiles, selT))
    o_win = from_heads_T(attention_T('win', qT, heads_K(flat(nsa_new[:, :, 4]), G), heads_VT(flat(nsa_new[:, :, 5]), G), bias_tiles))
    g = jax.nn.sigmoid(gates_flat.reshape(B, T, H, 3).astype(jnp.float32))
    out = g[..., 0:1] * o_cmp + g[..., 1:2] * o_sel.reshape(B, T, H, HD) + g[..., 2:3] * o_win.reshape(B, T, H, HD)
    return out.reshape(B, T, H * HD)


def token_mix(z, gz, past, lp, tab):
    B, T, _ = z.shape
    P = 0 if past is None else past['nsa'].shape[1]
    q_pos = P + jnp.arange(T, dtype=jnp.int32)
    qa, ka, va, q_nsa, nsa_kv, nsa_g, qc, kc, vc, qi, ki, wi, u, vg = split_last(z, IN_SPLITS)
    moba_new = jnp.stack([ka, va], axis=2).reshape(B, T, 2, N_HEADS, HD)
    nsa_new = nsa_kv.reshape(B, T, 6, NSA_KV_HEADS, HD)
    nsa_main_new = nsa_new[:, :, :4]
    win_new = nsa_new[:, :, 4:]
    dsa_new = jnp.stack([kc, vc], axis=2).reshape(B, T, 2, N_HEADS, HD)
    if past is None:
        nsa_all, win_all = nsa_main_new, win_new
    else:
        nsa_all = jnp.concatenate([past['nsa'], nsa_main_new], axis=1)
        win_all = jnp.concatenate([past['win'], win_new], axis=1)
    kw_pos0 = P + T - win_all.shape[1]
    scale = HD ** -0.5
    if past is None:
        bt = lp['bias_tiles']
        qaT = heads_T(qa, N_HEADS) * scale
        y_a = from_heads_T(attention_T('moba', qaT, heads_K(ka, N_HEADS), heads_VT(va, N_HEADS), bt[0], moba_gate(qaT, heads_K(ka, N_HEADS))))
        y_b = nsa_attention_prompt(q_nsa, nsa_g, nsa_new, q_pos, lp['cmp_w1'], lp['cmp_w2'], lp['cmp_pe'], tab[:, N_HEADS:2 * N_HEADS], bt[1])
        qiT = jnp.transpose(qi.reshape(B, T, DSA_IDX_HEADS, DSA_IDX_DIM), (0, 2, 3, 1))
        keep = dsa_mask(ki, qiT, jnp.transpose(wi, (0, 2, 1)), min(DSA_TOPK, T // 4))
        y_c = from_heads_T(attention_T('dsa', heads_T(qc, N_HEADS) * scale, heads_K(kc, N_HEADS), heads_VT(vc, N_HEADS), bt[2], keep))
    else:
        assert P % MOBA_BLOCK == 0 and T <= ROWS_T
        pt = past['pt']
        n_pages = pt.shape[1]
        far_a, near_a = decode_bias(tab[:, :N_HEADS], P, n_pages)
        qa_rows = (decode_rows(qa.reshape(B, T, N_HEADS, HD)) * scale).astype(_BF16)
        new_a = pad_rows(jnp.concatenate([ka, va], axis=-1), PAGE_SIZE)
        gs = moba_decode_gate(pt, qa_rows, past['moba_rows'])
        y_a = from_decode_rows(paged_attention('moba', pt, qa_rows, past['moba_rows'], new_a, far_a, near_a, gs), T)
        y_b = nsa_attention(q_nsa.reshape(B, T, N_HEADS, HD), nsa_g.reshape(B, T, N_HEADS, 3), nsa_all[:, :, 0], nsa_all[:, :, 1], nsa_all[:, :, 2], nsa_all[:, :, 3], win_all[:, :, 0], win_all[:, :, 1], kw_pos0, q_pos, lp['cmp_w1'], lp['cmp_w2'], lp['cmp_pe'], tab[:, N_HEADS:2 * N_HEADS])
        qi_rows = pad_rows(jnp.transpose(qi.reshape(B, T, DSA_IDX_HEADS, DSA_IDX_DIM), (0, 2, 1, 3)).reshape(B * DSA_IDX_HEADS, T, DSA_IDX_DIM), ROWS_T)
        qi_rows = qi_rows.reshape(B, DSA_IDX_HEADS, ROWS_T, DSA_IDX_DIM).astype(_BF16)
        wi_rows = pad_rows(jnp.transpose(wi, (0, 2, 1)).reshape(B * DSA_IDX_HEADS, T), ROWS_T).reshape(B, DSA_IDX_HEADS, ROWS_T, 1)
        score = dsa_decode_scores(pt, qi_rows, wi_rows, past['idx_rows'], pad_rows(ki, PAGE_SIZE))
        keep = dsa_decode_select(score, min(DSA_TOPK, (P + T) // 4), P)
        keep = jnp.transpose(keep.reshape(B, ROWS_T, n_pages + PP, PAGE_SIZE), (0, 2, 1, 3))
        far_c, near_c = decode_bias(tab[:, 2 * N_HEADS:], P, n_pages)
        qc_rows = (decode_rows(qc.reshape(B, T, N_HEADS, HD)) * scale).astype(_BF16)
        new_c = pad_rows(jnp.concatenate([kc, vc], axis=-1), PAGE_SIZE)
        y_c = from_decode_rows(paged_attention('dsa', pt, qc_rows, past['dsa_rows'], new_c, far_c, near_c, keep), T)
    y_d, v_rows = gmlp_sgu(u, vg, lp['gm_ws'], lp['gm_b'], lp['gm_norm'])
    branches = jnp.stack([y_a.reshape(B, T, MIX_W), y_b.reshape(B, T, MIX_W), y_c.reshape(B, T, MIX_W), y_d], axis=2)
    proj = jnp.einsum('btnc,ncd->btnd', branches, lp['w_branch'])
    gate = jax.nn.sigmoid(gz.reshape(B, T, N_BRANCH, D_MODEL) + lp['b_gate'])
    out = jnp.sum(gate * proj, axis=2) @ lp['w_out']
    win_keep = win_all[:, -min(NSA_WINDOW, P + T):]
    return out, (moba_new, nsa_main_new, win_keep, dsa_new, ki, v_rows)


def mem_attention_q(q, mkv, wo):
    B, T, _ = q.shape
    q = q.reshape(B, T, MEM_HEADS, MEM_HD)
    lg = jnp.einsum('bthd,bmhd->bthm', q, mkv[:, :, 0]).astype(jnp.float32) * MEM_HD ** -0.5
    p = jax.nn.softmax(lg, axis=-1)
    o = jnp.einsum('bthm,bmhd->bthd', p.astype(mkv.dtype), mkv[:, :, 1]).reshape(B, T, MEM_W)
    return o @ wo


def layer_pair(rows, shape_p, shape_s, past, mkv_p, mkv_s, lp, tab):
    n_p = shape_p[0] * shape_p[1]
    split = lambda a: (a[:n_p].reshape(shape_p[0], shape_p[1], -1), a[n_p:].reshape(shape_s[0], shape_s[1], -1))
    z_p, z_s = split(norm_matmul(rows, lp['norm_mix'], lp['w_in']))
    gz_p, gz_s = split(norm_matmul(rows, lp['norm_mix'], lp['w_gate']))
    mix_p, rp = token_mix(z_p, gz_p, None, lp, tab)
    mix_s, rs = token_mix(z_s, gz_s, past, lp, tab)
    rows = rows + jnp.concatenate([mix_p.reshape(-1, D_MODEL), mix_s.reshape(-1, D_MODEL)], axis=0)
    q_p, q_s = split(norm_matmul(rows, lp['norm_mem'], lp['w_mem_q']))
    mem_p = mem_attention_q(q_p, mkv_p, lp['w_mem_o'])
    mem_s = mem_attention_q(q_s, mkv_s, lp['w_mem_o'])
    rows = rows + jnp.concatenate([mem_p.reshape(-1, D_MODEL), mem_s.reshape(-1, D_MODEL)], axis=0)
    rows = peer_ffn_tokens(rows, lp['norm_ffn'], lp['peer_wq'], lp['peer_keys'], lp['peer_u'], lp['peer_v'])
    return rows, rp, rs


def kernel(x_prompt, x_sample, cache_moba_kv, cache_nsa_kv, cache_dsa_kv, cache_dsa_idx, state_nsa_win, cache_mem_kv, page_table, mem_prompt, rel_bias_table, w_in, nsa_cmp_w1, nsa_cmp_w2, nsa_cmp_pe, gm_ws, gm_b, gm_norm, w_branch, w_gate, b_gate, w_out, w_mem_q, w_mem_kv, w_mem_o, peer_wq, peer_keys, peer_u, peer_v, norm_mix, norm_mem, norm_ffn, norm_final):
    n_p = x_prompt.shape[0] * x_prompt.shape[1]
    rows = jnp.concatenate([x_prompt.reshape(-1, D_MODEL), x_sample.reshape(-1, D_MODEL)], axis=0)
    rows_p = []
    rows_s = []
    mem_rows = []
    Bp = x_prompt.shape[0]
    n_pool = cache_moba_kv.shape[1]
    bias_tiles =[rel_bias_tiles(rel_bias_table[:, n * N_HEADS:(n + 1) * N_HEADS]) for n in range(3)]
    for l in range(DEPTH):
        lp = {
            'w_in': w_in[l], 'cmp_w1': nsa_cmp_w1[l], 'cmp_w2': nsa_cmp_w2[l], 'cmp_pe': nsa_cmp_pe[l],
            'gm_ws': gm_ws[l], 'gm_b': gm_b[l], 'gm_norm': gm_norm[l],
            'w_branch': w_branch[l], 'w_gate': w_gate[l], 'b_gate': b_gate[l], 'w_out': w_out[l],
            'w_mem_q': w_mem_q[l], 'w_mem_o': w_mem_o[l],
            'peer_wq': peer_wq[l], 'peer_keys': peer_keys[l], 'peer_u': peer_u[l], 'peer_v': peer_v[l],
            'norm_mix': norm_mix[l], 'norm_mem': norm_mem[l], 'norm_ffn': norm_ffn[l],
        }
        mkv_p = (mem_prompt @ w_mem_kv[l]).reshape(Bp, MEM_TOKENS, 2, MEM_HEADS, MEM_HD)
        past = {
            'pt': page_table + l * n_pool,
            'moba_rows': cache_moba_kv.reshape(DEPTH * n_pool, PAGE_SIZE, 2 * MIX_W),
            'dsa_rows': cache_dsa_kv.reshape(DEPTH * n_pool, PAGE_SIZE, 2 * MIX_W),
            'idx_rows': cache_dsa_idx.reshape(DEPTH * n_pool, PAGE_SIZE, DSA_IDX_DIM),
            'nsa': gather_pages(cache_nsa_kv[l], page_table),
            'win': state_nsa_win[l],
        }
        lp['bias_tiles'] = bias_tiles
        rows, rp, rs = layer_pair(rows, x_prompt.shape, x_sample.shape, past, mkv_p, cache_mem_kv[l], lp, rel_bias_table)
        rows_p.append(rp)
        rows_s.append(rs)
        mem_rows.append(mkv_p)
    y_rows = rmsnorm(rows, norm_final)
    y_prompt = y_rows[:n_p].reshape(x_prompt.shape)
    y_sample = y_rows[n_p:].reshape(x_sample.shape)
    new_moba_kv_prompt = jnp.stack([r[0] for r in rows_p])
    new_moba_kv_sample = jnp.stack([r[0] for r in rows_s])
    new_nsa_kv_prompt = jnp.stack([r[1] for r in rows_p])
    new_nsa_kv_sample = jnp.stack([r[1] for r in rows_s])
    new_nsa_win_prompt = jnp.stack([r[2] for r in rows_p])
    new_nsa_win_sample = jnp.stack([r[2] for r in rows_s])
    new_dsa_kv_prompt = jnp.stack([r[3] for r in rows_p])
    new_dsa_kv_sample = jnp.stack([r[3] for r in rows_s])
    new_dsa_idx_prompt = jnp.stack([r[4] for r in rows_p])
    new_dsa_idx_sample = jnp.stack([r[4] for r in rows_s])
    new_mem_kv_prompt = jnp.stack(mem_rows)
    new_gmlp_v_sample = jnp.stack([r[5] for r in rows_s])
    return (y_prompt, y_sample, new_moba_kv_prompt, new_moba_kv_sample, new_nsa_kv_prompt, new_nsa_kv_sample, new_nsa_win_prompt, new_nsa_win_sample, new_dsa_kv_prompt, new_dsa_kv_sample, new_dsa_idx_prompt, new_dsa_idx_sample, new_mem_kv_prompt, new_gmlp_v_sample)
```

```python
import functools
import math
import jax
import jax.numpy as jnp
from jax import lax
import numpy as np
from jax.experimental import pallas as pl
from jax.experimental.pallas import tpu as pltpu

D_MODEL = 2048
BATCH = 4
SEQ = 2048
DEPTH = 2
DEC_BATCH = 8
DEC_SEQ = 4
PAST_LEN = 16384
PAGE_SIZE = 128

HD = 64
N_HEADS = 8
MIX_W = N_HEADS * HD
N_BRANCH = 4
MOBA_BLOCK = 256
MOBA_TOPK = 3
MOBA_QB = 32
NSA_KV_HEADS = 2
NSA_GROUP = N_HEADS // NSA_KV_HEADS
NSA_CMP_STRIDE = 16
NSA_CMP_LEN = 2 * NSA_CMP_STRIDE
NSA_SEL_BLOCK = 64
NSA_TOPN = 16
NSA_WINDOW = 512
NSA_QB = 64
WIN_QB = 128
DSA_IDX_HEADS = 4
DSA_IDX_DIM = 64
DSA_TOPK = 256
DSA_QB = 64
GM_CHUNK = 128
GM_GROUPS = 8
GM_W = 512
REL_BUCKETS = 32
REL_MAX_DIST = 128
N_REL_HEADS = 3 * N_HEADS
MEM_TOKENS = 256
MEM_HEADS = 4
MEM_HD = 128
MEM_W = MEM_HEADS * MEM_HD
PEER_HEADS = 8
PEER_NKEYS = 128
PEER_QDIM = 128
PEER_TOPK = 16
PEER_EXPERTS = PEER_NKEYS ** 2
PEER_BLOCK = 128
NORM_EPS = 1e-6
IN_SPLITS = (MIX_W, MIX_W, MIX_W, MIX_W, 6 * NSA_KV_HEADS * HD, 3 * N_HEADS, MIX_W, MIX_W, MIX_W, DSA_IDX_HEADS * DSA_IDX_DIM, DSA_IDX_DIM, DSA_IDX_HEADS, GM_W, GM_W)
IN_COLS = sum(IN_SPLITS)


def rmsnorm(x, g):
    xf = x.astype(jnp.float32)
    y = xf * lax.rsqrt(jnp.mean(xf * xf, axis=-1, keepdims=True) + NORM_EPS)
    return (y * g.astype(jnp.float32)).astype(x.dtype)


def masked_softmax(logits, mask):
    l = jnp.where(mask, logits.astype(jnp.float32), -jnp.inf)
    m = jnp.max(l, axis=-1, keepdims=True)
    m = jnp.where(jnp.isfinite(m), m, 0.0)
    e = jnp.exp(l - m)
    return e / jnp.maximum(jnp.sum(e, axis=-1, keepdims=True), 1e-30)


def rel_bucket(dist):
    n = jnp.maximum(dist, 0)
    max_exact = REL_BUCKETS // 2
    nf = jnp.maximum(n, 1).astype(jnp.float32)
    large = max_exact + (jnp.log(nf / max_exact) / math.log(REL_MAX_DIST / max_exact) * (REL_BUCKETS - max_exact)).astype(jnp.int32)
    return jnp.where(n < max_exact, n, jnp.minimum(large, REL_BUCKETS - 1))


def rel_bias(dist, tab):
    onehot = (rel_bucket(dist)[..., None] == jnp.arange(REL_BUCKETS)).astype(jnp.float32)
    spec = '...okb,bh->...hk' if dist.shape[-2] == 1 else '...hkb,bh->...hk'
    return jnp.einsum(spec, onehot, tab.astype(jnp.float32), precision=lax.Precision.HIGHEST)


def split_last(z, sizes):
    return jnp.split(z, np.cumsum(sizes)[:-1].tolist(), axis=-1)


def map_query_blocks(fn, qb, q_pos, *xs):
    T = q_pos.shape[0]
    qb = qb if T % qb == 0 else T
    nb = T // qb
    xb = tuple(jnp.moveaxis(x.reshape(x.shape[0], nb, qb, *x.shape[2:]), 1, 0) for x in xs)
    out = lax.map(lambda a: fn(*a), (q_pos.reshape(nb, qb),) + xb)
    return jnp.moveaxis(out, 0, 1).reshape(out.shape[1], T, *out.shape[3:])


def gather_pages(pool, page_table):
    g = pool[page_table]
    return g.reshape(g.shape[0], g.shape[1] * g.shape[2], *g.shape[3:])


def moba_attention(q, k, v, q_pos, tab):
    B, L, H, _ = k.shape
    nblk = -(-L // MOBA_BLOCK)
    pad = ((0, 0), (0, nblk * MOBA_BLOCK - L), (0, 0), (0, 0))
    kb = jnp.pad(k, pad).reshape(B, nblk, MOBA_BLOCK, H, HD)
    vb = jnp.pad(v, pad).reshape(B, nblk, MOBA_BLOCK, H, HD)
    k_mean = jnp.mean(kb.astype(jnp.float32), axis=2).astype(k.dtype)
    kb_t = jnp.transpose(kb, (0, 3, 1, 2, 4))
    vb_t = jnp.transpose(vb, (0, 3, 1, 2, 4))
    ntop = min(MOBA_TOPK, nblk)
    bi = jnp.arange(B)[:, None, None, None]
    hi = jnp.arange(H)[None, None, :, None]
    off = jnp.arange(MOBA_BLOCK)
    scale = HD ** -0.5

    def block_fn(qp, qq):
        qb = qp.shape[0]
        own = qp // MOBA_BLOCK
        gs = jnp.einsum('bqhd,bjhd->bqhj', qq, k_mean).astype(jnp.float32)
        fully_past = jnp.arange(nblk)[None, :] < own[:, None]
        gs = jnp.where(fully_past[None, :, None, :], gs, -jnp.inf)
        _, top = lax.top_k(gs, ntop)
        own_b = jnp.broadcast_to(own[None, :, None, None], (B, qb, H, 1))
        sel = jnp.concatenate([top, own_b], axis=-1)
        sel_ok = jnp.concatenate([top < own[None, :, None, None], jnp.ones(own_b.shape, bool)], axis=-1)
        kg = kb_t[bi, hi, sel].reshape(B, qb, H, (ntop + 1) * MOBA_BLOCK, HD)
        vg = vb_t[bi, hi, sel].reshape(B, qb, H, (ntop + 1) * MOBA_BLOCK, HD)
        pos = (sel[..., None] * MOBA_BLOCK + off).reshape(B, qb, H, -1)
        dist = qp[None, :, None, None] - pos
        ok = jnp.repeat(sel_ok, MOBA_BLOCK, axis=-1) & (dist >= 0)
        lg = jnp.einsum('bqhd,bqhkd->bqhk', qq, kg).astype(jnp.float32) * scale + rel_bias(dist, tab)
        p = masked_softmax(lg, ok)
        return jnp.einsum('bqhk,bqhkd->bqhd', p.astype(vg.dtype), vg)

    return map_query_blocks(block_fn, MOBA_QB, q_pos, q)


def nsa_attention(q, gates, kc, vc, ks, vs, kw, vw, kw_pos0, q_pos, w1, w2, pe, tab):
    B, L, G, _ = kc.shape
    T = q.shape[1]
    H = N_HEADS
    scale = HD ** -0.5
    nch = L // NSA_CMP_STRIDE
    ncmp = nch - 1

    def compress(x, i):
        c = x[:, :nch * NSA_CMP_STRIDE].reshape(B, nch, NSA_CMP_STRIDE, G, HD)
        blk = jnp.concatenate([c[:, :-1], c[:, 1:]], axis=2) + pe[i][None, None, :, None, :]
        blk = jnp.transpose(blk, (0, 1, 3, 2, 4)).reshape(B, ncmp, G, NSA_CMP_LEN * HD)
        return jax.nn.gelu(blk @ w1[i]) @ w2[i]

    kcmp = compress(kc, 0)
    vcmp = compress(vc, 1)
    cmp_end = jnp.arange(ncmp) * NSA_CMP_STRIDE + NSA_CMP_LEN - 1
    qg = q.reshape(B, T, G, NSA_GROUP, HD)
    dist_c = q_pos[:, None] - cmp_end[None, :]
    lg_c = jnp.einsum('btgrd,bngd->btgrn', qg, kcmp).astype(jnp.float32).reshape(B, T, H, ncmp) * scale
    lg_c = lg_c + rel_bias(dist_c[None, :, None, :], tab)
    p_c = masked_softmax(lg_c, (dist_c >= 0)[None, :, None, :])
    p_cg = p_c.reshape(B, T, G, NSA_GROUP, ncmp)
    o_cmp = jnp.einsum('btgrn,bngd->btgrd', p_cg.astype(vcmp.dtype), vcmp).reshape(B, T, H, HD)
    nsel = -(-L // NSA_SEL_BLOCK)
    ii = np.arange(ncmp)[:, None]
    jj_np = np.arange(nsel)[None, :]
    overlap = ((ii * NSA_CMP_STRIDE < (jj_np + 1) * NSA_SEL_BLOCK) & (ii * NSA_CMP_STRIDE + NSA_CMP_LEN > jj_np * NSA_SEL_BLOCK)).astype(np.float32)
    imp = jnp.einsum('btgrn,nj->btgj', p_cg, jnp.asarray(overlap))
    cur = q_pos // NSA_SEL_BLOCK
    jj = jnp.arange(nsel)[None, :]
    causal_blk = jj <= cur[:, None]
    forced = (jj == 0) | (jj == cur[:, None]) | (jj == cur[:, None] - 1)
    imp = jnp.where(forced[None, :, None, :], jnp.inf, jnp.where(causal_blk[None, :, None, :], imp, -jnp.inf))
    ntop = min(NSA_TOPN, nsel)
    _, sel = lax.top_k(imp, ntop)
    pad = ((0, 0), (0, nsel * NSA_SEL_BLOCK - L), (0, 0), (0, 0))
    ksb = jnp.transpose(jnp.pad(ks, pad).reshape(B, nsel, NSA_SEL_BLOCK, G, HD), (0, 3, 1, 2, 4))
    vsb = jnp.transpose(jnp.pad(vs, pad).reshape(B, nsel, NSA_SEL_BLOCK, G, HD), (0, 3, 1, 2, 4))
    bi = jnp.arange(B)[:, None, None, None]
    gi = jnp.arange(G)[None, None, :, None]
    off = jnp.arange(NSA_SEL_BLOCK)

    def sel_fn(qp, qq, ss):
        qb = qp.shape[0]
        kg = ksb[bi, gi, ss].reshape(B, qb, G, ntop * NSA_SEL_BLOCK, HD)
        vg = vsb[bi, gi, ss].reshape(B, qb, G, ntop * NSA_SEL_BLOCK, HD)
        pos = (ss[..., None] * NSA_SEL_BLOCK + off).reshape(B, qb, G, -1)
        dist = jnp.repeat(qp[None, :, None, None] - pos, NSA_GROUP, axis=2)
        lg = jnp.einsum('bqgrd,bqgkd->bqgrk', qq.reshape(B, qb, G, NSA_GROUP, HD), kg).astype(jnp.float32).reshape(B, qb, H, -1) * scale
        p = masked_softmax(lg + rel_bias(dist, tab), dist >= 0)
        o = jnp.einsum('bqgrk,bqgkd->bqgrd', p.reshape(B, qb, G, NSA_GROUP, -1).astype(vg.dtype), vg)
        return o.reshape(B, qb, H, HD)

    o_sel = map_query_blocks(sel_fn, NSA_QB, q_pos, q, sel)
    W = NSA_WINDOW
    wpad = ((0, 0), (W, 0), (0, 0), (0, 0))
    kwp = jnp.pad(kw, wpad)
    vwp = jnp.pad(vw, wpad)

    def win_fn(qp, qq):
        qb = qp.shape[0]
        span = W + qb - 1
        start = qp[0] - kw_pos0 + 1
        kk = lax.dynamic_slice_in_dim(kwp, start, span, axis=1)
        vv = lax.dynamic_slice_in_dim(vwp, start, span, axis=1)
        pos = qp[0] - W + 1 + jnp.arange(span)
        dist = qp[:, None] - pos[None, :]
        ok = (dist >= 0) & (dist < W) & (pos[None, :] >= 0)
        lg = jnp.einsum('bqgrd,bkgd->bqgrk', qq.reshape(B, qb, G, NSA_GROUP, HD), kk).astype(jnp.float32).reshape(B, qb, H, span) * scale
        p = masked_softmax(lg + rel_bias(dist[None, :, None, :], tab), ok[None, :, None, :])
        o = jnp.einsum('bqgrk,bkgd->bqgrd', p.reshape(B, qb, G, NSA_GROUP, span).astype(vv.dtype), vv)
        return o.reshape(B, qb, H, HD)

    o_win = map_query_blocks(win_fn, WIN_QB, q_pos, q)
    g = jax.nn.sigmoid(gates.astype(jnp.float32)).astype(q.dtype)
    return g[..., 0:1] * o_cmp + g[..., 1:2] * o_sel + g[..., 2:3] * o_win


def dsa_attention(q, qi, wi, k, v, ki, q_pos, tab):
    B, L, H, _ = k.shape
    kk = min(DSA_TOPK, L // 4)
    bi = jnp.arange(B)[:, None, None]
    scale = HD ** -0.5

    def block_fn(qp, qq, qqi, wwi):
        s = jax.nn.relu(jnp.einsum('bqhd,bld->bqhl', qqi, ki).astype(jnp.float32) * DSA_IDX_DIM ** -0.5)
        score = jnp.einsum('bqh,bqhl->bql', wwi.astype(jnp.float32), s)
        admissible = jnp.arange(L)[None, :] <= qp[:, None]
        score = jnp.where(admissible[None], score, -jnp.inf)
        _, sel = lax.top_k(score, kk)
        kg = k[bi, sel]
        vg = v[bi, sel]
        dist = (qp[None, :, None] - sel)[:, :, None, :]
        lg = jnp.einsum('bqhd,bqkhd->bqhk', qq, kg).astype(jnp.float32) * scale + rel_bias(dist, tab)
        p = masked_softmax(lg, dist >= 0)
        return jnp.einsum('bqhk,bqkhd->bqhd', p.astype(vg.dtype), vg)

    return map_query_blocks(block_fn, DSA_QB, q_pos, q, qi, wi)


def gmlp_sgu(u, v, ws, b, gnorm):
    B, T, _ = u.shape
    vn = rmsnorm(v, gnorm)
    nc = -(-T // GM_CHUNK)
    vc = jnp.pad(vn, ((0, 0), (0, nc * GM_CHUNK - T), (0, 0))).reshape(B, nc, GM_CHUNK, GM_GROUPS, GM_W // GM_GROUPS)
    wsm = ws * jnp.tril(jnp.ones((GM_CHUNK, GM_CHUNK), ws.dtype))
    mixed = jnp.einsum('gij,bcjgd->bcigd', wsm, vc) + b.T[None, None, :, :, None]
    mixed = mixed.reshape(B, nc * GM_CHUNK, GM_W)[:, :T]
    return u * mixed, vn


def mem_attention(h, mkv, wq, wo):
    B, T, _ = h.shape
    q = (h @ wq).reshape(B, T, MEM_HEADS, MEM_HD)
    lg = jnp.einsum('bthd,bmhd->bthm', q, mkv[:, :, 0]).astype(jnp.float32) * MEM_HD ** -0.5
    p = jax.nn.softmax(lg, axis=-1)
    o = jnp.einsum('bthm,bmhd->bthd', p.astype(mkv.dtype), mkv[:, :, 1]).reshape(B, T, MEM_W)
    return o @ wo


def peer_ffn(h, wq, keys, U, V):
    B, T, D = h.shape
    N = B * T
    nb = -(-N // PEER_BLOCK)
    hf = jnp.pad(h.reshape(N, D), ((0, nb * PEER_BLOCK - N), (0, 0))).reshape(nb, PEER_BLOCK, D)

    def block_fn(x):
        q = (x @ wq).reshape(PEER_BLOCK, PEER_HEADS, 2, PEER_QDIM // 2)
        s = jnp.einsum('nhcd,hckd->nhck', q, keys).astype(jnp.float32)
        v1, i1 = lax.top_k(s[:, :, 0], PEER_TOPK)
        v2, i2 = lax.top_k(s[:, :, 1], PEER_TOPK)
        cand = (v1[..., :, None] + v2[..., None, :]).reshape(PEER_BLOCK, PEER_HEADS, PEER_TOPK * PEER_TOPK)
        sv, ci = lax.top_k(cand, PEER_TOPK)
        e = jnp.take_along_axis(i1, ci // PEER_TOPK, axis=-1) * PEER_NKEYS + jnp.take_along_axis(i2, ci % PEER_TOPK, axis=-1)
        g = jax.nn.softmax(sv, axis=-1)
        ug = U[e]
        vg = V[e]
        act = jax.nn.gelu(jnp.einsum('nd,nhkd->nhk', x, ug).astype(jnp.float32))
        return jnp.einsum('nhk,nhkd->nd', (g * act).astype(x.dtype), vg)

    out = lax.map(block_fn, hf).reshape(nb * PEER_BLOCK, D)[:N]
    return out.reshape(B, T, D)


def _norm_matmul_kernel(x_ref, g_ref, w_ref, o_ref, xn_ref):
    @pl.when(pl.program_id(1) == 0)
    def _():
        x = x_ref[...]
        y = x * lax.rsqrt(jnp.mean(x * x, axis=-1, keepdims=True) + NORM_EPS)
        xn_ref[...] = (y * g_ref[...]).astype(jnp.bfloat16)

    hw = w_ref.shape[1] // 2
    for s in range(2):
        cols = slice(s * hw, (s + 1) * hw)
        o_ref[:, cols] = jnp.dot(xn_ref[...], w_ref[:, cols].astype(jnp.bfloat16), preferred_element_type=jnp.float32)


def norm_matmul(x, g, w, tm=512, tn=512):
    N, D = x.shape
    C = w.shape[1]
    tm = min(tm, N)
    tn = min(tn, C)
    return pl.pallas_call(
        _norm_matmul_kernel,
        grid=(pl.cdiv(N, tm), pl.cdiv(C, tn)),
        in_specs=[
            pl.BlockSpec((tm, D), lambda i, j: (i, 0)),
            pl.BlockSpec((1, D), lambda i, j: (0, 0)),
            pl.BlockSpec((D, tn), lambda i, j: (0, j)),
        ],
        out_specs=pl.BlockSpec((tm, tn), lambda i, j: (i, j)),
        out_shape=jax.ShapeDtypeStruct((N, C), jnp.float32),
        scratch_shapes=[pltpu.VMEM((tm, D), jnp.bfloat16)],
        compiler_params=pltpu.CompilerParams(
            dimension_semantics=("arbitrary", "arbitrary"), vmem_limit_bytes=48 * 1024 * 1024),
        name="norm_matmul",
    )(x, g.reshape(1, D), w)


TQ = 256
TK = 256
NEG_BIG = -1e30
VMEM_LIMIT = 48 * 1024 * 1024
_BF16 = jnp.bfloat16
_F32 = jnp.float32


HEADS_PER_STEP = 2


def _attn_kernel(mode, shared_kv, qT_ref, k_ref, vT_ref, bias_ref, *rest):
    if mode == 'win':
        (o_ref,) = rest
        m_ref = None
    else:
        m_ref, o_ref = rest
    qi = pl.program_id(2)
    heads = range(HEADS_PER_STEP)
    q = [qT_ref[0, hh].astype(_BF16) for hh in heads]
    lane = lax.broadcasted_iota(jnp.int32, (TK, TQ), 1)
    sub = lax.broadcasted_iota(jnp.int32, (TK, TQ), 0)
    rel0 = lane - sub

    def body(kj, carry):
        dist = rel0 + (qi - kj) * TQ
        ok_all = dist >= 0
        if mode == 'sel':
            rows = m_ref[0, 0, kj]
            blk = sub // NSA_SEL_BLOCK
            r = jnp.where(blk == 0, rows[0:1], jnp.where(blk == 1, rows[1:2], jnp.where(blk == 2, rows[2:3], rows[3:4])))
            ok_all = ok_all & (r > 0.5)
        elif mode == 'dsa':
            ok_all = ok_all & (m_ref[0, pl.ds(pl.multiple_of(kj * TK, TK), TK), :].astype(_F32) > 0.5)
        elif mode == 'win':
            ok_all = ok_all & (dist < NSA_WINDOW)
        out = []
        for hh in heads:
            m, l, acc = carry[hh]
            hk = 0 if shared_kv else hh
            k_t = k_ref[0, hk, pl.ds(pl.multiple_of(kj * TK, TK), TK), :].astype(_BF16)
            s = jnp.dot(k_t, q[hh], preferred_element_type=_F32)
            s = s + bias_ref[hh, jnp.minimum(qi - kj, 2)]
            ok = ok_all & (m_ref[0, hh, kj] > 0.5) if mode == 'moba' else ok_all
            s = jnp.where(ok, s, NEG_BIG)
            m_new = jnp.maximum(m, jnp.max(s, axis=0, keepdims=True))
            p = jnp.where(ok, jnp.exp(s - m_new), 0.0)
            alpha = jnp.exp(m - m_new)
            l = alpha * l + jnp.sum(p, axis=0, keepdims=True)
            v_t = vT_ref[0, hk, kj].astype(_BF16)
            acc = alpha * acc + jnp.dot(v_t, p.astype(_BF16), preferred_element_type=_F32)
            out.append((m_new, l, acc))
        return tuple(out)

    lo = jnp.maximum(qi - (NSA_WINDOW // TK), 0) if mode == 'win' else 0
    init = tuple((jnp.full((1, TQ), NEG_BIG, _F32), jnp.zeros((1, TQ), _F32), jnp.zeros((HD, TQ), _F32)) for _ in heads)
    res = lax.fori_loop(lo, qi + 1, body, init)
    for hh in heads:
        m, l, acc = res[hh]
        o_ref[0, hh] = acc / jnp.maximum(l, 1e-30)


def attention_T(mode, qT, k, vT, bias_tiles, mask=None):
    B, H, _, T = qT.shape
    Hkv, L = k.shape[1], k.shape[2]
    grp = H // Hkv
    hp = HEADS_PER_STEP
    assert H % hp == 0 and (grp == 1 or grp % hp == 0)
    shared_kv = grp > 1
    kvh = 1 if shared_kv else hp
    kv_idx = (lambda g: g * hp // grp) if shared_kv else (lambda g: g)
    in_specs = [
        pl.BlockSpec((1, hp, HD, TQ), lambda b, g, i: (b, g, 0, i)),
        pl.BlockSpec((1, kvh, L, HD), lambda b, g, i: (b, kv_idx(g), 0, 0)),
        pl.BlockSpec((1, kvh, L // TK, HD, TK), lambda b, g, i: (b, kv_idx(g), 0, 0, 0)),
        pl.BlockSpec((hp, 3, TK, TQ), lambda b, g, i: (g, 0, 0, 0)),
    ]
    args = [qT, k, vT, bias_tiles]
    if mode == 'moba':
        in_specs.append(pl.BlockSpec((1, hp, L // TK, 1, TQ), lambda b, g, i: (b, g, 0, 0, i)))
        args.append(mask.reshape(B, H, L // TK, 1, T))
    elif mode == 'sel':
        per = TK // NSA_SEL_BLOCK
        in_specs.append(pl.BlockSpec((1, 1, L // TK, per, TQ), lambda b, g, i: (b, kv_idx(g), 0, 0, i)))
        args.append(mask.reshape(B, Hkv, L // TK, per, T))
    elif mode == 'dsa':
        in_specs.append(pl.BlockSpec((1, L, TQ), lambda b, g, i: (b, 0, i)))
        args.append(mask)
    return pl.pallas_call(
        functools.partial(_attn_kernel, mode, shared_kv),
        grid=(B, H // hp, T // TQ),
        in_specs=in_specs,
        out_specs=pl.BlockSpec((1, hp, HD, TQ), lambda b, g, i: (b, g, 0, i)),
        out_shape=jax.ShapeDtypeStruct((B, H, HD, T), _F32),
        compiler_params=pltpu.CompilerParams(
            dimension_semantics=("arbitrary", "arbitrary", "arbitrary"), vmem_limit_bytes=VMEM_LIMIT),
        name="attn_" + mode,
    )(*args)


def _moba_gate_kernel(qT_ref, k_ref, sel_ref):
    nblk, T = sel_ref.shape[2], sel_ref.shape[3]
    q = qT_ref[0, 0].astype(_BF16)
    k = k_ref[0, 0]
    km = jnp.mean(k.reshape(nblk, MOBA_BLOCK, HD), axis=1)
    gs = jnp.dot(km.astype(_BF16), q, preferred_element_type=_F32)
    j = lax.broadcasted_iota(jnp.int32, (nblk, T), 0)
    own = lax.broadcasted_iota(jnp.int32, (nblk, T), 1) // MOBA_BLOCK
    fully_past = j < own
    gs = jnp.where(fully_past, gs, -jnp.inf)
    rank = jnp.zeros((nblk, T), jnp.int32)
    for jp in range(nblk):
        row = gs[jp:jp + 1, :]
        ahead = (row > gs) | ((row == gs) & (jp < j))
        rank = rank + ahead.astype(jnp.int32)
    sel = (fully_past & (rank < MOBA_TOPK)) | (j == own)
    sel_ref[0, 0] = sel.astype(_F32)


def moba_gate(qT, k):
    B, H, _, T = qT.shape
    L = k.shape[2]
    nblk = L // MOBA_BLOCK
    return pl.pallas_call(
        _moba_gate_kernel,
        grid=(B, H),
        in_specs=[pl.BlockSpec((1, 1, HD, T), lambda b, h: (b, h, 0, 0)),
                  pl.BlockSpec((1, 1, L, HD), lambda b, h: (b, h, 0, 0))],
        out_specs=pl.BlockSpec((1, 1, nblk, T), lambda b, h: (b, h, 0, 0)),
        out_shape=jax.ShapeDtypeStruct((B, H, nblk, T), _F32),
        compiler_params=pltpu.CompilerParams(dimension_semantics=("arbitrary", "arbitrary"), vmem_limit_bytes=VMEM_LIMIT),
        name="moba_gate",
    )(qT, k)


def _count_rows(pred):
    return jnp.sum(pred.astype(_F32), axis=0, keepdims=True)


def _dsa_mask_kernel(topk, ki_ref, qiT_ref, wiT_ref, mask_ref, key_ref):
    L = ki_ref.shape[1]
    qi = pl.program_id(1)
    ki = ki_ref[0].astype(_BF16)
    score = jnp.zeros((L, TQ), _F32)
    for hh in range(DSA_IDX_HEADS):
        s = jnp.dot(ki, qiT_ref[0, hh].astype(_BF16), preferred_element_type=_F32) * DSA_IDX_DIM ** -0.5
        score = score + wiT_ref[0, hh:hh + 1, :] * jnp.maximum(s, 0.0)
    idx = lax.broadcasted_iota(jnp.int32, (L, TQ), 0)
    q_pos = qi * TQ + lax.broadcasted_iota(jnp.int32, (L, TQ), 1)
    adm = idx <= q_pos
    score = jnp.where(adm, jnp.where(score == 0.0, 0.0, score), -jnp.inf)
    bits = pltpu.bitcast(score, jnp.int32)
    key_ref[...] = jnp.where(bits < 0, bits ^ jnp.int32(0x7FFFFFFF), bits)
    kf = jnp.float32(topk)
    int_min = jnp.int32(-2 ** 31)
    lo = jnp.where(_count_rows(key_ref[...] >= 0) >= kf, jnp.int32(0), int_min)

    def vstep(i, lo):
        cand = lo + jnp.left_shift(jnp.int32(1), 30 - i)
        return jnp.where(_count_rows(key_ref[...] >= cand) >= kf, cand, lo)

    thr = lax.fori_loop(0, 31, vstep, lo)
    need = kf - _count_rows(key_ref[...] > thr)
    nbits = max(1, (L - 1).bit_length())

    def istep(i, lo):
        cand = lo + jnp.left_shift(jnp.int32(1), nbits - 1 - i)
        c = _count_rows((key_ref[...] == thr) & (idx < cand))
        return jnp.where(c < need, cand, lo)

    last_tie = lax.fori_loop(0, nbits, istep, jnp.zeros((1, TQ), jnp.int32))
    key = key_ref[...]
    keep = ((key > thr) | ((key == thr) & (idx <= last_tie))) & adm
    mask_ref[0] = keep.astype(mask_ref.dtype)


def dsa_mask(ki, qiT, wiT, topk):
    B, L, _ = ki.shape
    T = qiT.shape[3]
    return pl.pallas_call(
        functools.partial(_dsa_mask_kernel, topk),
        grid=(B, T // TQ),
        in_specs=[pl.BlockSpec((1, L, DSA_IDX_DIM), lambda b, i: (b, 0, 0)),
                  pl.BlockSpec((1, DSA_IDX_HEADS, DSA_IDX_DIM, TQ), lambda b, i: (b, 0, 0, i)),
                  pl.BlockSpec((1, DSA_IDX_HEADS, TQ), lambda b, i: (b, 0, i))],
        out_specs=pl.BlockSpec((1, L, TQ), lambda b, i: (b, 0, i)),
        out_shape=jax.ShapeDtypeStruct((B, L, T), _BF16),
        scratch_shapes=[pltpu.VMEM((L, TQ), jnp.int32)],
        compiler_params=pltpu.CompilerParams(dimension_semantics=("arbitrary", "arbitrary"), vmem_limit_bytes=VMEM_LIMIT),
        name="dsa_mask",
    )(ki, qiT, wiT)


PEER_TN = 256
PEER_EB = 1024
PEER_IB = PEER_EB // PEER_NKEYS
RANK_OUT = 4096.0


def _extract_top(work, n):
    rows = work.shape[0]
    iota = lax.broadcasted_iota(jnp.int32, work.shape, 0)
    order = jnp.full(work.shape, RANK_OUT, _F32)
    vals = []
    idx = None
    for a in range(n):
        m = jnp.max(work, axis=0, keepdims=True)
        idx = jnp.min(jnp.where(work == m, iota, rows), axis=0, keepdims=True)
        hit = iota == idx
        vals.append(m)
        order = jnp.where(hit, jnp.float32(a), order)
        work = jnp.where(hit, -jnp.inf, work)
    return vals, order, idx


def _peer_select_kernel(x_ref, g_ref, wqT_ref, keys_ref, hx_ref, c1_ref, e1n_ref, r2_ref, e2_ref):
    x = x_ref[...]
    y = x * lax.rsqrt(jnp.mean(x * x, axis=1, keepdims=True) + NORM_EPS) * g_ref[...]
    hx = y.astype(_BF16)
    hx_ref[...] = hx
    qT = lax.dot_general(wqT_ref[...], hx, (((1,), (1,)), ((), ())), preferred_element_type=_F32).astype(_BF16)
    half = PEER_QDIM // 2
    for h in range(PEER_HEADS):
        sc = []
        for c in range(2):
            r0 = (h * 2 + c) * half
            sc.append(jnp.dot(keys_ref[h * 2 + c], qT[r0:r0 + half, :], preferred_element_type=_F32))
        v1, o1, _ = _extract_top(sc[0], PEER_TOPK)
        v2, o2, _ = _extract_top(sc[1], PEER_TOPK)
        v2s = jnp.concatenate(v2, axis=0)
        cand = jnp.concatenate([v1[a] + v2s for a in range(PEER_TOPK)], axis=0)
        ex1 = [jnp.exp(v1[a] - v1[0]) for a in range(PEER_TOPK)]
        ex2s = jnp.exp(v2s - v2[0])
        ecand = jnp.concatenate([ex1[a] * ex2s for a in range(PEER_TOPK)], axis=0)
        _, corder, _ = _extract_top(cand, PEER_TOPK)
        chosen = corder < RANK_OUT
        z = jnp.sum(jnp.where(chosen, ecand, 0.0), axis=0, keepdims=True)
        c1 = jnp.zeros(sc[0].shape, _F32)
        for a in range(PEER_TOPK):
            cnt = jnp.sum(chosen[a * PEER_TOPK:(a + 1) * PEER_TOPK].astype(_F32), axis=0, keepdims=True)
            c1 = jnp.where(o1 == jnp.float32(a), cnt, c1)
        c1_ref[h] = c1
        e1n_ref[h] = jnp.exp(sc[0] - v1[0]) / z
        r2_ref[h] = o2
        e2_ref[h] = jnp.exp(sc[1] - v2[0])


def peer_select(x, g, wqT, keys2):
    Np, D = x.shape
    tn = PEER_TN
    row = lambda: pl.BlockSpec((PEER_HEADS, PEER_NKEYS, tn), lambda t: (0, 0, t))
    rshape = jax.ShapeDtypeStruct((PEER_HEADS, PEER_NKEYS, Np), _F32)
    return pl.pallas_call(
        _peer_select_kernel,
        grid=(Np // tn,),
        in_specs=[pl.BlockSpec((tn, D), lambda t: (t, 0)),
                  pl.BlockSpec((1, D), lambda t: (0, 0)),
                  pl.BlockSpec((PEER_HEADS * PEER_QDIM, D), lambda t: (0, 0)),
                  pl.BlockSpec((2 * PEER_HEADS, PEER_NKEYS, PEER_QDIM // 2), lambda t: (0, 0, 0))],
        out_specs=[pl.BlockSpec((tn, D), lambda t: (t, 0)), row(), row(), row(), row()],
        out_shape=[jax.ShapeDtypeStruct((Np, D), _BF16), rshape, rshape, rshape, rshape],
        compiler_params=pltpu.CompilerParams(dimension_semantics=("arbitrary",), vmem_limit_bytes=VMEM_LIMIT),
        name="peer_select",
    )(x, g, wqT, keys2)


def _gelu_tanh(x):
    return 0.5 * x * (1.0 + jnp.tanh(math.sqrt(2.0 / math.pi) * (x + 0.044715 * (x * x * x))))


def _peer_dense_kernel(x_ref, hx_ref, u_ref, vT_ref, c1_ref, e1n_ref, r2_ref, e2_ref, o_ref, p_ref, acc_ref):
    e = pl.program_id(1)
    tn, D = x_ref.shape

    @pl.when(e == 0)
    def _():
        acc_ref[...] = jnp.zeros_like(acc_ref)

    nt = (((1,), (1,)), ((), ()))
    half = PEER_EB // 2
    for s in range(2):
        act = _gelu_tanh(lax.dot_general(u_ref[s * half:(s + 1) * half, :], hx_ref[...], nt, preferred_element_type=_F32))
        for ib in range(s * half // PEER_NKEYS, (s + 1) * half // PEER_NKEYS):
            w = jnp.zeros((PEER_NKEYS, tn), _F32)
            for h in range(PEER_HEADS):
                w = w + jnp.where(r2_ref[h] < c1_ref[h, ib:ib + 1, :], e1n_ref[h, ib:ib + 1, :] * e2_ref[h], 0.0)
            r0 = ib * PEER_NKEYS - s * half
            p_ref[ib * PEER_NKEYS:(ib + 1) * PEER_NKEYS, :] = (w * act[r0:r0 + PEER_NKEYS, :]).astype(_BF16)
    for s in range(2):
        rows = slice(s * (D // 2), (s + 1) * (D // 2))
        acc_ref[rows, :] += jnp.dot(vT_ref[rows, :], p_ref[...], preferred_element_type=_F32)

    @pl.when(e == pl.num_programs(1) - 1)
    def _():
        o_ref[...] = x_ref[...] + acc_ref[...].T


def peer_dense(x, hx, u, vT, c1, e1n, r2, e2):
    Np, D = x.shape
    E = u.shape[0]
    tn = PEER_TN
    rowi = lambda: pl.BlockSpec((PEER_HEADS, PEER_IB, tn), lambda t, e: (0, e, t))
    rowj = lambda: pl.BlockSpec((PEER_HEADS, PEER_NKEYS, tn), lambda t, e: (0, 0, t))
    return pl.pallas_call(
        _peer_dense_kernel,
        grid=(Np // tn, E // PEER_EB),
        in_specs=[pl.BlockSpec((tn, D), lambda t, e: (t, 0)),
                  pl.BlockSpec((tn, D), lambda t, e: (t, 0)),
                  pl.BlockSpec((PEER_EB, D), lambda t, e: (e, 0)),
                  pl.BlockSpec((D, PEER_EB), lambda t, e: (0, e)),
                  rowi(), rowi(), rowj(), rowj()],
        out_specs=pl.BlockSpec((tn, D), lambda t, e: (t, 0)),
        out_shape=jax.ShapeDtypeStruct((Np, D), _F32),
        scratch_shapes=[pltpu.VMEM((PEER_EB, tn), _BF16), pltpu.VMEM((D, tn), _F32)],
        compiler_params=pltpu.CompilerParams(dimension_semantics=("arbitrary", "arbitrary"), vmem_limit_bytes=VMEM_LIMIT),
        name="peer_dense",
    )(x, hx, u, vT, c1, e1n, r2, e2)


def peer_ffn_tokens(x_tokens, g, wq, keys, U, V):
    N, D = x_tokens.shape
    Np = -(-N // PEER_TN) * PEER_TN
    x = jnp.pad(x_tokens, ((0, Np - N), (0, 0)))
    keys2 = keys.reshape(2 * PEER_HEADS, PEER_NKEYS, PEER_QDIM // 2).astype(_BF16)
    hx, c1, e1n, r2, e2 = peer_select(x, g.reshape(1, D), wq.T.astype(_BF16), keys2)
    return peer_dense(x, hx, U.astype(_BF16), V.T.astype(_BF16), c1, e1n, r2, e2)[:N]


def bias_lookup(dist, tab):
    onehot = (rel_bucket(dist)[..., None] == jnp.arange(REL_BUCKETS)).astype(_F32)
    return jnp.einsum('...b,bh->h...', onehot, tab.astype(_F32), precision=lax.Precision.HIGHEST)


def rel_bias_tiles(tab):
    kk = jnp.arange(TK)[:, None]
    qq = jnp.arange(TQ)[None, :]
    dist = jnp.stack([d * TQ + qq - kk for d in range(3)])
    return bias_lookup(dist, tab)


ROWS_T = 8
DEC_ROWS = N_HEADS * ROWS_T
NEAR_TILES = 3


PP = 4


def _page_specs(block, n_pages):
    def spec(i):
        return pl.BlockSpec(block, lambda b, p, pt: (pt[b, jnp.minimum(PP * p + i, n_pages - 1)], 0, 0))
    return [spec(i) for i in range(PP)]


def _moba_decode_gate_kernel(pt_ref, q_ref, *rest):
    caches, gs_ref = rest[:PP], rest[PP]
    p = pl.program_id(1)
    per = MOBA_BLOCK // PAGE_SIZE

    @pl.when(p == 0)
    def _():
        gs_ref[...] = jnp.zeros_like(gs_ref)

    lane = lax.broadcasted_iota(jnp.int32, gs_ref.shape[1:], 1)
    for g in range(PP // per):
        ksum = sum(jnp.sum(caches[g * per + i][0], axis=0, keepdims=True) for i in range(per))
        km = (ksum * (1.0 / MOBA_BLOCK)).astype(_BF16).astype(_F32)
        col = jnp.sum(q_ref[0].astype(_F32) * km, axis=1, keepdims=True)
        gs_ref[0] = jnp.where(lane == p * (PP // per) + g, col, gs_ref[0])


def moba_decode_gate(pt, q_rows, cache_rows):
    B, n_pages = pt.shape
    W = N_HEADS * HD
    assert n_pages % PP == 0 and PP % (MOBA_BLOCK // PAGE_SIZE) == 0
    return pl.pallas_call(
        _moba_decode_gate_kernel,
        grid_spec=pltpu.PrefetchScalarGridSpec(
            num_scalar_prefetch=1,
            grid=(B, n_pages // PP),
            in_specs=[pl.BlockSpec((1, DEC_ROWS, W), lambda b, p, pt: (b, 0, 0))] + _page_specs((1, PAGE_SIZE, W), n_pages),
            out_specs=pl.BlockSpec((1, DEC_ROWS, 128), lambda b, p, pt: (b, 0, 0))),
        out_shape=jax.ShapeDtypeStruct((B, DEC_ROWS, 128), _F32),
        compiler_params=pltpu.CompilerParams(dimension_semantics=("arbitrary", "arbitrary"), vmem_limit_bytes=VMEM_LIMIT),
        name="moba_decode_gate",
    )(pt, q_rows, *([cache_rows] * PP))


def _paged_attn_kernel(mode, pt_ref, q_ref, *rest):
    caches = rest[:PP]
    new_ref, far_ref, near_ref, m_in_ref, o_ref, m_sc, l_sc, acc_sc, sel_sc = rest[PP:]
    p = pl.program_id(1)
    last = pl.num_programs(1) - 1
    n_pages = last * PP
    W = N_HEADS * HD
    lane = lax.broadcasted_iota(jnp.int32, (DEC_ROWS, PAGE_SIZE), 1)
    row_t = lax.broadcasted_iota(jnp.int32, (DEC_ROWS, PAGE_SIZE), 0) % ROWS_T

    @pl.when(p == 0)
    def _():
        m_sc[...] = jnp.full_like(m_sc, NEG_BIG)
        l_sc[...] = jnp.zeros_like(l_sc)
        acc_sc[...] = jnp.zeros_like(acc_sc)
        if mode == 'moba':
            nblk = n_pages // (MOBA_BLOCK // PAGE_SIZE)
            valid = lane < nblk
            gs = jnp.where(valid, m_in_ref[0], -jnp.inf)
            rank = jnp.zeros(gs.shape, jnp.int32)
            for jp in range(PAST_LEN // MOBA_BLOCK):
                col = gs[:, jp:jp + 1]
                rank = rank + ((col > gs) | ((col == gs) & (jp < lane))).astype(jnp.int32)
            sel_sc[...] = (valid & (rank < MOBA_TOPK)).astype(_F32)

    def tile(blk, g, i, is_new):
        k = blk[:, :W].astype(_BF16)
        v = blk[:, W:].astype(_BF16)
        s = lax.dot_general(q_ref[0], k, (((1,), (1,)), ((), ())), preferred_element_type=_F32)
        near = near_ref[jnp.clip(g - (n_pages + 1 - NEAR_TILES), 0, NEAR_TILES - 1)]
        s = s + jnp.where(g >= n_pages + 1 - NEAR_TILES, near, far_ref[...])
        if mode == 'moba':
            if is_new:
                ok = lane <= row_t
            else:
                ok = jnp.sum(jnp.where(lane == g // (MOBA_BLOCK // PAGE_SIZE), sel_sc[...], 0.0), axis=1, keepdims=True) > 0.5
        else:
            ok = jnp.concatenate([m_in_ref[0, i]] * N_HEADS, axis=0) > 0.5
        s = jnp.where(ok, s, NEG_BIG)
        m_new = jnp.maximum(m_sc[...], jnp.max(s, axis=1, keepdims=True))
        pr = jnp.where(ok, jnp.exp(s - m_new), 0.0)
        alpha = jnp.exp(m_sc[...] - m_new)
        l_sc[...] = alpha * l_sc[...] + jnp.sum(pr, axis=1, keepdims=True)
        acc_sc[...] = alpha * acc_sc[...] + jnp.dot(pr.astype(_BF16), v, preferred_element_type=_F32)
        m_sc[...] = m_new

    @pl.when(p < last)
    def _():
        for i in range(PP):
            tile(caches[i][0], PP * p + i, i, False)

    @pl.when(p == last)
    def _():
        tile(new_ref[0], n_pages, 0, True)
        o_ref[0] = acc_sc[...] / jnp.maximum(l_sc[...], 1e-30)


def paged_attention(mode, pt, q_rows, cache_rows, new_rows, far, near, m_in):
    B, n_pages = pt.shape
    W = N_HEADS * HD
    assert n_pages % PP == 0
    if mode == 'moba':
        m_spec = pl.BlockSpec((1, DEC_ROWS, 128), lambda b, p, pt: (b, 0, 0))
    else:
        m_spec = pl.BlockSpec((1, PP, ROWS_T, PAGE_SIZE), lambda b, p, pt: (b, p, 0, 0))
    return pl.pallas_call(
        functools.partial(_paged_attn_kernel, mode),
        grid_spec=pltpu.PrefetchScalarGridSpec(
            num_scalar_prefetch=1,
            grid=(B, n_pages // PP + 1),
            in_specs=[pl.BlockSpec((1, DEC_ROWS, W), lambda b, p, pt: (b, 0, 0))] + _page_specs((1, PAGE_SIZE, 2 * W), n_pages) + [
                      pl.BlockSpec((1, PAGE_SIZE, 2 * W), lambda b, p, pt: (b, 0, 0)),
                      pl.BlockSpec((DEC_ROWS, 1), lambda b, p, pt: (0, 0)),
                      pl.BlockSpec((NEAR_TILES, DEC_ROWS, PAGE_SIZE), lambda b, p, pt: (0, 0, 0)),
                      m_spec],
            out_specs=pl.BlockSpec((1, DEC_ROWS, W), lambda b, p, pt: (b, 0, 0)),
            scratch_shapes=[pltpu.VMEM((DEC_ROWS, 1), _F32), pltpu.VMEM((DEC_ROWS, 1), _F32),
                            pltpu.VMEM((DEC_ROWS, W), _F32), pltpu.VMEM((DEC_ROWS, 128), _F32)]),
        out_shape=jax.ShapeDtypeStruct((B, DEC_ROWS, W), _F32),
        compiler_params=pltpu.CompilerParams(dimension_semantics=("arbitrary", "arbitrary"), vmem_limit_bytes=VMEM_LIMIT),
        name="paged_attn_" + mode,
    )(pt, q_rows, *([cache_rows] * PP), new_rows, far, near, m_in)


def _dsa_decode_score_kernel(pt_ref, qi_ref, wi_ref, *rest):
    caches, new_ref, score_ref = rest[:PP], rest[PP], rest[PP + 1]
    p = pl.program_id(1)
    last = pl.num_programs(1) - 1

    def tile_scores(ki):
        ki = ki.astype(_BF16)
        acc = jnp.zeros((ROWS_T, PAGE_SIZE), _F32)
        for hh in range(DSA_IDX_HEADS):
            s = lax.dot_general(qi_ref[0, hh], ki, (((1,), (1,)), ((), ())), preferred_element_type=_F32) * DSA_IDX_DIM ** -0.5
            acc = acc + wi_ref[0, hh] * jnp.maximum(s, 0.0)
        return acc

    @pl.when(p < last)
    def _():
        for i in range(PP):
            score_ref[0, :, i * PAGE_SIZE:(i + 1) * PAGE_SIZE] = tile_scores(caches[i][0])

    @pl.when(p == last)
    def _():
        score_ref[0] = jnp.zeros(score_ref.shape[1:], _F32)
        score_ref[0, :, 0:PAGE_SIZE] = tile_scores(new_ref[0])


def dsa_decode_scores(pt, qi_rows, wi_rows, cache_rows, new_rows):
    B, n_pages = pt.shape
    assert n_pages % PP == 0
    return pl.pallas_call(
        _dsa_decode_score_kernel,
        grid_spec=pltpu.PrefetchScalarGridSpec(
            num_scalar_prefetch=1,
            grid=(B, n_pages // PP + 1),
            in_specs=[pl.BlockSpec((1, DSA_IDX_HEADS, ROWS_T, DSA_IDX_DIM), lambda b, p, pt: (b, 0, 0, 0)),
                      pl.BlockSpec((1, DSA_IDX_HEADS, ROWS_T, 1), lambda b, p, pt: (b, 0, 0, 0))]
            + _page_specs((1, PAGE_SIZE, DSA_IDX_DIM), n_pages)
            + [pl.BlockSpec((1, PAGE_SIZE, DSA_IDX_DIM), lambda b, p, pt: (b, 0, 0))],
            out_specs=pl.BlockSpec((1, ROWS_T, PP * PAGE_SIZE), lambda b, p, pt: (b, 0, p))),
        out_shape=jax.ShapeDtypeStruct((B, ROWS_T, (n_pages + PP) * PAGE_SIZE), _F32),
        compiler_params=pltpu.CompilerParams(dimension_semantics=("arbitrary", "arbitrary"), vmem_limit_bytes=VMEM_LIMIT),
        name="dsa_decode_scores",
    )(pt, qi_rows, wi_rows, *([cache_rows] * PP), new_rows)


def _count_lanes(pred):
    return jnp.sum(pred.astype(_F32), axis=1, keepdims=True)


def _dsa_decode_select_kernel(topk, past_len, score_ref, keep_ref, key_ref):
    Lp = score_ref.shape[2]
    idx = lax.broadcasted_iota(jnp.int32, (ROWS_T, Lp), 1)
    q_pos = past_len + lax.broadcasted_iota(jnp.int32, (ROWS_T, Lp), 0)
    adm = idx <= q_pos
    score = score_ref[0]
    score = jnp.where(adm, jnp.where(score == 0.0, 0.0, score), -jnp.inf)
    bits = pltpu.bitcast(score, jnp.int32)
    key_ref[...] = jnp.where(bits < 0, bits ^ jnp.int32(0x7FFFFFFF), bits)
    kf = jnp.float32(topk)
    lo = jnp.where(_count_lanes(key_ref[...] >= 0) >= kf, jnp.int32(0), jnp.int32(-2 ** 31))

    def vstep(i, lo):
        cand = lo + jnp.left_shift(jnp.int32(1), 30 - i)
        return jnp.where(_count_lanes(key_ref[...] >= cand) >= kf, cand, lo)

    thr = lax.fori_loop(0, 31, vstep, lo)
    need = kf - _count_lanes(key_ref[...] > thr)
    nbits = max(1, (Lp - 1).bit_length())

    def istep(i, lo):
        cand = lo + jnp.left_shift(jnp.int32(1), nbits - 1 - i)
        c = _count_lanes((key_ref[...] == thr) & (idx < cand))
        return jnp.where(c < need, cand, lo)

    last_tie = lax.fori_loop(0, nbits, istep, jnp.zeros((ROWS_T, 1), jnp.int32))
    key = key_ref[...]
    keep = ((key > thr) | ((key == thr) & (idx <= last_tie))) & adm
    keep_ref[0] = keep.astype(_F32)


def dsa_decode_select(score, topk, past_len):
    B, _, Lp = score.shape
    return pl.pallas_call(
        functools.partial(_dsa_decode_select_kernel, topk, past_len),
        grid=(B,),
        in_specs=[pl.BlockSpec((1, ROWS_T, Lp), lambda b: (b, 0, 0))],
        out_specs=pl.BlockSpec((1, ROWS_T, Lp), lambda b: (b, 0, 0)),
        out_shape=jax.ShapeDtypeStruct((B, ROWS_T, Lp), _F32),
        scratch_shapes=[pltpu.VMEM((ROWS_T, Lp), jnp.int32)],
        compiler_params=pltpu.CompilerParams(dimension_semantics=("arbitrary",), vmem_limit_bytes=VMEM_LIMIT),
        name="dsa_decode_select",
    )(score)


def decode_rows(q):
    B, T, H, _ = q.shape
    qp = jnp.pad(jnp.transpose(q, (0, 2, 1, 3)), ((0, 0), (0, 0), (0, ROWS_T - T), (0, 0)))
    rows = qp[:, :, :, None, :] * jnp.eye(H, dtype=q.dtype)[None, :, None, :, None]
    return rows.reshape(B, H * ROWS_T, H * HD)


def from_decode_rows(o, T):
    B = o.shape[0]
    o5 = o.reshape(B, N_HEADS, ROWS_T, N_HEADS, HD)
    own = jnp.sum(o5 * jnp.eye(N_HEADS, dtype=o.dtype)[None, :, None, :, None], axis=3)
    return jnp.transpose(own[:, :, :T], (0, 2, 1, 3)).reshape(B, T, N_HEADS * HD)


def decode_bias(tab, past_len, n_pages):
    t = jnp.arange(ROWS_T)[None, :, None]
    tile = (n_pages + 1 - NEAR_TILES + jnp.arange(NEAR_TILES))[:, None, None]
    dist = past_len + t - (tile * PAGE_SIZE + jnp.arange(PAGE_SIZE)[None, None, :])
    near = jnp.transpose(bias_lookup(dist, tab), (1, 0, 2, 3)).reshape(NEAR_TILES, DEC_ROWS, PAGE_SIZE)
    far = jnp.repeat(tab[REL_BUCKETS - 1].astype(_F32), ROWS_T).reshape(DEC_ROWS, 1)
    return far, near


def pad_rows(a, n):
    return jnp.pad(a, ((0, 0), (0, n - a.shape[1])) + ((0, 0),) * (a.ndim - 2))


def heads_T(z, nh):
    B, T, _ = z.shape
    return jnp.transpose(z.reshape(B, T, nh, HD), (0, 2, 3, 1))


def heads_K(z, nh):
    B, L, _ = z.shape
    return jnp.transpose(z.reshape(B, L, nh, HD), (0, 2, 1, 3))


def heads_VT(z, nh):
    B, L, _ = z.shape
    return jnp.transpose(z.reshape(B, L // TK, TK, nh, HD), (0, 3, 1, 4, 2))


def from_heads_T(oT):
    B, H, _, T = oT.shape
    return jnp.transpose(oT, (0, 3, 1, 2)).reshape(B, T, H * HD)


def nsa_attention_prompt(q_flat, gates_flat, nsa_new, q_pos, w1, w2, pe, tab, bias_tiles):
    B, T, _ = q_flat.shape
    G, H, L = NSA_KV_HEADS, N_HEADS, T
    scale = HD ** -0.5
    q = q_flat.reshape(B, T, H, HD)
    kc, vc = nsa_new[:, :, 0], nsa_new[:, :, 1]
    nch = L // NSA_CMP_STRIDE
    ncmp = nch - 1

    def compress(x, i):
        c = x[:, :nch * NSA_CMP_STRIDE].reshape(B, nch, NSA_CMP_STRIDE, G, HD)
        blk = jnp.concatenate([c[:, :-1], c[:, 1:]], axis=2) + pe[i][None, None, :, None, :]
        blk = jnp.transpose(blk, (0, 1, 3, 2, 4)).reshape(B, ncmp, G, NSA_CMP_LEN * HD)
        return jax.nn.gelu(blk @ w1[i]) @ w2[i]

    kcmp = compress(kc, 0)
    vcmp = compress(vc, 1)
    cmp_end = jnp.arange(ncmp) * NSA_CMP_STRIDE + NSA_CMP_LEN - 1
    qg = q.reshape(B, T, G, NSA_GROUP, HD)
    dist_c = q_pos[:, None] - cmp_end[None, :]
    lg_c = jnp.einsum('btgrd,bngd->btgrn', qg, kcmp).astype(jnp.float32).reshape(B, T, H, ncmp) * scale
    lg_c = lg_c + rel_bias(dist_c[None, :, None, :], tab)
    p_c = masked_softmax(lg_c, (dist_c >= 0)[None, :, None, :])
    p_cg = p_c.reshape(B, T, G, NSA_GROUP, ncmp)
    o_cmp = jnp.einsum('btgrn,bngd->btgrd', p_cg.astype(vcmp.dtype), vcmp).reshape(B, T, H, HD)
    nsel = -(-L // NSA_SEL_BLOCK)
    ii = np.arange(ncmp)[:, None]
    jj_np = np.arange(nsel)[None, :]
    overlap = ((ii * NSA_CMP_STRIDE < (jj_np + 1) * NSA_SEL_BLOCK) & (ii * NSA_CMP_STRIDE + NSA_CMP_LEN > jj_np * NSA_SEL_BLOCK)).astype(np.float32)
    imp = jnp.einsum('btgrn,nj->btgj', p_cg, jnp.asarray(overlap))
    cur = q_pos // NSA_SEL_BLOCK
    jj = jnp.arange(nsel)[None, :]
    causal_blk = jj <= cur[:, None]
    forced = (jj == 0) | (jj == cur[:, None]) | (jj == cur[:, None] - 1)
    imp = jnp.where(forced[None, :, None, :], jnp.inf, jnp.where(causal_blk[None, :, None, :], imp, -jnp.inf))
    _, sel = lax.top_k(imp, min(NSA_TOPN, nsel))
    chosen = jnp.any(sel[..., None] == jnp.arange(nsel), axis=-2)
    selT = jnp.transpose(chosen, (0, 2, 3, 1)).astype(_F32)
    qT = heads_T(q_flat, H) * scale
    flat = lambda a: a.reshape(B, L, G * HD)
    o_sel = from_heads_T(attention_T('sel', qT, heads_K(flat(nsa_new[:, :, 2]), G), heads_VT(flat(nsa_new[:, :, 3]), G), bias_tiles, selT))
    o_win = from_heads_T(attention_T('win', qT, heads_K(flat(nsa_new[:, :, 4]), G), heads_VT(flat(nsa_new[:, :, 5]), G), bias_tiles))
    g = jax.nn.sigmoid(gates_flat.reshape(B, T, H, 3).astype(jnp.float32))
    out = g[..., 0:1] * o_cmp + g[..., 1:2] * o_sel.reshape(B, T, H, HD) + g[..., 2:3] * o_win.reshape(B, T, H, HD)
    return out.reshape(B, T, H * HD)


def token_mix(z, gz, past, lp, tab):
    B, T, _ = z.shape
    P = 0 if past is None else past['nsa'].shape[1]
    q_pos = P + jnp.arange(T, dtype=jnp.int32)
    qa, ka, va, q_nsa, nsa_kv, nsa_g, qc, kc, vc, qi, ki, wi, u, vg = split_last(z, IN_SPLITS)
    moba_new = jnp.stack([ka, va], axis=2).reshape(B, T, 2, N_HEADS, HD)
    nsa_new = nsa_kv.reshape(B, T, 6, NSA_KV_HEADS, HD)
    nsa_main_new = nsa_new[:, :, :4]
    win_new = nsa_new[:, :, 4:]
    dsa_new = jnp.stack([kc, vc], axis=2).reshape(B, T, 2, N_HEADS, HD)
    if past is None:
        nsa_all, win_all = nsa_main_new, win_new
    else:
        nsa_all = jnp.concatenate([past['nsa'], nsa_main_new], axis=1)
        win_all = jnp.concatenate([past['win'], win_new], axis=1)
    kw_pos0 = P + T - win_all.shape[1]
    scale = HD ** -0.5
    if past is None:
        bt = lp['bias_tiles']
        qaT = heads_T(qa, N_HEADS) * scale
        y_a = from_heads_T(attention_T('moba', qaT, heads_K(ka, N_HEADS), heads_VT(va, N_HEADS), bt[0], moba_gate(qaT, heads_K(ka, N_HEADS))))
        y_b = nsa_attention_prompt(q_nsa, nsa_g, nsa_new, q_pos, lp['cmp_w1'], lp['cmp_w2'], lp['cmp_pe'], tab[:, N_HEADS:2 * N_HEADS], bt[1])
        qiT = jnp.transpose(qi.reshape(B, T, DSA_IDX_HEADS, DSA_IDX_DIM), (0, 2, 3, 1))
        keep = dsa_mask(ki, qiT, jnp.transpose(wi, (0, 2, 1)), min(DSA_TOPK, T // 4))
        y_c = from_heads_T(attention_T('dsa', heads_T(qc, N_HEADS) * scale, heads_K(kc, N_HEADS), heads_VT(vc, N_HEADS), bt[2], keep))
    else:
        assert P % MOBA_BLOCK == 0 and T <= ROWS_T
        pt = past['pt']
        n_pages = pt.shape[1]
        far_a, near_a = decode_bias(tab[:, :N_HEADS], P, n_pages)
        qa_rows = (decode_rows(qa.reshape(B, T, N_HEADS, HD)) * scale).astype(_BF16)
        new_a = pad_rows(jnp.concatenate([ka, va], axis=-1), PAGE_SIZE)
        gs = moba_decode_gate(pt, qa_rows, past['moba_rows'])
        y_a = from_decode_rows(paged_attention('moba', pt, qa_rows, past['moba_rows'], new_a, far_a, near_a, gs), T)
        y_b = nsa_attention(q_nsa.reshape(B, T, N_HEADS, HD), nsa_g.reshape(B, T, N_HEADS, 3), nsa_all[:, :, 0], nsa_all[:, :, 1], nsa_all[:, :, 2], nsa_all[:, :, 3], win_all[:, :, 0], win_all[:, :, 1], kw_pos0, q_pos, lp['cmp_w1'], lp['cmp_w2'], lp['cmp_pe'], tab[:, N_HEADS:2 * N_HEADS])
        qi_rows = pad_rows(jnp.transpose(qi.reshape(B, T, DSA_IDX_HEADS, DSA_IDX_DIM), (0, 2, 1, 3)).reshape(B * DSA_IDX_HEADS, T, DSA_IDX_DIM), ROWS_T)
        qi_rows = qi_rows.reshape(B, DSA_IDX_HEADS, ROWS_T, DSA_IDX_DIM).astype(_BF16)
        wi_rows = pad_rows(jnp.transpose(wi, (0, 2, 1)).reshape(B * DSA_IDX_HEADS, T), ROWS_T).reshape(B, DSA_IDX_HEADS, ROWS_T, 1)
        score = dsa_decode_scores(pt, qi_rows, wi_rows, past['idx_rows'], pad_rows(ki, PAGE_SIZE))
        keep = dsa_decode_select(score, min(DSA_TOPK, (P + T) // 4), P)
        keep = jnp.transpose(keep.reshape(B, ROWS_T, n_pages + PP, PAGE_SIZE), (0, 2, 1, 3))
        far_c, near_c = decode_bias(tab[:, 2 * N_HEADS:], P, n_pages)
        qc_rows = (decode_rows(qc.reshape(B, T, N_HEADS, HD)) * scale).astype(_BF16)
        new_c = pad_rows(jnp.concatenate([kc, vc], axis=-1), PAGE_SIZE)
        y_c = from_decode_rows(paged_attention('dsa', pt, qc_rows, past['dsa_rows'], new_c, far_c, near_c, keep), T)
    y_d, v_rows = gmlp_sgu(u, vg, lp['gm_ws'], lp['gm_b'], lp['gm_norm'])
    branches = jnp.stack([y_a.reshape(B, T, MIX_W), y_b.reshape(B, T, MIX_W), y_c.reshape(B, T, MIX_W), y_d], axis=2)
    proj = jnp.einsum('btnc,ncd->btnd', branches, lp['w_branch'])
    gate = jax.nn.sigmoid(gz.reshape(B, T, N_BRANCH, D_MODEL) + lp['b_gate'])
    out = jnp.sum(gate * proj, axis=2) @ lp['w_out']
    win_keep = win_all[:, -min(NSA_WINDOW, P + T):]
    return out, (moba_new, nsa_main_new, win_keep, dsa_new, ki, v_rows)


def mem_attention_q(q, mkv, wo):
    B, T, _ = q.shape
    q = q.reshape(B, T, MEM_HEADS, MEM_HD)
    lg = jnp.einsum('bthd,bmhd->bthm', q, mkv[:, :, 0]).astype(jnp.float32) * MEM_HD ** -0.5
    p = jax.nn.softmax(lg, axis=-1)
    o = jnp.einsum('bthm,bmhd->bthd', p.astype(mkv.dtype), mkv[:, :, 1]).reshape(B, T, MEM_W)
    return o @ wo


def layer_pair(rows, shape_p, shape_s, past, mkv_p, mkv_s, lp, tab):
    n_p = shape_p[0] * shape_p[1]
    split = lambda a: (a[:n_p].reshape(shape_p[0], shape_p[1], -1), a[n_p:].reshape(shape_s[0], shape_s[1], -1))
    z_p, z_s = split(norm_matmul(rows, lp['norm_mix'], lp['w_in']))
    gz_p, gz_s = split(norm_matmul(rows, lp['norm_mix'], lp['w_gate']))
    mix_p, rp = token_mix(z_p, gz_p, None, lp, tab)
    mix_s, rs = token_mix(z_s, gz_s, past, lp, tab)
    rows = rows + jnp.concatenate([mix_p.reshape(-1, D_MODEL), mix_s.reshape(-1, D_MODEL)], axis=0)
    q_p, q_s = split(norm_matmul(rows, lp['norm_mem'], lp['w_mem_q']))
    mem_p = mem_attention_q(q_p, mkv_p, lp['w_mem_o'])
    mem_s = mem_attention_q(q_s, mkv_s, lp['w_mem_o'])
    rows = rows + jnp.concatenate([mem_p.reshape(-1, D_MODEL), mem_s.reshape(-1, D_MODEL)], axis=0)
    rows = peer_ffn_tokens(rows, lp['norm_ffn'], lp['peer_wq'], lp['peer_keys'], lp['peer_u'], lp['peer_v'])
    return rows, rp, rs


def kernel(x_prompt, x_sample, cache_moba_kv, cache_nsa_kv, cache_dsa_kv, cache_dsa_idx, state_nsa_win, cache_mem_kv, page_table, mem_prompt, rel_bias_table, w_in, nsa_cmp_w1, nsa_cmp_w2, nsa_cmp_pe, gm_ws, gm_b, gm_norm, w_branch, w_gate, b_gate, w_out, w_mem_q, w_mem_kv, w_mem_o, peer_wq, peer_keys, peer_u, peer_v, norm_mix, norm_mem, norm_ffn, norm_final):
    n_p = x_prompt.shape[0] * x_prompt.shape[1]
    rows = jnp.concatenate([x_prompt.reshape(-1, D_MODEL), x_sample.reshape(-1, D_MODEL)], axis=0)
    rows_p = []
    rows_s = []
    mem_rows = []
    Bp = x_prompt.shape[0]
    n_pool = cache_moba_kv.shape[1]
    bias_tiles =[rel_bias_tiles(rel_bias_table[:, n * N_HEADS:(n + 1) * N_HEADS]) for n in range(3)]
    for l in range(DEPTH):
        lp = {
            'w_in': w_in[l], 'cmp_w1': nsa_cmp_w1[l], 'cmp_w2': nsa_cmp_w2[l], 'cmp_pe': nsa_cmp_pe[l],
            'gm_ws': gm_ws[l], 'gm_b': gm_b[l], 'gm_norm': gm_norm[l],
            'w_branch': w_branch[l], 'w_gate': w_gate[l], 'b_gate': b_gate[l], 'w_out': w_out[l],
            'w_mem_q': w_mem_q[l], 'w_mem_o': w_mem_o[l],
            'peer_wq': peer_wq[l], 'peer_keys': peer_keys[l], 'peer_u': peer_u[l], 'peer_v': peer_v[l],
            'norm_mix': norm_mix[l], 'norm_mem': norm_mem[l], 'norm_ffn': norm_ffn[l],
        }
        mkv_p = (mem_prompt @ w_mem_kv[l]).reshape(Bp, MEM_TOKENS, 2, MEM_HEADS, MEM_HD)
        past = {
            'pt': page_table + l * n_pool,
            'moba_rows': cache_moba_kv.reshape(DEPTH * n_pool, PAGE_SIZE, 2 * MIX_W),
            'dsa_rows': cache_dsa_kv.reshape(DEPTH * n_pool, PAGE_SIZE, 2 * MIX_W),
            'idx_rows': cache_dsa_idx.reshape(DEPTH * n_pool, PAGE_SIZE, DSA_IDX_DIM),
            'nsa': gather_pages(cache_nsa_kv[l], page_table),
            'win': state_nsa_win[l],
        }
        lp['bias_tiles'] = bias_tiles
        rows, rp, rs = layer_pair(rows, x_prompt.shape, x_sample.shape, past, mkv_p, cache_mem_kv[l], lp, rel_bias_table)
        rows_p.append(rp)
        rows_s.append(rs)
        mem_rows.append(mkv_p)
    y_rows = rmsnorm(rows, norm_final)
    y_prompt = y_rows[:n_p].reshape(x_prompt.shape)
    y_sample = y_rows[n_p:].reshape(x_sample.shape)
    new_moba_kv_prompt = jnp.stack([r[0] for r in rows_p])
    new_moba_kv_sample = jnp.stack([r[0] for r in rows_s])
    new_nsa_kv_prompt = jnp.stack([r[1] for r in rows_p])
    new_nsa_kv_sample = jnp.stack([r[1] for r in rows_s])
    new_nsa_win_prompt = jnp.stack([r[2] for r in rows_p])
    new_nsa_win_sample = jnp.stack([r[2] for r in rows_s])
    new_dsa_kv_prompt = jnp.stack([r[3] for r in rows_p])
    new_dsa_kv_sample = jnp.stack([r[3] for r in rows_s])
    new_dsa_idx_prompt = jnp.stack([r[4] for r in rows_p])
    new_dsa_idx_sample = jnp.stack([r[4] for r in rows_s])
    new_mem_kv_prompt = jnp.stack(mem_rows)
    new_gmlp_v_sample = jnp.stack([r[5] for r in rows_s])
    return (y_prompt, y_sample, new_moba_kv_prompt, new_moba_kv_sample, new_nsa_kv_prompt, new_nsa_kv_sample, new_nsa_win_prompt, new_nsa_win_sample, new_dsa_kv_prompt, new_dsa_kv_sample, new_dsa_idx_prompt, new_dsa_idx_sample, new_mem_kv_prompt, new_gmlp_v_sample)
```

```python
import functools
import math
import jax
import jax.numpy as jnp
from jax import lax
import numpy as np
from jax.experimental import pallas as pl
from jax.experimental.pallas import tpu as pltpu

D_MODEL = 2048
BATCH = 4
SEQ = 2048
DEPTH = 2
DEC_BATCH = 8
DEC_SEQ = 4
PAST_LEN = 16384
PAGE_SIZE = 128

HD = 64
N_HEADS = 8
MIX_W = N_HEADS * HD
N_BRANCH = 4
MOBA_BLOCK = 256
MOBA_TOPK = 3
MOBA_QB = 32
NSA_KV_HEADS = 2
NSA_GROUP = N_HEADS // NSA_KV_HEADS
NSA_CMP_STRIDE = 16
NSA_CMP_LEN = 2 * NSA_CMP_STRIDE
NSA_SEL_BLOCK = 64
NSA_TOPN = 16
NSA_WINDOW = 512
NSA_QB = 64
WIN_QB = 128
DSA_IDX_HEADS = 4
DSA_IDX_DIM = 64
DSA_TOPK = 256
DSA_QB = 64
GM_CHUNK = 128
GM_GROUPS = 8
GM_W = 512
REL_BUCKETS = 32
REL_MAX_DIST = 128
N_REL_HEADS = 3 * N_HEADS
MEM_TOKENS = 256
MEM_HEADS = 4
MEM_HD = 128
MEM_W = MEM_HEADS * MEM_HD
PEER_HEADS = 8
PEER_NKEYS = 128
PEER_QDIM = 128
PEER_TOPK = 16
PEER_EXPERTS = PEER_NKEYS ** 2
PEER_BLOCK = 128
NORM_EPS = 1e-6
IN_SPLITS = (MIX_W, MIX_W, MIX_W, MIX_W, 6 * NSA_KV_HEADS * HD, 3 * N_HEADS, MIX_W, MIX_W, MIX_W, DSA_IDX_HEADS * DSA_IDX_DIM, DSA_IDX_DIM, DSA_IDX_HEADS, GM_W, GM_W)
IN_COLS = sum(IN_SPLITS)


def rmsnorm(x, g):
    xf = x.astype(jnp.float32)
    y = xf * lax.rsqrt(jnp.mean(xf * xf, axis=-1, keepdims=True) + NORM_EPS)
    return (y * g.astype(jnp.float32)).astype(x.dtype)


def masked_softmax(logits, mask):
    l = jnp.where(mask, logits.astype(jnp.float32), -jnp.inf)
    m = jnp.max(l, axis=-1, keepdims=True)
    m = jnp.where(jnp.isfinite(m), m, 0.0)
    e = jnp.exp(l - m)
    return e / jnp.maximum(jnp.sum(e, axis=-1, keepdims=True), 1e-30)


def rel_bucket(dist):
    n = jnp.maximum(dist, 0)
    max_exact = REL_BUCKETS // 2
    nf = jnp.maximum(n, 1).astype(jnp.float32)
    large = max_exact + (jnp.log(nf / max_exact) / math.log(REL_MAX_DIST / max_exact) * (REL_BUCKETS - max_exact)).astype(jnp.int32)
    return jnp.where(n < max_exact, n, jnp.minimum(large, REL_BUCKETS - 1))


def rel_bias(dist, tab):
    onehot = (rel_bucket(dist)[..., None] == jnp.arange(REL_BUCKETS)).astype(jnp.float32)
    spec = '...okb,bh->...hk' if dist.shape[-2] == 1 else '...hkb,bh->...hk'
    return jnp.einsum(spec, onehot, tab.astype(jnp.float32), precision=lax.Precision.HIGHEST)


def split_last(z, sizes):
    return jnp.split(z, np.cumsum(sizes)[:-1].tolist(), axis=-1)


def map_query_blocks(fn, qb, q_pos, *xs):
    T = q_pos.shape[0]
    qb = qb if T % qb == 0 else T
    nb = T // qb
    xb = tuple(jnp.moveaxis(x.reshape(x.shape[0], nb, qb, *x.shape[2:]), 1, 0) for x in xs)
    out = lax.map(lambda a: fn(*a), (q_pos.reshape(nb, qb),) + xb)
    return jnp.moveaxis(out, 0, 1).reshape(out.shape[1], T, *out.shape[3:])


def gather_pages(pool, page_table):
    g = pool[page_table]
    return g.reshape(g.shape[0], g.shape[1] * g.shape[2], *g.shape[3:])


def moba_attention(q, k, v, q_pos, tab):
    B, L, H, _ = k.shape
    nblk = -(-L // MOBA_BLOCK)
    pad = ((0, 0), (0, nblk * MOBA_BLOCK - L), (0, 0), (0, 0))
    kb = jnp.pad(k, pad).reshape(B, nblk, MOBA_BLOCK, H, HD)
    vb = jnp.pad(v, pad).reshape(B, nblk, MOBA_BLOCK, H, HD)
    k_mean = jnp.mean(kb.astype(jnp.float32), axis=2).astype(k.dtype)
    kb_t = jnp.transpose(kb, (0, 3, 1, 2, 4))
    vb_t = jnp.transpose(vb, (0, 3, 1, 2, 4))
    ntop = min(MOBA_TOPK, nblk)
    bi = jnp.arange(B)[:, None, None, None]
    hi = jnp.arange(H)[None, None, :, None]
    off = jnp.arange(MOBA_BLOCK)
    scale = HD ** -0.5

    def block_fn(qp, qq):
        qb = qp.shape[0]
        own = qp // MOBA_BLOCK
        gs = jnp.einsum('bqhd,bjhd->bqhj', qq, k_mean).astype(jnp.float32)
        fully_past = jnp.arange(nblk)[None, :] < own[:, None]
        gs = jnp.where(fully_past[None, :, None, :], gs, -jnp.inf)
        _, top = lax.top_k(gs, ntop)
        own_b = jnp.broadcast_to(own[None, :, None, None], (B, qb, H, 1))
        sel = jnp.concatenate([top, own_b], axis=-1)
        sel_ok = jnp.concatenate([top < own[None, :, None, None], jnp.ones(own_b.shape, bool)], axis=-1)
        kg = kb_t[bi, hi, sel].reshape(B, qb, H, (ntop + 1) * MOBA_BLOCK, HD)
        vg = vb_t[bi, hi, sel].reshape(B, qb, H, (ntop + 1) * MOBA_BLOCK, HD)
        pos = (sel[..., None] * MOBA_BLOCK + off).reshape(B, qb, H, -1)
        dist = qp[None, :, None, None] - pos
        ok = jnp.repeat(sel_ok, MOBA_BLOCK, axis=-1) & (dist >= 0)
        lg = jnp.einsum('bqhd,bqhkd->bqhk', qq, kg).astype(jnp.float32) * scale + rel_bias(dist, tab)
        p = masked_softmax(lg, ok)
        return jnp.einsum('bqhk,bqhkd->bqhd', p.astype(vg.dtype), vg)

    return map_query_blocks(block_fn, MOBA_QB, q_pos, q)


def nsa_attention(q, gates, kc, vc, ks, vs, kw, vw, kw_pos0, q_pos, w1, w2, pe, tab):
    B, L, G, _ = kc.shape
    T = q.shape[1]
    H = N_HEADS
    scale = HD ** -0.5
    nch = L // NSA_CMP_STRIDE
    ncmp = nch - 1

    def compress(x, i):
        c = x[:, :nch * NSA_CMP_STRIDE].reshape(B, nch, NSA_CMP_STRIDE, G, HD)
        blk = jnp.concatenate([c[:, :-1], c[:, 1:]], axis=2) + pe[i][None, None, :, None, :]
        blk = jnp.transpose(blk, (0, 1, 3, 2, 4)).reshape(B, ncmp, G, NSA_CMP_LEN * HD)
        return jax.nn.gelu(blk @ w1[i]) @ w2[i]

    kcmp = compress(kc, 0)
    vcmp = compress(vc, 1)
    cmp_end = jnp.arange(ncmp) * NSA_CMP_STRIDE + NSA_CMP_LEN - 1
    qg = q.reshape(B, T, G, NSA_GROUP, HD)
    dist_c = q_pos[:, None] - cmp_end[None, :]
    lg_c = jnp.einsum('btgrd,bngd->btgrn', qg, kcmp).astype(jnp.float32).reshape(B, T, H, ncmp) * scale
    lg_c = lg_c + rel_bias(dist_c[None, :, None, :], tab)
    p_c = masked_softmax(lg_c, (dist_c >= 0)[None, :, None, :])
    p_cg = p_c.reshape(B, T, G, NSA_GROUP, ncmp)
    o_cmp = jnp.einsum('btgrn,bngd->btgrd', p_cg.astype(vcmp.dtype), vcmp).reshape(B, T, H, HD)
    nsel = -(-L // NSA_SEL_BLOCK)
    ii = np.arange(ncmp)[:, None]
    jj_np = np.arange(nsel)[None, :]
    overlap = ((ii * NSA_CMP_STRIDE < (jj_np + 1) * NSA_SEL_BLOCK) & (ii * NSA_CMP_STRIDE + NSA_CMP_LEN > jj_np * NSA_SEL_BLOCK)).astype(np.float32)
    imp = jnp.einsum('btgrn,nj->btgj', p_cg, jnp.asarray(overlap))
    cur = q_pos // NSA_SEL_BLOCK
    jj = jnp.arange(nsel)[None, :]
    causal_blk = jj <= cur[:, None]
    forced = (jj == 0) | (jj == cur[:, None]) | (jj == cur[:, None] - 1)
    imp = jnp.where(forced[None, :, None, :], jnp.inf, jnp.where(causal_blk[None, :, None, :], imp, -jnp.inf))
    ntop = min(NSA_TOPN, nsel)
    _, sel = lax.top_k(imp, ntop)
    pad = ((0, 0), (0, nsel * NSA_SEL_BLOCK - L), (0, 0), (0, 0))
    ksb = jnp.transpose(jnp.pad(ks, pad).reshape(B, nsel, NSA_SEL_BLOCK, G, HD), (0, 3, 1, 2, 4))
    vsb = jnp.transpose(jnp.pad(vs, pad).reshape(B, nsel, NSA_SEL_BLOCK, G, HD), (0, 3, 1, 2, 4))
    bi = jnp.arange(B)[:, None, None, None]
    gi = jnp.arange(G)[None, None, :, None]
    off = jnp.arange(NSA_SEL_BLOCK)

    def sel_fn(qp, qq, ss):
        qb = qp.shape[0]
        kg = ksb[bi, gi, ss].reshape(B, qb, G, ntop * NSA_SEL_BLOCK, HD)
        vg = vsb[bi, gi, ss].reshape(B, qb, G, ntop * NSA_SEL_BLOCK, HD)
        pos = (ss[..., None] * NSA_SEL_BLOCK + off).reshape(B, qb, G, -1)
        dist = jnp.repeat(qp[None, :, None, None] - pos, NSA_GROUP, axis=2)
        lg = jnp.einsum('bqgrd,bqgkd->bqgrk', qq.reshape(B, qb, G, NSA_GROUP, HD), kg).astype(jnp.float32).reshape(B, qb, H, -1) * scale
        p = masked_softmax(lg + rel_bias(dist, tab), dist >= 0)
        o = jnp.einsum('bqgrk,bqgkd->bqgrd', p.reshape(B, qb, G, NSA_GROUP, -1).astype(vg.dtype), vg)
        return o.reshape(B, qb, H, HD)

    o_sel = map_query_blocks(sel_fn, NSA_QB, q_pos, q, sel)
    W = NSA_WINDOW
    wpad = ((0, 0), (W, 0), (0, 0), (0, 0))
    kwp = jnp.pad(kw, wpad)
    vwp = jnp.pad(vw, wpad)

    def win_fn(qp, qq):
        qb = qp.shape[0]
        span = W + qb - 1
        start = qp[0] - kw_pos0 + 1
        kk = lax.dynamic_slice_in_dim(kwp, start, span, axis=1)
        vv = lax.dynamic_slice_in_dim(vwp, start, span, axis=1)
        pos = qp[0] - W + 1 + jnp.arange(span)
        dist = qp[:, None] - pos[None, :]
        ok = (dist >= 0) & (dist < W) & (pos[None, :] >= 0)
        lg = jnp.einsum('bqgrd,bkgd->bqgrk', qq.reshape(B, qb, G, NSA_GROUP, HD), kk).astype(jnp.float32).reshape(B, qb, H, span) * scale
        p = masked_softmax(lg + rel_bias(dist[None, :, None, :], tab), ok[None, :, None, :])
        o = jnp.einsum('bqgrk,bkgd->bqgrd', p.reshape(B, qb, G, NSA_GROUP, span).astype(vv.dtype), vv)
        return o.reshape(B, qb, H, HD)

    o_win = map_query_blocks(win_fn, WIN_QB, q_pos, q)
    g = jax.nn.sigmoid(gates.astype(jnp.float32)).astype(q.dtype)
    return g[..., 0:1] * o_cmp + g[..., 1:2] * o_sel + g[..., 2:3] * o_win


def dsa_attention(q, qi, wi, k, v, ki, q_pos, tab):
    B, L, H, _ = k.shape
    kk = min(DSA_TOPK, L // 4)
    bi = jnp.arange(B)[:, None, None]
    scale = HD ** -0.5

    def block_fn(qp, qq, qqi, wwi):
        s = jax.nn.relu(jnp.einsum('bqhd,bld->bqhl', qqi, ki).astype(jnp.float32) * DSA_IDX_DIM ** -0.5)
        score = jnp.einsum('bqh,bqhl->bql', wwi.astype(jnp.float32), s)
        admissible = jnp.arange(L)[None, :] <= qp[:, None]
        score = jnp.where(admissible[None], score, -jnp.inf)
        _, sel = lax.top_k(score, kk)
        kg = k[bi, sel]
        vg = v[bi, sel]
        dist = (qp[None, :, None] - sel)[:, :, None, :]
        lg = jnp.einsum('bqhd,bqkhd->bqhk', qq, kg).astype(jnp.float32) * scale + rel_bias(dist, tab)
        p = masked_softmax(lg, dist >= 0)
        return jnp.einsum('bqhk,bqkhd->bqhd', p.astype(vg.dtype), vg)

    return map_query_blocks(block_fn, DSA_QB, q_pos, q, qi, wi)


def gmlp_sgu(u, v, ws, b, gnorm):
    B, T, _ = u.shape
    vn = rmsnorm(v, gnorm)
    nc = -(-T // GM_CHUNK)
    vc = jnp.pad(vn, ((0, 0), (0, nc * GM_CHUNK - T), (0, 0))).reshape(B, nc, GM_CHUNK, GM_GROUPS, GM_W // GM_GROUPS)
    wsm = ws * jnp.tril(jnp.ones((GM_CHUNK, GM_CHUNK), ws.dtype))
    mixed = jnp.einsum('gij,bcjgd->bcigd', wsm, vc) + b.T[None, None, :, :, None]
    mixed = mixed.reshape(B, nc * GM_CHUNK, GM_W)[:, :T]
    return u * mixed, vn


def mem_attention(h, mkv, wq, wo):
    B, T, _ = h.shape
    q = (h @ wq).reshape(B, T, MEM_HEADS, MEM_HD)
    lg = jnp.einsum('bthd,bmhd->bthm', q, mkv[:, :, 0]).astype(jnp.float32) * MEM_HD ** -0.5
    p = jax.nn.softmax(lg, axis=-1)
    o = jnp.einsum('bthm,bmhd->bthd', p.astype(mkv.dtype), mkv[:, :, 1]).reshape(B, T, MEM_W)
    return o @ wo


def peer_ffn(h, wq, keys, U, V):
    B, T, D = h.shape
    N = B * T
    nb = -(-N // PEER_BLOCK)
    hf = jnp.pad(h.reshape(N, D), ((0, nb * PEER_BLOCK - N), (0, 0))).reshape(nb, PEER_BLOCK, D)

    def block_fn(x):
        q = (x @ wq).reshape(PEER_BLOCK, PEER_HEADS, 2, PEER_QDIM // 2)
        s = jnp.einsum('nhcd,hckd->nhck', q, keys).astype(jnp.float32)
        v1, i1 = lax.top_k(s[:, :, 0], PEER_TOPK)
        v2, i2 = lax.top_k(s[:, :, 1], PEER_TOPK)
        cand = (v1[..., :, None] + v2[..., None, :]).reshape(PEER_BLOCK, PEER_HEADS, PEER_TOPK * PEER_TOPK)
        sv, ci = lax.top_k(cand, PEER_TOPK)
        e = jnp.take_along_axis(i1, ci // PEER_TOPK, axis=-1) * PEER_NKEYS + jnp.take_along_axis(i2, ci % PEER_TOPK, axis=-1)
        g = jax.nn.softmax(sv, axis=-1)
        ug = U[e]
        vg = V[e]
        act = jax.nn.gelu(jnp.einsum('nd,nhkd->nhk', x, ug).astype(jnp.float32))
        return jnp.einsum('nhk,nhkd->nd', (g * act).astype(x.dtype), vg)

    out = lax.map(block_fn, hf).reshape(nb * PEER_BLOCK, D)[:N]
    return out.reshape(B, T, D)


def _norm_matmul_kernel(x_ref, g_ref, w_ref, o_ref, xn_ref):
    @pl.when(pl.program_id(1) == 0)
    def _():
        x = x_ref[...]
        y = x * lax.rsqrt(jnp.mean(x * x, axis=-1, keepdims=True) + NORM_EPS)
        xn_ref[...] = (y * g_ref[...]).astype(jnp.bfloat16)

    hw = w_ref.shape[1] // 2
    for s in range(2):
        cols = slice(s * hw, (s + 1) * hw)
        o_ref[:, cols] = jnp.dot(xn_ref[...], w_ref[:, cols].astype(jnp.bfloat16), preferred_element_type=jnp.float32)


def norm_matmul(x, g, w, tm=512, tn=512):
    N, D = x.shape
    C = w.shape[1]
    tm = min(tm, N)
    tn = min(tn, C)
    return pl.pallas_call(
        _norm_matmul_kernel,
        grid=(pl.cdiv(N, tm), pl.cdiv(C, tn)),
        in_specs=[
            pl.BlockSpec((tm, D), lambda i, j: (i, 0)),
            pl.BlockSpec((1, D), lambda i, j: (0, 0)),
            pl.BlockSpec((D, tn), lambda i, j: (0, j)),
        ],
        out_specs=pl.BlockSpec((tm, tn), lambda i, j: (i, j)),
        out_shape=jax.ShapeDtypeStruct((N, C), jnp.float32),
        scratch_shapes=[pltpu.VMEM((tm, D), jnp.bfloat16)],
        compiler_params=pltpu.CompilerParams(
            dimension_semantics=("arbitrary", "arbitrary"), vmem_limit_bytes=48 * 1024 * 1024),
        name="norm_matmul",
    )(x, g.reshape(1, D), w.astype(jnp.bfloat16))


TQ = 256
TK = 256
NEG_BIG = -1e30
VMEM_LIMIT = 48 * 1024 * 1024
_BF16 = jnp.bfloat16
_F32 = jnp.float32


HEADS_PER_STEP = 2


def _attn_kernel(mode, shared_kv, qT_ref, k_ref, vT_ref, bias_ref, *rest):
    if mode == 'win':
        (o_ref,) = rest
        m_ref = None
    else:
        m_ref, o_ref = rest
    qi = pl.program_id(2)
    heads = range(HEADS_PER_STEP)
    q = [qT_ref[0, hh].astype(_BF16) for hh in heads]
    lane = lax.broadcasted_iota(jnp.int32, (TK, TQ), 1)
    sub = lax.broadcasted_iota(jnp.int32, (TK, TQ), 0)
    rel0 = lane - sub

    def body(kj, carry):
        dist = rel0 + (qi - kj) * TQ
        ok_all = dist >= 0
        if mode == 'sel':
            rows = m_ref[0, 0, kj]
            blk = sub // NSA_SEL_BLOCK
            r = jnp.where(blk == 0, rows[0:1], jnp.where(blk == 1, rows[1:2], jnp.where(blk == 2, rows[2:3], rows[3:4])))
            ok_all = ok_all & (r > 0.5)
        elif mode == 'dsa':
            ok_all = ok_all & (m_ref[0, pl.ds(pl.multiple_of(kj * TK, TK), TK), :].astype(_F32) > 0.5)
        elif mode == 'win':
            ok_all = ok_all & (dist < NSA_WINDOW)
        out = []
        for hh in heads:
            m, l, acc = carry[hh]
            hk = 0 if shared_kv else hh
            k_t = k_ref[0, hk, pl.ds(pl.multiple_of(kj * TK, TK), TK), :].astype(_BF16)
            s = jnp.dot(k_t, q[hh], preferred_element_type=_F32)
            s = s + bias_ref[hh, jnp.minimum(qi - kj, 2)]
            ok = ok_all & (m_ref[0, hh, kj] > 0.5) if mode == 'moba' else ok_all
            s = jnp.where(ok, s, NEG_BIG)
            m_new = jnp.maximum(m, jnp.max(s, axis=0, keepdims=True))
            p = jnp.where(ok, jnp.exp(s - m_new), 0.0)
            alpha = jnp.exp(m - m_new)
            l = alpha * l + jnp.sum(p, axis=0, keepdims=True)
            v_t = vT_ref[0, hk, kj].astype(_BF16)
            acc = alpha * acc + jnp.dot(v_t, p.astype(_BF16), preferred_element_type=_F32)
            out.append((m_new, l, acc))
        return tuple(out)

    lo = jnp.maximum(qi - (NSA_WINDOW // TK), 0) if mode == 'win' else 0
    init = tuple((jnp.full((1, TQ), NEG_BIG, _F32), jnp.zeros((1, TQ), _F32), jnp.zeros((HD, TQ), _F32)) for _ in heads)
    res = lax.fori_loop(lo, qi + 1, body, init)
    for hh in heads:
        m, l, acc = res[hh]
        o_ref[0, hh] = acc / jnp.maximum(l, 1e-30)


def attention_T(mode, qT, k, vT, bias_tiles, mask=None):
    B, H, _, T = qT.shape
    Hkv, L = k.shape[1], k.shape[2]
    grp = H // Hkv
    hp = HEADS_PER_STEP
    assert H % hp == 0 and (grp == 1 or grp % hp == 0)
    shared_kv = grp > 1
    kvh = 1 if shared_kv else hp
    kv_idx = (lambda g: g * hp // grp) if shared_kv else (lambda g: g)
    in_specs = [
        pl.BlockSpec((1, hp, HD, TQ), lambda b, g, i: (b, g, 0, i)),
        pl.BlockSpec((1, kvh, L, HD), lambda b, g, i: (b, kv_idx(g), 0, 0)),
        pl.BlockSpec((1, kvh, L // TK, HD, TK), lambda b, g, i: (b, kv_idx(g), 0, 0, 0)),
        pl.BlockSpec((hp, 3, TK, TQ), lambda b, g, i: (g, 0, 0, 0)),
    ]
    args = [qT, k, vT, bias_tiles]
    if mode == 'moba':
        in_specs.append(pl.BlockSpec((1, hp, L // TK, 1, TQ), lambda b, g, i: (b, g, 0, 0, i)))
        args.append(mask.reshape(B, H, L // TK, 1, T))
    elif mode == 'sel':
        per = TK // NSA_SEL_BLOCK
        in_specs.append(pl.BlockSpec((1, 1, L // TK, per, TQ), lambda b, g, i: (b, kv_idx(g), 0, 0, i)))
        args.append(mask.reshape(B, Hkv, L // TK, per, T))
    elif mode == 'dsa':
        in_specs.append(pl.BlockSpec((1, L, TQ), lambda b, g, i: (b, 0, i)))
        args.append(mask)
    return pl.pallas_call(
        functools.partial(_attn_kernel, mode, shared_kv),
        grid=(B, H // hp, T // TQ),
        in_specs=in_specs,
        out_specs=pl.BlockSpec((1, hp, HD, TQ), lambda b, g, i: (b, g, 0, i)),
        out_shape=jax.ShapeDtypeStruct((B, H, HD, T), _F32),
        compiler_params=pltpu.CompilerParams(
            dimension_semantics=("arbitrary", "arbitrary", "arbitrary"), vmem_limit_bytes=VMEM_LIMIT),
        name="attn_" + mode,
    )(*args)


def _moba_gate_kernel(qT_ref, k_ref, sel_ref):
    nblk, T = sel_ref.shape[2], sel_ref.shape[3]
    q = qT_ref[0, 0].astype(_BF16)
    k = k_ref[0, 0]
    km = jnp.mean(k.reshape(nblk, MOBA_BLOCK, HD), axis=1)
    gs = jnp.dot(km.astype(_BF16), q, preferred_element_type=_F32)
    j = lax.broadcasted_iota(jnp.int32, (nblk, T), 0)
    own = lax.broadcasted_iota(jnp.int32, (nblk, T), 1) // MOBA_BLOCK
    fully_past = j < own
    gs = jnp.where(fully_past, gs, -jnp.inf)
    rank = jnp.zeros((nblk, T), jnp.int32)
    for jp in range(nblk):
        row = gs[jp:jp + 1, :]
        ahead = (row > gs) | ((row == gs) & (jp < j))
        rank = rank + ahead.astype(jnp.int32)
    sel = (fully_past & (rank < MOBA_TOPK)) | (j == own)
    sel_ref[0, 0] = sel.astype(_F32)


def moba_gate(qT, k):
    B, H, _, T = qT.shape
    L = k.shape[2]
    nblk = L // MOBA_BLOCK
    return pl.pallas_call(
        _moba_gate_kernel,
        grid=(B, H),
        in_specs=[pl.BlockSpec((1, 1, HD, T), lambda b, h: (b, h, 0, 0)),
                  pl.BlockSpec((1, 1, L, HD), lambda b, h: (b, h, 0, 0))],
        out_specs=pl.BlockSpec((1, 1, nblk, T), lambda b, h: (b, h, 0, 0)),
        out_shape=jax.ShapeDtypeStruct((B, H, nblk, T), _F32),
        compiler_params=pltpu.CompilerParams(dimension_semantics=("arbitrary", "arbitrary"), vmem_limit_bytes=VMEM_LIMIT),
        name="moba_gate",
    )(qT, k)


def _count_rows(pred):
    return jnp.sum(pred.astype(_F32), axis=0, keepdims=True)


def _dsa_mask_kernel(topk, ki_ref, qiT_ref, wiT_ref, mask_ref, key_ref):
    L = ki_ref.shape[1]
    qi = pl.program_id(1)
    ki = ki_ref[0].astype(_BF16)
    score = jnp.zeros((L, TQ), _F32)
    for hh in range(DSA_IDX_HEADS):
        s = jnp.dot(ki, qiT_ref[0, hh].astype(_BF16), preferred_element_type=_F32) * DSA_IDX_DIM ** -0.5
        score = score + wiT_ref[0, hh:hh + 1, :] * jnp.maximum(s, 0.0)
    idx = lax.broadcasted_iota(jnp.int32, (L, TQ), 0)
    q_pos = qi * TQ + lax.broadcasted_iota(jnp.int32, (L, TQ), 1)
    adm = idx <= q_pos
    score = jnp.where(adm, jnp.where(score == 0.0, 0.0, score), -jnp.inf)
    bits = pltpu.bitcast(score, jnp.int32)
    key_ref[...] = jnp.where(bits < 0, bits ^ jnp.int32(0x7FFFFFFF), bits)
    kf = jnp.float32(topk)
    int_min = jnp.int32(-2 ** 31)
    sub = lax.broadcasted_iota(jnp.int32, (TK, TQ), 0)

    def count(pred):
        def tile(kt, acc):
            rows = pl.ds(pl.multiple_of(kt * TK, TK), TK)
            return acc + _count_rows(pred(key_ref[rows, :], kt * TK + sub))
        return lax.fori_loop(0, qi + 1, tile, jnp.zeros((1, TQ), _F32))

    lo = jnp.where(count(lambda k, i: k >= 0) >= kf, jnp.int32(0), int_min)

    def vstep(i, lo):
        cand = lo + jnp.left_shift(jnp.int32(1), 30 - i)
        return jnp.where(count(lambda k, i: k >= cand) >= kf, cand, lo)

    thr = lax.fori_loop(0, 31, vstep, lo)
    need = kf - count(lambda k, i: k > thr)
    nbits = max(1, (L - 1).bit_length())

    def istep(i, lo):
        cand = lo + jnp.left_shift(jnp.int32(1), nbits - 1 - i)
        c = count(lambda k, ix: (k == thr) & (ix < cand))
        return jnp.where(c < need, cand, lo)

    last_tie = lax.fori_loop(0, nbits, istep, jnp.zeros((1, TQ), jnp.int32))
    key = key_ref[...]
    keep = ((key > thr) | ((key == thr) & (idx <= last_tie))) & adm
    mask_ref[0] = keep.astype(mask_ref.dtype)


def dsa_mask(ki, qiT, wiT, topk):
    B, L, _ = ki.shape
    T = qiT.shape[3]
    return pl.pallas_call(
        functools.partial(_dsa_mask_kernel, topk),
        grid=(B, T // TQ),
        in_specs=[pl.BlockSpec((1, L, DSA_IDX_DIM), lambda b, i: (b, 0, 0)),
                  pl.BlockSpec((1, DSA_IDX_HEADS, DSA_IDX_DIM, TQ), lambda b, i: (b, 0, 0, i)),
                  pl.BlockSpec((1, DSA_IDX_HEADS, TQ), lambda b, i: (b, 0, i))],
        out_specs=pl.BlockSpec((1, L, TQ), lambda b, i: (b, 0, i)),
        out_shape=jax.ShapeDtypeStruct((B, L, T), _BF16),
        scratch_shapes=[pltpu.VMEM((L, TQ), jnp.int32)],
        compiler_params=pltpu.CompilerParams(dimension_semantics=("arbitrary", "arbitrary"), vmem_limit_bytes=VMEM_LIMIT),
        name="dsa_mask",
    )(ki, qiT, wiT)


PEER_TN = 256
PEER_EB = 1024
PEER_IB = PEER_EB // PEER_NKEYS
RANK_OUT = 4096.0


def _extract_top(work, n):
    rows = work.shape[0]
    iota = lax.broadcasted_iota(jnp.int32, work.shape, 0)
    order = jnp.full(work.shape, RANK_OUT, _F32)
    vals = []
    idx = None
    for a in range(n):
        m = jnp.max(work, axis=0, keepdims=True)
        idx = jnp.min(jnp.where(work == m, iota, rows), axis=0, keepdims=True)
        hit = iota == idx
        vals.append(m)
        order = jnp.where(hit, jnp.float32(a), order)
        work = jnp.where(hit, -jnp.inf, work)
    return vals, order, idx


def _peer_select_kernel(x_ref, g_ref, wqT_ref, keys_ref, hx_ref, c1_ref, e1n_ref, r2_ref, e2_ref):
    x = x_ref[...]
    y = x * lax.rsqrt(jnp.mean(x * x, axis=1, keepdims=True) + NORM_EPS) * g_ref[...]
    hx = y.astype(_BF16)
    hx_ref[...] = hx
    qT = lax.dot_general(wqT_ref[...], hx, (((1,), (1,)), ((), ())), preferred_element_type=_F32).astype(_BF16)
    half = PEER_QDIM // 2
    for h in range(PEER_HEADS):
        sc = []
        for c in range(2):
            r0 = (h * 2 + c) * half
            sc.append(jnp.dot(keys_ref[h * 2 + c], qT[r0:r0 + half, :], preferred_element_type=_F32))
        v1, o1, _ = _extract_top(sc[0], PEER_TOPK)
        v2, o2, _ = _extract_top(sc[1], PEER_TOPK)
        v1s = jnp.concatenate(v1, axis=0)
        v2s = jnp.concatenate(v2, axis=0)
        ex1s = jnp.exp(v1s - v1[0])
        ex2s = jnp.exp(v2s - v2[0])
        lo_a, hi_a = PEER_TOPK // 2, PEER_TOPK
        cand = jnp.concatenate([v1[0] + v2s] + [v1[a] + v2s[:lo_a] for a in range(1, lo_a)] + [v1s[lo_a:hi_a] + v2[0]], axis=0)
        ecand = jnp.concatenate([ex1s[0:1] * ex2s] + [ex1s[a:a + 1] * ex2s[:lo_a] for a in range(1, lo_a)]
                                + [ex1s[lo_a:hi_a] * ex2s[0:1]], axis=0)
        _, corder, _ = _extract_top(cand, PEER_TOPK)
        chosen = (corder < RANK_OUT).astype(_F32)
        z = jnp.sum(chosen * ecand, axis=0, keepdims=True)
        tail0 = PEER_TOPK + (lo_a - 1) * lo_a
        c1 = jnp.zeros(sc[0].shape, _F32)
        for a in range(PEER_TOPK):
            if a == 0:
                grp = chosen[0:PEER_TOPK]
            elif a < lo_a:
                grp = chosen[PEER_TOPK + (a - 1) * lo_a:PEER_TOPK + a * lo_a]
            else:
                grp = chosen[tail0 + a - lo_a:tail0 + a - lo_a + 1]
            cnt = jnp.sum(grp, axis=0, keepdims=True)
            c1 = jnp.where(o1 == jnp.float32(a), cnt, c1)
        c1_ref[h] = c1
        e1n_ref[h] = jnp.exp(sc[0] - v1[0]) / z
        r2_ref[h] = o2
        e2_ref[h] = jnp.exp(sc[1] - v2[0])


def peer_select(x, g, wqT, keys2):
    Np, D = x.shape
    tn = PEER_TN
    row = lambda: pl.BlockSpec((PEER_HEADS, PEER_NKEYS, tn), lambda t: (0, 0, t))
    rshape = jax.ShapeDtypeStruct((PEER_HEADS, PEER_NKEYS, Np), _F32)
    return pl.pallas_call(
        _peer_select_kernel,
        grid=(Np // tn,),
        in_specs=[pl.BlockSpec((tn, D), lambda t: (t, 0)),
                  pl.BlockSpec((1, D), lambda t: (0, 0)),
                  pl.BlockSpec((PEER_HEADS * PEER_QDIM, D), lambda t: (0, 0)),
                  pl.BlockSpec((2 * PEER_HEADS, PEER_NKEYS, PEER_QDIM // 2), lambda t: (0, 0, 0))],
        out_specs=[pl.BlockSpec((tn, D), lambda t: (t, 0)), row(), row(), row(), row()],
        out_shape=[jax.ShapeDtypeStruct((Np, D), _BF16), rshape, rshape, rshape, rshape],
        compiler_params=pltpu.CompilerParams(dimension_semantics=("arbitrary",), vmem_limit_bytes=VMEM_LIMIT),
        name="peer_select",
    )(x, g, wqT, keys2)


def _gelu_tanh(x):
    return 0.5 * x * (1.0 + jnp.tanh(math.sqrt(2.0 / math.pi) * (x + 0.044715 * (x * x * x))))


def _peer_dense_kernel(x_ref, hx_ref, u_ref, vT_ref, c1_ref, e1n_ref, r2_ref, e2_ref, o_ref, p_ref, acc_ref):
    e = pl.program_id(1)
    tn, D = x_ref.shape

    @pl.when(e == 0)
    def _():
        acc_ref[...] = jnp.zeros_like(acc_ref)

    nt = (((1,), (1,)), ((), ()))
    half = PEER_EB // 2
    for s in range(2):
        act = _gelu_tanh(lax.dot_general(u_ref[s * half:(s + 1) * half, :], hx_ref[...], nt, preferred_element_type=_F32))
        for ib in range(s * half // PEER_NKEYS, (s + 1) * half // PEER_NKEYS):
            w = jnp.zeros((PEER_NKEYS, tn), _F32)
            for h in range(PEER_HEADS):
                w = w + jnp.where(r2_ref[h] < c1_ref[h, ib:ib + 1, :], e1n_ref[h, ib:ib + 1, :] * e2_ref[h], 0.0)
            r0 = ib * PEER_NKEYS - s * half
            p_ref[ib * PEER_NKEYS:(ib + 1) * PEER_NKEYS, :] = (w * act[r0:r0 + PEER_NKEYS, :]).astype(_BF16)
    for s in range(2):
        rows = slice(s * (D // 2), (s + 1) * (D // 2))
        acc_ref[rows, :] += jnp.dot(vT_ref[rows, :], p_ref[...], preferred_element_type=_F32)

    @pl.when(e == pl.num_programs(1) - 1)
    def _():
        o_ref[...] = x_ref[...] + acc_ref[...].T


def peer_dense(x, hx, u, vT, c1, e1n, r2, e2):
    Np, D = x.shape
    E = u.shape[0]
    tn = PEER_TN
    rowi = lambda: pl.BlockSpec((PEER_HEADS, PEER_IB, tn), lambda t, e: (0, e, t))
    rowj = lambda: pl.BlockSpec((PEER_HEADS, PEER_NKEYS, tn), lambda t, e: (0, 0, t))
    return pl.pallas_call(
        _peer_dense_kernel,
        grid=(Np // tn, E // PEER_EB),
        in_specs=[pl.BlockSpec((tn, D), lambda t, e: (t, 0)),
                  pl.BlockSpec((tn, D), lambda t, e: (t, 0)),
                  pl.BlockSpec((PEER_EB, D), lambda t, e: (e, 0)),
                  pl.BlockSpec((D, PEER_EB), lambda t, e: (0, e)),
                  rowi(), rowi(), rowj(), rowj()],
        out_specs=pl.BlockSpec((tn, D), lambda t, e: (t, 0)),
        out_shape=jax.ShapeDtypeStruct((Np, D), _F32),
        scratch_shapes=[pltpu.VMEM((PEER_EB, tn), _BF16), pltpu.VMEM((D, tn), _F32)],
        compiler_params=pltpu.CompilerParams(dimension_semantics=("arbitrary", "arbitrary"), vmem_limit_bytes=VMEM_LIMIT),
        name="peer_dense",
    )(x, hx, u, vT, c1, e1n, r2, e2)


def peer_ffn_tokens(x_tokens, g, wq, keys, U, V):
    N, D = x_tokens.shape
    Np = -(-N // PEER_TN) * PEER_TN
    x = jnp.pad(x_tokens, ((0, Np - N), (0, 0)))
    keys2 = keys.reshape(2 * PEER_HEADS, PEER_NKEYS, PEER_QDIM // 2).astype(_BF16)
    hx, c1, e1n, r2, e2 = peer_select(x, g.reshape(1, D), wq.T.astype(_BF16), keys2)
    return peer_dense(x, hx, U.astype(_BF16), V.T.astype(_BF16), c1, e1n, r2, e2)[:N]


def bias_lookup(dist, tab):
    onehot = (rel_bucket(dist)[..., None] == jnp.arange(REL_BUCKETS)).astype(_F32)
    return jnp.einsum('...b,bh->h...', onehot, tab.astype(_F32), precision=lax.Precision.HIGHEST)


def rel_bias_tiles(tab):
    kk = jnp.arange(TK)[:, None]
    qq = jnp.arange(TQ)[None, :]
    dist = jnp.stack([d * TQ + qq - kk for d in range(3)])
    return bias_lookup(dist, tab)


ROWS_T = 8
DEC_ROWS = N_HEADS * ROWS_T
NEAR_TILES = 3


PP = 4


def _page_specs(block, n_pages):
    def spec(i):
        return pl.BlockSpec(block, lambda b, p, pt: (pt[b, jnp.minimum(PP * p + i, n_pages - 1)], 0, 0))
    return [spec(i) for i in range(PP)]


def _moba_decode_gate_kernel(pt_ref, q_ref, *rest):
    caches, gs_ref = rest[:PP], rest[PP]
    p = pl.program_id(1)
    per = MOBA_BLOCK // PAGE_SIZE

    @pl.when(p == 0)
    def _():
        gs_ref[...] = jnp.zeros_like(gs_ref)

    lane = lax.broadcasted_iota(jnp.int32, gs_ref.shape[1:], 1)
    for g in range(PP // per):
        ksum = sum(jnp.sum(caches[g * per + i][0], axis=0, keepdims=True) for i in range(per))
        km = (ksum * (1.0 / MOBA_BLOCK)).astype(_BF16).astype(_F32)
        col = jnp.sum(q_ref[0].astype(_F32) * km, axis=1, keepdims=True)
        gs_ref[0] = jnp.where(lane == p * (PP // per) + g, col, gs_ref[0])


def moba_decode_gate(pt, q_rows, cache_rows):
    B, n_pages = pt.shape
    W = N_HEADS * HD
    assert n_pages % PP == 0 and PP % (MOBA_BLOCK // PAGE_SIZE) == 0
    return pl.pallas_call(
        _moba_decode_gate_kernel,
        grid_spec=pltpu.PrefetchScalarGridSpec(
            num_scalar_prefetch=1,
            grid=(B, n_pages // PP),
            in_specs=[pl.BlockSpec((1, DEC_ROWS, W), lambda b, p, pt: (b, 0, 0))] + _page_specs((1, PAGE_SIZE, W), n_pages),
            out_specs=pl.BlockSpec((1, DEC_ROWS, 128), lambda b, p, pt: (b, 0, 0))),
        out_shape=jax.ShapeDtypeStruct((B, DEC_ROWS, 128), _F32),
        compiler_params=pltpu.CompilerParams(dimension_semantics=("arbitrary", "arbitrary"), vmem_limit_bytes=VMEM_LIMIT),
        name="moba_decode_gate",
    )(pt, q_rows, *([cache_rows] * PP))


def _paged_attn_kernel(mode, pt_ref, q_ref, *rest):
    caches = rest[:PP]
    new_ref, far_ref, near_ref, m_in_ref, o_ref, m_sc, l_sc, acc_sc, sel_sc = rest[PP:]
    p = pl.program_id(1)
    last = pl.num_programs(1) - 1
    n_pages = last * PP
    W = N_HEADS * HD
    lane = lax.broadcasted_iota(jnp.int32, (DEC_ROWS, PAGE_SIZE), 1)
    row_t = lax.broadcasted_iota(jnp.int32, (DEC_ROWS, PAGE_SIZE), 0) % ROWS_T

    @pl.when(p == 0)
    def _():
        m_sc[...] = jnp.full_like(m_sc, NEG_BIG)
        l_sc[...] = jnp.zeros_like(l_sc)
        acc_sc[...] = jnp.zeros_like(acc_sc)
        if mode == 'moba':
            nblk = n_pages // (MOBA_BLOCK // PAGE_SIZE)
            valid = lane < nblk
            gs = jnp.where(valid, m_in_ref[0], -jnp.inf)
            rank = jnp.zeros(gs.shape, jnp.int32)
            for jp in range(PAST_LEN // MOBA_BLOCK):
                col = gs[:, jp:jp + 1]
                rank = rank + ((col > gs) | ((col == gs) & (jp < lane))).astype(jnp.int32)
            sel_sc[...] = (valid & (rank < MOBA_TOPK)).astype(_F32)

    def tile(blk, g, i, is_new):
        k = blk[:, :W].astype(_BF16)
        v = blk[:, W:].astype(_BF16)
        s = lax.dot_general(q_ref[0], k, (((1,), (1,)), ((), ())), preferred_element_type=_F32)
        near = near_ref[jnp.clip(g - (n_pages + 1 - NEAR_TILES), 0, NEAR_TILES - 1)]
        s = s + jnp.where(g >= n_pages + 1 - NEAR_TILES, near, far_ref[...])
        if mode == 'moba':
            if is_new:
                ok = lane <= row_t
            else:
                ok = jnp.sum(jnp.where(lane == g // (MOBA_BLOCK // PAGE_SIZE), sel_sc[...], 0.0), axis=1, keepdims=True) > 0.5
        else:
            ok = jnp.concatenate([m_in_ref[0, i]] * N_HEADS, axis=0) > 0.5
        s = jnp.where(ok, s, NEG_BIG)
        m_new = jnp.maximum(m_sc[...], jnp.max(s, axis=1, keepdims=True))
        pr = jnp.where(ok, jnp.exp(s - m_new), 0.0)
        alpha = jnp.exp(m_sc[...] - m_new)
        l_sc[...] = alpha * l_sc[...] + jnp.sum(pr, axis=1, keepdims=True)
        acc_sc[...] = alpha * acc_sc[...] + jnp.dot(pr.astype(_BF16), v, preferred_element_type=_F32)
        m_sc[...] = m_new

    @pl.when(p < last)
    def _():
        for i in range(PP):
            tile(caches[i][0], PP * p + i, i, False)

    @pl.when(p == last)
    def _():
        tile(new_ref[0], n_pages, 0, True)
        o_ref[0] = acc_sc[...] / jnp.maximum(l_sc[...], 1e-30)


def paged_attention(mode, pt, q_rows, cache_rows, new_rows, far, near, m_in):
    B, n_pages = pt.shape
    W = N_HEADS * HD
    assert n_pages % PP == 0
    if mode == 'moba':
        m_spec = pl.BlockSpec((1, DEC_ROWS, 128), lambda b, p, pt: (b, 0, 0))
    else:
        m_spec = pl.BlockSpec((1, PP, ROWS_T, PAGE_SIZE), lambda b, p, pt: (b, p, 0, 0))
    return pl.pallas_call(
        functools.partial(_paged_attn_kernel, mode),
        grid_spec=pltpu.PrefetchScalarGridSpec(
            num_scalar_prefetch=1,
            grid=(B, n_pages // PP + 1),
            in_specs=[pl.BlockSpec((1, DEC_ROWS, W), lambda b, p, pt: (b, 0, 0))] + _page_specs((1, PAGE_SIZE, 2 * W), n_pages) + [
                      pl.BlockSpec((1, PAGE_SIZE, 2 * W), lambda b, p, pt: (b, 0, 0)),
                      pl.BlockSpec((DEC_ROWS, 1), lambda b, p, pt: (0, 0)),
                      pl.BlockSpec((NEAR_TILES, DEC_ROWS, PAGE_SIZE), lambda b, p, pt: (0, 0, 0)),
                      m_spec],
            out_specs=pl.BlockSpec((1, DEC_ROWS, W), lambda b, p, pt: (b, 0, 0)),
            scratch_shapes=[pltpu.VMEM((DEC_ROWS, 1), _F32), pltpu.VMEM((DEC_ROWS, 1), _F32),
                            pltpu.VMEM((DEC_ROWS, W), _F32), pltpu.VMEM((DEC_ROWS, 128), _F32)]),
        out_shape=jax.ShapeDtypeStruct((B, DEC_ROWS, W), _F32),
        compiler_params=pltpu.CompilerParams(dimension_semantics=("arbitrary", "arbitrary"), vmem_limit_bytes=VMEM_LIMIT),
        name="paged_attn_" + mode,
    )(pt, q_rows, *([cache_rows] * PP), new_rows, far, near, m_in)


def _dsa_decode_score_kernel(pt_ref, qi_ref, wi_ref, *rest):
    caches, new_ref, score_ref = rest[:PP], rest[PP], rest[PP + 1]
    p = pl.program_id(1)
    last = pl.num_programs(1) - 1

    def tile_scores(ki):
        ki = ki.astype(_BF16)
        acc = jnp.zeros((ROWS_T, PAGE_SIZE), _F32)
        for hh in range(DSA_IDX_HEADS):
            s = lax.dot_general(qi_ref[0, hh], ki, (((1,), (1,)), ((), ())), preferred_element_type=_F32) * DSA_IDX_DIM ** -0.5
            acc = acc + wi_ref[0, hh] * jnp.maximum(s, 0.0)
        return acc

    @pl.when(p < last)
    def _():
        for i in range(PP):
            score_ref[0, :, i * PAGE_SIZE:(i + 1) * PAGE_SIZE] = tile_scores(caches[i][0])

    @pl.when(p == last)
    def _():
        score_ref[0] = jnp.zeros(score_ref.shape[1:], _F32)
        score_ref[0, :, 0:PAGE_SIZE] = tile_scores(new_ref[0])


def dsa_decode_scores(pt, qi_rows, wi_rows, cache_rows, new_rows):
    B, n_pages = pt.shape
    assert n_pages % PP == 0
    return pl.pallas_call(
        _dsa_decode_score_kernel,
        grid_spec=pltpu.PrefetchScalarGridSpec(
            num_scalar_prefetch=1,
            grid=(B, n_pages // PP + 1),
            in_specs=[pl.BlockSpec((1, DSA_IDX_HEADS, ROWS_T, DSA_IDX_DIM), lambda b, p, pt: (b, 0, 0, 0)),
                      pl.BlockSpec((1, DSA_IDX_HEADS, ROWS_T, 1), lambda b, p, pt: (b, 0, 0, 0))]
            + _page_specs((1, PAGE_SIZE, DSA_IDX_DIM), n_pages)
            + [pl.BlockSpec((1, PAGE_SIZE, DSA_IDX_DIM), lambda b, p, pt: (b, 0, 0))],
            out_specs=pl.BlockSpec((1, ROWS_T, PP * PAGE_SIZE), lambda b, p, pt: (b, 0, p))),
        out_shape=jax.ShapeDtypeStruct((B, ROWS_T, (n_pages + PP) * PAGE_SIZE), _F32),
        compiler_params=pltpu.CompilerParams(dimension_semantics=("arbitrary", "arbitrary"), vmem_limit_bytes=VMEM_LIMIT),
        name="dsa_decode_scores",
    )(pt, qi_rows, wi_rows, *([cache_rows] * PP), new_rows)


def _count_lanes(pred):
    return jnp.sum(pred.astype(_F32), axis=1, keepdims=True)


def _dsa_decode_select_kernel(topk, past_len, score_ref, keep_ref, key_ref):
    Lp = score_ref.shape[2]
    idx = lax.broadcasted_iota(jnp.int32, (ROWS_T, Lp), 1)
    q_pos = past_len + lax.broadcasted_iota(jnp.int32, (ROWS_T, Lp), 0)
    adm = idx <= q_pos
    score = score_ref[0]
    score = jnp.where(adm, jnp.where(score == 0.0, 0.0, score), -jnp.inf)
    bits = pltpu.bitcast(score, jnp.int32)
    key_ref[...] = jnp.where(bits < 0, bits ^ jnp.int32(0x7FFFFFFF), bits)
    kf = jnp.float32(topk)
    lo = jnp.where(_count_lanes(key_ref[...] >= 0) >= kf, jnp.int32(0), jnp.int32(-2 ** 31))

    def vstep(i, lo):
        cand = lo + jnp.left_shift(jnp.int32(1), 30 - i)
        return jnp.where(_count_lanes(key_ref[...] >= cand) >= kf, cand, lo)

    thr = lax.fori_loop(0, 31, vstep, lo)
    need = kf - _count_lanes(key_ref[...] > thr)
    nbits = max(1, (Lp - 1).bit_length())

    def istep(i, lo):
        cand = lo + jnp.left_shift(jnp.int32(1), nbits - 1 - i)
        c = _count_lanes((key_ref[...] == thr) & (idx < cand))
        return jnp.where(c < need, cand, lo)

    last_tie = lax.fori_loop(0, nbits, istep, jnp.zeros((ROWS_T, 1), jnp.int32))
    key = key_ref[...]
    keep = ((key > thr) | ((key == thr) & (idx <= last_tie))) & adm
    keep_ref[0] = keep.astype(_F32)


def dsa_decode_select(score, topk, past_len):
    B, _, Lp = score.shape
    return pl.pallas_call(
        functools.partial(_dsa_decode_select_kernel, topk, past_len),
        grid=(B,),
        in_specs=[pl.BlockSpec((1, ROWS_T, Lp), lambda b: (b, 0, 0))],
        out_specs=pl.BlockSpec((1, ROWS_T, Lp), lambda b: (b, 0, 0)),
        out_shape=jax.ShapeDtypeStruct((B, ROWS_T, Lp), _F32),
        scratch_shapes=[pltpu.VMEM((ROWS_T, Lp), jnp.int32)],
        compiler_params=pltpu.CompilerParams(dimension_semantics=("arbitrary",), vmem_limit_bytes=VMEM_LIMIT),
        name="dsa_decode_select",
    )(score)


def decode_rows(q):
    B, T, H, _ = q.shape
    qp = jnp.pad(jnp.transpose(q, (0, 2, 1, 3)), ((0, 0), (0, 0), (0, ROWS_T - T), (0, 0)))
    rows = qp[:, :, :, None, :] * jnp.eye(H, dtype=q.dtype)[None, :, None, :, None]
    return rows.reshape(B, H * ROWS_T, H * HD)


def from_decode_rows(o, T):
    B = o.shape[0]
    o5 = o.reshape(B, N_HEADS, ROWS_T, N_HEADS, HD)
    own = jnp.sum(o5 * jnp.eye(N_HEADS, dtype=o.dtype)[None, :, None, :, None], axis=3)
    return jnp.transpose(own[:, :, :T], (0, 2, 1, 3)).reshape(B, T, N_HEADS * HD)


def decode_bias(tab, past_len, n_pages):
    t = jnp.arange(ROWS_T)[None, :, None]
    tile = (n_pages + 1 - NEAR_TILES + jnp.arange(NEAR_TILES))[:, None, None]
    dist = past_len + t - (tile * PAGE_SIZE + jnp.arange(PAGE_SIZE)[None, None, :])
    near = jnp.transpose(bias_lookup(dist, tab), (1, 0, 2, 3)).reshape(NEAR_TILES, DEC_ROWS, PAGE_SIZE)
    far = jnp.repeat(tab[REL_BUCKETS - 1].astype(_F32), ROWS_T).reshape(DEC_ROWS, 1)
    return far, near


def pad_rows(a, n):
    return jnp.pad(a, ((0, 0), (0, n - a.shape[1])) + ((0, 0),) * (a.ndim - 2))


def heads_T(z, nh):
    B, T, _ = z.shape
    return jnp.transpose(z.reshape(B, T, nh, HD), (0, 2, 3, 1))


def heads_K(z, nh):
    B, L, _ = z.shape
    return jnp.transpose(z.reshape(B, L, nh, HD), (0, 2, 1, 3))


def heads_VT(z, nh):
    B, L, _ = z.shape
    return jnp.transpose(z.reshape(B, L // TK, TK, nh, HD), (0, 3, 1, 4, 2))


def from_heads_T(oT):
    B, H, _, T = oT.shape
    return jnp.transpose(oT, (0, 3, 1, 2)).reshape(B, T, H * HD)


def nsa_attention_prompt(q_flat, gates_flat, nsa_new, q_pos, w1, w2, pe, tab, bias_tiles):
    B, T, _ = q_flat.shape
    G, H, L = NSA_KV_HEADS, N_HEADS, T
    scale = HD ** -0.5
    q = q_flat.reshape(B, T, H, HD)
    kc, vc = nsa_new[:, :, 0], nsa_new[:, :, 1]
    nch = L // NSA_CMP_STRIDE
    ncmp = nch - 1

    def compress(x, i):
        c = x[:, :nch * NSA_CMP_STRIDE].reshape(B, nch, NSA_CMP_STRIDE, G, HD)
        blk = jnp.concatenate([c[:, :-1], c[:, 1:]], axis=2) + pe[i][None, None, :, None, :]
        blk = jnp.transpose(blk, (0, 1, 3, 2, 4)).reshape(B, ncmp, G, NSA_CMP_LEN * HD)
        return jax.nn.gelu(blk @ w1[i]) @ w2[i]

    kcmp = compress(kc, 0)
    vcmp = compress(vc, 1)
    cmp_end = jnp.arange(ncmp) * NSA_CMP_STRIDE + NSA_CMP_LEN - 1
    qg = q.reshape(B, T, G, NSA_GROUP, HD)
    dist_c = q_pos[:, None] - cmp_end[None, :]
    lg_c = jnp.einsum('btgrd,bngd->btgrn', qg, kcmp).astype(jnp.float32).reshape(B, T, H, ncmp) * scale
    lg_c = lg_c + rel_bias(dist_c[None, :, None, :], tab)
    p_c = masked_softmax(lg_c, (dist_c >= 0)[None, :, None, :])
    p_cg = p_c.reshape(B, T, G, NSA_GROUP, ncmp)
    o_cmp = jnp.einsum('btgrn,bngd->btgrd', p_cg.astype(vcmp.dtype), vcmp).reshape(B, T, H, HD)
    nsel = -(-L // NSA_SEL_BLOCK)
    ii = np.arange(ncmp)[:, None]
    jj_np = np.arange(nsel)[None, :]
    overlap = ((ii * NSA_CMP_STRIDE < (jj_np + 1) * NSA_SEL_BLOCK) & (ii * NSA_CMP_STRIDE + NSA_CMP_LEN > jj_np * NSA_SEL_BLOCK)).astype(np.float32)
    imp = jnp.einsum('btgrn,nj->btgj', p_cg, jnp.asarray(overlap))
    cur = q_pos // NSA_SEL_BLOCK
    jj = jnp.arange(nsel)[None, :]
    causal_blk = jj <= cur[:, None]
    forced = (jj == 0) | (jj == cur[:, None]) | (jj == cur[:, None] - 1)
    imp = jnp.where(forced[None, :, None, :], jnp.inf, jnp.where(causal_blk[None, :, None, :], imp, -jnp.inf))
    _, sel = lax.top_k(imp, min(NSA_TOPN, nsel))
    chosen = jnp.any(sel[..., None] == jnp.arange(nsel), axis=-2)
    selT = jnp.transpose(chosen, (0, 2, 3, 1)).astype(_F32)
    qT = heads_T(q_flat, H) * scale
    flat = lambda a: a.reshape(B, L, G * HD)
    o_sel = from_heads_T(attention_T('sel', qT, heads_K(flat(nsa_new[:, :, 2]), G), heads_VT(flat(nsa_new[:, :, 3]), G), bias_tiles, selT))
    o_win = from_heads_T(attention_T('win', qT, heads_K(flat(nsa_new[:, :, 4]), G), heads_VT(flat(nsa_new[:, :, 5]), G), bias_tiles))
    g = jax.nn.sigmoid(gates_flat.reshape(B, T, H, 3).astype(jnp.float32))
    out = g[..., 0:1] * o_cmp + g[..., 1:2] * o_sel.reshape(B, T, H, HD) + g[..., 2:3] * o_win.reshape(B, T, H, HD)
    return out.reshape(B, T, H * HD)


def token_mix(z, past, lp, tab):
    B, T, _ = z.shape
    P = 0 if past is None else past['nsa'].shape[1]
    q_pos = P + jnp.arange(T, dtype=jnp.int32)
    qa, ka, va, q_nsa, nsa_kv, nsa_g, qc, kc, vc, qi, ki, wi, u, vg = split_last(z, IN_SPLITS)
    moba_new = jnp.stack([ka, va], axis=2).reshape(B, T, 2, N_HEADS, HD)
    nsa_new = nsa_kv.reshape(B, T, 6, NSA_KV_HEADS, HD)
    nsa_main_new = nsa_new[:, :, :4]
    win_new = nsa_new[:, :, 4:]
    dsa_new = jnp.stack([kc, vc], axis=2).reshape(B, T, 2, N_HEADS, HD)
    if past is None:
        nsa_all, win_all = nsa_main_new, win_new
    else:
        nsa_all = jnp.concatenate([past['nsa'], nsa_main_new], axis=1)
        win_all = jnp.concatenate([past['win'], win_new], axis=1)
    kw_pos0 = P + T - win_all.shape[1]
    scale = HD ** -0.5
    if past is None:
        bt = lp['bias_tiles']
        qaT = heads_T(qa, N_HEADS) * scale
        y_a = from_heads_T(attention_T('moba', qaT, heads_K(ka, N_HEADS), heads_VT(va, N_HEADS), bt[0], moba_gate(qaT, heads_K(ka, N_HEADS))))
        y_b = nsa_attention_prompt(q_nsa, nsa_g, nsa_new, q_pos, lp['cmp_w1'], lp['cmp_w2'], lp['cmp_pe'], tab[:, N_HEADS:2 * N_HEADS], bt[1])
        qiT = jnp.transpose(qi.reshape(B, T, DSA_IDX_HEADS, DSA_IDX_DIM), (0, 2, 3, 1))
        keep = dsa_mask(ki, qiT, jnp.transpose(wi, (0, 2, 1)), min(DSA_TOPK, T // 4))
        y_c = from_heads_T(attention_T('dsa', heads_T(qc, N_HEADS) * scale, heads_K(kc, N_HEADS), heads_VT(vc, N_HEADS), bt[2], keep))
    else:
        assert P % MOBA_BLOCK == 0 and T <= ROWS_T
        pt = past['pt']
        n_pages = pt.shape[1]
        far_a, near_a = decode_bias(tab[:, :N_HEADS], P, n_pages)
        qa_rows = (decode_rows(qa.reshape(B, T, N_HEADS, HD)) * scale).astype(_BF16)
        new_a = pad_rows(jnp.concatenate([ka, va], axis=-1), PAGE_SIZE)
        gs = moba_decode_gate(pt, qa_rows, past['moba_rows'])
        y_a = from_decode_rows(paged_attention('moba', pt, qa_rows, past['moba_rows'], new_a, far_a, near_a, gs), T)
        y_b = nsa_attention(q_nsa.reshape(B, T, N_HEADS, HD), nsa_g.reshape(B, T, N_HEADS, 3), nsa_all[:, :, 0], nsa_all[:, :, 1], nsa_all[:, :, 2], nsa_all[:, :, 3], win_all[:, :, 0], win_all[:, :, 1], kw_pos0, q_pos, lp['cmp_w1'], lp['cmp_w2'], lp['cmp_pe'], tab[:, N_HEADS:2 * N_HEADS])
        qi_rows = pad_rows(jnp.transpose(qi.reshape(B, T, DSA_IDX_HEADS, DSA_IDX_DIM), (0, 2, 1, 3)).reshape(B * DSA_IDX_HEADS, T, DSA_IDX_DIM), ROWS_T)
        qi_rows = qi_rows.reshape(B, DSA_IDX_HEADS, ROWS_T, DSA_IDX_DIM).astype(_BF16)
        wi_rows = pad_rows(jnp.transpose(wi, (0, 2, 1)).reshape(B * DSA_IDX_HEADS, T), ROWS_T).reshape(B, DSA_IDX_HEADS, ROWS_T, 1)
        score = dsa_decode_scores(pt, qi_rows, wi_rows, past['idx_rows'], pad_rows(ki, PAGE_SIZE))
        keep = dsa_decode_select(score, min(DSA_TOPK, (P + T) // 4), P)
        keep = jnp.transpose(keep.reshape(B, ROWS_T, n_pages + PP, PAGE_SIZE), (0, 2, 1, 3))
        far_c, near_c = decode_bias(tab[:, 2 * N_HEADS:], P, n_pages)
        qc_rows = (decode_rows(qc.reshape(B, T, N_HEADS, HD)) * scale).astype(_BF16)
        new_c = pad_rows(jnp.concatenate([kc, vc], axis=-1), PAGE_SIZE)
        y_c = from_decode_rows(paged_attention('dsa', pt, qc_rows, past['dsa_rows'], new_c, far_c, near_c, keep), T)
    y_d, v_rows = gmlp_sgu(u, vg, lp['gm_ws'], lp['gm_b'], lp['gm_norm'])
    branches = [y.reshape(B * T, MIX_W) for y in (y_a, y_b, y_c, y_d)]
    win_keep = win_all[:, -min(NSA_WINDOW, P + T):]
    return branches, (moba_new, nsa_main_new, win_keep, dsa_new, ki, v_rows)


def gated_branch_mix(branches, gz, lp):
    mixed = 0.0
    for n in range(N_BRANCH):
        gate = jax.nn.sigmoid(gz[:, n * D_MODEL:(n + 1) * D_MODEL] + lp['b_gate'][n])
        mixed = mixed + gate * (branches[n] @ lp['w_branch'][n])
    return mixed @ lp['w_out']


def mem_attention_q(q, mkv, wo):
    B, T, _ = q.shape
    q = q.reshape(B, T, MEM_HEADS, MEM_HD)
    lg = jnp.einsum('bthd,bmhd->bthm', q, mkv[:, :, 0]).astype(jnp.float32) * MEM_HD ** -0.5
    p = jax.nn.softmax(lg, axis=-1)
    o = jnp.einsum('bthm,bmhd->bthd', p.astype(mkv.dtype), mkv[:, :, 1]).reshape(B, T, MEM_W)
    return o @ wo


def layer_pair(rows, shape_p, shape_s, past, mkv_p, mkv_s, lp, tab):
    n_p = shape_p[0] * shape_p[1]
    split = lambda a: (a[:n_p].reshape(shape_p[0], shape_p[1], -1), a[n_p:].reshape(shape_s[0], shape_s[1], -1))
    z_p, z_s = split(norm_matmul(rows, lp['norm_mix'], lp['w_in']))
    gz = norm_matmul(rows, lp['norm_mix'], lp['w_gate'])
    br_p, rp = token_mix(z_p, None, lp, tab)
    br_s, rs = token_mix(z_s, past, lp, tab)
    branches = [jnp.concatenate([bp, bs], axis=0) for bp, bs in zip(br_p, br_s)]
    rows = rows + gated_branch_mix(branches, gz, lp)
    q_p, q_s = split(norm_matmul(rows, lp['norm_mem'], lp['w_mem_q']))
    mem_p = mem_attention_q(q_p, mkv_p, lp['w_mem_o'])
    mem_s = mem_attention_q(q_s, mkv_s, lp['w_mem_o'])
    rows = rows + jnp.concatenate([mem_p.reshape(-1, D_MODEL), mem_s.reshape(-1, D_MODEL)], axis=0)
    rows = peer_ffn_tokens(rows, lp['norm_ffn'], lp['peer_wq'], lp['peer_keys'], lp['peer_u'], lp['peer_v'])
    return rows, rp, rs


def kernel(x_prompt, x_sample, cache_moba_kv, cache_nsa_kv, cache_dsa_kv, cache_dsa_idx, state_nsa_win, cache_mem_kv, page_table, mem_prompt, rel_bias_table, w_in, nsa_cmp_w1, nsa_cmp_w2, nsa_cmp_pe, gm_ws, gm_b, gm_norm, w_branch, w_gate, b_gate, w_out, w_mem_q, w_mem_kv, w_mem_o, peer_wq, peer_keys, peer_u, peer_v, norm_mix, norm_mem, norm_ffn, norm_final):
    n_p = x_prompt.shape[0] * x_prompt.shape[1]
    rows = jnp.concatenate([x_prompt.reshape(-1, D_MODEL), x_sample.reshape(-1, D_MODEL)], axis=0)
    rows_p = []
    rows_s = []
    mem_rows = []
    Bp = x_prompt.shape[0]
    n_pool = cache_moba_kv.shape[1]
    bias_tiles =[rel_bias_tiles(rel_bias_table[:, n * N_HEADS:(n + 1) * N_HEADS]) for n in range(3)]
    for l in range(DEPTH):
        lp = {
            'w_in': w_in[l], 'cmp_w1': nsa_cmp_w1[l], 'cmp_w2': nsa_cmp_w2[l], 'cmp_pe': nsa_cmp_pe[l],
            'gm_ws': gm_ws[l], 'gm_b': gm_b[l], 'gm_norm': gm_norm[l],
            'w_branch': w_branch[l], 'w_gate': w_gate[l], 'b_gate': b_gate[l], 'w_out': w_out[l],
            'w_mem_q': w_mem_q[l], 'w_mem_o': w_mem_o[l],
            'peer_wq': peer_wq[l], 'peer_keys': peer_keys[l], 'peer_u': peer_u[l], 'peer_v': peer_v[l],
            'norm_mix': norm_mix[l], 'norm_mem': norm_mem[l], 'norm_ffn': norm_ffn[l],
        }
        mkv_p = (mem_prompt @ w_mem_kv[l]).reshape(Bp, MEM_TOKENS, 2, MEM_HEADS, MEM_HD)
        past = {
            'pt': page_table + l * n_pool,
            'moba_rows': cache_moba_kv.reshape(DEPTH * n_pool, PAGE_SIZE, 2 * MIX_W),
            'dsa_rows': cache_dsa_kv.reshape(DEPTH * n_pool, PAGE_SIZE, 2 * MIX_W),
            'idx_rows': cache_dsa_idx.reshape(DEPTH * n_pool, PAGE_SIZE, DSA_IDX_DIM),
            'nsa': gather_pages(cache_nsa_kv[l], page_table),
            'win': state_nsa_win[l],
        }
        lp['bias_tiles'] = bias_tiles
        rows, rp, rs = layer_pair(rows, x_prompt.shape, x_sample.shape, past, mkv_p, cache_mem_kv[l], lp, rel_bias_table)
        rows_p.append(rp)
        rows_s.append(rs)
        mem_rows.append(mkv_p)
    y_rows = rmsnorm(rows, norm_final)
    y_prompt = y_rows[:n_p].reshape(x_prompt.shape)
    y_sample = y_rows[n_p:].reshape(x_sample.shape)
    new_moba_kv_prompt = jnp.stack([r[0] for r in rows_p])
    new_moba_kv_sample = jnp.stack([r[0] for r in rows_s])
    new_nsa_kv_prompt = jnp.stack([r[1] for r in rows_p])
    new_nsa_kv_sample = jnp.stack([r[1] for r in rows_s])
    new_nsa_win_prompt = jnp.stack([r[2] for r in rows_p])
    new_nsa_win_sample = jnp.stack([r[2] for r in rows_s])
    new_dsa_kv_prompt = jnp.stack([r[3] for r in rows_p])
    new_dsa_kv_sample = jnp.stack([r[3] for r in rows_s])
    new_dsa_idx_prompt = jnp.stack([r[4] for r in rows_p])
    new_dsa_idx_sample = jnp.stack([r[4] for r in rows_s])
    new_mem_kv_prompt = jnp.stack(mem_rows)
    new_gmlp_v_sample = jnp.stack([r[5] for r in rows_s])
    return (y_prompt, y_sample, new_moba_kv_prompt, new_moba_kv_sample, new_nsa_kv_prompt, new_nsa_kv_sample, new_nsa_win_prompt, new_nsa_win_sample, new_dsa_kv_prompt, new_dsa_kv_sample, new_dsa_idx_prompt, new_dsa_idx_sample, new_mem_kv_prompt, new_gmlp_v_sample)
```

```python
import functools
import math
import jax
import jax.numpy as jnp
from jax import lax
import numpy as np
from jax.experimental import pallas as pl
from jax.experimental.pallas import tpu as pltpu

D_MODEL = 2048
BATCH = 4
SEQ = 2048
DEPTH = 2
DEC_BATCH = 8
DEC_SEQ = 4
PAST_LEN = 16384
PAGE_SIZE = 128

HD = 64
N_HEADS = 8
MIX_W = N_HEADS * HD
N_BRANCH = 4
MOBA_BLOCK = 256
MOBA_TOPK = 3
MOBA_QB = 32
NSA_KV_HEADS = 2
NSA_GROUP = N_HEADS // NSA_KV_HEADS
NSA_CMP_STRIDE = 16
NSA_CMP_LEN = 2 * NSA_CMP_STRIDE
NSA_SEL_BLOCK = 64
NSA_TOPN = 16
NSA_WINDOW = 512
NSA_QB = 64
WIN_QB = 128
DSA_IDX_HEADS = 4
DSA_IDX_DIM = 64
DSA_TOPK = 256
DSA_QB = 64
GM_CHUNK = 128
GM_GROUPS = 8
GM_W = 512
REL_BUCKETS = 32
REL_MAX_DIST = 128
N_REL_HEADS = 3 * N_HEADS
MEM_TOKENS = 256
MEM_HEADS = 4
MEM_HD = 128
MEM_W = MEM_HEADS * MEM_HD
PEER_HEADS = 8
PEER_NKEYS = 128
PEER_QDIM = 128
PEER_TOPK = 16
PEER_EXPERTS = PEER_NKEYS ** 2
PEER_BLOCK = 128
NORM_EPS = 1e-6
IN_SPLITS = (MIX_W, MIX_W, MIX_W, MIX_W, 6 * NSA_KV_HEADS * HD, 3 * N_HEADS, MIX_W, MIX_W, MIX_W, DSA_IDX_HEADS * DSA_IDX_DIM, DSA_IDX_DIM, DSA_IDX_HEADS, GM_W, GM_W)
IN_COLS = sum(IN_SPLITS)


def rmsnorm(x, g):
    xf = x.astype(jnp.float32)
    y = xf * lax.rsqrt(jnp.mean(xf * xf, axis=-1, keepdims=True) + NORM_EPS)
    return (y * g.astype(jnp.float32)).astype(x.dtype)


def masked_softmax(logits, mask):
    l = jnp.where(mask, logits.astype(jnp.float32), -jnp.inf)
    m = jnp.max(l, axis=-1, keepdims=True)
    m = jnp.where(jnp.isfinite(m), m, 0.0)
    e = jnp.exp(l - m)
    return e / jnp.maximum(jnp.sum(e, axis=-1, keepdims=True), 1e-30)


def rel_bucket(dist):
    n = jnp.maximum(dist, 0)
    max_exact = REL_BUCKETS // 2
    nf = jnp.maximum(n, 1).astype(jnp.float32)
    large = max_exact + (jnp.log(nf / max_exact) / math.log(REL_MAX_DIST / max_exact) * (REL_BUCKETS - max_exact)).astype(jnp.int32)
    return jnp.where(n < max_exact, n, jnp.minimum(large, REL_BUCKETS - 1))


def rel_bias(dist, tab):
    onehot = (rel_bucket(dist)[..., None] == jnp.arange(REL_BUCKETS)).astype(jnp.float32)
    spec = '...okb,bh->...hk' if dist.shape[-2] == 1 else '...hkb,bh->...hk'
    return jnp.einsum(spec, onehot, tab.astype(jnp.float32), precision=lax.Precision.HIGHEST)


def split_last(z, sizes):
    return jnp.split(z, np.cumsum(sizes)[:-1].tolist(), axis=-1)


def map_query_blocks(fn, qb, q_pos, *xs):
    T = q_pos.shape[0]
    qb = qb if T % qb == 0 else T
    nb = T // qb
    xb = tuple(jnp.moveaxis(x.reshape(x.shape[0], nb, qb, *x.shape[2:]), 1, 0) for x in xs)
    out = lax.map(lambda a: fn(*a), (q_pos.reshape(nb, qb),) + xb)
    return jnp.moveaxis(out, 0, 1).reshape(out.shape[1], T, *out.shape[3:])


def gather_pages(pool, page_table):
    g = pool[page_table]
    return g.reshape(g.shape[0], g.shape[1] * g.shape[2], *g.shape[3:])


def moba_attention(q, k, v, q_pos, tab):
    B, L, H, _ = k.shape
    nblk = -(-L // MOBA_BLOCK)
    pad = ((0, 0), (0, nblk * MOBA_BLOCK - L), (0, 0), (0, 0))
    kb = jnp.pad(k, pad).reshape(B, nblk, MOBA_BLOCK, H, HD)
    vb = jnp.pad(v, pad).reshape(B, nblk, MOBA_BLOCK, H, HD)
    k_mean = jnp.mean(kb.astype(jnp.float32), axis=2).astype(k.dtype)
    kb_t = jnp.transpose(kb, (0, 3, 1, 2, 4))
    vb_t = jnp.transpose(vb, (0, 3, 1, 2, 4))
    ntop = min(MOBA_TOPK, nblk)
    bi = jnp.arange(B)[:, None, None, None]
    hi = jnp.arange(H)[None, None, :, None]
    off = jnp.arange(MOBA_BLOCK)
    scale = HD ** -0.5

    def block_fn(qp, qq):
        qb = qp.shape[0]
        own = qp // MOBA_BLOCK
        gs = jnp.einsum('bqhd,bjhd->bqhj', qq, k_mean).astype(jnp.float32)
        fully_past = jnp.arange(nblk)[None, :] < own[:, None]
        gs = jnp.where(fully_past[None, :, None, :], gs, -jnp.inf)
        _, top = lax.top_k(gs, ntop)
        own_b = jnp.broadcast_to(own[None, :, None, None], (B, qb, H, 1))
        sel = jnp.concatenate([top, own_b], axis=-1)
        sel_ok = jnp.concatenate([top < own[None, :, None, None], jnp.ones(own_b.shape, bool)], axis=-1)
        kg = kb_t[bi, hi, sel].reshape(B, qb, H, (ntop + 1) * MOBA_BLOCK, HD)
        vg = vb_t[bi, hi, sel].reshape(B, qb, H, (ntop + 1) * MOBA_BLOCK, HD)
        pos = (sel[..., None] * MOBA_BLOCK + off).reshape(B, qb, H, -1)
        dist = qp[None, :, None, None] - pos
        ok = jnp.repeat(sel_ok, MOBA_BLOCK, axis=-1) & (dist >= 0)
        lg = jnp.einsum('bqhd,bqhkd->bqhk', qq, kg).astype(jnp.float32) * scale + rel_bias(dist, tab)
        p = masked_softmax(lg, ok)
        return jnp.einsum('bqhk,bqhkd->bqhd', p.astype(vg.dtype), vg)

    return map_query_blocks(block_fn, MOBA_QB, q_pos, q)


def nsa_attention(q, gates, kc, vc, ks, vs, kw, vw, kw_pos0, q_pos, w1, w2, pe, tab):
    B, L, G, _ = kc.shape
    T = q.shape[1]
    H = N_HEADS
    scale = HD ** -0.5
    nch = L // NSA_CMP_STRIDE
    ncmp = nch - 1

    def compress(x, i):
        c = x[:, :nch * NSA_CMP_STRIDE].reshape(B, nch, NSA_CMP_STRIDE, G, HD)
        blk = jnp.concatenate([c[:, :-1], c[:, 1:]], axis=2) + pe[i][None, None, :, None, :]
        blk = jnp.transpose(blk, (0, 1, 3, 2, 4)).reshape(B, ncmp, G, NSA_CMP_LEN * HD)
        return jax.nn.gelu(blk @ w1[i]) @ w2[i]

    kcmp = compress(kc, 0)
    vcmp = compress(vc, 1)
    cmp_end = jnp.arange(ncmp) * NSA_CMP_STRIDE + NSA_CMP_LEN - 1
    qg = q.reshape(B, T, G, NSA_GROUP, HD)
    dist_c = q_pos[:, None] - cmp_end[None, :]
    lg_c = jnp.einsum('btgrd,bngd->btgrn', qg, kcmp).astype(jnp.float32).reshape(B, T, H, ncmp) * scale
    lg_c = lg_c + rel_bias(dist_c[None, :, None, :], tab)
    p_c = masked_softmax(lg_c, (dist_c >= 0)[None, :, None, :])
    p_cg = p_c.reshape(B, T, G, NSA_GROUP, ncmp)
    o_cmp = jnp.einsum('btgrn,bngd->btgrd', p_cg.astype(vcmp.dtype), vcmp).reshape(B, T, H, HD)
    nsel = -(-L // NSA_SEL_BLOCK)
    ii = np.arange(ncmp)[:, None]
    jj_np = np.arange(nsel)[None, :]
    overlap = ((ii * NSA_CMP_STRIDE < (jj_np + 1) * NSA_SEL_BLOCK) & (ii * NSA_CMP_STRIDE + NSA_CMP_LEN > jj_np * NSA_SEL_BLOCK)).astype(np.float32)
    imp = jnp.einsum('btgrn,nj->btgj', p_cg, jnp.asarray(overlap))
    cur = q_pos // NSA_SEL_BLOCK
    jj = jnp.arange(nsel)[None, :]
    causal_blk = jj <= cur[:, None]
    forced = (jj == 0) | (jj == cur[:, None]) | (jj == cur[:, None] - 1)
    imp = jnp.where(forced[None, :, None, :], jnp.inf, jnp.where(causal_blk[None, :, None, :], imp, -jnp.inf))
    ntop = min(NSA_TOPN, nsel)
    _, sel = lax.top_k(imp, ntop)
    pad = ((0, 0), (0, nsel * NSA_SEL_BLOCK - L), (0, 0), (0, 0))
    ksb = jnp.transpose(jnp.pad(ks, pad).reshape(B, nsel, NSA_SEL_BLOCK, G, HD), (0, 3, 1, 2, 4))
    vsb = jnp.transpose(jnp.pad(vs, pad).reshape(B, nsel, NSA_SEL_BLOCK, G, HD), (0, 3, 1, 2, 4))
    bi = jnp.arange(B)[:, None, None, None]
    gi = jnp.arange(G)[None, None, :, None]
    off = jnp.arange(NSA_SEL_BLOCK)

    def sel_fn(qp, qq, ss):
        qb = qp.shape[0]
        kg = ksb[bi, gi, ss].reshape(B, qb, G, ntop * NSA_SEL_BLOCK, HD)
        vg = vsb[bi, gi, ss].reshape(B, qb, G, ntop * NSA_SEL_BLOCK, HD)
        pos = (ss[..., None] * NSA_SEL_BLOCK + off).reshape(B, qb, G, -1)
        dist = jnp.repeat(qp[None, :, None, None] - pos, NSA_GROUP, axis=2)
        lg = jnp.einsum('bqgrd,bqgkd->bqgrk', qq.reshape(B, qb, G, NSA_GROUP, HD), kg).astype(jnp.float32).reshape(B, qb, H, -1) * scale
        p = masked_softmax(lg + rel_bias(dist, tab), dist >= 0)
        o = jnp.einsum('bqgrk,bqgkd->bqgrd', p.reshape(B, qb, G, NSA_GROUP, -1).astype(vg.dtype), vg)
        return o.reshape(B, qb, H, HD)

    o_sel = map_query_blocks(sel_fn, NSA_QB, q_pos, q, sel)
    W = NSA_WINDOW
    wpad = ((0, 0), (W, 0), (0, 0), (0, 0))
    kwp = jnp.pad(kw, wpad)
    vwp = jnp.pad(vw, wpad)

    def win_fn(qp, qq):
        qb = qp.shape[0]
        span = W + qb - 1
        start = qp[0] - kw_pos0 + 1
        kk = lax.dynamic_slice_in_dim(kwp, start, span, axis=1)
        vv = lax.dynamic_slice_in_dim(vwp, start, span, axis=1)
        pos = qp[0] - W + 1 + jnp.arange(span)
        dist = qp[:, None] - pos[None, :]
        ok = (dist >= 0) & (dist < W) & (pos[None, :] >= 0)
        lg = jnp.einsum('bqgrd,bkgd->bqgrk', qq.reshape(B, qb, G, NSA_GROUP, HD), kk).astype(jnp.float32).reshape(B, qb, H, span) * scale
        p = masked_softmax(lg + rel_bias(dist[None, :, None, :], tab), ok[None, :, None, :])
        o = jnp.einsum('bqgrk,bkgd->bqgrd', p.reshape(B, qb, G, NSA_GROUP, span).astype(vv.dtype), vv)
        return o.reshape(B, qb, H, HD)

    o_win = map_query_blocks(win_fn, WIN_QB, q_pos, q)
    g = jax.nn.sigmoid(gates.astype(jnp.float32)).astype(q.dtype)
    return g[..., 0:1] * o_cmp + g[..., 1:2] * o_sel + g[..., 2:3] * o_win


def dsa_attention(q, qi, wi, k, v, ki, q_pos, tab):
    B, L, H, _ = k.shape
    kk = min(DSA_TOPK, L // 4)
    bi = jnp.arange(B)[:, None, None]
    scale = HD ** -0.5

    def block_fn(qp, qq, qqi, wwi):
        s = jax.nn.relu(jnp.einsum('bqhd,bld->bqhl', qqi, ki).astype(jnp.float32) * DSA_IDX_DIM ** -0.5)
        score = jnp.einsum('bqh,bqhl->bql', wwi.astype(jnp.float32), s)
        admissible = jnp.arange(L)[None, :] <= qp[:, None]
        score = jnp.where(admissible[None], score, -jnp.inf)
        _, sel = lax.top_k(score, kk)
        kg = k[bi, sel]
        vg = v[bi, sel]
        dist = (qp[None, :, None] - sel)[:, :, None, :]
        lg = jnp.einsum('bqhd,bqkhd->bqhk', qq, kg).astype(jnp.float32) * scale + rel_bias(dist, tab)
        p = masked_softmax(lg, dist >= 0)
        return jnp.einsum('bqhk,bqkhd->bqhd', p.astype(vg.dtype), vg)

    return map_query_blocks(block_fn, DSA_QB, q_pos, q, qi, wi)


def gmlp_sgu(u, v, ws, b, gnorm):
    B, T, _ = u.shape
    vn = rmsnorm(v, gnorm)
    nc = -(-T // GM_CHUNK)
    vc = jnp.pad(vn, ((0, 0), (0, nc * GM_CHUNK - T), (0, 0))).reshape(B, nc, GM_CHUNK, GM_GROUPS, GM_W // GM_GROUPS)
    wsm = ws * jnp.tril(jnp.ones((GM_CHUNK, GM_CHUNK), ws.dtype))
    mixed = jnp.einsum('gij,bcjgd->bcigd', wsm, vc) + b.T[None, None, :, :, None]
    mixed = mixed.reshape(B, nc * GM_CHUNK, GM_W)[:, :T]
    return u * mixed, vn


def mem_attention(h, mkv, wq, wo):
    B, T, _ = h.shape
    q = (h @ wq).reshape(B, T, MEM_HEADS, MEM_HD)
    lg = jnp.einsum('bthd,bmhd->bthm', q, mkv[:, :, 0]).astype(jnp.float32) * MEM_HD ** -0.5
    p = jax.nn.softmax(lg, axis=-1)
    o = jnp.einsum('bthm,bmhd->bthd', p.astype(mkv.dtype), mkv[:, :, 1]).reshape(B, T, MEM_W)
    return o @ wo


def peer_ffn(h, wq, keys, U, V):
    B, T, D = h.shape
    N = B * T
    nb = -(-N // PEER_BLOCK)
    hf = jnp.pad(h.reshape(N, D), ((0, nb * PEER_BLOCK - N), (0, 0))).reshape(nb, PEER_BLOCK, D)

    def block_fn(x):
        q = (x @ wq).reshape(PEER_BLOCK, PEER_HEADS, 2, PEER_QDIM // 2)
        s = jnp.einsum('nhcd,hckd->nhck', q, keys).astype(jnp.float32)
        v1, i1 = lax.top_k(s[:, :, 0], PEER_TOPK)
        v2, i2 = lax.top_k(s[:, :, 1], PEER_TOPK)
        cand = (v1[..., :, None] + v2[..., None, :]).reshape(PEER_BLOCK, PEER_HEADS, PEER_TOPK * PEER_TOPK)
        sv, ci = lax.top_k(cand, PEER_TOPK)
        e = jnp.take_along_axis(i1, ci // PEER_TOPK, axis=-1) * PEER_NKEYS + jnp.take_along_axis(i2, ci % PEER_TOPK, axis=-1)
        g = jax.nn.softmax(sv, axis=-1)
        ug = U[e]
        vg = V[e]
        act = jax.nn.gelu(jnp.einsum('nd,nhkd->nhk', x, ug).astype(jnp.float32))
        return jnp.einsum('nhk,nhkd->nd', (g * act).astype(x.dtype), vg)

    out = lax.map(block_fn, hf).reshape(nb * PEER_BLOCK, D)[:N]
    return out.reshape(B, T, D)


def _norm_matmul_kernel(x_ref, g_ref, w_ref, o_ref, xn_ref):
    @pl.when(pl.program_id(1) == 0)
    def _():
        x = x_ref[...]
        y = x * lax.rsqrt(jnp.mean(x * x, axis=-1, keepdims=True) + NORM_EPS)
        xn_ref[...] = (y * g_ref[...]).astype(jnp.bfloat16)

    hw = w_ref.shape[1] // 2
    for s in range(2):
        cols = slice(s * hw, (s + 1) * hw)
        o_ref[:, cols] = jnp.dot(xn_ref[...], w_ref[:, cols].astype(jnp.bfloat16), preferred_element_type=jnp.float32)


def norm_matmul(x, g, w, tm=512, tn=512):
    N, D = x.shape
    C = w.shape[1]
    tm = min(tm, N)
    tn = min(tn, C)
    return pl.pallas_call(
        _norm_matmul_kernel,
        grid=(pl.cdiv(N, tm), pl.cdiv(C, tn)),
        in_specs=[
            pl.BlockSpec((tm, D), lambda i, j: (i, 0)),
            pl.BlockSpec((1, D), lambda i, j: (0, 0)),
            pl.BlockSpec((D, tn), lambda i, j: (0, j)),
        ],
        out_specs=pl.BlockSpec((tm, tn), lambda i, j: (i, j)),
        out_shape=jax.ShapeDtypeStruct((N, C), jnp.float32),
        scratch_shapes=[pltpu.VMEM((tm, D), jnp.bfloat16)],
        compiler_params=pltpu.CompilerParams(
            dimension_semantics=("arbitrary", "arbitrary"), vmem_limit_bytes=48 * 1024 * 1024),
        name="norm_matmul",
    )(x, g.reshape(1, D), w.astype(jnp.bfloat16))


TQ = 256
TK = 256
NEG_BIG = -1e30
VMEM_LIMIT = 48 * 1024 * 1024
_BF16 = jnp.bfloat16
_F32 = jnp.float32


HEADS_PER_STEP = 2


def _attn_kernel(mode, shared_kv, qT_ref, k_ref, vT_ref, bias_ref, *rest):
    if mode == 'win':
        (o_ref,) = rest
        m_ref = None
    else:
        m_ref, o_ref = rest
    qi = pl.program_id(2)
    heads = range(HEADS_PER_STEP)
    q = [qT_ref[0, hh].astype(_BF16) for hh in heads]
    lane = lax.broadcasted_iota(jnp.int32, (TK, TQ), 1)
    sub = lax.broadcasted_iota(jnp.int32, (TK, TQ), 0)
    rel0 = lane - sub

    def body(kj, carry):
        dist = rel0 + (qi - kj) * TQ
        ok_all = dist >= 0
        if mode == 'sel':
            rows = m_ref[0, 0, kj]
            blk = sub // NSA_SEL_BLOCK
            r = jnp.where(blk == 0, rows[0:1], jnp.where(blk == 1, rows[1:2], jnp.where(blk == 2, rows[2:3], rows[3:4])))
            ok_all = ok_all & (r > 0.5)
        elif mode == 'dsa':
            ok_all = ok_all & (m_ref[0, pl.ds(pl.multiple_of(kj * TK, TK), TK), :].astype(_F32) > 0.5)
        elif mode == 'win':
            ok_all = ok_all & (dist < NSA_WINDOW)
        out = []
        for hh in heads:
            m, l, acc = carry[hh]
            hk = 0 if shared_kv else hh
            k_t = k_ref[0, hk, pl.ds(pl.multiple_of(kj * TK, TK), TK), :].astype(_BF16)
            s = jnp.dot(k_t, q[hh], preferred_element_type=_F32)
            s = s + bias_ref[hh, jnp.minimum(qi - kj, 2)]
            ok = ok_all & (m_ref[0, hh, kj] > 0.5) if mode == 'moba' else ok_all
            s = jnp.where(ok, s, NEG_BIG)
            m_new = jnp.maximum(m, jnp.max(s, axis=0, keepdims=True))
            p = jnp.where(ok, jnp.exp(s - m_new), 0.0)
            alpha = jnp.exp(m - m_new)
            l = alpha * l + jnp.sum(p, axis=0, keepdims=True)
            v_t = vT_ref[0, hk, kj].astype(_BF16)
            acc = alpha * acc + jnp.dot(v_t, p.astype(_BF16), preferred_element_type=_F32)
            out.append((m_new, l, acc))
        return tuple(out)

    lo = jnp.maximum(qi - (NSA_WINDOW // TK), 0) if mode == 'win' else 0
    init = tuple((jnp.full((1, TQ), NEG_BIG, _F32), jnp.zeros((1, TQ), _F32), jnp.zeros((HD, TQ), _F32)) for _ in heads)
    res = lax.fori_loop(lo, qi + 1, body, init)
    for hh in heads:
        m, l, acc = res[hh]
        o_ref[0, hh] = acc / jnp.maximum(l, 1e-30)


def attention_T(mode, qT, k, vT, bias_tiles, mask=None):
    B, H, _, T = qT.shape
    Hkv, L = k.shape[1], k.shape[2]
    grp = H // Hkv
    hp = HEADS_PER_STEP
    assert H % hp == 0 and (grp == 1 or grp % hp == 0)
    shared_kv = grp > 1
    kvh = 1 if shared_kv else hp
    kv_idx = (lambda g: g * hp // grp) if shared_kv else (lambda g: g)
    in_specs = [
        pl.BlockSpec((1, hp, HD, TQ), lambda b, g, i: (b, g, 0, i)),
        pl.BlockSpec((1, kvh, L, HD), lambda b, g, i: (b, kv_idx(g), 0, 0)),
        pl.BlockSpec((1, kvh, L // TK, HD, TK), lambda b, g, i: (b, kv_idx(g), 0, 0, 0)),
        pl.BlockSpec((hp, 3, TK, TQ), lambda b, g, i: (g, 0, 0, 0)),
    ]
    args = [qT, k, vT, bias_tiles]
    if mode == 'moba':
        in_specs.append(pl.BlockSpec((1, hp, L // TK, 1, TQ), lambda b, g, i: (b, g, 0, 0, i)))
        args.append(mask.reshape(B, H, L // TK, 1, T))
    elif mode == 'sel':
        per = TK // NSA_SEL_BLOCK
        in_specs.append(pl.BlockSpec((1, 1, L // TK, per, TQ), lambda b, g, i: (b, kv_idx(g), 0, 0, i)))
        args.append(mask.reshape(B, Hkv, L // TK, per, T))
    elif mode == 'dsa':
        in_specs.append(pl.BlockSpec((1, L, TQ), lambda b, g, i: (b, 0, i)))
        args.append(mask)
    return pl.pallas_call(
        functools.partial(_attn_kernel, mode, shared_kv),
        grid=(B, H // hp, T // TQ),
        in_specs=in_specs,
        out_specs=pl.BlockSpec((1, hp, HD, TQ), lambda b, g, i: (b, g, 0, i)),
        out_shape=jax.ShapeDtypeStruct((B, H, HD, T), _F32),
        compiler_params=pltpu.CompilerParams(
            dimension_semantics=("arbitrary", "arbitrary", "arbitrary"), vmem_limit_bytes=VMEM_LIMIT),
        name="attn_" + mode,
    )(*args)


def _moba_gate_kernel(qT_ref, k_ref, sel_ref):
    nblk, T = sel_ref.shape[2], sel_ref.shape[3]
    q = qT_ref[0, 0].astype(_BF16)
    k = k_ref[0, 0]
    km = jnp.mean(k.reshape(nblk, MOBA_BLOCK, HD), axis=1)
    gs = jnp.dot(km.astype(_BF16), q, preferred_element_type=_F32)
    j = lax.broadcasted_iota(jnp.int32, (nblk, T), 0)
    own = lax.broadcasted_iota(jnp.int32, (nblk, T), 1) // MOBA_BLOCK
    fully_past = j < own
    gs = jnp.where(fully_past, gs, -jnp.inf)
    rank = jnp.zeros((nblk, T), jnp.int32)
    for jp in range(nblk):
        row = gs[jp:jp + 1, :]
        ahead = (row > gs) | ((row == gs) & (jp < j))
        rank = rank + ahead.astype(jnp.int32)
    sel = (fully_past & (rank < MOBA_TOPK)) | (j == own)
    sel_ref[0, 0] = sel.astype(_F32)


def moba_gate(qT, k):
    B, H, _, T = qT.shape
    L = k.shape[2]
    nblk = L // MOBA_BLOCK
    return pl.pallas_call(
        _moba_gate_kernel,
        grid=(B, H),
        in_specs=[pl.BlockSpec((1, 1, HD, T), lambda b, h: (b, h, 0, 0)),
                  pl.BlockSpec((1, 1, L, HD), lambda b, h: (b, h, 0, 0))],
        out_specs=pl.BlockSpec((1, 1, nblk, T), lambda b, h: (b, h, 0, 0)),
        out_shape=jax.ShapeDtypeStruct((B, H, nblk, T), _F32),
        compiler_params=pltpu.CompilerParams(dimension_semantics=("arbitrary", "arbitrary"), vmem_limit_bytes=VMEM_LIMIT),
        name="moba_gate",
    )(qT, k)


def _count_rows(pred):
    return jnp.sum(pred.astype(_F32), axis=0, keepdims=True)


def _dsa_mask_kernel(topk, ki_ref, qiT_ref, wiT_ref, mask_ref, key_ref):
    L = ki_ref.shape[1]
    qi = pl.program_id(1)
    ki = ki_ref[0].astype(_BF16)
    score = jnp.zeros((L, TQ), _F32)
    for hh in range(DSA_IDX_HEADS):
        s = jnp.dot(ki, qiT_ref[0, hh].astype(_BF16), preferred_element_type=_F32) * DSA_IDX_DIM ** -0.5
        score = score + wiT_ref[0, hh:hh + 1, :] * jnp.maximum(s, 0.0)
    idx = lax.broadcasted_iota(jnp.int32, (L, TQ), 0)
    q_pos = qi * TQ + lax.broadcasted_iota(jnp.int32, (L, TQ), 1)
    adm = idx <= q_pos
    score = jnp.where(adm, jnp.where(score == 0.0, 0.0, score), -jnp.inf)
    bits = pltpu.bitcast(score, jnp.int32)
    key_ref[...] = jnp.where(bits < 0, bits ^ jnp.int32(0x7FFFFFFF), bits)
    kf = jnp.float32(topk)
    int_min = jnp.int32(-2 ** 31)
    sub = lax.broadcasted_iota(jnp.int32, (TK, TQ), 0)

    def count(pred):
        def tile(kt, acc):
            rows = pl.ds(pl.multiple_of(kt * TK, TK), TK)
            return acc + _count_rows(pred(key_ref[rows, :], kt * TK + sub))
        return lax.fori_loop(0, qi + 1, tile, jnp.zeros((1, TQ), _F32))

    lo = jnp.where(count(lambda k, i: k >= 0) >= kf, jnp.int32(0), int_min)

    def vstep(i, lo):
        cand = lo + jnp.left_shift(jnp.int32(1), 30 - i)
        return jnp.where(count(lambda k, i: k >= cand) >= kf, cand, lo)

    thr = lax.fori_loop(0, 31, vstep, lo)
    need = kf - count(lambda k, i: k > thr)
    nbits = max(1, (L - 1).bit_length())

    def istep(i, lo):
        cand = lo + jnp.left_shift(jnp.int32(1), nbits - 1 - i)
        c = count(lambda k, ix: (k == thr) & (ix < cand))
        return jnp.where(c < need, cand, lo)

    last_tie = lax.fori_loop(0, nbits, istep, jnp.zeros((1, TQ), jnp.int32))
    key = key_ref[...]
    keep = ((key > thr) | ((key == thr) & (idx <= last_tie))) & adm
    mask_ref[0] = keep.astype(mask_ref.dtype)


def dsa_mask(ki, qiT, wiT, topk):
    B, L, _ = ki.shape
    T = qiT.shape[3]
    return pl.pallas_call(
        functools.partial(_dsa_mask_kernel, topk),
        grid=(B, T // TQ),
        in_specs=[pl.BlockSpec((1, L, DSA_IDX_DIM), lambda b, i: (b, 0, 0)),
                  pl.BlockSpec((1, DSA_IDX_HEADS, DSA_IDX_DIM, TQ), lambda b, i: (b, 0, 0, i)),
                  pl.BlockSpec((1, DSA_IDX_HEADS, TQ), lambda b, i: (b, 0, i))],
        out_specs=pl.BlockSpec((1, L, TQ), lambda b, i: (b, 0, i)),
        out_shape=jax.ShapeDtypeStruct((B, L, T), _BF16),
        scratch_shapes=[pltpu.VMEM((L, TQ), jnp.int32)],
        compiler_params=pltpu.CompilerParams(dimension_semantics=("arbitrary", "arbitrary"), vmem_limit_bytes=VMEM_LIMIT),
        name="dsa_mask",
    )(ki, qiT, wiT)


PEER_TN = 256
PEER_EB = 1024
PEER_IB = PEER_EB // PEER_NKEYS
RANK_OUT = 4096.0


def _extract_top(work, n):
    rows = work.shape[0]
    iota = lax.broadcasted_iota(jnp.int32, work.shape, 0)
    order = jnp.full(work.shape, RANK_OUT, _F32)
    vals = []
    idx = None
    for a in range(n):
        m = jnp.max(work, axis=0, keepdims=True)
        idx = jnp.min(jnp.where(work == m, iota, rows), axis=0, keepdims=True)
        hit = iota == idx
        vals.append(m)
        order = jnp.where(hit, jnp.float32(a), order)
        work = jnp.where(hit, -jnp.inf, work)
    return vals, order, idx


def _peer_select_kernel(x_ref, g_ref, wqT_ref, keys_ref, hx_ref, c1_ref, e1n_ref, r2_ref, e2_ref):
    x = x_ref[...]
    y = x * lax.rsqrt(jnp.mean(x * x, axis=1, keepdims=True) + NORM_EPS) * g_ref[...]
    hx = y.astype(_BF16)
    hx_ref[...] = hx
    qT = lax.dot_general(wqT_ref[...], hx, (((1,), (1,)), ((), ())), preferred_element_type=_F32).astype(_BF16)
    half = PEER_QDIM // 2
    for h in range(PEER_HEADS):
        sc = []
        for c in range(2):
            r0 = (h * 2 + c) * half
            sc.append(jnp.dot(keys_ref[h * 2 + c], qT[r0:r0 + half, :], preferred_element_type=_F32))
        v1, o1, _ = _extract_top(sc[0], PEER_TOPK)
        v2, o2, _ = _extract_top(sc[1], PEER_TOPK)
        v1s = jnp.concatenate(v1, axis=0)
        v2s = jnp.concatenate(v2, axis=0)
        ex1s = jnp.exp(v1s - v1[0])
        ex2s = jnp.exp(v2s - v2[0])
        lo_a, hi_a = PEER_TOPK // 2, PEER_TOPK
        cand = jnp.concatenate([v1[0] + v2s] + [v1[a] + v2s[:lo_a] for a in range(1, lo_a)] + [v1s[lo_a:hi_a] + v2[0]], axis=0)
        ecand = jnp.concatenate([ex1s[0:1] * ex2s] + [ex1s[a:a + 1] * ex2s[:lo_a] for a in range(1, lo_a)]
                                + [ex1s[lo_a:hi_a] * ex2s[0:1]], axis=0)
        _, corder, _ = _extract_top(cand, PEER_TOPK)
        chosen = (corder < RANK_OUT).astype(_F32)
        z = jnp.sum(chosen * ecand, axis=0, keepdims=True)
        tail0 = PEER_TOPK + (lo_a - 1) * lo_a
        c1 = jnp.zeros(sc[0].shape, _F32)
        for a in range(PEER_TOPK):
            if a == 0:
                grp = chosen[0:PEER_TOPK]
            elif a < lo_a:
                grp = chosen[PEER_TOPK + (a - 1) * lo_a:PEER_TOPK + a * lo_a]
            else:
                grp = chosen[tail0 + a - lo_a:tail0 + a - lo_a + 1]
            cnt = jnp.sum(grp, axis=0, keepdims=True)
            c1 = jnp.where(o1 == jnp.float32(a), cnt, c1)
        c1_ref[h] = c1
        e1n_ref[h] = jnp.exp(sc[0] - v1[0]) / z
        r2_ref[h] = o2
        e2_ref[h] = jnp.exp(sc[1] - v2[0])


def peer_select(x, g, wqT, keys2):
    Np, D = x.shape
    tn = PEER_TN
    row = lambda: pl.BlockSpec((PEER_HEADS, PEER_NKEYS, tn), lambda t: (0, 0, t))
    rshape = jax.ShapeDtypeStruct((PEER_HEADS, PEER_NKEYS, Np), _F32)
    return pl.pallas_call(
        _peer_select_kernel,
        grid=(Np // tn,),
        in_specs=[pl.BlockSpec((tn, D), lambda t: (t, 0)),
                  pl.BlockSpec((1, D), lambda t: (0, 0)),
                  pl.BlockSpec((PEER_HEADS * PEER_QDIM, D), lambda t: (0, 0)),
                  pl.BlockSpec((2 * PEER_HEADS, PEER_NKEYS, PEER_QDIM // 2), lambda t: (0, 0, 0))],
        out_specs=[pl.BlockSpec((tn, D), lambda t: (t, 0)), row(), row(), row(), row()],
        out_shape=[jax.ShapeDtypeStruct((Np, D), _BF16), rshape, rshape, rshape, rshape],
        compiler_params=pltpu.CompilerParams(dimension_semantics=("arbitrary",), vmem_limit_bytes=VMEM_LIMIT),
        name="peer_select",
    )(x, g, wqT, keys2)


def _gelu_tanh(x):
    return 0.5 * x * (1.0 + jnp.tanh(math.sqrt(2.0 / math.pi) * (x + 0.044715 * (x * x * x))))


def _peer_dense_kernel(x_ref, hx_ref, u_ref, vT_ref, c1_ref, e1n_ref, r2_ref, e2_ref, o_ref, p_ref, acc_ref):
    e = pl.program_id(1)
    tn, D = x_ref.shape

    @pl.when(e == 0)
    def _():
        acc_ref[...] = jnp.zeros_like(acc_ref)

    nt = (((1,), (1,)), ((), ()))
    half = PEER_EB // 2
    for s in range(2):
        act = _gelu_tanh(lax.dot_general(u_ref[s * half:(s + 1) * half, :], hx_ref[...], nt, preferred_element_type=_F32))
        for ib in range(s * half // PEER_NKEYS, (s + 1) * half // PEER_NKEYS):
            w = jnp.zeros((PEER_NKEYS, tn), _F32)
            for h in range(PEER_HEADS):
                w = w + jnp.where(r2_ref[h] < c1_ref[h, ib:ib + 1, :], e1n_ref[h, ib:ib + 1, :] * e2_ref[h], 0.0)
            r0 = ib * PEER_NKEYS - s * half
            p_ref[ib * PEER_NKEYS:(ib + 1) * PEER_NKEYS, :] = (w * act[r0:r0 + PEER_NKEYS, :]).astype(_BF16)
    for s in range(2):
        rows = slice(s * (D // 2), (s + 1) * (D // 2))
        acc_ref[rows, :] += jnp.dot(vT_ref[rows, :], p_ref[...], preferred_element_type=_F32)

    @pl.when(e == pl.num_programs(1) - 1)
    def _():
        o_ref[...] = x_ref[...] + acc_ref[...].T


def peer_dense(x, hx, u, vT, c1, e1n, r2, e2):
    Np, D = x.shape
    E = u.shape[0]
    tn = PEER_TN
    rowi = lambda: pl.BlockSpec((PEER_HEADS, PEER_IB, tn), lambda t, e: (0, e, t))
    rowj = lambda: pl.BlockSpec((PEER_HEADS, PEER_NKEYS, tn), lambda t, e: (0, 0, t))
    return pl.pallas_call(
        _peer_dense_kernel,
        grid=(Np // tn, E // PEER_EB),
        in_specs=[pl.BlockSpec((tn, D), lambda t, e: (t, 0)),
                  pl.BlockSpec((tn, D), lambda t, e: (t, 0)),
                  pl.BlockSpec((PEER_EB, D), lambda t, e: (e, 0)),
                  pl.BlockSpec((D, PEER_EB), lambda t, e: (0, e)),
                  rowi(), rowi(), rowj(), rowj()],
        out_specs=pl.BlockSpec((tn, D), lambda t, e: (t, 0)),
        out_shape=jax.ShapeDtypeStruct((Np, D), _F32),
        scratch_shapes=[pltpu.VMEM((PEER_EB, tn), _BF16), pltpu.VMEM((D, tn), _F32)],
        compiler_params=pltpu.CompilerParams(dimension_semantics=("arbitrary", "arbitrary"), vmem_limit_bytes=VMEM_LIMIT),
        name="peer_dense",
    )(x, hx, u, vT, c1, e1n, r2, e2)


def peer_ffn_tokens(x_tokens, g, wq, keys, U, V):
    N, D = x_tokens.shape
    Np = -(-N // PEER_TN) * PEER_TN
    x = jnp.pad(x_tokens, ((0, Np - N), (0, 0)))
    keys2 = keys.reshape(2 * PEER_HEADS, PEER_NKEYS, PEER_QDIM // 2).astype(_BF16)
    hx, c1, e1n, r2, e2 = peer_select(x, g.reshape(1, D), wq.T.astype(_BF16), keys2)
    return peer_dense(x, hx, U.astype(_BF16), V.T.astype(_BF16), c1, e1n, r2, e2)[:N]


def bias_lookup(dist, tab):
    onehot = (rel_bucket(dist)[..., None] == jnp.arange(REL_BUCKETS)).astype(_F32)
    return jnp.einsum('...b,bh->h...', onehot, tab.astype(_F32), precision=lax.Precision.HIGHEST)


def rel_bias_tiles(tab):
    kk = jnp.arange(TK)[:, None]
    qq = jnp.arange(TQ)[None, :]
    dist = jnp.stack([d * TQ + qq - kk for d in range(3)])
    return bias_lookup(dist, tab)


ROWS_T = 8
DEC_ROWS = N_HEADS * ROWS_T
NEAR_TILES = 3


PP = 4


def _page_specs(block, n_pages):
    def spec(i):
        return pl.BlockSpec(block, lambda b, p, pt: (pt[b, jnp.minimum(PP * p + i, n_pages - 1)], 0, 0))
    return [spec(i) for i in range(PP)]


def _moba_decode_gate_kernel(pt_ref, q_ref, *rest):
    caches, gs_ref = rest[:PP], rest[PP]
    p = pl.program_id(1)
    per = MOBA_BLOCK // PAGE_SIZE

    @pl.when(p == 0)
    def _():
        gs_ref[...] = jnp.zeros_like(gs_ref)

    lane = lax.broadcasted_iota(jnp.int32, gs_ref.shape[1:], 1)
    for g in range(PP // per):
        ksum = sum(jnp.sum(caches[g * per + i][0], axis=0, keepdims=True) for i in range(per))
        km = (ksum * (1.0 / MOBA_BLOCK)).astype(_BF16).astype(_F32)
        col = jnp.sum(q_ref[0].astype(_F32) * km, axis=1, keepdims=True)
        gs_ref[0] = jnp.where(lane == p * (PP // per) + g, col, gs_ref[0])


def moba_decode_gate(pt, q_rows, cache_rows):
    B, n_pages = pt.shape
    W = N_HEADS * HD
    assert n_pages % PP == 0 and PP % (MOBA_BLOCK // PAGE_SIZE) == 0
    return pl.pallas_call(
        _moba_decode_gate_kernel,
        grid_spec=pltpu.PrefetchScalarGridSpec(
            num_scalar_prefetch=1,
            grid=(B, n_pages // PP),
            in_specs=[pl.BlockSpec((1, DEC_ROWS, W), lambda b, p, pt: (b, 0, 0))] + _page_specs((1, PAGE_SIZE, W), n_pages),
            out_specs=pl.BlockSpec((1, DEC_ROWS, 128), lambda b, p, pt: (b, 0, 0))),
        out_shape=jax.ShapeDtypeStruct((B, DEC_ROWS, 128), _F32),
        compiler_params=pltpu.CompilerParams(dimension_semantics=("arbitrary", "arbitrary"), vmem_limit_bytes=VMEM_LIMIT),
        name="moba_decode_gate",
    )(pt, q_rows, *([cache_rows] * PP))


def _paged_attn_kernel(mode, pt_ref, q_ref, *rest):
    caches = rest[:PP]
    new_ref, far_ref, near_ref, m_in_ref, o_ref, m_sc, l_sc, acc_sc, sel_sc = rest[PP:]
    p = pl.program_id(1)
    last = pl.num_programs(1) - 1
    n_pages = last * PP
    W = N_HEADS * HD
    lane = lax.broadcasted_iota(jnp.int32, (DEC_ROWS, PAGE_SIZE), 1)
    row_t = lax.broadcasted_iota(jnp.int32, (DEC_ROWS, PAGE_SIZE), 0) % ROWS_T

    @pl.when(p == 0)
    def _():
        m_sc[...] = jnp.full_like(m_sc, NEG_BIG)
        l_sc[...] = jnp.zeros_like(l_sc)
        acc_sc[...] = jnp.zeros_like(acc_sc)
        if mode == 'moba':
            nblk = n_pages // (MOBA_BLOCK // PAGE_SIZE)
            valid = lane < nblk
            gs = jnp.where(valid, m_in_ref[0], -jnp.inf)
            rank = jnp.zeros(gs.shape, jnp.int32)
            for jp in range(PAST_LEN // MOBA_BLOCK):
                col = gs[:, jp:jp + 1]
                rank = rank + ((col > gs) | ((col == gs) & (jp < lane))).astype(jnp.int32)
            sel_sc[...] = (valid & (rank < MOBA_TOPK)).astype(_F32)

    def tile_scores(blk, g, i, is_new):
        k = blk[:, :W].astype(_BF16)
        s = lax.dot_general(q_ref[0], k, (((1,), (1,)), ((), ())), preferred_element_type=_F32)
        near = near_ref[jnp.clip(g - (n_pages + 1 - NEAR_TILES), 0, NEAR_TILES - 1)]
        s = s + jnp.where(g >= n_pages + 1 - NEAR_TILES, near, far_ref[...])
        if mode == 'moba':
            if is_new:
                ok = lane <= row_t
            else:
                in_blk = jnp.sum(jnp.where(lane == g // (MOBA_BLOCK // PAGE_SIZE), sel_sc[...], 0.0), axis=1, keepdims=True)
                ok = jnp.broadcast_to(in_blk, s.shape) > 0.5
        else:
            ok = jnp.concatenate([m_in_ref[0, i]] * N_HEADS, axis=0) > 0.5
        return jnp.where(ok, s, NEG_BIG), ok.astype(_F32)

    def update(s, ok, v):
        m_new = jnp.maximum(m_sc[...], jnp.max(s, axis=1, keepdims=True))
        pr = jnp.where(ok > 0.5, jnp.exp(s - m_new), 0.0)
        alpha = jnp.exp(m_sc[...] - m_new)
        l_sc[...] = alpha * l_sc[...] + jnp.sum(pr, axis=1, keepdims=True)
        acc_sc[...] = alpha * acc_sc[...] + jnp.dot(pr.astype(_BF16), v, preferred_element_type=_F32)
        m_sc[...] = m_new

    @pl.when(p < last)
    def _():
        parts = [tile_scores(caches[i][0], PP * p + i, i, False) for i in range(PP)]
        v_all = jnp.concatenate([caches[i][0][:, W:].astype(_BF16) for i in range(PP)], axis=0)
        update(jnp.concatenate([s for s, _ in parts], axis=1), jnp.concatenate([ok for _, ok in parts], axis=1), v_all)

    @pl.when(p == last)
    def _():
        s, ok = tile_scores(new_ref[0], n_pages, 0, True)
        update(s, ok, new_ref[0][:, W:].astype(_BF16))
        o_ref[0] = acc_sc[...] / jnp.maximum(l_sc[...], 1e-30)


def paged_attention(mode, pt, q_rows, cache_rows, new_rows, far, near, m_in):
    B, n_pages = pt.shape
    W = N_HEADS * HD
    assert n_pages % PP == 0
    if mode == 'moba':
        m_spec = pl.BlockSpec((1, DEC_ROWS, 128), lambda b, p, pt: (b, 0, 0))
    else:
        m_spec = pl.BlockSpec((1, PP, ROWS_T, PAGE_SIZE), lambda b, p, pt: (b, p, 0, 0))
    return pl.pallas_call(
        functools.partial(_paged_attn_kernel, mode),
        grid_spec=pltpu.PrefetchScalarGridSpec(
            num_scalar_prefetch=1,
            grid=(B, n_pages // PP + 1),
            in_specs=[pl.BlockSpec((1, DEC_ROWS, W), lambda b, p, pt: (b, 0, 0))] + _page_specs((1, PAGE_SIZE, 2 * W), n_pages) + [
                      pl.BlockSpec((1, PAGE_SIZE, 2 * W), lambda b, p, pt: (b, 0, 0)),
                      pl.BlockSpec((DEC_ROWS, 1), lambda b, p, pt: (0, 0)),
                      pl.BlockSpec((NEAR_TILES, DEC_ROWS, PAGE_SIZE), lambda b, p, pt: (0, 0, 0)),
                      m_spec],
            out_specs=pl.BlockSpec((1, DEC_ROWS, W), lambda b, p, pt: (b, 0, 0)),
            scratch_shapes=[pltpu.VMEM((DEC_ROWS, 1), _F32), pltpu.VMEM((DEC_ROWS, 1), _F32),
                            pltpu.VMEM((DEC_ROWS, W), _F32), pltpu.VMEM((DEC_ROWS, 128), _F32)]),
        out_shape=jax.ShapeDtypeStruct((B, DEC_ROWS, W), _F32),
        compiler_params=pltpu.CompilerParams(dimension_semantics=("arbitrary", "arbitrary"), vmem_limit_bytes=VMEM_LIMIT),
        name="paged_attn_" + mode,
    )(pt, q_rows, *([cache_rows] * PP), new_rows, far, near, m_in)


def _dsa_decode_score_kernel(pt_ref, qi_ref, wi_ref, *rest):
    caches, new_ref, score_ref = rest[:PP], rest[PP], rest[PP + 1]
    p = pl.program_id(1)
    last = pl.num_programs(1) - 1

    def tile_scores(ki):
        ki = ki.astype(_BF16)
        acc = jnp.zeros((ROWS_T, PAGE_SIZE), _F32)
        for hh in range(DSA_IDX_HEADS):
            s = lax.dot_general(qi_ref[0, hh], ki, (((1,), (1,)), ((), ())), preferred_element_type=_F32) * DSA_IDX_DIM ** -0.5
            acc = acc + wi_ref[0, hh] * jnp.maximum(s, 0.0)
        return acc

    @pl.when(p < last)
    def _():
        for i in range(PP):
            score_ref[0, :, i * PAGE_SIZE:(i + 1) * PAGE_SIZE] = tile_scores(caches[i][0])

    @pl.when(p == last)
    def _():
        score_ref[0] = jnp.zeros(score_ref.shape[1:], _F32)
        score_ref[0, :, 0:PAGE_SIZE] = tile_scores(new_ref[0])


def dsa_decode_scores(pt, qi_rows, wi_rows, cache_rows, new_rows):
    B, n_pages = pt.shape
    assert n_pages % PP == 0
    return pl.pallas_call(
        _dsa_decode_score_kernel,
        grid_spec=pltpu.PrefetchScalarGridSpec(
            num_scalar_prefetch=1,
            grid=(B, n_pages // PP + 1),
            in_specs=[pl.BlockSpec((1, DSA_IDX_HEADS, ROWS_T, DSA_IDX_DIM), lambda b, p, pt: (b, 0, 0, 0)),
                      pl.BlockSpec((1, DSA_IDX_HEADS, ROWS_T, 1), lambda b, p, pt: (b, 0, 0, 0))]
            + _page_specs((1, PAGE_SIZE, DSA_IDX_DIM), n_pages)
            + [pl.BlockSpec((1, PAGE_SIZE, DSA_IDX_DIM), lambda b, p, pt: (b, 0, 0))],
            out_specs=pl.BlockSpec((1, ROWS_T, PP * PAGE_SIZE), lambda b, p, pt: (b, 0, p))),
        out_shape=jax.ShapeDtypeStruct((B, ROWS_T, (n_pages + PP) * PAGE_SIZE), _F32),
        compiler_params=pltpu.CompilerParams(dimension_semantics=("arbitrary", "arbitrary"), vmem_limit_bytes=VMEM_LIMIT),
        name="dsa_decode_scores",
    )(pt, qi_rows, wi_rows, *([cache_rows] * PP), new_rows)


def _count_lanes(pred):
    return jnp.sum(pred.astype(_F32), axis=1, keepdims=True)


def _dsa_decode_select_kernel(topk, past_len, score_ref, keep_ref, key_ref):
    Lp = score_ref.shape[2]
    idx = lax.broadcasted_iota(jnp.int32, (ROWS_T, Lp), 1)
    q_pos = past_len + lax.broadcasted_iota(jnp.int32, (ROWS_T, Lp), 0)
    adm = idx <= q_pos
    score = score_ref[0]
    score = jnp.where(adm, jnp.where(score == 0.0, 0.0, score), -jnp.inf)
    bits = pltpu.bitcast(score, jnp.int32)
    key_ref[...] = jnp.where(bits < 0, bits ^ jnp.int32(0x7FFFFFFF), bits)
    kf = jnp.float32(topk)
    lo = jnp.where(_count_lanes(key_ref[...] >= 0) >= kf, jnp.int32(0), jnp.int32(-2 ** 31))

    def vstep(i, lo):
        cand = lo + jnp.left_shift(jnp.int32(1), 30 - i)
        return jnp.where(_count_lanes(key_ref[...] >= cand) >= kf, cand, lo)

    thr = lax.fori_loop(0, 31, vstep, lo)
    need = kf - _count_lanes(key_ref[...] > thr)
    nbits = max(1, (Lp - 1).bit_length())

    def istep(i, lo):
        cand = lo + jnp.left_shift(jnp.int32(1), nbits - 1 - i)
        c = _count_lanes((key_ref[...] == thr) & (idx < cand))
        return jnp.where(c < need, cand, lo)

    last_tie = lax.fori_loop(0, nbits, istep, jnp.zeros((ROWS_T, 1), jnp.int32))
    key = key_ref[...]
    keep = ((key > thr) | ((key == thr) & (idx <= last_tie))) & adm
    keep_ref[0] = keep.astype(_F32)


def dsa_decode_select(score, topk, past_len):
    B, _, Lp = score.shape
    return pl.pallas_call(
        functools.partial(_dsa_decode_select_kernel, topk, past_len),
        grid=(B,),
        in_specs=[pl.BlockSpec((1, ROWS_T, Lp), lambda b: (b, 0, 0))],
        out_specs=pl.BlockSpec((1, ROWS_T, Lp), lambda b: (b, 0, 0)),
        out_shape=jax.ShapeDtypeStruct((B, ROWS_T, Lp), _F32),
        scratch_shapes=[pltpu.VMEM((ROWS_T, Lp), jnp.int32)],
        compiler_params=pltpu.CompilerParams(dimension_semantics=("arbitrary",), vmem_limit_bytes=VMEM_LIMIT),
        name="dsa_decode_select",
    )(score)


def decode_rows(q):
    B, T, H, _ = q.shape
    qp = jnp.pad(jnp.transpose(q, (0, 2, 1, 3)), ((0, 0), (0, 0), (0, ROWS_T - T), (0, 0)))
    rows = qp[:, :, :, None, :] * jnp.eye(H, dtype=q.dtype)[None, :, None, :, None]
    return rows.reshape(B, H * ROWS_T, H * HD)


def from_decode_rows(o, T):
    B = o.shape[0]
    o5 = o.reshape(B, N_HEADS, ROWS_T, N_HEADS, HD)
    own = jnp.sum(o5 * jnp.eye(N_HEADS, dtype=o.dtype)[None, :, None, :, None], axis=3)
    return jnp.transpose(own[:, :, :T], (0, 2, 1, 3)).reshape(B, T, N_HEADS * HD)


def decode_bias(tab, past_len, n_pages):
    t = jnp.arange(ROWS_T)[None, :, None]
    tile = (n_pages + 1 - NEAR_TILES + jnp.arange(NEAR_TILES))[:, None, None]
    dist = past_len + t - (tile * PAGE_SIZE + jnp.arange(PAGE_SIZE)[None, None, :])
    near = jnp.transpose(bias_lookup(dist, tab), (1, 0, 2, 3)).reshape(NEAR_TILES, DEC_ROWS, PAGE_SIZE)
    far = jnp.repeat(tab[REL_BUCKETS - 1].astype(_F32), ROWS_T).reshape(DEC_ROWS, 1)
    return far, near


def pad_rows(a, n):
    return jnp.pad(a, ((0, 0), (0, n - a.shape[1])) + ((0, 0),) * (a.ndim - 2))


def heads_T(z, nh):
    B, T, _ = z.shape
    return jnp.transpose(z.reshape(B, T, nh, HD), (0, 2, 3, 1))


def heads_K(z, nh):
    B, L, _ = z.shape
    return jnp.transpose(z.reshape(B, L, nh, HD), (0, 2, 1, 3))


def heads_VT(z, nh):
    B, L, _ = z.shape
    return jnp.transpose(z.reshape(B, L // TK, TK, nh, HD), (0, 3, 1, 4, 2))


def from_heads_T(oT):
    B, H, _, T = oT.shape
    return jnp.transpose(oT, (0, 3, 1, 2)).reshape(B, T, H * HD)


def nsa_attention_prompt(q_flat, gates_flat, nsa_new, q_pos, w1, w2, pe, tab, bias_tiles):
    B, T, _ = q_flat.shape
    G, H, L = NSA_KV_HEADS, N_HEADS, T
    scale = HD ** -0.5
    q = q_flat.reshape(B, T, H, HD)
    kc, vc = nsa_new[:, :, 0], nsa_new[:, :, 1]
    nch = L // NSA_CMP_STRIDE
    ncmp = nch - 1

    def compress(x, i):
        c = x[:, :nch * NSA_CMP_STRIDE].reshape(B, nch, NSA_CMP_STRIDE, G, HD)
        blk = jnp.concatenate([c[:, :-1], c[:, 1:]], axis=2) + pe[i][None, None, :, None, :]
        blk = jnp.transpose(blk, (0, 1, 3, 2, 4)).reshape(B, ncmp, G, NSA_CMP_LEN * HD)
        return jax.nn.gelu(blk @ w1[i]) @ w2[i]

    kcmp = compress(kc, 0)
    vcmp = compress(vc, 1)
    cmp_end = jnp.arange(ncmp) * NSA_CMP_STRIDE + NSA_CMP_LEN - 1
    qg = q.reshape(B, T, G, NSA_GROUP, HD)
    dist_c = q_pos[:, None] - cmp_end[None, :]
    lg_c = jnp.einsum('btgrd,bngd->btgrn', qg, kcmp).astype(jnp.float32).reshape(B, T, H, ncmp) * scale
    lg_c = lg_c + rel_bias(dist_c[None, :, None, :], tab)
    p_c = masked_softmax(lg_c, (dist_c >= 0)[None, :, None, :])
    p_cg = p_c.reshape(B, T, G, NSA_GROUP, ncmp)
    o_cmp = jnp.einsum('btgrn,bngd->btgrd', p_cg.astype(vcmp.dtype), vcmp).reshape(B, T, H, HD)
    nsel = -(-L // NSA_SEL_BLOCK)
    ii = np.arange(ncmp)[:, None]
    jj_np = np.arange(nsel)[None, :]
    overlap = ((ii * NSA_CMP_STRIDE < (jj_np + 1) * NSA_SEL_BLOCK) & (ii * NSA_CMP_STRIDE + NSA_CMP_LEN > jj_np * NSA_SEL_BLOCK)).astype(np.float32)
    imp = jnp.einsum('btgrn,nj->btgj', p_cg, jnp.asarray(overlap))
    cur = q_pos // NSA_SEL_BLOCK
    jj = jnp.arange(nsel)[None, :]
    causal_blk = jj <= cur[:, None]
    forced = (jj == 0) | (jj == cur[:, None]) | (jj == cur[:, None] - 1)
    imp = jnp.where(forced[None, :, None, :], jnp.inf, jnp.where(causal_blk[None, :, None, :], imp, -jnp.inf))
    _, sel = lax.top_k(imp, min(NSA_TOPN, nsel))
    chosen = jnp.any(sel[..., None] == jnp.arange(nsel), axis=-2)
    selT = jnp.transpose(chosen, (0, 2, 3, 1)).astype(_F32)
    qT = heads_T(q_flat, H) * scale
    flat = lambda a: a.reshape(B, L, G * HD)
    o_sel = from_heads_T(attention_T('sel', qT, heads_K(flat(nsa_new[:, :, 2]), G), heads_VT(flat(nsa_new[:, :, 3]), G), bias_tiles, selT))
    o_win = from_heads_T(attention_T('win', qT, heads_K(flat(nsa_new[:, :, 4]), G), heads_VT(flat(nsa_new[:, :, 5]), G), bias_tiles))
    g = jax.nn.sigmoid(gates_flat.reshape(B, T, H, 3).astype(jnp.float32))
    out = g[..., 0:1] * o_cmp + g[..., 1:2] * o_sel.reshape(B, T, H, HD) + g[..., 2:3] * o_win.reshape(B, T, H, HD)
    return out.reshape(B, T, H * HD)


def token_mix(z, past, lp, tab):
    B, T, _ = z.shape
    P = 0 if past is None else past['nsa'].shape[1]
    q_pos = P + jnp.arange(T, dtype=jnp.int32)
    qa, ka, va, q_nsa, nsa_kv, nsa_g, qc, kc, vc, qi, ki, wi, u, vg = split_last(z, IN_SPLITS)
    moba_new = jnp.stack([ka, va], axis=2).reshape(B, T, 2, N_HEADS, HD)
    nsa_new = nsa_kv.reshape(B, T, 6, NSA_KV_HEADS, HD)
    nsa_main_new = nsa_new[:, :, :4]
    win_new = nsa_new[:, :, 4:]
    dsa_new = jnp.stack([kc, vc], axis=2).reshape(B, T, 2, N_HEADS, HD)
    if past is None:
        nsa_all, win_all = nsa_main_new, win_new
    else:
        nsa_all = jnp.concatenate([past['nsa'], nsa_main_new], axis=1)
        win_all = jnp.concatenate([past['win'], win_new], axis=1)
    kw_pos0 = P + T - win_all.shape[1]
    scale = HD ** -0.5
    if past is None:
        bt = lp['bias_tiles']
        qaT = heads_T(qa, N_HEADS) * scale
        y_a = from_heads_T(attention_T('moba', qaT, heads_K(ka, N_HEADS), heads_VT(va, N_HEADS), bt[0], moba_gate(qaT, heads_K(ka, N_HEADS))))
        y_b = nsa_attention_prompt(q_nsa, nsa_g, nsa_new, q_pos, lp['cmp_w1'], lp['cmp_w2'], lp['cmp_pe'], tab[:, N_HEADS:2 * N_HEADS], bt[1])
        qiT = jnp.transpose(qi.reshape(B, T, DSA_IDX_HEADS, DSA_IDX_DIM), (0, 2, 3, 1))
        keep = dsa_mask(ki, qiT, jnp.transpose(wi, (0, 2, 1)), min(DSA_TOPK, T // 4))
        y_c = from_heads_T(attention_T('dsa', heads_T(qc, N_HEADS) * scale, heads_K(kc, N_HEADS), heads_VT(vc, N_HEADS), bt[2], keep))
    else:
        assert P % MOBA_BLOCK == 0 and T <= ROWS_T
        pt = past['pt']
        n_pages = pt.shape[1]
        far_a, near_a = decode_bias(tab[:, :N_HEADS], P, n_pages)
        qa_rows = (decode_rows(qa.reshape(B, T, N_HEADS, HD)) * scale).astype(_BF16)
        new_a = pad_rows(jnp.concatenate([ka, va], axis=-1), PAGE_SIZE)
        gs = moba_decode_gate(pt, qa_rows, past['moba_rows'])
        y_a = from_decode_rows(paged_attention('moba', pt, qa_rows, past['moba_rows'], new_a, far_a, near_a, gs), T)
        y_b = nsa_attention(q_nsa.reshape(B, T, N_HEADS, HD), nsa_g.reshape(B, T, N_HEADS, 3), nsa_all[:, :, 0], nsa_all[:, :, 1], nsa_all[:, :, 2], nsa_all[:, :, 3], win_all[:, :, 0], win_all[:, :, 1], kw_pos0, q_pos, lp['cmp_w1'], lp['cmp_w2'], lp['cmp_pe'], tab[:, N_HEADS:2 * N_HEADS])
        qi_rows = pad_rows(jnp.transpose(qi.reshape(B, T, DSA_IDX_HEADS, DSA_IDX_DIM), (0, 2, 1, 3)).reshape(B * DSA_IDX_HEADS, T, DSA_IDX_DIM), ROWS_T)
        qi_rows = qi_rows.reshape(B, DSA_IDX_HEADS, ROWS_T, DSA_IDX_DIM).astype(_BF16)
        wi_rows = pad_rows(jnp.transpose(wi, (0, 2, 1)).reshape(B * DSA_IDX_HEADS, T), ROWS_T).reshape(B, DSA_IDX_HEADS, ROWS_T, 1)
        score = dsa_decode_scores(pt, qi_rows, wi_rows, past['idx_rows'], pad_rows(ki, PAGE_SIZE))
        keep = dsa_decode_select(score, min(DSA_TOPK, (P + T) // 4), P)
        keep = jnp.transpose(keep.reshape(B, ROWS_T, n_pages + PP, PAGE_SIZE), (0, 2, 1, 3))
        far_c, near_c = decode_bias(tab[:, 2 * N_HEADS:], P, n_pages)
        qc_rows = (decode_rows(qc.reshape(B, T, N_HEADS, HD)) * scale).astype(_BF16)
        new_c = pad_rows(jnp.concatenate([kc, vc], axis=-1), PAGE_SIZE)
        y_c = from_decode_rows(paged_attention('dsa', pt, qc_rows, past['dsa_rows'], new_c, far_c, near_c, keep), T)
    y_d, v_rows = gmlp_sgu(u, vg, lp['gm_ws'], lp['gm_b'], lp['gm_norm'])
    branches = [y.reshape(B * T, MIX_W) for y in (y_a, y_b, y_c, y_d)]
    win_keep = win_all[:, -min(NSA_WINDOW, P + T):]
    return branches, (moba_new, nsa_main_new, win_keep, dsa_new, ki, v_rows)


def gated_branch_mix(branches, gz, lp):
    mixed = 0.0
    for n in range(N_BRANCH):
        gate = jax.nn.sigmoid(gz[:, n * D_MODEL:(n + 1) * D_MODEL] + lp['b_gate'][n])
        mixed = mixed + gate * (branches[n] @ lp['w_branch'][n])
    return mixed @ lp['w_out']


def mem_attention_q(q, mkv, wo):
    B, T, _ = q.shape
    q = q.reshape(B, T, MEM_HEADS, MEM_HD)
    lg = jnp.einsum('bthd,bmhd->bthm', q, mkv[:, :, 0]).astype(jnp.float32) * MEM_HD ** -0.5
    p = jax.nn.softmax(lg, axis=-1)
    o = jnp.einsum('bthm,bmhd->bthd', p.astype(mkv.dtype), mkv[:, :, 1]).reshape(B, T, MEM_W)
    return o @ wo


def layer_pair(rows, shape_p, shape_s, past, mkv_p, mkv_s, lp, tab):
    n_p = shape_p[0] * shape_p[1]
    split = lambda a: (a[:n_p].reshape(shape_p[0], shape_p[1], -1), a[n_p:].reshape(shape_s[0], shape_s[1], -1))
    z_p, z_s = split(norm_matmul(rows, lp['norm_mix'], lp['w_in']))
    gz = norm_matmul(rows, lp['norm_mix'], lp['w_gate'])
    br_p, rp = token_mix(z_p, None, lp, tab)
    br_s, rs = token_mix(z_s, past, lp, tab)
    branches = [jnp.concatenate([bp, bs], axis=0) for bp, bs in zip(br_p, br_s)]
    rows = rows + gated_branch_mix(branches, gz, lp)
    q_p, q_s = split(norm_matmul(rows, lp['norm_mem'], lp['w_mem_q']))
    mem_p = mem_attention_q(q_p, mkv_p, lp['w_mem_o'])
    mem_s = mem_attention_q(q_s, mkv_s, lp['w_mem_o'])
    rows = rows + jnp.concatenate([mem_p.reshape(-1, D_MODEL), mem_s.reshape(-1, D_MODEL)], axis=0)
    rows = peer_ffn_tokens(rows, lp['norm_ffn'], lp['peer_wq'], lp['peer_keys'], lp['peer_u'], lp['peer_v'])
    return rows, rp, rs


def kernel(x_prompt, x_sample, cache_moba_kv, cache_nsa_kv, cache_dsa_kv, cache_dsa_idx, state_nsa_win, cache_mem_kv, page_table, mem_prompt, rel_bias_table, w_in, nsa_cmp_w1, nsa_cmp_w2, nsa_cmp_pe, gm_ws, gm_b, gm_norm, w_branch, w_gate, b_gate, w_out, w_mem_q, w_mem_kv, w_mem_o, peer_wq, peer_keys, peer_u, peer_v, norm_mix, norm_mem, norm_ffn, norm_final):
    n_p = x_prompt.shape[0] * x_prompt.shape[1]
    rows = jnp.concatenate([x_prompt.reshape(-1, D_MODEL), x_sample.reshape(-1, D_MODEL)], axis=0)
    rows_p = []
    rows_s = []
    mem_rows = []
    Bp = x_prompt.shape[0]
    n_pool = cache_moba_kv.shape[1]
    bias_tiles =[rel_bias_tiles(rel_bias_table[:, n * N_HEADS:(n + 1) * N_HEADS]) for n in range(3)]
    for l in range(DEPTH):
        lp = {
            'w_in': w_in[l], 'cmp_w1': nsa_cmp_w1[l], 'cmp_w2': nsa_cmp_w2[l], 'cmp_pe': nsa_cmp_pe[l],
            'gm_ws': gm_ws[l], 'gm_b': gm_b[l], 'gm_norm': gm_norm[l],
            'w_branch': w_branch[l], 'w_gate': w_gate[l], 'b_gate': b_gate[l], 'w_out': w_out[l],
            'w_mem_q': w_mem_q[l], 'w_mem_o': w_mem_o[l],
            'peer_wq': peer_wq[l], 'peer_keys': peer_keys[l], 'peer_u': peer_u[l], 'peer_v': peer_v[l],
            'norm_mix': norm_mix[l], 'norm_mem': norm_mem[l], 'norm_ffn': norm_ffn[l],
        }
        mkv_p = (mem_prompt @ w_mem_kv[l]).reshape(Bp, MEM_TOKENS, 2, MEM_HEADS, MEM_HD)
        past = {
            'pt': page_table + l * n_pool,
            'moba_rows': cache_moba_kv.reshape(DEPTH * n_pool, PAGE_SIZE, 2 * MIX_W),
            'dsa_rows': cache_dsa_kv.reshape(DEPTH * n_pool, PAGE_SIZE, 2 * MIX_W),
            'idx_rows': cache_dsa_idx.reshape(DEPTH * n_pool, PAGE_SIZE, DSA_IDX_DIM),
            'nsa': gather_pages(cache_nsa_kv[l], page_table),
            'win': state_nsa_win[l],
        }
        lp['bias_tiles'] = bias_tiles
        rows, rp, rs = layer_pair(rows, x_prompt.shape, x_sample.shape, past, mkv_p, cache_mem_kv[l], lp, rel_bias_table)
        rows_p.append(rp)
        rows_s.append(rs)
        mem_rows.append(mkv_p)
    y_rows = rmsnorm(rows, norm_final)
    y_prompt = y_rows[:n_p].reshape(x_prompt.shape)
    y_sample = y_rows[n_p:].reshape(x_sample.shape)
    new_moba_kv_prompt = jnp.stack([r[0] for r in rows_p])
    new_moba_kv_sample = jnp.stack([r[0] for r in rows_s])
    new_nsa_kv_prompt = jnp.stack([r[1] for r in rows_p])
    new_nsa_kv_sample = jnp.stack([r[1] for r in rows_s])
    new_nsa_win_prompt = jnp.stack([r[2] for r in rows_p])
    new_nsa_win_sample = jnp.stack([r[2] for r in rows_s])
    new_dsa_kv_prompt = jnp.stack([r[3] for r in rows_p])
    new_dsa_kv_sample = jnp.stack([r[3] for r in rows_s])
    new_dsa_idx_prompt = jnp.stack([r[4] for r in rows_p])
    new_dsa_idx_sample = jnp.stack([r[4] for r in rows_s])
    new_mem_kv_prompt = jnp.stack(mem_rows)
    new_gmlp_v_sample = jnp.stack([r[5] for r in rows_s])
    return (y_prompt, y_sample, new_moba_kv_prompt, new_moba_kv_sample, new_nsa_kv_prompt, new_nsa_kv_sample, new_nsa_win_prompt, new_nsa_win_sample, new_dsa_kv_prompt, new_dsa_kv_sample, new_dsa_idx_prompt, new_dsa_idx_sample, new_mem_kv_prompt, new_gmlp_v_sample)
```

```python
import functools
import math
import jax
import jax.numpy as jnp
from jax import lax
import numpy as np
from jax.experimental import pallas as pl
from jax.experimental.pallas import tpu as pltpu

D_MODEL = 2048
DEPTH = 2
DEC_SEQ = 4
PAGE_SIZE = 128
VMEM_LIMIT = 48 * 1024 * 1024

HD = 64
N_HEADS = 8
MIX_W = N_HEADS * HD
N_BRANCH = 4
MOBA_BLOCK = 256
MOBA_TOPK = 3
NSA_KV_HEADS = 2
NSA_GROUP = N_HEADS // NSA_KV_HEADS
NSA_CMP_STRIDE = 16
NSA_CMP_LEN = 2 * NSA_CMP_STRIDE
NSA_SEL_BLOCK = 64
NSA_TOPN = 16
NSA_WINDOW = 512
NSA_QB = 64
WIN_QB = 128
DSA_IDX_HEADS = 4
DSA_IDX_DIM = 64
DSA_TOPK = 256
GM_CHUNK = 128
GM_GROUPS = 8
GM_W = 512
REL_BUCKETS = 32
REL_MAX_DIST = 128
MEM_TOKENS = 256
MEM_HEADS = 4
MEM_HD = 128
MEM_W = MEM_HEADS * MEM_HD
PEER_HEADS = 8
PEER_NKEYS = 128
PEER_QDIM = 128
PEER_TOPK = 16
NORM_EPS = 1e-6
IN_SPLITS = (MIX_W, MIX_W, MIX_W, MIX_W, 6 * NSA_KV_HEADS * HD, 3 * N_HEADS, MIX_W, MIX_W, MIX_W, DSA_IDX_HEADS * DSA_IDX_DIM, DSA_IDX_DIM, DSA_IDX_HEADS, GM_W, GM_W)
IN_COLS = sum(IN_SPLITS)


def rmsnorm(x, g):
    xf = x.astype(jnp.float32)
    y = xf * lax.rsqrt(jnp.mean(xf * xf, axis=-1, keepdims=True) + NORM_EPS)
    return (y * g.astype(jnp.float32)).astype(x.dtype)


def masked_softmax(logits, mask):
    l = jnp.where(mask, logits.astype(jnp.float32), -jnp.inf)
    m = jnp.max(l, axis=-1, keepdims=True)
    m = jnp.where(jnp.isfinite(m), m, 0.0)
    e = jnp.exp(l - m)
    return e / jnp.maximum(jnp.sum(e, axis=-1, keepdims=True), 1e-30)


def rel_bucket(dist):
    n = jnp.maximum(dist, 0)
    max_exact = REL_BUCKETS // 2
    nf = jnp.maximum(n, 1).astype(jnp.float32)
    large = max_exact + (jnp.log(nf / max_exact) / math.log(REL_MAX_DIST / max_exact) * (REL_BUCKETS - max_exact)).astype(jnp.int32)
    return jnp.where(n < max_exact, n, jnp.minimum(large, REL_BUCKETS - 1))


def rel_bias(dist, tab):
    onehot = (rel_bucket(dist)[..., None] == jnp.arange(REL_BUCKETS)).astype(jnp.float32)
    spec = '...okb,bh->...hk' if dist.shape[-2] == 1 else '...hkb,bh->...hk'
    return jnp.einsum(spec, onehot, tab.astype(jnp.float32), precision=lax.Precision.HIGHEST)


def split_last(z, sizes):
    return jnp.split(z, np.cumsum(sizes)[:-1].tolist(), axis=-1)


def map_query_blocks(fn, qb, q_pos, *xs):
    T = q_pos.shape[0]
    qb = qb if T % qb == 0 else T
    nb = T // qb
    xb = tuple(jnp.moveaxis(x.reshape(x.shape[0], nb, qb, *x.shape[2:]), 1, 0) for x in xs)
    out = lax.map(lambda a: fn(*a), (q_pos.reshape(nb, qb),) + xb)
    return jnp.moveaxis(out, 0, 1).reshape(out.shape[1], T, *out.shape[3:])


def gather_pages(pool, page_table):
    g = pool[page_table]
    return g.reshape(g.shape[0], g.shape[1] * g.shape[2], *g.shape[3:])


def nsa_attention(q, gates, kc, vc, ks, vs, kw, vw, kw_pos0, q_pos, w1, w2, pe, tab):
    B, L, G, _ = kc.shape
    T = q.shape[1]
    H = N_HEADS
    scale = HD ** -0.5
    nch = L // NSA_CMP_STRIDE
    ncmp = nch - 1

    def compress(x, i):
        c = x[:, :nch * NSA_CMP_STRIDE].reshape(B, nch, NSA_CMP_STRIDE, G, HD)
        blk = jnp.concatenate([c[:, :-1], c[:, 1:]], axis=2) + pe[i][None, None, :, None, :]
        blk = jnp.transpose(blk, (0, 1, 3, 2, 4)).reshape(B, ncmp, G, NSA_CMP_LEN * HD)
        return jax.nn.gelu(blk @ w1[i]) @ w2[i]

    kcmp = compress(kc, 0)
    vcmp = compress(vc, 1)
    cmp_end = jnp.arange(ncmp) * NSA_CMP_STRIDE + NSA_CMP_LEN - 1
    qg = q.reshape(B, T, G, NSA_GROUP, HD)
    dist_c = q_pos[:, None] - cmp_end[None, :]
    lg_c = jnp.einsum('btgrd,bngd->btgrn', qg, kcmp).astype(jnp.float32).reshape(B, T, H, ncmp) * scale
    lg_c = lg_c + rel_bias(dist_c[None, :, None, :], tab)
    p_c = masked_softmax(lg_c, (dist_c >= 0)[None, :, None, :])
    p_cg = p_c.reshape(B, T, G, NSA_GROUP, ncmp)
    o_cmp = jnp.einsum('btgrn,bngd->btgrd', p_cg.astype(vcmp.dtype), vcmp).reshape(B, T, H, HD)
    nsel = -(-L // NSA_SEL_BLOCK)
    ii = np.arange(ncmp)[:, None]
    jj_np = np.arange(nsel)[None, :]
    overlap = ((ii * NSA_CMP_STRIDE < (jj_np + 1) * NSA_SEL_BLOCK) & (ii * NSA_CMP_STRIDE + NSA_CMP_LEN > jj_np * NSA_SEL_BLOCK)).astype(np.float32)
    imp = jnp.einsum('btgrn,nj->btgj', p_cg, jnp.asarray(overlap))
    cur = q_pos // NSA_SEL_BLOCK
    jj = jnp.arange(nsel)[None, :]
    causal_blk = jj <= cur[:, None]
    forced = (jj == 0) | (jj == cur[:, None]) | (jj == cur[:, None] - 1)
    imp = jnp.where(forced[None, :, None, :], jnp.inf, jnp.where(causal_blk[None, :, None, :], imp, -jnp.inf))
    ntop = min(NSA_TOPN, nsel)
    _, sel = lax.top_k(imp, ntop)
    pad = ((0, 0), (0, nsel * NSA_SEL_BLOCK - L), (0, 0), (0, 0))
    ksb = jnp.transpose(jnp.pad(ks, pad).reshape(B, nsel, NSA_SEL_BLOCK, G, HD), (0, 3, 1, 2, 4))
    vsb = jnp.transpose(jnp.pad(vs, pad).reshape(B, nsel, NSA_SEL_BLOCK, G, HD), (0, 3, 1, 2, 4))
    bi = jnp.arange(B)[:, None, None, None]
    gi = jnp.arange(G)[None, None, :, None]
    off = jnp.arange(NSA_SEL_BLOCK)

    def sel_fn(qp, qq, ss):
        qb = qp.shape[0]
        kg = ksb[bi, gi, ss].reshape(B, qb, G, ntop * NSA_SEL_BLOCK, HD)
        vg = vsb[bi, gi, ss].reshape(B, qb, G, ntop * NSA_SEL_BLOCK, HD)
        pos = (ss[..., None] * NSA_SEL_BLOCK + off).reshape(B, qb, G, -1)
        dist = jnp.repeat(qp[None, :, None, None] - pos, NSA_GROUP, axis=2)
        lg = jnp.einsum('bqgrd,bqgkd->bqgrk', qq.reshape(B, qb, G, NSA_GROUP, HD), kg).astype(jnp.float32).reshape(B, qb, H, -1) * scale
        p = masked_softmax(lg + rel_bias(dist, tab), dist >= 0)
        o = jnp.einsum('bqgrk,bqgkd->bqgrd', p.reshape(B, qb, G, NSA_GROUP, -1).astype(vg.dtype), vg)
        return o.reshape(B, qb, H, HD)

    o_sel = map_query_blocks(sel_fn, NSA_QB, q_pos, q, sel)
    W = NSA_WINDOW
    wpad = ((0, 0), (W, 0), (0, 0), (0, 0))
    kwp = jnp.pad(kw, wpad)
    vwp = jnp.pad(vw, wpad)

    def win_fn(qp, qq):
        qb = qp.shape[0]
        span = W + qb - 1
        start = qp[0] - kw_pos0 + 1
        kk = lax.dynamic_slice_in_dim(kwp, start, span, axis=1)
        vv = lax.dynamic_slice_in_dim(vwp, start, span, axis=1)
        pos = qp[0] - W + 1 + jnp.arange(span)
        dist = qp[:, None] - pos[None, :]
        ok = (dist >= 0) & (dist < W) & (pos[None, :] >= 0)
        lg = jnp.einsum('bqgrd,bkgd->bqgrk', qq.reshape(B, qb, G, NSA_GROUP, HD), kk).astype(jnp.float32).reshape(B, qb, H, span) * scale
        p = masked_softmax(lg + rel_bias(dist[None, :, None, :], tab), ok[None, :, None, :])
        o = jnp.einsum('bqgrk,bkgd->bqgrd', p.reshape(B, qb, G, NSA_GROUP, span).astype(vv.dtype), vv)
        return o.reshape(B, qb, H, HD)

    o_win = map_query_blocks(win_fn, WIN_QB, q_pos, q)
    g = jax.nn.sigmoid(gates.astype(jnp.float32)).astype(q.dtype)
    return g[..., 0:1] * o_cmp + g[..., 1:2] * o_sel + g[..., 2:3] * o_win


def gmlp_sgu(u, v, ws, b, gnorm):
    B, T, _ = u.shape
    vn = rmsnorm(v, gnorm)
    nc = -(-T // GM_CHUNK)
    vc = jnp.pad(vn, ((0, 0), (0, nc * GM_CHUNK - T), (0, 0))).reshape(B, nc, GM_CHUNK, GM_GROUPS, GM_W // GM_GROUPS)
    wsm = ws * jnp.tril(jnp.ones((GM_CHUNK, GM_CHUNK), ws.dtype))
    mixed = jnp.einsum('gij,bcjgd->bcigd', wsm, vc) + b.T[None, None, :, :, None]
    mixed = mixed.reshape(B, nc * GM_CHUNK, GM_W)[:, :T]
    return u * mixed, vn


def _norm_matmul_kernel(x_ref, g_ref, w_ref, o_ref, xn_ref):
    @pl.when(pl.program_id(1) == 0)
    def _():
        x = x_ref[...]
        y = x * lax.rsqrt(jnp.mean(x * x, axis=-1, keepdims=True) + NORM_EPS)
        xn_ref[...] = (y * g_ref[...]).astype(jnp.bfloat16)

    hw = w_ref.shape[1] // 2
    for s in range(2):
        cols = slice(s * hw, (s + 1) * hw)
        o_ref[:, cols] = jnp.dot(xn_ref[...], w_ref[:, cols].astype(jnp.bfloat16), preferred_element_type=jnp.float32)


def norm_matmul(x, g, w, tm=512, tn=512):
    N, D = x.shape
    C = w.shape[1]
    tm = min(tm, N)
    tn = min(tn, C)
    return pl.pallas_call(
        _norm_matmul_kernel,
        grid=(pl.cdiv(N, tm), pl.cdiv(C, tn)),
        in_specs=[
            pl.BlockSpec((tm, D), lambda i, j: (i, 0)),
            pl.BlockSpec((1, D), lambda i, j: (0, 0)),
            pl.BlockSpec((D, tn), lambda i, j: (0, j)),
        ],
        out_specs=pl.BlockSpec((tm, tn), lambda i, j: (i, j)),
        out_shape=jax.ShapeDtypeStruct((N, C), jnp.float32),
        scratch_shapes=[pltpu.VMEM((tm, D), jnp.bfloat16)],
        compiler_params=pltpu.CompilerParams(
            dimension_semantics=("arbitrary", "arbitrary"), vmem_limit_bytes=VMEM_LIMIT),
        name="norm_matmul",
    )(x, g.reshape(1, D), w.astype(jnp.bfloat16))


TQ = 256
TK = 256
NEG_BIG = -1e30
_BF16 = jnp.bfloat16
_F32 = jnp.float32


HEADS_PER_STEP = 2


def _attn_kernel(mode, shared_kv, qT_ref, k_ref, vT_ref, bias_ref, *rest):
    if mode == 'win':
        (o_ref,) = rest
        m_ref = None
    else:
        m_ref, o_ref = rest
    qi = pl.program_id(2)
    heads = range(HEADS_PER_STEP)
    q = [qT_ref[0, hh].astype(_BF16) for hh in heads]
    lane = lax.broadcasted_iota(jnp.int32, (TK, TQ), 1)
    sub = lax.broadcasted_iota(jnp.int32, (TK, TQ), 0)
    rel0 = lane - sub

    def body(kj, carry):
        dist = rel0 + (qi - kj) * TQ
        ok_all = dist >= 0
        if mode == 'sel':
            rows = m_ref[0, 0, kj]
            blk = sub // NSA_SEL_BLOCK
            r = jnp.where(blk == 0, rows[0:1], jnp.where(blk == 1, rows[1:2], jnp.where(blk == 2, rows[2:3], rows[3:4])))
            ok_all = ok_all & (r > 0.5)
        elif mode == 'dsa':
            ok_all = ok_all & (m_ref[0, pl.ds(pl.multiple_of(kj * TK, TK), TK), :].astype(_F32) > 0.5)
        elif mode == 'win':
            ok_all = ok_all & (dist < NSA_WINDOW)
        out = []
        for hh in heads:
            m, l, acc = carry[hh]
            hk = 0 if shared_kv else hh
            k_t = k_ref[0, hk, pl.ds(pl.multiple_of(kj * TK, TK), TK), :].astype(_BF16)
            s = jnp.dot(k_t, q[hh], preferred_element_type=_F32)
            s = s + bias_ref[hh, jnp.minimum(qi - kj, 2)]
            ok = ok_all & (m_ref[0, hh, kj] > 0.5) if mode == 'moba' else ok_all
            s = jnp.where(ok, s, NEG_BIG)
            m_new = jnp.maximum(m, jnp.max(s, axis=0, keepdims=True))
            p = jnp.where(ok, jnp.exp(s - m_new), 0.0)
            alpha = jnp.exp(m - m_new)
            l = alpha * l + jnp.sum(p, axis=0, keepdims=True)
            v_t = vT_ref[0, hk, kj].astype(_BF16)
            acc = alpha * acc + jnp.dot(v_t, p.astype(_BF16), preferred_element_type=_F32)
            out.append((m_new, l, acc))
        return tuple(out)

    lo = jnp.maximum(qi - (NSA_WINDOW // TK), 0) if mode == 'win' else 0
    init = tuple((jnp.full((1, TQ), NEG_BIG, _F32), jnp.zeros((1, TQ), _F32), jnp.zeros((HD, TQ), _F32)) for _ in heads)
    res = lax.fori_loop(lo, qi + 1, body, init)
    for hh in heads:
        m, l, acc = res[hh]
        o_ref[0, hh] = acc / jnp.maximum(l, 1e-30)


def attention_T(mode, qT, k, vT, bias_tiles, mask=None):
    B, H, _, T = qT.shape
    Hkv, L = k.shape[1], k.shape[2]
    grp = H // Hkv
    hp = HEADS_PER_STEP
    assert H % hp == 0 and (grp == 1 or grp % hp == 0)
    shared_kv = grp > 1
    kvh = 1 if shared_kv else hp
    kv_idx = (lambda g: g * hp // grp) if shared_kv else (lambda g: g)
    in_specs = [
        pl.BlockSpec((1, hp, HD, TQ), lambda b, g, i: (b, g, 0, i)),
        pl.BlockSpec((1, kvh, L, HD), lambda b, g, i: (b, kv_idx(g), 0, 0)),
        pl.BlockSpec((1, kvh, L // TK, HD, TK), lambda b, g, i: (b, kv_idx(g), 0, 0, 0)),
        pl.BlockSpec((hp, 3, TK, TQ), lambda b, g, i: (g, 0, 0, 0)),
    ]
    args = [qT, k, vT, bias_tiles]
    if mode == 'moba':
        in_specs.append(pl.BlockSpec((1, hp, L // TK, 1, TQ), lambda b, g, i: (b, g, 0, 0, i)))
        args.append(mask.reshape(B, H, L // TK, 1, T))
    elif mode == 'sel':
        per = TK // NSA_SEL_BLOCK
        in_specs.append(pl.BlockSpec((1, 1, L // TK, per, TQ), lambda b, g, i: (b, kv_idx(g), 0, 0, i)))
        args.append(mask.reshape(B, Hkv, L // TK, per, T))
    elif mode == 'dsa':
        in_specs.append(pl.BlockSpec((1, L, TQ), lambda b, g, i: (b, 0, i)))
        args.append(mask)
    return pl.pallas_call(
        functools.partial(_attn_kernel, mode, shared_kv),
        grid=(B, H // hp, T // TQ),
        in_specs=in_specs,
        out_specs=pl.BlockSpec((1, hp, HD, TQ), lambda b, g, i: (b, g, 0, i)),
        out_shape=jax.ShapeDtypeStruct((B, H, HD, T), _F32),
        compiler_params=pltpu.CompilerParams(
            dimension_semantics=("arbitrary", "arbitrary", "arbitrary"), vmem_limit_bytes=VMEM_LIMIT),
        name="attn_" + mode,
    )(*args)


def _moba_gate_kernel(qT_ref, k_ref, sel_ref):
    nblk, T = sel_ref.shape[2], sel_ref.shape[3]
    q = qT_ref[0, 0].astype(_BF16)
    k = k_ref[0, 0]
    km = jnp.mean(k.reshape(nblk, MOBA_BLOCK, HD), axis=1)
    gs = jnp.dot(km.astype(_BF16), q, preferred_element_type=_F32)
    j = lax.broadcasted_iota(jnp.int32, (nblk, T), 0)
    own = lax.broadcasted_iota(jnp.int32, (nblk, T), 1) // MOBA_BLOCK
    fully_past = j < own
    gs = jnp.where(fully_past, gs, -jnp.inf)
    rank = jnp.zeros((nblk, T), jnp.int32)
    for jp in range(nblk):
        row = gs[jp:jp + 1, :]
        ahead = (row > gs) | ((row == gs) & (jp < j))
        rank = rank + ahead.astype(jnp.int32)
    sel = (fully_past & (rank < MOBA_TOPK)) | (j == own)
    sel_ref[0, 0] = sel.astype(_F32)


def moba_gate(qT, k):
    B, H, _, T = qT.shape
    L = k.shape[2]
    nblk = L // MOBA_BLOCK
    return pl.pallas_call(
        _moba_gate_kernel,
        grid=(B, H),
        in_specs=[pl.BlockSpec((1, 1, HD, T), lambda b, h: (b, h, 0, 0)),
                  pl.BlockSpec((1, 1, L, HD), lambda b, h: (b, h, 0, 0))],
        out_specs=pl.BlockSpec((1, 1, nblk, T), lambda b, h: (b, h, 0, 0)),
        out_shape=jax.ShapeDtypeStruct((B, H, nblk, T), _F32),
        compiler_params=pltpu.CompilerParams(dimension_semantics=("arbitrary", "arbitrary"), vmem_limit_bytes=VMEM_LIMIT),
        name="moba_gate",
    )(qT, k)


def _count_rows(pred):
    return jnp.sum(pred.astype(_F32), axis=0, keepdims=True)


def _dsa_mask_kernel(topk, ki_ref, qiT_ref, wiT_ref, mask_ref, key_ref):
    L = ki_ref.shape[1]
    qi = pl.program_id(1)
    ki = ki_ref[0].astype(_BF16)
    score = jnp.zeros((L, TQ), _F32)
    for hh in range(DSA_IDX_HEADS):
        s = jnp.dot(ki, qiT_ref[0, hh].astype(_BF16), preferred_element_type=_F32) * DSA_IDX_DIM ** -0.5
        score = score + wiT_ref[0, hh:hh + 1, :] * jnp.maximum(s, 0.0)
    idx = lax.broadcasted_iota(jnp.int32, (L, TQ), 0)
    q_pos = qi * TQ + lax.broadcasted_iota(jnp.int32, (L, TQ), 1)
    adm = idx <= q_pos
    score = jnp.where(adm, jnp.where(score == 0.0, 0.0, score), -jnp.inf)
    bits = pltpu.bitcast(score, jnp.int32)
    key_ref[...] = jnp.where(bits < 0, bits ^ jnp.int32(0x7FFFFFFF), bits)
    kf = jnp.float32(topk)
    int_min = jnp.int32(-2 ** 31)
    sub = lax.broadcasted_iota(jnp.int32, (TK, TQ), 0)

    def count(pred):
        def tile(kt, acc):
            rows = pl.ds(pl.multiple_of(kt * TK, TK), TK)
            return acc + _count_rows(pred(key_ref[rows, :], kt * TK + sub))
        return lax.fori_loop(0, qi + 1, tile, jnp.zeros((1, TQ), _F32))

    lo = jnp.where(count(lambda k, i: k >= 0) >= kf, jnp.int32(0), int_min)

    def vstep(i, lo):
        cand = lo + jnp.left_shift(jnp.int32(1), 30 - i)
        return jnp.where(count(lambda k, i: k >= cand) >= kf, cand, lo)

    thr = lax.fori_loop(0, 31, vstep, lo)
    need = kf - count(lambda k, i: k > thr)
    nbits = max(1, (L - 1).bit_length())

    def istep(i, lo):
        cand = lo + jnp.left_shift(jnp.int32(1), nbits - 1 - i)
        c = count(lambda k, ix: (k == thr) & (ix < cand))
        return jnp.where(c < need, cand, lo)

    last_tie = lax.fori_loop(0, nbits, istep, jnp.zeros((1, TQ), jnp.int32))
    key = key_ref[...]
    keep = ((key > thr) | ((key == thr) & (idx <= last_tie))) & adm
    mask_ref[0] = keep.astype(mask_ref.dtype)


def dsa_mask(ki, qiT, wiT, topk):
    B, L, _ = ki.shape
    T = qiT.shape[3]
    return pl.pallas_call(
        functools.partial(_dsa_mask_kernel, topk),
        grid=(B, T // TQ),
        in_specs=[pl.BlockSpec((1, L, DSA_IDX_DIM), lambda b, i: (b, 0, 0)),
                  pl.BlockSpec((1, DSA_IDX_HEADS, DSA_IDX_DIM, TQ), lambda b, i: (b, 0, 0, i)),
                  pl.BlockSpec((1, DSA_IDX_HEADS, TQ), lambda b, i: (b, 0, i))],
        out_specs=pl.BlockSpec((1, L, TQ), lambda b, i: (b, 0, i)),
        out_shape=jax.ShapeDtypeStruct((B, L, T), _BF16),
        scratch_shapes=[pltpu.VMEM((L, TQ), jnp.int32)],
        compiler_params=pltpu.CompilerParams(dimension_semantics=("arbitrary", "arbitrary"), vmem_limit_bytes=VMEM_LIMIT),
        name="dsa_mask",
    )(ki, qiT, wiT)


PEER_TN = 256
PEER_EB = 1024
PEER_IB = PEER_EB // PEER_NKEYS
RANK_OUT = 4096.0


def _extract_top(work, n):
    rows = work.shape[0]
    iota = lax.broadcasted_iota(jnp.int32, work.shape, 0)
    order = jnp.full(work.shape, RANK_OUT, _F32)
    vals = []
    idx = None
    for a in range(n):
        m = jnp.max(work, axis=0, keepdims=True)
        idx = jnp.min(jnp.where(work == m, iota, rows), axis=0, keepdims=True)
        hit = iota == idx
        vals.append(m)
        order = jnp.where(hit, jnp.float32(a), order)
        work = jnp.where(hit, -jnp.inf, work)
    return vals, order, idx


def _peer_select_kernel(x_ref, g_ref, wqT_ref, keys_ref, hx_ref, c1_ref, e1n_ref, r2_ref, e2_ref):
    x = x_ref[...]
    y = x * lax.rsqrt(jnp.mean(x * x, axis=1, keepdims=True) + NORM_EPS) * g_ref[...]
    hx = y.astype(_BF16)
    hx_ref[...] = hx
    qT = lax.dot_general(wqT_ref[...], hx, (((1,), (1,)), ((), ())), preferred_element_type=_F32).astype(_BF16)
    half = PEER_QDIM // 2
    for h in range(PEER_HEADS):
        sc = []
        for c in range(2):
            r0 = (h * 2 + c) * half
            sc.append(jnp.dot(keys_ref[h * 2 + c], qT[r0:r0 + half, :], preferred_element_type=_F32))
        v1, o1, _ = _extract_top(sc[0], PEER_TOPK)
        v2, o2, _ = _extract_top(sc[1], PEER_TOPK)
        v1s = jnp.concatenate(v1, axis=0)
        v2s = jnp.concatenate(v2, axis=0)
        ex1s = jnp.exp(v1s - v1[0])
        ex2s = jnp.exp(v2s - v2[0])
        lo_a, hi_a = PEER_TOPK // 2, PEER_TOPK
        cand = jnp.concatenate([v1[0] + v2s] + [v1[a] + v2s[:lo_a] for a in range(1, lo_a)] + [v1s[lo_a:hi_a] + v2[0]], axis=0)
        ecand = jnp.concatenate([ex1s[0:1] * ex2s] + [ex1s[a:a + 1] * ex2s[:lo_a] for a in range(1, lo_a)]
                                + [ex1s[lo_a:hi_a] * ex2s[0:1]], axis=0)
        _, corder, _ = _extract_top(cand, PEER_TOPK)
        chosen = (corder < RANK_OUT).astype(_F32)
        z = jnp.sum(chosen * ecand, axis=0, keepdims=True)
        tail0 = PEER_TOPK + (lo_a - 1) * lo_a
        c1 = jnp.zeros(sc[0].shape, _F32)
        for a in range(PEER_TOPK):
            if a == 0:
                grp = chosen[0:PEER_TOPK]
            elif a < lo_a:
                grp = chosen[PEER_TOPK + (a - 1) * lo_a:PEER_TOPK + a * lo_a]
            else:
                grp = chosen[tail0 + a - lo_a:tail0 + a - lo_a + 1]
            cnt = jnp.sum(grp, axis=0, keepdims=True)
            c1 = jnp.where(o1 == jnp.float32(a), cnt, c1)
        c1_ref[h] = c1
        e1n_ref[h] = jnp.exp(sc[0] - v1[0]) / z
        r2_ref[h] = o2
        e2_ref[h] = jnp.exp(sc[1] - v2[0])


def peer_select(x, g, wqT, keys2):
    Np, D = x.shape
    tn = PEER_TN
    row = lambda: pl.BlockSpec((PEER_HEADS, PEER_NKEYS, tn), lambda t: (0, 0, t))
    rshape = jax.ShapeDtypeStruct((PEER_HEADS, PEER_NKEYS, Np), _F32)
    return pl.pallas_call(
        _peer_select_kernel,
        grid=(Np // tn,),
        in_specs=[pl.BlockSpec((tn, D), lambda t: (t, 0)),
                  pl.BlockSpec((1, D), lambda t: (0, 0)),
                  pl.BlockSpec((PEER_HEADS * PEER_QDIM, D), lambda t: (0, 0)),
                  pl.BlockSpec((2 * PEER_HEADS, PEER_NKEYS, PEER_QDIM // 2), lambda t: (0, 0, 0))],
        out_specs=[pl.BlockSpec((tn, D), lambda t: (t, 0)), row(), row(), row(), row()],
        out_shape=[jax.ShapeDtypeStruct((Np, D), _BF16), rshape, rshape, rshape, rshape],
        compiler_params=pltpu.CompilerParams(dimension_semantics=("arbitrary",), vmem_limit_bytes=VMEM_LIMIT),
        name="peer_select",
    )(x, g, wqT, keys2)


def _gelu_tanh(x):
    return 0.5 * x * (1.0 + jnp.tanh(math.sqrt(2.0 / math.pi) * (x + 0.044715 * (x * x * x))))


def _peer_dense_kernel(x_ref, hx_ref, u_ref, vT_ref, c1_ref, e1n_ref, r2_ref, e2_ref, o_ref, p_ref, acc_ref):
    e = pl.program_id(1)
    tn, D = x_ref.shape

    @pl.when(e == 0)
    def _():
        acc_ref[...] = jnp.zeros_like(acc_ref)

    nt = (((1,), (1,)), ((), ()))
    half = PEER_EB // 2
    for s in range(2):
        act = _gelu_tanh(lax.dot_general(u_ref[s * half:(s + 1) * half, :], hx_ref[...], nt, preferred_element_type=_F32))
        for ib in range(s * half // PEER_NKEYS, (s + 1) * half // PEER_NKEYS):
            w = jnp.zeros((PEER_NKEYS, tn), _F32)
            for h in range(PEER_HEADS):
                w = w + jnp.where(r2_ref[h] < c1_ref[h, ib:ib + 1, :], e1n_ref[h, ib:ib + 1, :] * e2_ref[h], 0.0)
            r0 = ib * PEER_NKEYS - s * half
            p_ref[ib * PEER_NKEYS:(ib + 1) * PEER_NKEYS, :] = (w * act[r0:r0 + PEER_NKEYS, :]).astype(_BF16)
    for s in range(2):
        rows = slice(s * (D // 2), (s + 1) * (D // 2))
        acc_ref[rows, :] += jnp.dot(vT_ref[rows, :], p_ref[...], preferred_element_type=_F32)

    @pl.when(e == pl.num_programs(1) - 1)
    def _():
        o_ref[...] = x_ref[...] + acc_ref[...].T


def peer_dense(x, hx, u, vT, c1, e1n, r2, e2):
    Np, D = x.shape
    E = u.shape[0]
    tn = PEER_TN
    rowi = lambda: pl.BlockSpec((PEER_HEADS, PEER_IB, tn), lambda t, e: (0, e, t))
    rowj = lambda: pl.BlockSpec((PEER_HEADS, PEER_NKEYS, tn), lambda t, e: (0, 0, t))
    return pl.pallas_call(
        _peer_dense_kernel,
        grid=(Np // tn, E // PEER_EB),
        in_specs=[pl.BlockSpec((tn, D), lambda t, e: (t, 0)),
                  pl.BlockSpec((tn, D), lambda t, e: (t, 0)),
                  pl.BlockSpec((PEER_EB, D), lambda t, e: (e, 0)),
                  pl.BlockSpec((D, PEER_EB), lambda t, e: (0, e)),
                  rowi(), rowi(), rowj(), rowj()],
        out_specs=pl.BlockSpec((tn, D), lambda t, e: (t, 0)),
        out_shape=jax.ShapeDtypeStruct((Np, D), _F32),
        scratch_shapes=[pltpu.VMEM((PEER_EB, tn), _BF16), pltpu.VMEM((D, tn), _F32)],
        compiler_params=pltpu.CompilerParams(dimension_semantics=("arbitrary", "arbitrary"), vmem_limit_bytes=VMEM_LIMIT),
        name="peer_dense",
    )(x, hx, u, vT, c1, e1n, r2, e2)


def peer_ffn_tokens(x_tokens, g, wq, keys, U, V):
    N, D = x_tokens.shape
    Np = -(-N // PEER_TN) * PEER_TN
    x = jnp.pad(x_tokens, ((0, Np - N), (0, 0)))
    keys2 = keys.reshape(2 * PEER_HEADS, PEER_NKEYS, PEER_QDIM // 2).astype(_BF16)
    hx, c1, e1n, r2, e2 = peer_select(x, g.reshape(1, D), wq.T.astype(_BF16), keys2)
    return peer_dense(x, hx, U.astype(_BF16), V.T.astype(_BF16), c1, e1n, r2, e2)[:N]


def bias_lookup(dist, tab):
    onehot = (rel_bucket(dist)[..., None] == jnp.arange(REL_BUCKETS)).astype(_F32)
    return jnp.einsum('...b,bh->h...', onehot, tab.astype(_F32), precision=lax.Precision.HIGHEST)


def rel_bias_tiles(tab):
    kk = jnp.arange(TK)[:, None]
    qq = jnp.arange(TQ)[None, :]
    dist = jnp.stack([d * TQ + qq - kk for d in range(3)])
    return bias_lookup(dist, tab)


ROWS_T = 8
DEC_ROWS = N_HEADS * ROWS_T
NEAR_TILES = 3


PP = 4


def _page_specs(block, n_pages):
    def spec(i):
        return pl.BlockSpec(block, lambda b, p, pt: (pt[b, jnp.minimum(PP * p + i, n_pages - 1)], 0, 0))
    return [spec(i) for i in range(PP)]


def _moba_decode_gate_kernel(pt_ref, q_ref, *rest):
    caches, gs_ref = rest[:PP], rest[PP]
    p = pl.program_id(1)
    per = MOBA_BLOCK // PAGE_SIZE

    @pl.when(p == 0)
    def _():
        gs_ref[...] = jnp.zeros_like(gs_ref)

    lane = lax.broadcasted_iota(jnp.int32, gs_ref.shape[1:], 1)
    for g in range(PP // per):
        ksum = sum(jnp.sum(caches[g * per + i][0], axis=0, keepdims=True) for i in range(per))
        km = (ksum * (1.0 / MOBA_BLOCK)).astype(_BF16).astype(_F32)
        col = jnp.sum(q_ref[0].astype(_F32) * km, axis=1, keepdims=True)
        gs_ref[0] = jnp.where(lane == p * (PP // per) + g, col, gs_ref[0])


def moba_decode_gate(pt, q_rows, cache_rows):
    B, n_pages = pt.shape
    W = N_HEADS * HD
    assert n_pages % PP == 0 and PP % (MOBA_BLOCK // PAGE_SIZE) == 0
    return pl.pallas_call(
        _moba_decode_gate_kernel,
        grid_spec=pltpu.PrefetchScalarGridSpec(
            num_scalar_prefetch=1,
            grid=(B, n_pages // PP),
            in_specs=[pl.BlockSpec((1, DEC_ROWS, W), lambda b, p, pt: (b, 0, 0))] + _page_specs((1, PAGE_SIZE, W), n_pages),
            out_specs=pl.BlockSpec((1, DEC_ROWS, 128), lambda b, p, pt: (b, 0, 0))),
        out_shape=jax.ShapeDtypeStruct((B, DEC_ROWS, 128), _F32),
        compiler_params=pltpu.CompilerParams(dimension_semantics=("arbitrary", "arbitrary"), vmem_limit_bytes=VMEM_LIMIT),
        name="moba_decode_gate",
    )(pt, q_rows, *([cache_rows] * PP))


def _paged_attn_kernel(mode, nblk, pt_ref, q_ref, *rest):
    caches = rest[:PP]
    new_ref, far_ref, near_ref, m_in_ref, o_ref, m_sc, l_sc, acc_sc, sel_sc = rest[PP:]
    p = pl.program_id(1)
    last = pl.num_programs(1) - 1
    n_pages = last * PP
    W = N_HEADS * HD
    lane = lax.broadcasted_iota(jnp.int32, (DEC_ROWS, PAGE_SIZE), 1)
    row_t = lax.broadcasted_iota(jnp.int32, (DEC_ROWS, PAGE_SIZE), 0) % ROWS_T

    @pl.when(p == 0)
    def _():
        m_sc[...] = jnp.full_like(m_sc, NEG_BIG)
        l_sc[...] = jnp.zeros_like(l_sc)
        acc_sc[...] = jnp.zeros_like(acc_sc)
        if mode == 'moba':
            valid = lane < nblk
            gs = jnp.where(valid, m_in_ref[0], -jnp.inf)
            rank = jnp.zeros(gs.shape, jnp.int32)
            for jp in range(nblk):
                col = gs[:, jp:jp + 1]
                rank = rank + ((col > gs) | ((col == gs) & (jp < lane))).astype(jnp.int32)
            sel_sc[...] = (valid & (rank < MOBA_TOPK)).astype(_F32)

    def tile_scores(blk, g, i, is_new):
        k = blk[:, :W].astype(_BF16)
        s = lax.dot_general(q_ref[0], k, (((1,), (1,)), ((), ())), preferred_element_type=_F32)
        near = near_ref[jnp.clip(g - (n_pages + 1 - NEAR_TILES), 0, NEAR_TILES - 1)]
        s = s + jnp.where(g >= n_pages + 1 - NEAR_TILES, near, far_ref[...])
        if mode == 'moba':
            if is_new:
                ok = lane <= row_t
            else:
                in_blk = jnp.sum(jnp.where(lane == g // (MOBA_BLOCK // PAGE_SIZE), sel_sc[...], 0.0), axis=1, keepdims=True)
                ok = jnp.broadcast_to(in_blk, s.shape) > 0.5
        else:
            ok = jnp.concatenate([m_in_ref[0, i]] * N_HEADS, axis=0) > 0.5
        return jnp.where(ok, s, NEG_BIG), ok.astype(_F32)

    def update(s, ok, v):
        m_new = jnp.maximum(m_sc[...], jnp.max(s, axis=1, keepdims=True))
        pr = jnp.where(ok > 0.5, jnp.exp(s - m_new), 0.0)
        alpha = jnp.exp(m_sc[...] - m_new)
        l_sc[...] = alpha * l_sc[...] + jnp.sum(pr, axis=1, keepdims=True)
        acc_sc[...] = alpha * acc_sc[...] + jnp.dot(pr.astype(_BF16), v, preferred_element_type=_F32)
        m_sc[...] = m_new

    @pl.when(p < last)
    def _():
        parts = [tile_scores(caches[i][0], PP * p + i, i, False) for i in range(PP)]
        v_all = jnp.concatenate([caches[i][0][:, W:].astype(_BF16) for i in range(PP)], axis=0)
        update(jnp.concatenate([s for s, _ in parts], axis=1), jnp.concatenate([ok for _, ok in parts], axis=1), v_all)

    @pl.when(p == last)
    def _():
        s, ok = tile_scores(new_ref[0], n_pages, 0, True)
        update(s, ok, new_ref[0][:, W:].astype(_BF16))
        o_ref[0] = acc_sc[...] / jnp.maximum(l_sc[...], 1e-30)


def paged_attention(mode, pt, q_rows, cache_rows, new_rows, far, near, m_in):
    B, n_pages = pt.shape
    W = N_HEADS * HD
    assert n_pages % PP == 0
    if mode == 'moba':
        m_spec = pl.BlockSpec((1, DEC_ROWS, 128), lambda b, p, pt: (b, 0, 0))
    else:
        m_spec = pl.BlockSpec((1, PP, ROWS_T, PAGE_SIZE), lambda b, p, pt: (b, p, 0, 0))
    return pl.pallas_call(
        functools.partial(_paged_attn_kernel, mode, n_pages // (MOBA_BLOCK // PAGE_SIZE)),
        grid_spec=pltpu.PrefetchScalarGridSpec(
            num_scalar_prefetch=1,
            grid=(B, n_pages // PP + 1),
            in_specs=[pl.BlockSpec((1, DEC_ROWS, W), lambda b, p, pt: (b, 0, 0))] + _page_specs((1, PAGE_SIZE, 2 * W), n_pages) + [
                      pl.BlockSpec((1, PAGE_SIZE, 2 * W), lambda b, p, pt: (b, 0, 0)),
                      pl.BlockSpec((DEC_ROWS, 1), lambda b, p, pt: (0, 0)),
                      pl.BlockSpec((NEAR_TILES, DEC_ROWS, PAGE_SIZE), lambda b, p, pt: (0, 0, 0)),
                      m_spec],
            out_specs=pl.BlockSpec((1, DEC_ROWS, W), lambda b, p, pt: (b, 0, 0)),
            scratch_shapes=[pltpu.VMEM((DEC_ROWS, 1), _F32), pltpu.VMEM((DEC_ROWS, 1), _F32),
                            pltpu.VMEM((DEC_ROWS, W), _F32), pltpu.VMEM((DEC_ROWS, 128), _F32)]),
        out_shape=jax.ShapeDtypeStruct((B, DEC_ROWS, W), _F32),
        compiler_params=pltpu.CompilerParams(dimension_semantics=("arbitrary", "arbitrary"), vmem_limit_bytes=VMEM_LIMIT),
        name="paged_attn_" + mode,
    )(pt, q_rows, *([cache_rows] * PP), new_rows, far, near, m_in)


def _dsa_decode_score_kernel(pt_ref, qi_ref, wi_ref, *rest):
    caches, new_ref, score_ref = rest[:PP], rest[PP], rest[PP + 1]
    p = pl.program_id(1)
    last = pl.num_programs(1) - 1

    def tile_scores(ki):
        ki = ki.astype(_BF16)
        acc = jnp.zeros((ROWS_T, PAGE_SIZE), _F32)
        for hh in range(DSA_IDX_HEADS):
            s = lax.dot_general(qi_ref[0, hh], ki, (((1,), (1,)), ((), ())), preferred_element_type=_F32) * DSA_IDX_DIM ** -0.5
            acc = acc + wi_ref[0, hh] * jnp.maximum(s, 0.0)
        return acc

    @pl.when(p < last)
    def _():
        for i in range(PP):
            score_ref[0, :, i * PAGE_SIZE:(i + 1) * PAGE_SIZE] = tile_scores(caches[i][0])

    @pl.when(p == last)
    def _():
        score_ref[0] = jnp.zeros(score_ref.shape[1:], _F32)
        score_ref[0, :, 0:PAGE_SIZE] = tile_scores(new_ref[0])


def dsa_decode_scores(pt, qi_rows, wi_rows, cache_rows, new_rows):
    B, n_pages = pt.shape
    assert n_pages % PP == 0
    return pl.pallas_call(
        _dsa_decode_score_kernel,
        grid_spec=pltpu.PrefetchScalarGridSpec(
            num_scalar_prefetch=1,
            grid=(B, n_pages // PP + 1),
            in_specs=[pl.BlockSpec((1, DSA_IDX_HEADS, ROWS_T, DSA_IDX_DIM), lambda b, p, pt: (b, 0, 0, 0)),
                      pl.BlockSpec((1, DSA_IDX_HEADS, ROWS_T, 1), lambda b, p, pt: (b, 0, 0, 0))]
            + _page_specs((1, PAGE_SIZE, DSA_IDX_DIM), n_pages)
            + [pl.BlockSpec((1, PAGE_SIZE, DSA_IDX_DIM), lambda b, p, pt: (b, 0, 0))],
            out_specs=pl.BlockSpec((1, ROWS_T, PP * PAGE_SIZE), lambda b, p, pt: (b, 0, p))),
        out_shape=jax.ShapeDtypeStruct((B, ROWS_T, (n_pages + PP) * PAGE_SIZE), _F32),
        compiler_params=pltpu.CompilerParams(dimension_semantics=("arbitrary", "arbitrary"), vmem_limit_bytes=VMEM_LIMIT),
        name="dsa_decode_scores",
    )(pt, qi_rows, wi_rows, *([cache_rows] * PP), new_rows)


def _count_lanes(pred):
    return jnp.sum(pred.astype(_F32), axis=1, keepdims=True)


def _dsa_decode_select_kernel(topk, past_len, score_ref, keep_ref, key_ref):
    Lp = score_ref.shape[2]
    idx = lax.broadcasted_iota(jnp.int32, (ROWS_T, Lp), 1)
    q_pos = past_len + lax.broadcasted_iota(jnp.int32, (ROWS_T, Lp), 0)
    adm = idx <= q_pos
    score = score_ref[0]
    score = jnp.where(adm, jnp.where(score == 0.0, 0.0, score), -jnp.inf)
    bits = pltpu.bitcast(score, jnp.int32)
    key_ref[...] = jnp.where(bits < 0, bits ^ jnp.int32(0x7FFFFFFF), bits)
    kf = jnp.float32(topk)
    lo = jnp.where(_count_lanes(key_ref[...] >= 0) >= kf, jnp.int32(0), jnp.int32(-2 ** 31))

    def vstep(i, lo):
        cand = lo + jnp.left_shift(jnp.int32(1), 30 - i)
        return jnp.where(_count_lanes(key_ref[...] >= cand) >= kf, cand, lo)

    thr = lax.fori_loop(0, 31, vstep, lo)
    need = kf - _count_lanes(key_ref[...] > thr)
    nbits = max(1, (Lp - 1).bit_length())

    def istep(i, lo):
        cand = lo + jnp.left_shift(jnp.int32(1), nbits - 1 - i)
        c = _count_lanes((key_ref[...] == thr) & (idx < cand))
        return jnp.where(c < need, cand, lo)

    last_tie = lax.fori_loop(0, nbits, istep, jnp.zeros((ROWS_T, 1), jnp.int32))
    key = key_ref[...]
    keep = ((key > thr) | ((key == thr) & (idx <= last_tie))) & adm
    keep_ref[0] = keep.astype(_F32)


def dsa_decode_select(score, topk, past_len):
    B, _, Lp = score.shape
    return pl.pallas_call(
        functools.partial(_dsa_decode_select_kernel, topk, past_len),
        grid=(B,),
        in_specs=[pl.BlockSpec((1, ROWS_T, Lp), lambda b: (b, 0, 0))],
        out_specs=pl.BlockSpec((1, ROWS_T, Lp), lambda b: (b, 0, 0)),
        out_shape=jax.ShapeDtypeStruct((B, ROWS_T, Lp), _F32),
        scratch_shapes=[pltpu.VMEM((ROWS_T, Lp), jnp.int32)],
        compiler_params=pltpu.CompilerParams(dimension_semantics=("arbitrary",), vmem_limit_bytes=VMEM_LIMIT),
        name="dsa_decode_select",
    )(score)


def decode_rows(q):
    B, T, H, _ = q.shape
    qp = jnp.pad(jnp.transpose(q, (0, 2, 1, 3)), ((0, 0), (0, 0), (0, ROWS_T - T), (0, 0)))
    rows = qp[:, :, :, None, :] * jnp.eye(H, dtype=q.dtype)[None, :, None, :, None]
    return rows.reshape(B, H * ROWS_T, H * HD)


def from_decode_rows(o, T):
    B = o.shape[0]
    o5 = o.reshape(B, N_HEADS, ROWS_T, N_HEADS, HD)
    own = jnp.sum(o5 * jnp.eye(N_HEADS, dtype=o.dtype)[None, :, None, :, None], axis=3)
    return jnp.transpose(own[:, :, :T], (0, 2, 1, 3)).reshape(B, T, N_HEADS * HD)


def decode_bias(tab, past_len, n_pages):
    t = jnp.arange(ROWS_T)[None, :, None]
    tile = (n_pages + 1 - NEAR_TILES + jnp.arange(NEAR_TILES))[:, None, None]
    dist = past_len + t - (tile * PAGE_SIZE + jnp.arange(PAGE_SIZE)[None, None, :])
    near = jnp.transpose(bias_lookup(dist, tab), (1, 0, 2, 3)).reshape(NEAR_TILES, DEC_ROWS, PAGE_SIZE)
    far = jnp.repeat(tab[REL_BUCKETS - 1].astype(_F32), ROWS_T).reshape(DEC_ROWS, 1)
    return far, near


def pad_rows(a, n):
    return jnp.pad(a, ((0, 0), (0, n - a.shape[1])) + ((0, 0),) * (a.ndim - 2))


def heads_T(z, nh):
    B, T, _ = z.shape
    return jnp.transpose(z.reshape(B, T, nh, HD), (0, 2, 3, 1))


def heads_K(z, nh):
    B, L, _ = z.shape
    return jnp.transpose(z.reshape(B, L, nh, HD), (0, 2, 1, 3))


def heads_VT(z, nh):
    B, L, _ = z.shape
    return jnp.transpose(z.reshape(B, L // TK, TK, nh, HD), (0, 3, 1, 4, 2))


def from_heads_T(oT):
    B, H, _, T = oT.shape
    return jnp.transpose(oT, (0, 3, 1, 2)).reshape(B, T, H * HD)


def nsa_attention_prompt(q_flat, gates_flat, nsa_new, q_pos, w1, w2, pe, tab, bias_tiles):
    B, T, _ = q_flat.shape
    G, H, L = NSA_KV_HEADS, N_HEADS, T
    scale = HD ** -0.5
    q = q_flat.reshape(B, T, H, HD)
    kc, vc = nsa_new[:, :, 0], nsa_new[:, :, 1]
    nch = L // NSA_CMP_STRIDE
    ncmp = nch - 1

    def compress(x, i):
        c = x[:, :nch * NSA_CMP_STRIDE].reshape(B, nch, NSA_CMP_STRIDE, G, HD)
        blk = jnp.concatenate([c[:, :-1], c[:, 1:]], axis=2) + pe[i][None, None, :, None, :]
        blk = jnp.transpose(blk, (0, 1, 3, 2, 4)).reshape(B, ncmp, G, NSA_CMP_LEN * HD)
        return jax.nn.gelu(blk @ w1[i]) @ w2[i]

    kcmp = compress(kc, 0)
    vcmp = compress(vc, 1)
    cmp_end = jnp.arange(ncmp) * NSA_CMP_STRIDE + NSA_CMP_LEN - 1
    qg = q.reshape(B, T, G, NSA_GROUP, HD)
    dist_c = q_pos[:, None] - cmp_end[None, :]
    lg_c = jnp.einsum('btgrd,bngd->btgrn', qg, kcmp).astype(jnp.float32).reshape(B, T, H, ncmp) * scale
    lg_c = lg_c + rel_bias(dist_c[None, :, None, :], tab)
    p_c = masked_softmax(lg_c, (dist_c >= 0)[None, :, None, :])
    p_cg = p_c.reshape(B, T, G, NSA_GROUP, ncmp)
    o_cmp = jnp.einsum('btgrn,bngd->btgrd', p_cg.astype(vcmp.dtype), vcmp).reshape(B, T, H, HD)
    nsel = -(-L // NSA_SEL_BLOCK)
    ii = np.arange(ncmp)[:, None]
    jj_np = np.arange(nsel)[None, :]
    overlap = ((ii * NSA_CMP_STRIDE < (jj_np + 1) * NSA_SEL_BLOCK) & (ii * NSA_CMP_STRIDE + NSA_CMP_LEN > jj_np * NSA_SEL_BLOCK)).astype(np.float32)
    imp = jnp.einsum('btgrn,nj->btgj', p_cg, jnp.asarray(overlap))
    cur = q_pos // NSA_SEL_BLOCK
    jj = jnp.arange(nsel)[None, :]
    causal_blk = jj <= cur[:, None]
    forced = (jj == 0) | (jj == cur[:, None]) | (jj == cur[:, None] - 1)
    imp = jnp.where(forced[None, :, None, :], jnp.inf, jnp.where(causal_blk[None, :, None, :], imp, -jnp.inf))
    _, sel = lax.top_k(imp, min(NSA_TOPN, nsel))
    chosen = jnp.any(sel[..., None] == jnp.arange(nsel), axis=-2)
    selT = jnp.transpose(chosen, (0, 2, 3, 1)).astype(_F32)
    qT = heads_T(q_flat, H) * scale
    flat = lambda a: a.reshape(B, L, G * HD)
    o_sel = from_heads_T(attention_T('sel', qT, heads_K(flat(nsa_new[:, :, 2]), G), heads_VT(flat(nsa_new[:, :, 3]), G), bias_tiles, selT))
    o_win = from_heads_T(attention_T('win', qT, heads_K(flat(nsa_new[:, :, 4]), G), heads_VT(flat(nsa_new[:, :, 5]), G), bias_tiles))
    g = jax.nn.sigmoid(gates_flat.reshape(B, T, H, 3).astype(jnp.float32))
    out = g[..., 0:1] * o_cmp + g[..., 1:2] * o_sel.reshape(B, T, H, HD) + g[..., 2:3] * o_win.reshape(B, T, H, HD)
    return out.reshape(B, T, H * HD)


def token_mix(z, past, lp, tab):
    B, T, _ = z.shape
    P = 0 if past is None else past['nsa'].shape[1]
    q_pos = P + jnp.arange(T, dtype=jnp.int32)
    qa, ka, va, q_nsa, nsa_kv, nsa_g, qc, kc, vc, qi, ki, wi, u, vg = split_last(z, IN_SPLITS)
    moba_new = jnp.stack([ka, va], axis=2).reshape(B, T, 2, N_HEADS, HD)
    nsa_new = nsa_kv.reshape(B, T, 6, NSA_KV_HEADS, HD)
    nsa_main_new = nsa_new[:, :, :4]
    win_new = nsa_new[:, :, 4:]
    dsa_new = jnp.stack([kc, vc], axis=2).reshape(B, T, 2, N_HEADS, HD)
    if past is None:
        nsa_all, win_all = nsa_main_new, win_new
    else:
        nsa_all = jnp.concatenate([past['nsa'], nsa_main_new], axis=1)
        win_all = jnp.concatenate([past['win'], win_new], axis=1)
    kw_pos0 = P + T - win_all.shape[1]
    scale = HD ** -0.5
    if past is None:
        bt = lp['bias_tiles']
        qaT = heads_T(qa, N_HEADS) * scale
        y_a = from_heads_T(attention_T('moba', qaT, heads_K(ka, N_HEADS), heads_VT(va, N_HEADS), bt[0], moba_gate(qaT, heads_K(ka, N_HEADS))))
        y_b = nsa_attention_prompt(q_nsa, nsa_g, nsa_new, q_pos, lp['cmp_w1'], lp['cmp_w2'], lp['cmp_pe'], tab[:, N_HEADS:2 * N_HEADS], bt[1])
        qiT = jnp.transpose(qi.reshape(B, T, DSA_IDX_HEADS, DSA_IDX_DIM), (0, 2, 3, 1))
        keep = dsa_mask(ki, qiT, jnp.transpose(wi, (0, 2, 1)), min(DSA_TOPK, T // 4))
        y_c = from_heads_T(attention_T('dsa', heads_T(qc, N_HEADS) * scale, heads_K(kc, N_HEADS), heads_VT(vc, N_HEADS), bt[2], keep))
    else:
        assert P % MOBA_BLOCK == 0 and T <= ROWS_T
        pt = past['pt']
        n_pages = pt.shape[1]
        far_a, near_a = decode_bias(tab[:, :N_HEADS], P, n_pages)
        qa_rows = (decode_rows(qa.reshape(B, T, N_HEADS, HD)) * scale).astype(_BF16)
        new_a = pad_rows(jnp.concatenate([ka, va], axis=-1), PAGE_SIZE)
        gs = moba_decode_gate(pt, qa_rows, past['moba_rows'])
        y_a = from_decode_rows(paged_attention('moba', pt, qa_rows, past['moba_rows'], new_a, far_a, near_a, gs), T)
        y_b = nsa_attention(q_nsa.reshape(B, T, N_HEADS, HD), nsa_g.reshape(B, T, N_HEADS, 3), nsa_all[:, :, 0], nsa_all[:, :, 1], nsa_all[:, :, 2], nsa_all[:, :, 3], win_all[:, :, 0], win_all[:, :, 1], kw_pos0, q_pos, lp['cmp_w1'], lp['cmp_w2'], lp['cmp_pe'], tab[:, N_HEADS:2 * N_HEADS])
        qi_rows = pad_rows(jnp.transpose(qi.reshape(B, T, DSA_IDX_HEADS, DSA_IDX_DIM), (0, 2, 1, 3)).reshape(B * DSA_IDX_HEADS, T, DSA_IDX_DIM), ROWS_T)
        qi_rows = qi_rows.reshape(B, DSA_IDX_HEADS, ROWS_T, DSA_IDX_DIM).astype(_BF16)
        wi_rows = pad_rows(jnp.transpose(wi, (0, 2, 1)).reshape(B * DSA_IDX_HEADS, T), ROWS_T).reshape(B, DSA_IDX_HEADS, ROWS_T, 1)
        score = dsa_decode_scores(pt, qi_rows, wi_rows, past['idx_rows'], pad_rows(ki, PAGE_SIZE))
        keep = dsa_decode_select(score, min(DSA_TOPK, (P + T) // 4), P)
        keep = jnp.transpose(keep.reshape(B, ROWS_T, n_pages + PP, PAGE_SIZE), (0, 2, 1, 3))
        far_c, near_c = decode_bias(tab[:, 2 * N_HEADS:], P, n_pages)
        qc_rows = (decode_rows(qc.reshape(B, T, N_HEADS, HD)) * scale).astype(_BF16)
        new_c = pad_rows(jnp.concatenate([kc, vc], axis=-1), PAGE_SIZE)
        y_c = from_decode_rows(paged_attention('dsa', pt, qc_rows, past['dsa_rows'], new_c, far_c, near_c, keep), T)
    y_d, v_rows = gmlp_sgu(u, vg, lp['gm_ws'], lp['gm_b'], lp['gm_norm'])
    branches = [y.reshape(B * T, MIX_W) for y in (y_a, y_b, y_c, y_d)]
    win_keep = win_all[:, -min(NSA_WINDOW, P + T):]
    return branches, (moba_new, nsa_main_new, win_keep, dsa_new, ki, v_rows)


def gated_branch_mix(branches, gz, lp):
    mixed = 0.0
    for n in range(N_BRANCH):
        gate = jax.nn.sigmoid(gz[:, n * D_MODEL:(n + 1) * D_MODEL] + lp['b_gate'][n])
        mixed = mixed + gate * (branches[n] @ lp['w_branch'][n])
    return mixed @ lp['w_out']


def mem_attention_q(q, mkv, wo):
    B, T, _ = q.shape
    q = q.reshape(B, T, MEM_HEADS, MEM_HD)
    lg = jnp.einsum('bthd,bmhd->bthm', q, mkv[:, :, 0]).astype(jnp.float32) * MEM_HD ** -0.5
    p = jax.nn.softmax(lg, axis=-1)
    o = jnp.einsum('bthm,bmhd->bthd', p.astype(mkv.dtype), mkv[:, :, 1]).reshape(B, T, MEM_W)
    return o @ wo


def layer_pair(rows, shape_p, shape_s, past, mkv_p, mkv_s, lp, tab):
    n_p = shape_p[0] * shape_p[1]
    split = lambda a: (a[:n_p].reshape(shape_p[0], shape_p[1], -1), a[n_p:].reshape(shape_s[0], shape_s[1], -1))
    z_p, z_s = split(norm_matmul(rows, lp['norm_mix'], lp['w_in']))
    gz = norm_matmul(rows, lp['norm_mix'], lp['w_gate'])
    br_p, rp = token_mix(z_p, None, lp, tab)
    br_s, rs = token_mix(z_s, past, lp, tab)
    branches = [jnp.concatenate([bp, bs], axis=0) for bp, bs in zip(br_p, br_s)]
    rows = rows + gated_branch_mix(branches, gz, lp)
    q_p, q_s = split(norm_matmul(rows, lp['norm_mem'], lp['w_mem_q']))
    mem_p = mem_attention_q(q_p, mkv_p, lp['w_mem_o'])
    mem_s = mem_attention_q(q_s, mkv_s, lp['w_mem_o'])
    rows = rows + jnp.concatenate([mem_p.reshape(-1, D_MODEL), mem_s.reshape(-1, D_MODEL)], axis=0)
    rows = peer_ffn_tokens(rows, lp['norm_ffn'], lp['peer_wq'], lp['peer_keys'], lp['peer_u'], lp['peer_v'])
    return rows, rp, rs


def kernel(x_prompt, x_sample, cache_moba_kv, cache_nsa_kv, cache_dsa_kv, cache_dsa_idx, state_nsa_win, cache_mem_kv, page_table, mem_prompt, rel_bias_table, w_in, nsa_cmp_w1, nsa_cmp_w2, nsa_cmp_pe, gm_ws, gm_b, gm_norm, w_branch, w_gate, b_gate, w_out, w_mem_q, w_mem_kv, w_mem_o, peer_wq, peer_keys, peer_u, peer_v, norm_mix, norm_mem, norm_ffn, norm_final):
    n_p = x_prompt.shape[0] * x_prompt.shape[1]
    rows = jnp.concatenate([x_prompt.reshape(-1, D_MODEL), x_sample.reshape(-1, D_MODEL)], axis=0)
    rows_p = []
    rows_s = []
    mem_rows = []
    Bp = x_prompt.shape[0]
    n_pool = cache_moba_kv.shape[1]
    bias_tiles =[rel_bias_tiles(rel_bias_table[:, n * N_HEADS:(n + 1) * N_HEADS]) for n in range(3)]
    for l in range(DEPTH):
        lp = {
            'w_in': w_in[l], 'cmp_w1': nsa_cmp_w1[l], 'cmp_w2': nsa_cmp_w2[l], 'cmp_pe': nsa_cmp_pe[l],
            'gm_ws': gm_ws[l], 'gm_b': gm_b[l], 'gm_norm': gm_norm[l],
            'w_branch': w_branch[l], 'w_gate': w_gate[l], 'b_gate': b_gate[l], 'w_out': w_out[l],
            'w_mem_q': w_mem_q[l], 'w_mem_o': w_mem_o[l],
            'peer_wq': peer_wq[l], 'peer_keys': peer_keys[l], 'peer_u': peer_u[l], 'peer_v': peer_v[l],
            'norm_mix': norm_mix[l], 'norm_mem': norm_mem[l], 'norm_ffn': norm_ffn[l],
        }
        mkv_p = (mem_prompt @ w_mem_kv[l]).reshape(Bp, MEM_TOKENS, 2, MEM_HEADS, MEM_HD)
        past = {
            'pt': page_table + l * n_pool,
            'moba_rows': cache_moba_kv.reshape(DEPTH * n_pool, PAGE_SIZE, 2 * MIX_W),
            'dsa_rows': cache_dsa_kv.reshape(DEPTH * n_pool, PAGE_SIZE, 2 * MIX_W),
            'idx_rows': cache_dsa_idx.reshape(DEPTH * n_pool, PAGE_SIZE, DSA_IDX_DIM),
            'nsa': gather_pages(cache_nsa_kv.reshape((DEPTH * n_pool,) + cache_nsa_kv.shape[2:]), page_table + l * n_pool),
            'win': state_nsa_win[l],
        }
        lp['bias_tiles'] = bias_tiles
        rows, rp, rs = layer_pair(rows, x_prompt.shape, x_sample.shape, past, mkv_p, cache_mem_kv[l], lp, rel_bias_table)
        rows_p.append(rp)
        rows_s.append(rs)
        mem_rows.append(mkv_p)
    y_rows = rmsnorm(rows, norm_final)
    y_prompt = y_rows[:n_p].reshape(x_prompt.shape)
    y_sample = y_rows[n_p:].reshape(x_sample.shape)
    new_moba_kv_prompt = jnp.stack([r[0] for r in rows_p])
    new_moba_kv_sample = jnp.stack([r[0] for r in rows_s])
    new_nsa_kv_prompt = jnp.stack([r[1] for r in rows_p])
    new_nsa_kv_sample = jnp.stack([r[1] for r in rows_s])
    new_nsa_win_prompt = jnp.stack([r[2] for r in rows_p])
    new_nsa_win_sample = jnp.stack([r[2] for r in rows_s])
    new_dsa_kv_prompt = jnp.stack([r[3] for r in rows_p])
    new_dsa_kv_sample = jnp.stack([r[3] for r in rows_s])
    new_dsa_idx_prompt = jnp.stack([r[4] for r in rows_p])
    new_dsa_idx_sample = jnp.stack([r[4] for r in rows_s])
    new_mem_kv_prompt = jnp.stack(mem_rows)
    new_gmlp_v_sample = jnp.stack([r[5] for r in rows_s])
    return (y_prompt, y_sample, new_moba_kv_prompt, new_moba_kv_sample, new_nsa_kv_prompt, new_nsa_kv_sample, new_nsa_win_prompt, new_nsa_win_sample, new_dsa_kv_prompt, new_dsa_kv_sample, new_dsa_idx_prompt, new_dsa_idx_sample, new_mem_kv_prompt, new_gmlp_v_sample)
```

```python
import functools
import math
import jax
import jax.numpy as jnp
from jax import lax
import numpy as np
from jax.experimental import pallas as pl
from jax.experimental.pallas import tpu as pltpu

D_MODEL = 2048
DEPTH = 2
DEC_SEQ = 4
PAGE_SIZE = 128
VMEM_LIMIT = 48 * 1024 * 1024

HD = 64
N_HEADS = 8
MIX_W = N_HEADS * HD
N_BRANCH = 4
MOBA_BLOCK = 256
MOBA_TOPK = 3
NSA_KV_HEADS = 2
NSA_GROUP = N_HEADS // NSA_KV_HEADS
NSA_CMP_STRIDE = 16
NSA_CMP_LEN = 2 * NSA_CMP_STRIDE
NSA_SEL_BLOCK = 64
NSA_TOPN = 16
NSA_WINDOW = 512
NSA_QB = 64
WIN_QB = 128
DSA_IDX_HEADS = 4
DSA_IDX_DIM = 64
DSA_TOPK = 256
GM_CHUNK = 128
GM_GROUPS = 8
GM_W = 512
REL_BUCKETS = 32
REL_MAX_DIST = 128
MEM_TOKENS = 256
MEM_HEADS = 4
MEM_HD = 128
MEM_W = MEM_HEADS * MEM_HD
PEER_HEADS = 8
PEER_NKEYS = 128
PEER_QDIM = 128
PEER_TOPK = 16
NORM_EPS = 1e-6
IN_SPLITS = (MIX_W, MIX_W, MIX_W, MIX_W, 6 * NSA_KV_HEADS * HD, 3 * N_HEADS, MIX_W, MIX_W, MIX_W, DSA_IDX_HEADS * DSA_IDX_DIM, DSA_IDX_DIM, DSA_IDX_HEADS, GM_W, GM_W)
IN_COLS = sum(IN_SPLITS)


def rmsnorm(x, g):
    xf = x.astype(jnp.float32)
    y = xf * lax.rsqrt(jnp.mean(xf * xf, axis=-1, keepdims=True) + NORM_EPS)
    return (y * g.astype(jnp.float32)).astype(x.dtype)


def masked_softmax(logits, mask):
    l = jnp.where(mask, logits.astype(jnp.float32), -jnp.inf)
    m = jnp.max(l, axis=-1, keepdims=True)
    m = jnp.where(jnp.isfinite(m), m, 0.0)
    e = jnp.exp(l - m)
    return e / jnp.maximum(jnp.sum(e, axis=-1, keepdims=True), 1e-30)


def rel_bucket(dist):
    n = jnp.maximum(dist, 0)
    max_exact = REL_BUCKETS // 2
    nf = jnp.maximum(n, 1).astype(jnp.float32)
    large = max_exact + (jnp.log(nf / max_exact) / math.log(REL_MAX_DIST / max_exact) * (REL_BUCKETS - max_exact)).astype(jnp.int32)
    return jnp.where(n < max_exact, n, jnp.minimum(large, REL_BUCKETS - 1))


def rel_bias(dist, tab):
    onehot = (rel_bucket(dist)[..., None] == jnp.arange(REL_BUCKETS)).astype(jnp.float32)
    spec = '...okb,bh->...hk' if dist.shape[-2] == 1 else '...hkb,bh->...hk'
    return jnp.einsum(spec, onehot, tab.astype(jnp.float32), precision=lax.Precision.HIGHEST)


def split_last(z, sizes):
    return jnp.split(z, np.cumsum(sizes)[:-1].tolist(), axis=-1)


def map_query_blocks(fn, qb, q_pos, *xs):
    T = q_pos.shape[0]
    qb = qb if T % qb == 0 else T
    nb = T // qb
    xb = tuple(jnp.moveaxis(x.reshape(x.shape[0], nb, qb, *x.shape[2:]), 1, 0) for x in xs)
    out = lax.map(lambda a: fn(*a), (q_pos.reshape(nb, qb),) + xb)
    return jnp.moveaxis(out, 0, 1).reshape(out.shape[1], T, *out.shape[3:])


def gather_pages(pool, page_table):
    g = pool[page_table]
    return g.reshape(g.shape[0], g.shape[1] * g.shape[2], *g.shape[3:])


def nsa_attention(q, gates, kc, vc, ks, vs, kw, vw, kw_pos0, q_pos, w1, w2, pe, tab):
    B, L, G, _ = kc.shape
    T = q.shape[1]
    H = N_HEADS
    scale = HD ** -0.5
    nch = L // NSA_CMP_STRIDE
    ncmp = nch - 1

    def compress(x, i):
        c = x[:, :nch * NSA_CMP_STRIDE].reshape(B, nch, NSA_CMP_STRIDE, G, HD)
        blk = jnp.concatenate([c[:, :-1], c[:, 1:]], axis=2) + pe[i][None, None, :, None, :]
        blk = jnp.transpose(blk, (0, 1, 3, 2, 4)).reshape(B, ncmp, G, NSA_CMP_LEN * HD)
        return jax.nn.gelu(blk @ w1[i]) @ w2[i]

    kcmp = compress(kc, 0)
    vcmp = compress(vc, 1)
    cmp_end = jnp.arange(ncmp) * NSA_CMP_STRIDE + NSA_CMP_LEN - 1
    qg = q.reshape(B, T, G, NSA_GROUP, HD)
    dist_c = q_pos[:, None] - cmp_end[None, :]
    lg_c = jnp.einsum('btgrd,bngd->btgrn', qg, kcmp).astype(jnp.float32).reshape(B, T, H, ncmp) * scale
    lg_c = lg_c + rel_bias(dist_c[None, :, None, :], tab)
    p_c = masked_softmax(lg_c, (dist_c >= 0)[None, :, None, :])
    p_cg = p_c.reshape(B, T, G, NSA_GROUP, ncmp)
    o_cmp = jnp.einsum('btgrn,bngd->btgrd', p_cg.astype(vcmp.dtype), vcmp).reshape(B, T, H, HD)
    nsel = -(-L // NSA_SEL_BLOCK)
    ii = np.arange(ncmp)[:, None]
    jj_np = np.arange(nsel)[None, :]
    overlap = ((ii * NSA_CMP_STRIDE < (jj_np + 1) * NSA_SEL_BLOCK) & (ii * NSA_CMP_STRIDE + NSA_CMP_LEN > jj_np * NSA_SEL_BLOCK)).astype(np.float32)
    imp = jnp.einsum('btgrn,nj->btgj', p_cg, jnp.asarray(overlap))
    cur = q_pos // NSA_SEL_BLOCK
    jj = jnp.arange(nsel)[None, :]
    causal_blk = jj <= cur[:, None]
    forced = (jj == 0) | (jj == cur[:, None]) | (jj == cur[:, None] - 1)
    imp = jnp.where(forced[None, :, None, :], jnp.inf, jnp.where(causal_blk[None, :, None, :], imp, -jnp.inf))
    ntop = min(NSA_TOPN, nsel)
    _, sel = lax.top_k(imp, ntop)
    pad = ((0, 0), (0, nsel * NSA_SEL_BLOCK - L), (0, 0), (0, 0))
    ksb = jnp.transpose(jnp.pad(ks, pad).reshape(B, nsel, NSA_SEL_BLOCK, G, HD), (0, 3, 1, 2, 4))
    vsb = jnp.transpose(jnp.pad(vs, pad).reshape(B, nsel, NSA_SEL_BLOCK, G, HD), (0, 3, 1, 2, 4))
    bi = jnp.arange(B)[:, None, None, None]
    gi = jnp.arange(G)[None, None, :, None]
    off = jnp.arange(NSA_SEL_BLOCK)

    def sel_fn(qp, qq, ss):
        qb = qp.shape[0]
        kg = ksb[bi, gi, ss].reshape(B, qb, G, ntop * NSA_SEL_BLOCK, HD)
        vg = vsb[bi, gi, ss].reshape(B, qb, G, ntop * NSA_SEL_BLOCK, HD)
        pos = (ss[..., None] * NSA_SEL_BLOCK + off).reshape(B, qb, G, -1)
        dist = jnp.repeat(qp[None, :, None, None] - pos, NSA_GROUP, axis=2)
        lg = jnp.einsum('bqgrd,bqgkd->bqgrk', qq.reshape(B, qb, G, NSA_GROUP, HD), kg).astype(jnp.float32).reshape(B, qb, H, -1) * scale
        p = masked_softmax(lg + rel_bias(dist, tab), dist >= 0)
        o = jnp.einsum('bqgrk,bqgkd->bqgrd', p.reshape(B, qb, G, NSA_GROUP, -1).astype(vg.dtype), vg)
        return o.reshape(B, qb, H, HD)

    o_sel = map_query_blocks(sel_fn, NSA_QB, q_pos, q, sel)
    W = NSA_WINDOW
    wpad = ((0, 0), (W, 0), (0, 0), (0, 0))
    kwp = jnp.pad(kw, wpad)
    vwp = jnp.pad(vw, wpad)

    def win_fn(qp, qq):
        qb = qp.shape[0]
        span = W + qb - 1
        start = qp[0] - kw_pos0 + 1
        kk = lax.dynamic_slice_in_dim(kwp, start, span, axis=1)
        vv = lax.dynamic_slice_in_dim(vwp, start, span, axis=1)
        pos = qp[0] - W + 1 + jnp.arange(span)
        dist = qp[:, None] - pos[None, :]
        ok = (dist >= 0) & (dist < W) & (pos[None, :] >= 0)
        lg = jnp.einsum('bqgrd,bkgd->bqgrk', qq.reshape(B, qb, G, NSA_GROUP, HD), kk).astype(jnp.float32).reshape(B, qb, H, span) * scale
        p = masked_softmax(lg + rel_bias(dist[None, :, None, :], tab), ok[None, :, None, :])
        o = jnp.einsum('bqgrk,bkgd->bqgrd', p.reshape(B, qb, G, NSA_GROUP, span).astype(vv.dtype), vv)
        return o.reshape(B, qb, H, HD)

    o_win = map_query_blocks(win_fn, WIN_QB, q_pos, q)
    g = jax.nn.sigmoid(gates.astype(jnp.float32)).astype(q.dtype)
    return g[..., 0:1] * o_cmp + g[..., 1:2] * o_sel + g[..., 2:3] * o_win


def gmlp_sgu(u, v, ws, b, gnorm):
    B, T, _ = u.shape
    vn = rmsnorm(v, gnorm)
    nc = -(-T // GM_CHUNK)
    vc = jnp.pad(vn, ((0, 0), (0, nc * GM_CHUNK - T), (0, 0))).reshape(B, nc, GM_CHUNK, GM_GROUPS, GM_W // GM_GROUPS)
    wsm = ws * jnp.tril(jnp.ones((GM_CHUNK, GM_CHUNK), ws.dtype))
    mixed = jnp.einsum('gij,bcjgd->bcigd', wsm, vc) + b.T[None, None, :, :, None]
    mixed = mixed.reshape(B, nc * GM_CHUNK, GM_W)[:, :T]
    return u * mixed, vn


def _norm_matmul_kernel(x_ref, g_ref, w_ref, o_ref, xn_ref):
    @pl.when(pl.program_id(1) == 0)
    def _():
        x = x_ref[...]
        y = x * lax.rsqrt(jnp.mean(x * x, axis=-1, keepdims=True) + NORM_EPS)
        xn_ref[...] = (y * g_ref[...]).astype(jnp.bfloat16)

    hw = w_ref.shape[1] // 2
    for s in range(2):
        cols = slice(s * hw, (s + 1) * hw)
        o_ref[:, cols] = jnp.dot(xn_ref[...], w_ref[:, cols].astype(jnp.bfloat16), preferred_element_type=jnp.float32)


def norm_matmul(x, g, w, tm=512, tn=512):
    N, D = x.shape
    C = w.shape[1]
    tm = min(tm, N)
    tn = min(tn, C)
    return pl.pallas_call(
        _norm_matmul_kernel,
        grid=(pl.cdiv(N, tm), pl.cdiv(C, tn)),
        in_specs=[
            pl.BlockSpec((tm, D), lambda i, j: (i, 0)),
            pl.BlockSpec((1, D), lambda i, j: (0, 0)),
            pl.BlockSpec((D, tn), lambda i, j: (0, j)),
        ],
        out_specs=pl.BlockSpec((tm, tn), lambda i, j: (i, j)),
        out_shape=jax.ShapeDtypeStruct((N, C), jnp.float32),
        scratch_shapes=[pltpu.VMEM((tm, D), jnp.bfloat16)],
        compiler_params=pltpu.CompilerParams(
            dimension_semantics=("arbitrary", "arbitrary"), vmem_limit_bytes=VMEM_LIMIT),
        name="norm_matmul",
    )(x, g.reshape(1, D), w.astype(jnp.bfloat16))


TQ = 256
TK = 256
NEG_BIG = -1e30
_BF16 = jnp.bfloat16
_F32 = jnp.float32


HEADS_PER_STEP = 2


def _attn_kernel(mode, shared_kv, qT_ref, k_ref, vT_ref, bias_ref, *rest):
    if mode == 'win':
        (o_ref,) = rest
        m_ref = None
    else:
        m_ref, o_ref = rest
    qi = pl.program_id(2)
    heads = range(HEADS_PER_STEP)
    q = [qT_ref[0, hh].astype(_BF16) for hh in heads]
    lane = lax.broadcasted_iota(jnp.int32, (TK, TQ), 1)
    sub = lax.broadcasted_iota(jnp.int32, (TK, TQ), 0)
    rel0 = lane - sub

    def body(kj, carry):
        dist = rel0 + (qi - kj) * TQ
        ok_all = dist >= 0
        if mode == 'sel':
            rows = m_ref[0, 0, kj]
            blk = sub // NSA_SEL_BLOCK
            r = jnp.where(blk == 0, rows[0:1], jnp.where(blk == 1, rows[1:2], jnp.where(blk == 2, rows[2:3], rows[3:4])))
            ok_all = ok_all & (r > 0.5)
        elif mode == 'dsa':
            ok_all = ok_all & (m_ref[0, pl.ds(pl.multiple_of(kj * TK, TK), TK), :].astype(_F32) > 0.5)
        elif mode == 'win':
            ok_all = ok_all & (dist < NSA_WINDOW)
        out = []
        for hh in heads:
            m, l, acc = carry[hh]
            hk = 0 if shared_kv else hh
            k_t = k_ref[0, hk, pl.ds(pl.multiple_of(kj * TK, TK), TK), :].astype(_BF16)
            s = jnp.dot(k_t, q[hh], preferred_element_type=_F32)
            s = s + bias_ref[hh, jnp.minimum(qi - kj, 2)]
            ok = ok_all & (m_ref[0, hh, kj] > 0.5) if mode == 'moba' else ok_all
            s = jnp.where(ok, s, NEG_BIG)
            m_new = jnp.maximum(m, jnp.max(s, axis=0, keepdims=True))
            p = jnp.where(ok, jnp.exp(s - m_new), 0.0)
            alpha = jnp.exp(m - m_new)
            l = alpha * l + jnp.sum(p, axis=0, keepdims=True)
            v_t = vT_ref[0, hk, kj].astype(_BF16)
            acc = alpha * acc + jnp.dot(v_t, p.astype(_BF16), preferred_element_type=_F32)
            out.append((m_new, l, acc))
        return tuple(out)

    lo = jnp.maximum(qi - (NSA_WINDOW // TK), 0) if mode == 'win' else 0
    init = tuple((jnp.full((1, TQ), NEG_BIG, _F32), jnp.zeros((1, TQ), _F32), jnp.zeros((HD, TQ), _F32)) for _ in heads)
    res = lax.fori_loop(lo, qi + 1, body, init)
    for hh in heads:
        m, l, acc = res[hh]
        o_ref[0, hh] = acc / jnp.maximum(l, 1e-30)


def attention_T(mode, qT, k, vT, bias_tiles, mask=None):
    B, H, _, T = qT.shape
    Hkv, L = k.shape[1], k.shape[2]
    grp = H // Hkv
    hp = HEADS_PER_STEP
    assert H % hp == 0 and (grp == 1 or grp % hp == 0)
    shared_kv = grp > 1
    kvh = 1 if shared_kv else hp
    kv_idx = (lambda g: g * hp // grp) if shared_kv else (lambda g: g)
    in_specs = [
        pl.BlockSpec((1, hp, HD, TQ), lambda b, g, i: (b, g, 0, i)),
        pl.BlockSpec((1, kvh, L, HD), lambda b, g, i: (b, kv_idx(g), 0, 0)),
        pl.BlockSpec((1, kvh, L // TK, HD, TK), lambda b, g, i: (b, kv_idx(g), 0, 0, 0)),
        pl.BlockSpec((hp, 3, TK, TQ), lambda b, g, i: (g, 0, 0, 0)),
    ]
    args = [qT, k, vT, bias_tiles]
    if mode == 'moba':
        in_specs.append(pl.BlockSpec((1, hp, L // TK, 1, TQ), lambda b, g, i: (b, g, 0, 0, i)))
        args.append(mask.reshape(B, H, L // TK, 1, T))
    elif mode == 'sel':
        per = TK // NSA_SEL_BLOCK
        in_specs.append(pl.BlockSpec((1, 1, L // TK, per, TQ), lambda b, g, i: (b, kv_idx(g), 0, 0, i)))
        args.append(mask.reshape(B, Hkv, L // TK, per, T))
    elif mode == 'dsa':
        in_specs.append(pl.BlockSpec((1, L, TQ), lambda b, g, i: (b, 0, i)))
        args.append(mask)
    return pl.pallas_call(
        functools.partial(_attn_kernel, mode, shared_kv),
        grid=(B, H // hp, T // TQ),
        in_specs=in_specs,
        out_specs=pl.BlockSpec((1, hp, HD, TQ), lambda b, g, i: (b, g, 0, i)),
        out_shape=jax.ShapeDtypeStruct((B, H, HD, T), _F32),
        compiler_params=pltpu.CompilerParams(
            dimension_semantics=("arbitrary", "arbitrary", "arbitrary"), vmem_limit_bytes=VMEM_LIMIT),
        name="attn_" + mode,
    )(*args)


def _moba_gate_kernel(qT_ref, k_ref, sel_ref):
    nblk, T = sel_ref.shape[2], sel_ref.shape[3]
    q = qT_ref[0, 0].astype(_BF16)
    k = k_ref[0, 0]
    km = jnp.mean(k.reshape(nblk, MOBA_BLOCK, HD), axis=1)
    gs = jnp.dot(km.astype(_BF16), q, preferred_element_type=_F32)
    j = lax.broadcasted_iota(jnp.int32, (nblk, T), 0)
    own = lax.broadcasted_iota(jnp.int32, (nblk, T), 1) // MOBA_BLOCK
    fully_past = j < own
    gs = jnp.where(fully_past, gs, -jnp.inf)
    rank = jnp.zeros((nblk, T), jnp.int32)
    for jp in range(nblk):
        row = gs[jp:jp + 1, :]
        ahead = (row > gs) | ((row == gs) & (jp < j))
        rank = rank + ahead.astype(jnp.int32)
    sel = (fully_past & (rank < MOBA_TOPK)) | (j == own)
    sel_ref[0, 0] = sel.astype(_F32)


def moba_gate(qT, k):
    B, H, _, T = qT.shape
    L = k.shape[2]
    nblk = L // MOBA_BLOCK
    return pl.pallas_call(
        _moba_gate_kernel,
        grid=(B, H),
        in_specs=[pl.BlockSpec((1, 1, HD, T), lambda b, h: (b, h, 0, 0)),
                  pl.BlockSpec((1, 1, L, HD), lambda b, h: (b, h, 0, 0))],
        out_specs=pl.BlockSpec((1, 1, nblk, T), lambda b, h: (b, h, 0, 0)),
        out_shape=jax.ShapeDtypeStruct((B, H, nblk, T), _F32),
        compiler_params=pltpu.CompilerParams(dimension_semantics=("arbitrary", "arbitrary"), vmem_limit_bytes=VMEM_LIMIT),
        name="moba_gate",
    )(qT, k)


def _count_rows(pred):
    return jnp.sum(pred.astype(_F32), axis=0, keepdims=True)


def _dsa_mask_kernel(topk, ki_ref, qiT_ref, wiT_ref, mask_ref, key_ref):
    L = ki_ref.shape[1]
    qi = pl.program_id(1)
    ki = ki_ref[0].astype(_BF16)
    score = jnp.zeros((L, TQ), _F32)
    for hh in range(DSA_IDX_HEADS):
        s = jnp.dot(ki, qiT_ref[0, hh].astype(_BF16), preferred_element_type=_F32) * DSA_IDX_DIM ** -0.5
        score = score + wiT_ref[0, hh:hh + 1, :] * jnp.maximum(s, 0.0)
    idx = lax.broadcasted_iota(jnp.int32, (L, TQ), 0)
    q_pos = qi * TQ + lax.broadcasted_iota(jnp.int32, (L, TQ), 1)
    adm = idx <= q_pos
    score = jnp.where(adm, jnp.where(score == 0.0, 0.0, score), -jnp.inf)
    bits = pltpu.bitcast(score, jnp.int32)
    key_ref[...] = jnp.where(bits < 0, bits ^ jnp.int32(0x7FFFFFFF), bits)
    kf = jnp.float32(topk)
    int_min = jnp.int32(-2 ** 31)
    sub = lax.broadcasted_iota(jnp.int32, (TK, TQ), 0)

    def count(pred):
        def tile(kt, acc):
            rows = pl.ds(pl.multiple_of(kt * TK, TK), TK)
            return acc + _count_rows(pred(key_ref[rows, :], kt * TK + sub))
        return lax.fori_loop(0, qi + 1, tile, jnp.zeros((1, TQ), _F32))

    lo = jnp.where(count(lambda k, i: k >= 0) >= kf, jnp.int32(0), int_min)

    def vstep(i, lo):
        cand = lo + jnp.left_shift(jnp.int32(1), 30 - i)
        return jnp.where(count(lambda k, i: k >= cand) >= kf, cand, lo)

    thr = lax.fori_loop(0, 31, vstep, lo)
    need = kf - count(lambda k, i: k > thr)
    nbits = max(1, (L - 1).bit_length())

    def istep(i, lo):
        cand = lo + jnp.left_shift(jnp.int32(1), nbits - 1 - i)
        c = count(lambda k, ix: (k == thr) & (ix < cand))
        return jnp.where(c < need, cand, lo)

    last_tie = lax.fori_loop(0, nbits, istep, jnp.zeros((1, TQ), jnp.int32))
    key = key_ref[...]
    keep = ((key > thr) | ((key == thr) & (idx <= last_tie))) & adm
    mask_ref[0] = keep.astype(mask_ref.dtype)


def dsa_mask(ki, qiT, wiT, topk):
    B, L, _ = ki.shape
    T = qiT.shape[3]
    return pl.pallas_call(
        functools.partial(_dsa_mask_kernel, topk),
        grid=(B, T // TQ),
        in_specs=[pl.BlockSpec((1, L, DSA_IDX_DIM), lambda b, i: (b, 0, 0)),
                  pl.BlockSpec((1, DSA_IDX_HEADS, DSA_IDX_DIM, TQ), lambda b, i: (b, 0, 0, i)),
                  pl.BlockSpec((1, DSA_IDX_HEADS, TQ), lambda b, i: (b, 0, i))],
        out_specs=pl.BlockSpec((1, L, TQ), lambda b, i: (b, 0, i)),
        out_shape=jax.ShapeDtypeStruct((B, L, T), _BF16),
        scratch_shapes=[pltpu.VMEM((L, TQ), jnp.int32)],
        compiler_params=pltpu.CompilerParams(dimension_semantics=("arbitrary", "arbitrary"), vmem_limit_bytes=VMEM_LIMIT),
        name="dsa_mask",
    )(ki, qiT, wiT)


PEER_TN = 256
PEER_EB = 1024
PEER_IB = PEER_EB // PEER_NKEYS
RANK_OUT = 4096.0


def _extract_top(work, n):
    rows = work.shape[0]
    iota = lax.broadcasted_iota(jnp.int32, work.shape, 0)
    order = jnp.full(work.shape, RANK_OUT, _F32)
    vals = []
    idx = None
    for a in range(n):
        m = jnp.max(work, axis=0, keepdims=True)
        idx = jnp.min(jnp.where(work == m, iota, rows), axis=0, keepdims=True)
        hit = iota == idx
        vals.append(m)
        order = jnp.where(hit, jnp.float32(a), order)
        work = jnp.where(hit, -jnp.inf, work)
    return vals, order, idx


def _peer_select_kernel(x_ref, g_ref, wqT_ref, keys_ref, hx_ref, c1_ref, e1n_ref, r2_ref, e2_ref):
    x = x_ref[...]
    y = x * lax.rsqrt(jnp.mean(x * x, axis=1, keepdims=True) + NORM_EPS) * g_ref[...]
    hx = y.astype(_BF16)
    hx_ref[...] = hx
    qT = lax.dot_general(wqT_ref[...], hx, (((1,), (1,)), ((), ())), preferred_element_type=_F32).astype(_BF16)
    half = PEER_QDIM // 2
    for h in range(PEER_HEADS):
        sc = []
        for c in range(2):
            r0 = (h * 2 + c) * half
            sc.append(jnp.dot(keys_ref[h * 2 + c], qT[r0:r0 + half, :], preferred_element_type=_F32))
        v1, o1, _ = _extract_top(sc[0], PEER_TOPK)
        v2, o2, _ = _extract_top(sc[1], PEER_TOPK)
        v1s = jnp.concatenate(v1, axis=0)
        v2s = jnp.concatenate(v2, axis=0)
        ex1s = jnp.exp(v1s - v1[0])
        ex2s = jnp.exp(v2s - v2[0])
        lo_a, hi_a = PEER_TOPK // 2, PEER_TOPK
        cand = jnp.concatenate([v1[0] + v2s] + [v1[a] + v2s[:lo_a] for a in range(1, lo_a)] + [v1s[lo_a:hi_a] + v2[0]], axis=0)
        ecand = jnp.concatenate([ex1s[0:1] * ex2s] + [ex1s[a:a + 1] * ex2s[:lo_a] for a in range(1, lo_a)]
                                + [ex1s[lo_a:hi_a] * ex2s[0:1]], axis=0)
        _, corder, _ = _extract_top(cand, PEER_TOPK)
        chosen = (corder < RANK_OUT).astype(_F32)
        z = jnp.sum(chosen * ecand, axis=0, keepdims=True)
        tail0 = PEER_TOPK + (lo_a - 1) * lo_a
        c1 = jnp.zeros(sc[0].shape, _F32)
        for a in range(PEER_TOPK):
            if a == 0:
                grp = chosen[0:PEER_TOPK]
            elif a < lo_a:
                grp = chosen[PEER_TOPK + (a - 1) * lo_a:PEER_TOPK + a * lo_a]
            else:
                grp = chosen[tail0 + a - lo_a:tail0 + a - lo_a + 1]
            cnt = jnp.sum(grp, axis=0, keepdims=True)
            c1 = jnp.where(o1 == jnp.float32(a), cnt, c1)
        c1_ref[h] = c1
        e1n_ref[h] = jnp.exp(sc[0] - v1[0]) / z
        r2_ref[h] = o2
        e2_ref[h] = jnp.exp(sc[1] - v2[0])


def peer_select(x, g, wqT, keys2):
    Np, D = x.shape
    tn = PEER_TN
    row = lambda: pl.BlockSpec((PEER_HEADS, PEER_NKEYS, tn), lambda t: (0, 0, t))
    rshape = jax.ShapeDtypeStruct((PEER_HEADS, PEER_NKEYS, Np), _F32)
    return pl.pallas_call(
        _peer_select_kernel,
        grid=(Np // tn,),
        in_specs=[pl.BlockSpec((tn, D), lambda t: (t, 0)),
                  pl.BlockSpec((1, D), lambda t: (0, 0)),
                  pl.BlockSpec((PEER_HEADS * PEER_QDIM, D), lambda t: (0, 0)),
                  pl.BlockSpec((2 * PEER_HEADS, PEER_NKEYS, PEER_QDIM // 2), lambda t: (0, 0, 0))],
        out_specs=[pl.BlockSpec((tn, D), lambda t: (t, 0)), row(), row(), row(), row()],
        out_shape=[jax.ShapeDtypeStruct((Np, D), _BF16), rshape, rshape, rshape, rshape],
        compiler_params=pltpu.CompilerParams(dimension_semantics=("arbitrary",), vmem_limit_bytes=VMEM_LIMIT),
        name="peer_select",
    )(x, g, wqT, keys2)


def _gelu_tanh(x):
    return 0.5 * x * (1.0 + jnp.tanh(math.sqrt(2.0 / math.pi) * (x + 0.044715 * (x * x * x))))


def _peer_dense_kernel(x_ref, hx_ref, u_ref, vT_ref, c1_ref, e1n_ref, r2_ref, e2_ref, o_ref, p_ref, acc_ref):
    e = pl.program_id(1)
    tn, D = x_ref.shape

    @pl.when(e == 0)
    def _():
        acc_ref[...] = jnp.zeros_like(acc_ref)

    nt = (((1,), (1,)), ((), ()))
    half = PEER_EB // 2
    for s in range(2):
        act = _gelu_tanh(lax.dot_general(u_ref[s * half:(s + 1) * half, :], hx_ref[...], nt, preferred_element_type=_F32))
        for ib in range(s * half // PEER_NKEYS, (s + 1) * half // PEER_NKEYS):
            w = jnp.zeros((PEER_NKEYS, tn), _F32)
            for h in range(PEER_HEADS):
                w = w + jnp.where(r2_ref[h] < c1_ref[h, ib:ib + 1, :], e1n_ref[h, ib:ib + 1, :] * e2_ref[h], 0.0)
            r0 = ib * PEER_NKEYS - s * half
            p_ref[ib * PEER_NKEYS:(ib + 1) * PEER_NKEYS, :] = (w * act[r0:r0 + PEER_NKEYS, :]).astype(_BF16)
    for s in range(2):
        rows = slice(s * (D // 2), (s + 1) * (D // 2))
        acc_ref[rows, :] += jnp.dot(vT_ref[rows, :], p_ref[...], preferred_element_type=_F32)

    @pl.when(e == pl.num_programs(1) - 1)
    def _():
        o_ref[...] = x_ref[...] + acc_ref[...].T


def peer_dense(x, hx, u, vT, c1, e1n, r2, e2):
    Np, D = x.shape
    E = u.shape[0]
    tn = PEER_TN
    rowi = lambda: pl.BlockSpec((PEER_HEADS, PEER_IB, tn), lambda t, e: (0, e, t))
    rowj = lambda: pl.BlockSpec((PEER_HEADS, PEER_NKEYS, tn), lambda t, e: (0, 0, t))
    return pl.pallas_call(
        _peer_dense_kernel,
        grid=(Np // tn, E // PEER_EB),
        in_specs=[pl.BlockSpec((tn, D), lambda t, e: (t, 0)),
                  pl.BlockSpec((tn, D), lambda t, e: (t, 0)),
                  pl.BlockSpec((PEER_EB, D), lambda t, e: (e, 0)),
                  pl.BlockSpec((D, PEER_EB), lambda t, e: (0, e)),
                  rowi(), rowi(), rowj(), rowj()],
        out_specs=pl.BlockSpec((tn, D), lambda t, e: (t, 0)),
        out_shape=jax.ShapeDtypeStruct((Np, D), _F32),
        scratch_shapes=[pltpu.VMEM((PEER_EB, tn), _BF16), pltpu.VMEM((D, tn), _F32)],
        compiler_params=pltpu.CompilerParams(dimension_semantics=("arbitrary", "arbitrary"), vmem_limit_bytes=VMEM_LIMIT),
        name="peer_dense",
    )(x, hx, u, vT, c1, e1n, r2, e2)


def peer_ffn_tokens(x_tokens, g, wq, keys, U, V):
    N, D = x_tokens.shape
    Np = -(-N // PEER_TN) * PEER_TN
    x = jnp.pad(x_tokens, ((0, Np - N), (0, 0)))
    keys2 = keys.reshape(2 * PEER_HEADS, PEER_NKEYS, PEER_QDIM // 2).astype(_BF16)
    hx, c1, e1n, r2, e2 = peer_select(x, g.reshape(1, D), wq.T.astype(_BF16), keys2)
    return peer_dense(x, hx, U.astype(_BF16), V.T.astype(_BF16), c1, e1n, r2, e2)[:N]


def bias_lookup(dist, tab):
    onehot = (rel_bucket(dist)[..., None] == jnp.arange(REL_BUCKETS)).astype(_F32)
    return jnp.einsum('...b,bh->h...', onehot, tab.astype(_F32), precision=lax.Precision.HIGHEST)


def rel_bias_tiles(tab):
    kk = jnp.arange(TK)[:, None]
    qq = jnp.arange(TQ)[None, :]
    dist = jnp.stack([d * TQ + qq - kk for d in range(3)])
    return bias_lookup(dist, tab)


ROWS_T = 8
DEC_ROWS = N_HEADS * ROWS_T
NEAR_TILES = 3


PP = 8


def _page_specs(block, n_pages):
    def spec(i):
        return pl.BlockSpec(block, lambda b, p, pt: (pt[b, jnp.minimum(PP * p + i, n_pages - 1)], 0, 0))
    return [spec(i) for i in range(PP)]


def _moba_decode_gate_kernel(pt_ref, q_ref, *rest):
    caches, gs_ref = rest[:PP], rest[PP]
    p = pl.program_id(1)
    per = MOBA_BLOCK // PAGE_SIZE

    @pl.when(p == 0)
    def _():
        gs_ref[...] = jnp.zeros_like(gs_ref)

    lane = lax.broadcasted_iota(jnp.int32, gs_ref.shape[1:], 1)
    for g in range(PP // per):
        ksum = sum(jnp.sum(caches[g * per + i][0], axis=0, keepdims=True) for i in range(per))
        km = (ksum * (1.0 / MOBA_BLOCK)).astype(_BF16).astype(_F32)
        col = jnp.sum(q_ref[0].astype(_F32) * km, axis=1, keepdims=True)
        gs_ref[0] = jnp.where(lane == p * (PP // per) + g, col, gs_ref[0])


def moba_decode_gate(pt, q_rows, cache_rows):
    B, n_pages = pt.shape
    W = N_HEADS * HD
    assert n_pages % PP == 0 and PP % (MOBA_BLOCK // PAGE_SIZE) == 0
    return pl.pallas_call(
        _moba_decode_gate_kernel,
        grid_spec=pltpu.PrefetchScalarGridSpec(
            num_scalar_prefetch=1,
            grid=(B, n_pages // PP),
            in_specs=[pl.BlockSpec((1, DEC_ROWS, W), lambda b, p, pt: (b, 0, 0))] + _page_specs((1, PAGE_SIZE, W), n_pages),
            out_specs=pl.BlockSpec((1, DEC_ROWS, 128), lambda b, p, pt: (b, 0, 0))),
        out_shape=jax.ShapeDtypeStruct((B, DEC_ROWS, 128), _F32),
        compiler_params=pltpu.CompilerParams(dimension_semantics=("arbitrary", "arbitrary"), vmem_limit_bytes=VMEM_LIMIT),
        name="moba_decode_gate",
    )(pt, q_rows, *([cache_rows] * PP))


def _paged_attn_kernel(mode, nblk, pt_ref, q_ref, *rest):
    caches = rest[:PP]
    new_ref, far_ref, near_ref, m_in_ref, o_ref, m_sc, l_sc, acc_sc, sel_sc = rest[PP:]
    p = pl.program_id(1)
    last = pl.num_programs(1) - 1
    n_pages = last * PP
    W = N_HEADS * HD
    lane = lax.broadcasted_iota(jnp.int32, (DEC_ROWS, PAGE_SIZE), 1)
    row_t = lax.broadcasted_iota(jnp.int32, (DEC_ROWS, PAGE_SIZE), 0) % ROWS_T

    @pl.when(p == 0)
    def _():
        m_sc[...] = jnp.full_like(m_sc, NEG_BIG)
        l_sc[...] = jnp.zeros_like(l_sc)
        acc_sc[...] = jnp.zeros_like(acc_sc)
        if mode == 'moba':
            valid = lane < nblk
            gs = jnp.where(valid, m_in_ref[0], -jnp.inf)
            rank = jnp.zeros(gs.shape, jnp.int32)
            for jp in range(nblk):
                col = gs[:, jp:jp + 1]
                rank = rank + ((col > gs) | ((col == gs) & (jp < lane))).astype(jnp.int32)
            sel_sc[...] = (valid & (rank < MOBA_TOPK)).astype(_F32)

    def tile_scores(blk, g, i, is_new):
        k = blk[:, :W].astype(_BF16)
        s = lax.dot_general(q_ref[0], k, (((1,), (1,)), ((), ())), preferred_element_type=_F32)
        near = near_ref[jnp.clip(g - (n_pages + 1 - NEAR_TILES), 0, NEAR_TILES - 1)]
        s = s + jnp.where(g >= n_pages + 1 - NEAR_TILES, near, far_ref[...])
        if mode == 'moba':
            if is_new:
                ok = lane <= row_t
            else:
                in_blk = jnp.sum(jnp.where(lane == g // (MOBA_BLOCK // PAGE_SIZE), sel_sc[...], 0.0), axis=1, keepdims=True)
                ok = jnp.broadcast_to(in_blk, s.shape) > 0.5
        else:
            ok = jnp.concatenate([m_in_ref[0, i]] * N_HEADS, axis=0) > 0.5
        return jnp.where(ok, s, NEG_BIG), ok.astype(_F32)

    def update(s, ok, v):
        m_new = jnp.maximum(m_sc[...], jnp.max(s, axis=1, keepdims=True))
        pr = jnp.where(ok > 0.5, jnp.exp(s - m_new), 0.0)
        alpha = jnp.exp(m_sc[...] - m_new)
        l_sc[...] = alpha * l_sc[...] + jnp.sum(pr, axis=1, keepdims=True)
        acc_sc[...] = alpha * acc_sc[...] + jnp.dot(pr.astype(_BF16), v, preferred_element_type=_F32)
        m_sc[...] = m_new

    @pl.when(p < last)
    def _():
        parts = [tile_scores(caches[i][0], PP * p + i, i, False) for i in range(PP)]
        v_all = jnp.concatenate([caches[i][0][:, W:].astype(_BF16) for i in range(PP)], axis=0)
        update(jnp.concatenate([s for s, _ in parts], axis=1), jnp.concatenate([ok for _, ok in parts], axis=1), v_all)

    @pl.when(p == last)
    def _():
        s, ok = tile_scores(new_ref[0], n_pages, 0, True)
        update(s, ok, new_ref[0][:, W:].astype(_BF16))
        o_ref[0] = acc_sc[...] / jnp.maximum(l_sc[...], 1e-30)


def paged_attention(mode, pt, q_rows, cache_rows, new_rows, far, near, m_in):
    B, n_pages = pt.shape
    W = N_HEADS * HD
    assert n_pages % PP == 0
    if mode == 'moba':
        m_spec = pl.BlockSpec((1, DEC_ROWS, 128), lambda b, p, pt: (b, 0, 0))
    else:
        m_spec = pl.BlockSpec((1, PP, ROWS_T, PAGE_SIZE), lambda b, p, pt: (b, p, 0, 0))
    return pl.pallas_call(
        functools.partial(_paged_attn_kernel, mode, n_pages // (MOBA_BLOCK // PAGE_SIZE)),
        grid_spec=pltpu.PrefetchScalarGridSpec(
            num_scalar_prefetch=1,
            grid=(B, n_pages // PP + 1),
            in_specs=[pl.BlockSpec((1, DEC_ROWS, W), lambda b, p, pt: (b, 0, 0))] + _page_specs((1, PAGE_SIZE, 2 * W), n_pages) + [
                      pl.BlockSpec((1, PAGE_SIZE, 2 * W), lambda b, p, pt: (b, 0, 0)),
                      pl.BlockSpec((DEC_ROWS, 1), lambda b, p, pt: (0, 0)),
                      pl.BlockSpec((NEAR_TILES, DEC_ROWS, PAGE_SIZE), lambda b, p, pt: (0, 0, 0)),
                      m_spec],
            out_specs=pl.BlockSpec((1, DEC_ROWS, W), lambda b, p, pt: (b, 0, 0)),
            scratch_shapes=[pltpu.VMEM((DEC_ROWS, 1), _F32), pltpu.VMEM((DEC_ROWS, 1), _F32),
                            pltpu.VMEM((DEC_ROWS, W), _F32), pltpu.VMEM((DEC_ROWS, 128), _F32)]),
        out_shape=jax.ShapeDtypeStruct((B, DEC_ROWS, W), _F32),
        compiler_params=pltpu.CompilerParams(dimension_semantics=("arbitrary", "arbitrary"), vmem_limit_bytes=VMEM_LIMIT),
        name="paged_attn_" + mode,
    )(pt, q_rows, *([cache_rows] * PP), new_rows, far, near, m_in)


def _dsa_decode_score_kernel(pt_ref, qi_ref, wi_ref, *rest):
    caches, new_ref, score_ref = rest[:PP], rest[PP], rest[PP + 1]
    p = pl.program_id(1)
    last = pl.num_programs(1) - 1

    def tile_scores(ki):
        ki = ki.astype(_BF16)
        acc = jnp.zeros((ROWS_T, PAGE_SIZE), _F32)
        for hh in range(DSA_IDX_HEADS):
            s = lax.dot_general(qi_ref[0, hh], ki, (((1,), (1,)), ((), ())), preferred_element_type=_F32) * DSA_IDX_DIM ** -0.5
            acc = acc + wi_ref[0, hh] * jnp.maximum(s, 0.0)
        return acc

    @pl.when(p < last)
    def _():
        for i in range(PP):
            score_ref[0, :, i * PAGE_SIZE:(i + 1) * PAGE_SIZE] = tile_scores(caches[i][0])

    @pl.when(p == last)
    def _():
        score_ref[0] = jnp.zeros(score_ref.shape[1:], _F32)
        score_ref[0, :, 0:PAGE_SIZE] = tile_scores(new_ref[0])


def dsa_decode_scores(pt, qi_rows, wi_rows, cache_rows, new_rows):
    B, n_pages = pt.shape
    assert n_pages % PP == 0
    return pl.pallas_call(
        _dsa_decode_score_kernel,
        grid_spec=pltpu.PrefetchScalarGridSpec(
            num_scalar_prefetch=1,
            grid=(B, n_pages // PP + 1),
            in_specs=[pl.BlockSpec((1, DSA_IDX_HEADS, ROWS_T, DSA_IDX_DIM), lambda b, p, pt: (b, 0, 0, 0)),
                      pl.BlockSpec((1, DSA_IDX_HEADS, ROWS_T, 1), lambda b, p, pt: (b, 0, 0, 0))]
            + _page_specs((1, PAGE_SIZE, DSA_IDX_DIM), n_pages)
            + [pl.BlockSpec((1, PAGE_SIZE, DSA_IDX_DIM), lambda b, p, pt: (b, 0, 0))],
            out_specs=pl.BlockSpec((1, ROWS_T, PP * PAGE_SIZE), lambda b, p, pt: (b, 0, p))),
        out_shape=jax.ShapeDtypeStruct((B, ROWS_T, (n_pages + PP) * PAGE_SIZE), _F32),
        compiler_params=pltpu.CompilerParams(dimension_semantics=("arbitrary", "arbitrary"), vmem_limit_bytes=VMEM_LIMIT),
        name="dsa_decode_scores",
    )(pt, qi_rows, wi_rows, *([cache_rows] * PP), new_rows)


def _count_lanes(pred):
    return jnp.sum(pred.astype(_F32), axis=1, keepdims=True)


def _dsa_decode_select_kernel(topk, past_len, score_ref, keep_ref, key_ref):
    Lp = score_ref.shape[2]
    idx = lax.broadcasted_iota(jnp.int32, (ROWS_T, Lp), 1)
    q_pos = past_len + lax.broadcasted_iota(jnp.int32, (ROWS_T, Lp), 0)
    adm = idx <= q_pos
    score = score_ref[0]
    score = jnp.where(adm, jnp.where(score == 0.0, 0.0, score), -jnp.inf)
    bits = pltpu.bitcast(score, jnp.int32)
    key_ref[...] = jnp.where(bits < 0, bits ^ jnp.int32(0x7FFFFFFF), bits)
    kf = jnp.float32(topk)
    lo = jnp.where(_count_lanes(key_ref[...] >= 0) >= kf, jnp.int32(0), jnp.int32(-2 ** 31))

    def vstep(i, lo):
        cand = lo + jnp.left_shift(jnp.int32(1), 30 - i)
        return jnp.where(_count_lanes(key_ref[...] >= cand) >= kf, cand, lo)

    thr = lax.fori_loop(0, 31, vstep, lo)
    need = kf - _count_lanes(key_ref[...] > thr)
    nbits = max(1, (Lp - 1).bit_length())

    def istep(i, lo):
        cand = lo + jnp.left_shift(jnp.int32(1), nbits - 1 - i)
        c = _count_lanes((key_ref[...] == thr) & (idx < cand))
        return jnp.where(c < need, cand, lo)

    last_tie = lax.fori_loop(0, nbits, istep, jnp.zeros((ROWS_T, 1), jnp.int32))
    key = key_ref[...]
    keep = ((key > thr) | ((key == thr) & (idx <= last_tie))) & adm
    keep_ref[0] = keep.astype(_F32)


def dsa_decode_select(score, topk, past_len):
    B, _, Lp = score.shape
    return pl.pallas_call(
        functools.partial(_dsa_decode_select_kernel, topk, past_len),
        grid=(B,),
        in_specs=[pl.BlockSpec((1, ROWS_T, Lp), lambda b: (b, 0, 0))],
        out_specs=pl.BlockSpec((1, ROWS_T, Lp), lambda b: (b, 0, 0)),
        out_shape=jax.ShapeDtypeStruct((B, ROWS_T, Lp), _F32),
        scratch_shapes=[pltpu.VMEM((ROWS_T, Lp), jnp.int32)],
        compiler_params=pltpu.CompilerParams(dimension_semantics=("arbitrary",), vmem_limit_bytes=VMEM_LIMIT),
        name="dsa_decode_select",
    )(score)


def decode_rows(q):
    B, T, H, _ = q.shape
    qp = jnp.pad(jnp.transpose(q, (0, 2, 1, 3)), ((0, 0), (0, 0), (0, ROWS_T - T), (0, 0)))
    rows = qp[:, :, :, None, :] * jnp.eye(H, dtype=q.dtype)[None, :, None, :, None]
    return rows.reshape(B, H * ROWS_T, H * HD)


def from_decode_rows(o, T):
    B = o.shape[0]
    o5 = o.reshape(B, N_HEADS, ROWS_T, N_HEADS, HD)
    own = jnp.sum(o5 * jnp.eye(N_HEADS, dtype=o.dtype)[None, :, None, :, None], axis=3)
    return jnp.transpose(own[:, :, :T], (0, 2, 1, 3)).reshape(B, T, N_HEADS * HD)


def decode_bias(tab, past_len, n_pages):
    t = jnp.arange(ROWS_T)[None, :, None]
    tile = (n_pages + 1 - NEAR_TILES + jnp.arange(NEAR_TILES))[:, None, None]
    dist = past_len + t - (tile * PAGE_SIZE + jnp.arange(PAGE_SIZE)[None, None, :])
    near = jnp.transpose(bias_lookup(dist, tab), (1, 0, 2, 3)).reshape(NEAR_TILES, DEC_ROWS, PAGE_SIZE)
    far = jnp.repeat(tab[REL_BUCKETS - 1].astype(_F32), ROWS_T).reshape(DEC_ROWS, 1)
    return far, near


def pad_rows(a, n):
    return jnp.pad(a, ((0, 0), (0, n - a.shape[1])) + ((0, 0),) * (a.ndim - 2))


def heads_T(z, nh):
    B, T, _ = z.shape
    return jnp.transpose(z.reshape(B, T, nh, HD), (0, 2, 3, 1))


def heads_K(z, nh):
    B, L, _ = z.shape
    return jnp.transpose(z.reshape(B, L, nh, HD), (0, 2, 1, 3))


def heads_VT(z, nh):
    B, L, _ = z.shape
    return jnp.transpose(z.reshape(B, L // TK, TK, nh, HD), (0, 3, 1, 4, 2))


def from_heads_T(oT):
    B, H, _, T = oT.shape
    return jnp.transpose(oT, (0, 3, 1, 2)).reshape(B, T, H * HD)


def nsa_attention_prompt(q_flat, gates_flat, nsa_new, q_pos, w1, w2, pe, tab, bias_tiles):
    B, T, _ = q_flat.shape
    G, H, L = NSA_KV_HEADS, N_HEADS, T
    scale = HD ** -0.5
    q = q_flat.reshape(B, T, H, HD)
    kc, vc = nsa_new[:, :, 0], nsa_new[:, :, 1]
    nch = L // NSA_CMP_STRIDE
    ncmp = nch - 1

    def compress(x, i):
        c = x[:, :nch * NSA_CMP_STRIDE].reshape(B, nch, NSA_CMP_STRIDE, G, HD)
        blk = jnp.concatenate([c[:, :-1], c[:, 1:]], axis=2) + pe[i][None, None, :, None, :]
        blk = jnp.transpose(blk, (0, 1, 3, 2, 4)).reshape(B, ncmp, G, NSA_CMP_LEN * HD)
        return jax.nn.gelu(blk @ w1[i]) @ w2[i]

    kcmp = compress(kc, 0)
    vcmp = compress(vc, 1)
    cmp_end = jnp.arange(ncmp) * NSA_CMP_STRIDE + NSA_CMP_LEN - 1
    qg = q.reshape(B, T, G, NSA_GROUP, HD)
    dist_c = q_pos[:, None] - cmp_end[None, :]
    lg_c = jnp.einsum('btgrd,bngd->btgrn', qg, kcmp).astype(jnp.float32).reshape(B, T, H, ncmp) * scale
    lg_c = lg_c + rel_bias(dist_c[None, :, None, :], tab)
    p_c = masked_softmax(lg_c, (dist_c >= 0)[None, :, None, :])
    p_cg = p_c.reshape(B, T, G, NSA_GROUP, ncmp)
    o_cmp = jnp.einsum('btgrn,bngd->btgrd', p_cg.astype(vcmp.dtype), vcmp).reshape(B, T, H, HD)
    nsel = -(-L // NSA_SEL_BLOCK)
    ii = np.arange(ncmp)[:, None]
    jj_np = np.arange(nsel)[None, :]
    overlap = ((ii * NSA_CMP_STRIDE < (jj_np + 1) * NSA_SEL_BLOCK) & (ii * NSA_CMP_STRIDE + NSA_CMP_LEN > jj_np * NSA_SEL_BLOCK)).astype(np.float32)
    imp = jnp.einsum('btgrn,nj->btgj', p_cg, jnp.asarray(overlap))
    cur = q_pos // NSA_SEL_BLOCK
    jj = jnp.arange(nsel)[None, :]
    causal_blk = jj <= cur[:, None]
    forced = (jj == 0) | (jj == cur[:, None]) | (jj == cur[:, None] - 1)
    imp = jnp.where(forced[None, :, None, :], jnp.inf, jnp.where(causal_blk[None, :, None, :], imp, -jnp.inf))
    _, sel = lax.top_k(imp, min(NSA_TOPN, nsel))
    chosen = jnp.any(sel[..., None] == jnp.arange(nsel), axis=-2)
    selT = jnp.transpose(chosen, (0, 2, 3, 1)).astype(_F32)
    qT = heads_T(q_flat, H) * scale
    flat = lambda a: a.reshape(B, L, G * HD)
    o_sel = from_heads_T(attention_T('sel', qT, heads_K(flat(nsa_new[:, :, 2]), G), heads_VT(flat(nsa_new[:, :, 3]), G), bias_tiles, selT))
    o_win = from_heads_T(attention_T('win', qT, heads_K(flat(nsa_new[:, :, 4]), G), heads_VT(flat(nsa_new[:, :, 5]), G), bias_tiles))
    g = jax.nn.sigmoid(gates_flat.reshape(B, T, H, 3).astype(jnp.float32))
    out = g[..., 0:1] * o_cmp + g[..., 1:2] * o_sel.reshape(B, T, H, HD) + g[..., 2:3] * o_win.reshape(B, T, H, HD)
    return out.reshape(B, T, H * HD)


def token_mix(z, past, lp, tab):
    B, T, _ = z.shape
    P = 0 if past is None else past['nsa'].shape[1]
    q_pos = P + jnp.arange(T, dtype=jnp.int32)
    qa, ka, va, q_nsa, nsa_kv, nsa_g, qc, kc, vc, qi, ki, wi, u, vg = split_last(z, IN_SPLITS)
    moba_new = jnp.stack([ka, va], axis=2).reshape(B, T, 2, N_HEADS, HD)
    nsa_new = nsa_kv.reshape(B, T, 6, NSA_KV_HEADS, HD)
    nsa_main_new = nsa_new[:, :, :4]
    win_new = nsa_new[:, :, 4:]
    dsa_new = jnp.stack([kc, vc], axis=2).reshape(B, T, 2, N_HEADS, HD)
    if past is None:
        nsa_all, win_all = nsa_main_new, win_new
    else:
        nsa_all = jnp.concatenate([past['nsa'], nsa_main_new], axis=1)
        win_all = jnp.concatenate([past['win'], win_new], axis=1)
    kw_pos0 = P + T - win_all.shape[1]
    scale = HD ** -0.5
    if past is None:
        bt = lp['bias_tiles']
        qaT = heads_T(qa, N_HEADS) * scale
        y_a = from_heads_T(attention_T('moba', qaT, heads_K(ka, N_HEADS), heads_VT(va, N_HEADS), bt[0], moba_gate(qaT, heads_K(ka, N_HEADS))))
        y_b = nsa_attention_prompt(q_nsa, nsa_g, nsa_new, q_pos, lp['cmp_w1'], lp['cmp_w2'], lp['cmp_pe'], tab[:, N_HEADS:2 * N_HEADS], bt[1])
        qiT = jnp.transpose(qi.reshape(B, T, DSA_IDX_HEADS, DSA_IDX_DIM), (0, 2, 3, 1))
        keep = dsa_mask(ki, qiT, jnp.transpose(wi, (0, 2, 1)), min(DSA_TOPK, T // 4))
        y_c = from_heads_T(attention_T('dsa', heads_T(qc, N_HEADS) * scale, heads_K(kc, N_HEADS), heads_VT(vc, N_HEADS), bt[2], keep))
    else:
        assert P % MOBA_BLOCK == 0 and T <= ROWS_T
        pt = past['pt']
        n_pages = pt.shape[1]
        far_a, near_a = decode_bias(tab[:, :N_HEADS], P, n_pages)
        qa_rows = (decode_rows(qa.reshape(B, T, N_HEADS, HD)) * scale).astype(_BF16)
        new_a = pad_rows(jnp.concatenate([ka, va], axis=-1), PAGE_SIZE)
        gs = moba_decode_gate(pt, qa_rows, past['moba_rows'])
        y_a = from_decode_rows(paged_attention('moba', pt, qa_rows, past['moba_rows'], new_a, far_a, near_a, gs), T)
        y_b = nsa_attention(q_nsa.reshape(B, T, N_HEADS, HD), nsa_g.reshape(B, T, N_HEADS, 3), nsa_all[:, :, 0], nsa_all[:, :, 1], nsa_all[:, :, 2], nsa_all[:, :, 3], win_all[:, :, 0], win_all[:, :, 1], kw_pos0, q_pos, lp['cmp_w1'], lp['cmp_w2'], lp['cmp_pe'], tab[:, N_HEADS:2 * N_HEADS])
        qi_rows = pad_rows(jnp.transpose(qi.reshape(B, T, DSA_IDX_HEADS, DSA_IDX_DIM), (0, 2, 1, 3)).reshape(B * DSA_IDX_HEADS, T, DSA_IDX_DIM), ROWS_T)
        qi_rows = qi_rows.reshape(B, DSA_IDX_HEADS, ROWS_T, DSA_IDX_DIM).astype(_BF16)
        wi_rows = pad_rows(jnp.transpose(wi, (0, 2, 1)).reshape(B * DSA_IDX_HEADS, T), ROWS_T).reshape(B, DSA_IDX_HEADS, ROWS_T, 1)
        score = dsa_decode_scores(pt, qi_rows, wi_rows, past['idx_rows'], pad_rows(ki, PAGE_SIZE))
        keep = dsa_decode_select(score, min(DSA_TOPK, (P + T) // 4), P)
        keep = jnp.transpose(keep.reshape(B, ROWS_T, n_pages + PP, PAGE_SIZE), (0, 2, 1, 3))
        far_c, near_c = decode_bias(tab[:, 2 * N_HEADS:], P, n_pages)
        qc_rows = (decode_rows(qc.reshape(B, T, N_HEADS, HD)) * scale).astype(_BF16)
        new_c = pad_rows(jnp.concatenate([kc, vc], axis=-1), PAGE_SIZE)
        y_c = from_decode_rows(paged_attention('dsa', pt, qc_rows, past['dsa_rows'], new_c, far_c, near_c, keep), T)
    y_d, v_rows = gmlp_sgu(u, vg, lp['gm_ws'], lp['gm_b'], lp['gm_norm'])
    branches = [y.reshape(B * T, MIX_W) for y in (y_a, y_b, y_c, y_d)]
    win_keep = win_all[:, -min(NSA_WINDOW, P + T):]
    return branches, (moba_new, nsa_main_new, win_keep, dsa_new, ki, v_rows)


def gated_branch_mix(branches, gz, lp):
    mixed = 0.0
    for n in range(N_BRANCH):
        gate = jax.nn.sigmoid(gz[:, n * D_MODEL:(n + 1) * D_MODEL] + lp['b_gate'][n])
        mixed = mixed + gate * (branches[n] @ lp['w_branch'][n])
    return mixed @ lp['w_out']


def mem_attention_q(q, mkv, wo):
    B, T, _ = q.shape
    q = q.reshape(B, T, MEM_HEADS, MEM_HD)
    lg = jnp.einsum('bthd,bmhd->bthm', q, mkv[:, :, 0]).astype(jnp.float32) * MEM_HD ** -0.5
    p = jax.nn.softmax(lg, axis=-1)
    o = jnp.einsum('bthm,bmhd->bthd', p.astype(mkv.dtype), mkv[:, :, 1]).reshape(B, T, MEM_W)
    return o @ wo


def layer_pair(rows, shape_p, shape_s, past, mkv_p, mkv_s, lp, tab):
    n_p = shape_p[0] * shape_p[1]
    split = lambda a: (a[:n_p].reshape(shape_p[0], shape_p[1], -1), a[n_p:].reshape(shape_s[0], shape_s[1], -1))
    z_p, z_s = split(norm_matmul(rows, lp['norm_mix'], lp['w_in']))
    gz = norm_matmul(rows, lp['norm_mix'], lp['w_gate'])
    br_p, rp = token_mix(z_p, None, lp, tab)
    br_s, rs = token_mix(z_s, past, lp, tab)
    branches = [jnp.concatenate([bp, bs], axis=0) for bp, bs in zip(br_p, br_s)]
    rows = rows + gated_branch_mix(branches, gz, lp)
    q_p, q_s = split(norm_matmul(rows, lp['norm_mem'], lp['w_mem_q']))
    mem_p = mem_attention_q(q_p, mkv_p, lp['w_mem_o'])
    mem_s = mem_attention_q(q_s, mkv_s, lp['w_mem_o'])
    rows = rows + jnp.concatenate([mem_p.reshape(-1, D_MODEL), mem_s.reshape(-1, D_MODEL)], axis=0)
    rows = peer_ffn_tokens(rows, lp['norm_ffn'], lp['peer_wq'], lp['peer_keys'], lp['peer_u'], lp['peer_v'])
    return rows, rp, rs


def kernel(x_prompt, x_sample, cache_moba_kv, cache_nsa_kv, cache_dsa_kv, cache_dsa_idx, state_nsa_win, cache_mem_kv, page_table, mem_prompt, rel_bias_table, w_in, nsa_cmp_w1, nsa_cmp_w2, nsa_cmp_pe, gm_ws, gm_b, gm_norm, w_branch, w_gate, b_gate, w_out, w_mem_q, w_mem_kv, w_mem_o, peer_wq, peer_keys, peer_u, peer_v, norm_mix, norm_mem, norm_ffn, norm_final):
    n_p = x_prompt.shape[0] * x_prompt.shape[1]
    rows = jnp.concatenate([x_prompt.reshape(-1, D_MODEL), x_sample.reshape(-1, D_MODEL)], axis=0)
    rows_p = []
    rows_s = []
    mem_rows = []
    Bp = x_prompt.shape[0]
    n_pool = cache_moba_kv.shape[1]
    bias_tiles =[rel_bias_tiles(rel_bias_table[:, n * N_HEADS:(n + 1) * N_HEADS]) for n in range(3)]
    for l in range(DEPTH):
        lp = {
            'w_in': w_in[l], 'cmp_w1': nsa_cmp_w1[l], 'cmp_w2': nsa_cmp_w2[l], 'cmp_pe': nsa_cmp_pe[l],
            'gm_ws': gm_ws[l], 'gm_b': gm_b[l], 'gm_norm': gm_norm[l],
            'w_branch': w_branch[l], 'w_gate': w_gate[l], 'b_gate': b_gate[l], 'w_out': w_out[l],
            'w_mem_q': w_mem_q[l], 'w_mem_o': w_mem_o[l],
            'peer_wq': peer_wq[l], 'peer_keys': peer_keys[l], 'peer_u': peer_u[l], 'peer_v': peer_v[l],
            'norm_mix': norm_mix[l], 'norm_mem': norm_mem[l], 'norm_ffn': norm_ffn[l],
        }
        mkv_p = (mem_prompt @ w_mem_kv[l]).reshape(Bp, MEM_TOKENS, 2, MEM_HEADS, MEM_HD)
        past = {
            'pt': page_table + l * n_pool,
            'moba_rows': cache_moba_kv.reshape(DEPTH * n_pool, PAGE_SIZE, 2 * MIX_W),
            'dsa_rows': cache_dsa_kv.reshape(DEPTH * n_pool, PAGE_SIZE, 2 * MIX_W),
            'idx_rows': cache_dsa_idx.reshape(DEPTH * n_pool, PAGE_SIZE, DSA_IDX_DIM),
            'nsa': gather_pages(cache_nsa_kv.reshape((DEPTH * n_pool,) + cache_nsa_kv.shape[2:]), page_table + l * n_pool),
            'win': state_nsa_win[l],
        }
        lp['bias_tiles'] = bias_tiles
        rows, rp, rs = layer_pair(rows, x_prompt.shape, x_sample.shape, past, mkv_p, cache_mem_kv[l], lp, rel_bias_table)
        rows_p.append(rp)
        rows_s.append(rs)
        mem_rows.append(mkv_p)
    y_rows = rmsnorm(rows, norm_final)
    y_prompt = y_rows[:n_p].reshape(x_prompt.shape)
    y_sample = y_rows[n_p:].reshape(x_sample.shape)
    new_moba_kv_prompt = jnp.stack([r[0] for r in rows_p])
    new_moba_kv_sample = jnp.stack([r[0] for r in rows_s])
    new_nsa_kv_prompt = jnp.stack([r[1] for r in rows_p])
    new_nsa_kv_sample = jnp.stack([r[1] for r in rows_s])
    new_nsa_win_prompt = jnp.stack([r[2] for r in rows_p])
    new_nsa_win_sample = jnp.stack([r[2] for r in rows_s])
    new_dsa_kv_prompt = jnp.stack([r[3] for r in rows_p])
    new_dsa_kv_sample = jnp.stack([r[3] for r in rows_s])
    new_dsa_idx_prompt = jnp.stack([r[4] for r in rows_p])
    new_dsa_idx_sample = jnp.stack([r[4] for r in rows_s])
    new_mem_kv_prompt = jnp.stack(mem_rows)
    new_gmlp_v_sample = jnp.stack([r[5] for r in rows_s])
    return (y_prompt, y_sample, new_moba_kv_prompt, new_moba_kv_sample, new_nsa_kv_prompt, new_nsa_kv_sample, new_nsa_win_prompt, new_nsa_win_sample, new_dsa_kv_prompt, new_dsa_kv_sample, new_dsa_idx_prompt, new_dsa_idx_sample, new_mem_kv_prompt, new_gmlp_v_sample)
```

```python
import functools
import math
import jax
import jax.numpy as jnp
from jax import lax
import numpy as np
from jax.experimental import pallas as pl
from jax.experimental.pallas import tpu as pltpu

D_MODEL = 2048
DEPTH = 2
DEC_SEQ = 4
PAGE_SIZE = 128
VMEM_LIMIT = 48 * 1024 * 1024

HD = 64
N_HEADS = 8
MIX_W = N_HEADS * HD
N_BRANCH = 4
MOBA_BLOCK = 256
MOBA_TOPK = 3
NSA_KV_HEADS = 2
NSA_GROUP = N_HEADS // NSA_KV_HEADS
NSA_CMP_STRIDE = 16
NSA_CMP_LEN = 2 * NSA_CMP_STRIDE
NSA_SEL_BLOCK = 64
NSA_TOPN = 16
NSA_WINDOW = 512
NSA_QB = 64
WIN_QB = 128
DSA_IDX_HEADS = 4
DSA_IDX_DIM = 64
DSA_TOPK = 256
GM_CHUNK = 128
GM_GROUPS = 8
GM_W = 512
REL_BUCKETS = 32
REL_MAX_DIST = 128
MEM_TOKENS = 256
MEM_HEADS = 4
MEM_HD = 128
MEM_W = MEM_HEADS * MEM_HD
PEER_HEADS = 8
PEER_NKEYS = 128
PEER_QDIM = 128
PEER_TOPK = 16
NORM_EPS = 1e-6
IN_SPLITS = (MIX_W, MIX_W, MIX_W, MIX_W, 6 * NSA_KV_HEADS * HD, 3 * N_HEADS, MIX_W, MIX_W, MIX_W, DSA_IDX_HEADS * DSA_IDX_DIM, DSA_IDX_DIM, DSA_IDX_HEADS, GM_W, GM_W)
IN_COLS = sum(IN_SPLITS)


def rmsnorm(x, g):
    xf = x.astype(jnp.float32)
    y = xf * lax.rsqrt(jnp.mean(xf * xf, axis=-1, keepdims=True) + NORM_EPS)
    return (y * g.astype(jnp.float32)).astype(x.dtype)


def masked_softmax(logits, mask):
    l = jnp.where(mask, logits.astype(jnp.float32), -jnp.inf)
    m = jnp.max(l, axis=-1, keepdims=True)
    m = jnp.where(jnp.isfinite(m), m, 0.0)
    e = jnp.exp(l - m)
    return e / jnp.maximum(jnp.sum(e, axis=-1, keepdims=True), 1e-30)


def rel_bucket(dist):
    n = jnp.maximum(dist, 0)
    max_exact = REL_BUCKETS // 2
    nf = jnp.maximum(n, 1).astype(jnp.float32)
    large = max_exact + (jnp.log(nf / max_exact) / math.log(REL_MAX_DIST / max_exact) * (REL_BUCKETS - max_exact)).astype(jnp.int32)
    return jnp.where(n < max_exact, n, jnp.minimum(large, REL_BUCKETS - 1))


def rel_bias(dist, tab):
    onehot = (rel_bucket(dist)[..., None] == jnp.arange(REL_BUCKETS)).astype(jnp.float32)
    spec = '...okb,bh->...hk' if dist.shape[-2] == 1 else '...hkb,bh->...hk'
    return jnp.einsum(spec, onehot, tab.astype(jnp.float32), precision=lax.Precision.HIGHEST)


def split_last(z, sizes):
    return jnp.split(z, np.cumsum(sizes)[:-1].tolist(), axis=-1)


def map_query_blocks(fn, qb, q_pos, *xs):
    T = q_pos.shape[0]
    qb = qb if T % qb == 0 else T
    nb = T // qb
    xb = tuple(jnp.moveaxis(x.reshape(x.shape[0], nb, qb, *x.shape[2:]), 1, 0) for x in xs)
    out = lax.map(lambda a: fn(*a), (q_pos.reshape(nb, qb),) + xb)
    return jnp.moveaxis(out, 0, 1).reshape(out.shape[1], T, *out.shape[3:])


def gather_pages(pool, page_table):
    g = pool[page_table]
    return g.reshape(g.shape[0], g.shape[1] * g.shape[2], *g.shape[3:])


def nsa_attention(q, gates, kc, vc, ks, vs, kw, vw, kw_pos0, q_pos, w1, w2, pe, tab):
    B, L, G, _ = kc.shape
    T = q.shape[1]
    H = N_HEADS
    scale = HD ** -0.5
    nch = L // NSA_CMP_STRIDE
    ncmp = nch - 1

    def compress(x, i):
        c = x[:, :nch * NSA_CMP_STRIDE].reshape(B, nch, NSA_CMP_STRIDE, G, HD)
        blk = jnp.concatenate([c[:, :-1], c[:, 1:]], axis=2) + pe[i][None, None, :, None, :]
        blk = jnp.transpose(blk, (0, 1, 3, 2, 4)).reshape(B, ncmp, G, NSA_CMP_LEN * HD)
        return jax.nn.gelu(blk @ w1[i]) @ w2[i]

    kcmp = compress(kc, 0)
    vcmp = compress(vc, 1)
    cmp_end = jnp.arange(ncmp) * NSA_CMP_STRIDE + NSA_CMP_LEN - 1
    qg = q.reshape(B, T, G, NSA_GROUP, HD)
    dist_c = q_pos[:, None] - cmp_end[None, :]
    lg_c = jnp.einsum('btgrd,bngd->btgrn', qg, kcmp).astype(jnp.float32).reshape(B, T, H, ncmp) * scale
    lg_c = lg_c + rel_bias(dist_c[None, :, None, :], tab)
    p_c = masked_softmax(lg_c, (dist_c >= 0)[None, :, None, :])
    p_cg = p_c.reshape(B, T, G, NSA_GROUP, ncmp)
    o_cmp = jnp.einsum('btgrn,bngd->btgrd', p_cg.astype(vcmp.dtype), vcmp).reshape(B, T, H, HD)
    nsel = -(-L // NSA_SEL_BLOCK)
    ii = np.arange(ncmp)[:, None]
    jj_np = np.arange(nsel)[None, :]
    overlap = ((ii * NSA_CMP_STRIDE < (jj_np + 1) * NSA_SEL_BLOCK) & (ii * NSA_CMP_STRIDE + NSA_CMP_LEN > jj_np * NSA_SEL_BLOCK)).astype(np.float32)
    imp = jnp.einsum('btgrn,nj->btgj', p_cg, jnp.asarray(overlap))
    cur = q_pos // NSA_SEL_BLOCK
    jj = jnp.arange(nsel)[None, :]
    causal_blk = jj <= cur[:, None]
    forced = (jj == 0) | (jj == cur[:, None]) | (jj == cur[:, None] - 1)
    imp = jnp.where(forced[None, :, None, :], jnp.inf, jnp.where(causal_blk[None, :, None, :], imp, -jnp.inf))
    ntop = min(NSA_TOPN, nsel)
    _, sel = lax.top_k(imp, ntop)
    pad = ((0, 0), (0, nsel * NSA_SEL_BLOCK - L), (0, 0), (0, 0))
    ksb = jnp.transpose(jnp.pad(ks, pad).reshape(B, nsel, NSA_SEL_BLOCK, G, HD), (0, 3, 1, 2, 4))
    vsb = jnp.transpose(jnp.pad(vs, pad).reshape(B, nsel, NSA_SEL_BLOCK, G, HD), (0, 3, 1, 2, 4))
    bi = jnp.arange(B)[:, None, None, None]
    gi = jnp.arange(G)[None, None, :, None]
    off = jnp.arange(NSA_SEL_BLOCK)

    def sel_fn(qp, qq, ss):
        qb = qp.shape[0]
        kg = ksb[bi, gi, ss].reshape(B, qb, G, ntop * NSA_SEL_BLOCK, HD)
        vg = vsb[bi, gi, ss].reshape(B, qb, G, ntop * NSA_SEL_BLOCK, HD)
        pos = (ss[..., None] * NSA_SEL_BLOCK + off).reshape(B, qb, G, -1)
        dist = jnp.repeat(qp[None, :, None, None] - pos, NSA_GROUP, axis=2)
        lg = jnp.einsum('bqgrd,bqgkd->bqgrk', qq.reshape(B, qb, G, NSA_GROUP, HD), kg).astype(jnp.float32).reshape(B, qb, H, -1) * scale
        p = masked_softmax(lg + rel_bias(dist, tab), dist >= 0)
        o = jnp.einsum('bqgrk,bqgkd->bqgrd', p.reshape(B, qb, G, NSA_GROUP, -1).astype(vg.dtype), vg)
        return o.reshape(B, qb, H, HD)

    o_sel = map_query_blocks(sel_fn, NSA_QB, q_pos, q, sel)
    W = NSA_WINDOW
    wpad = ((0, 0), (W, 0), (0, 0), (0, 0))
    kwp = jnp.pad(kw, wpad)
    vwp = jnp.pad(vw, wpad)

    def win_fn(qp, qq):
        qb = qp.shape[0]
        span = W + qb - 1
        start = qp[0] - kw_pos0 + 1
        kk = lax.dynamic_slice_in_dim(kwp, start, span, axis=1)
        vv = lax.dynamic_slice_in_dim(vwp, start, span, axis=1)
        pos = qp[0] - W + 1 + jnp.arange(span)
        dist = qp[:, None] - pos[None, :]
        ok = (dist >= 0) & (dist < W) & (pos[None, :] >= 0)
        lg = jnp.einsum('bqgrd,bkgd->bqgrk', qq.reshape(B, qb, G, NSA_GROUP, HD), kk).astype(jnp.float32).reshape(B, qb, H, span) * scale
        p = masked_softmax(lg + rel_bias(dist[None, :, None, :], tab), ok[None, :, None, :])
        o = jnp.einsum('bqgrk,bkgd->bqgrd', p.reshape(B, qb, G, NSA_GROUP, span).astype(vv.dtype), vv)
        return o.reshape(B, qb, H, HD)

    o_win = map_query_blocks(win_fn, WIN_QB, q_pos, q)
    g = jax.nn.sigmoid(gates.astype(jnp.float32)).astype(q.dtype)
    return g[..., 0:1] * o_cmp + g[..., 1:2] * o_sel + g[..., 2:3] * o_win


def gmlp_sgu(u, v, ws, b, gnorm):
    B, T, _ = u.shape
    vn = rmsnorm(v, gnorm)
    nc = -(-T // GM_CHUNK)
    vc = jnp.pad(vn, ((0, 0), (0, nc * GM_CHUNK - T), (0, 0))).reshape(B, nc, GM_CHUNK, GM_GROUPS, GM_W // GM_GROUPS)
    wsm = ws * jnp.tril(jnp.ones((GM_CHUNK, GM_CHUNK), ws.dtype))
    mixed = jnp.einsum('gij,bcjgd->bcigd', wsm, vc) + b.T[None, None, :, :, None]
    mixed = mixed.reshape(B, nc * GM_CHUNK, GM_W)[:, :T]
    return u * mixed, vn


def _norm_matmul_kernel(x_ref, g_ref, w_ref, o_ref, xn_ref):
    @pl.when(pl.program_id(1) == 0)
    def _():
        x = x_ref[...]
        y = x * lax.rsqrt(jnp.mean(x * x, axis=-1, keepdims=True) + NORM_EPS)
        xn_ref[...] = (y * g_ref[...]).astype(jnp.bfloat16)

    hw = w_ref.shape[1] // 2
    for s in range(2):
        cols = slice(s * hw, (s + 1) * hw)
        o_ref[:, cols] = jnp.dot(xn_ref[...], w_ref[:, cols].astype(jnp.bfloat16), preferred_element_type=jnp.float32)


def norm_matmul(x, g, w, tm=512, tn=512):
    N, D = x.shape
    C = w.shape[1]
    tm = min(tm, N)
    tn = min(tn, C)
    return pl.pallas_call(
        _norm_matmul_kernel,
        grid=(pl.cdiv(N, tm), pl.cdiv(C, tn)),
        in_specs=[
            pl.BlockSpec((tm, D), lambda i, j: (i, 0)),
            pl.BlockSpec((1, D), lambda i, j: (0, 0)),
            pl.BlockSpec((D, tn), lambda i, j: (0, j)),
        ],
        out_specs=pl.BlockSpec((tm, tn), lambda i, j: (i, j)),
        out_shape=jax.ShapeDtypeStruct((N, C), jnp.float32),
        scratch_shapes=[pltpu.VMEM((tm, D), jnp.bfloat16)],
        compiler_params=pltpu.CompilerParams(
            dimension_semantics=("arbitrary", "arbitrary"), vmem_limit_bytes=VMEM_LIMIT),
        name="norm_matmul",
    )(x, g.reshape(1, D), w.astype(jnp.bfloat16))


TQ = 256
TK = 256
NEG_BIG = -1e30
_BF16 = jnp.bfloat16
_F32 = jnp.float32


HEADS_PER_STEP = 2


def _attn_kernel(mode, shared_kv, qT_ref, k_ref, vT_ref, bias_ref, *rest):
    if mode == 'win':
        (o_ref,) = rest
        m_ref = None
    else:
        m_ref, o_ref = rest
    qi = pl.program_id(2)
    heads = range(HEADS_PER_STEP)
    q = [qT_ref[0, hh].astype(_BF16) for hh in heads]
    lane = lax.broadcasted_iota(jnp.int32, (TK, TQ), 1)
    sub = lax.broadcasted_iota(jnp.int32, (TK, TQ), 0)
    rel0 = lane - sub

    def body(kj, carry):
        dist = rel0 + (qi - kj) * TQ
        ok_all = dist >= 0
        if mode == 'sel':
            rows = m_ref[0, 0, kj]
            blk = sub // NSA_SEL_BLOCK
            r = jnp.where(blk == 0, rows[0:1], jnp.where(blk == 1, rows[1:2], jnp.where(blk == 2, rows[2:3], rows[3:4])))
            ok_all = ok_all & (r > 0.5)
        elif mode == 'dsa':
            ok_all = ok_all & (m_ref[0, pl.ds(pl.multiple_of(kj * TK, TK), TK), :].astype(_F32) > 0.5)
        elif mode == 'win':
            ok_all = ok_all & (dist < NSA_WINDOW)
        out = []
        for hh in heads:
            m, l, acc = carry[hh]
            hk = 0 if shared_kv else hh
            k_t = k_ref[0, hk, pl.ds(pl.multiple_of(kj * TK, TK), TK), :].astype(_BF16)
            s = jnp.dot(k_t, q[hh], preferred_element_type=_F32)
            s = s + bias_ref[hh, jnp.minimum(qi - kj, 2)]
            ok = ok_all & (m_ref[0, hh, kj] > 0.5) if mode == 'moba' else ok_all
            s = jnp.where(ok, s, NEG_BIG)
            m_new = jnp.maximum(m, jnp.max(s, axis=0, keepdims=True))
            p = jnp.where(ok, jnp.exp(s - m_new), 0.0)
            alpha = jnp.exp(m - m_new)
            l = alpha * l + jnp.sum(p, axis=0, keepdims=True)
            v_t = vT_ref[0, hk, kj].astype(_BF16)
            acc = alpha * acc + jnp.dot(v_t, p.astype(_BF16), preferred_element_type=_F32)
            out.append((m_new, l, acc))
        return tuple(out)

    lo = jnp.maximum(qi - (NSA_WINDOW // TK), 0) if mode == 'win' else 0
    init = tuple((jnp.full((1, TQ), NEG_BIG, _F32), jnp.zeros((1, TQ), _F32), jnp.zeros((HD, TQ), _F32)) for _ in heads)
    res = lax.fori_loop(lo, qi + 1, body, init)
    for hh in heads:
        m, l, acc = res[hh]
        o_ref[0, hh] = acc / jnp.maximum(l, 1e-30)


def attention_T(mode, qT, k, vT, bias_tiles, mask=None):
    B, H, _, T = qT.shape
    Hkv, L = k.shape[1], k.shape[2]
    grp = H // Hkv
    hp = HEADS_PER_STEP
    assert H % hp == 0 and (grp == 1 or grp % hp == 0)
    shared_kv = grp > 1
    kvh = 1 if shared_kv else hp
    kv_idx = (lambda g: g * hp // grp) if shared_kv else (lambda g: g)
    in_specs = [
        pl.BlockSpec((1, hp, HD, TQ), lambda b, g, i: (b, g, 0, i)),
        pl.BlockSpec((1, kvh, L, HD), lambda b, g, i: (b, kv_idx(g), 0, 0)),
        pl.BlockSpec((1, kvh, L // TK, HD, TK), lambda b, g, i: (b, kv_idx(g), 0, 0, 0)),
        pl.BlockSpec((hp, 3, TK, TQ), lambda b, g, i: (g, 0, 0, 0)),
    ]
    args = [qT, k, vT, bias_tiles]
    if mode == 'moba':
        in_specs.append(pl.BlockSpec((1, hp, L // TK, 1, TQ), lambda b, g, i: (b, g, 0, 0, i)))
        args.append(mask.reshape(B, H, L // TK, 1, T))
    elif mode == 'sel':
        per = TK // NSA_SEL_BLOCK
        in_specs.append(pl.BlockSpec((1, 1, L // TK, per, TQ), lambda b, g, i: (b, kv_idx(g), 0, 0, i)))
        args.append(mask.reshape(B, Hkv, L // TK, per, T))
    elif mode == 'dsa':
        in_specs.append(pl.BlockSpec((1, L, TQ), lambda b, g, i: (b, 0, i)))
        args.append(mask)
    return pl.pallas_call(
        functools.partial(_attn_kernel, mode, shared_kv),
        grid=(B, H // hp, T // TQ),
        in_specs=in_specs,
        out_specs=pl.BlockSpec((1, hp, HD, TQ), lambda b, g, i: (b, g, 0, i)),
        out_shape=jax.ShapeDtypeStruct((B, H, HD, T), _F32),
        compiler_params=pltpu.CompilerParams(
            dimension_semantics=("arbitrary", "arbitrary", "arbitrary"), vmem_limit_bytes=VMEM_LIMIT),
        name="attn_" + mode,
    )(*args)


def _moba_gate_kernel(qT_ref, k_ref, sel_ref):
    nblk, T = sel_ref.shape[2], sel_ref.shape[3]
    q = qT_ref[0, 0].astype(_BF16)
    k = k_ref[0, 0]
    km = jnp.mean(k.reshape(nblk, MOBA_BLOCK, HD), axis=1)
    gs = jnp.dot(km.astype(_BF16), q, preferred_element_type=_F32)
    j = lax.broadcasted_iota(jnp.int32, (nblk, T), 0)
    own = lax.broadcasted_iota(jnp.int32, (nblk, T), 1) // MOBA_BLOCK
    fully_past = j < own
    gs = jnp.where(fully_past, gs, -jnp.inf)
    rank = jnp.zeros((nblk, T), jnp.int32)
    for jp in range(nblk):
        row = gs[jp:jp + 1, :]
        ahead = (row > gs) | ((row == gs) & (jp < j))
        rank = rank + ahead.astype(jnp.int32)
    sel = (fully_past & (rank < MOBA_TOPK)) | (j == own)
    sel_ref[0, 0] = sel.astype(_F32)


def moba_gate(qT, k):
    B, H, _, T = qT.shape
    L = k.shape[2]
    nblk = L // MOBA_BLOCK
    return pl.pallas_call(
        _moba_gate_kernel,
        grid=(B, H),
        in_specs=[pl.BlockSpec((1, 1, HD, T), lambda b, h: (b, h, 0, 0)),
                  pl.BlockSpec((1, 1, L, HD), lambda b, h: (b, h, 0, 0))],
        out_specs=pl.BlockSpec((1, 1, nblk, T), lambda b, h: (b, h, 0, 0)),
        out_shape=jax.ShapeDtypeStruct((B, H, nblk, T), _F32),
        compiler_params=pltpu.CompilerParams(dimension_semantics=("arbitrary", "arbitrary"), vmem_limit_bytes=VMEM_LIMIT),
        name="moba_gate",
    )(qT, k)


def _count_rows(pred):
    return jnp.sum(pred.astype(_F32), axis=0, keepdims=True)


def _dsa_mask_kernel(topk, ki_ref, qiT_ref, wiT_ref, mask_ref, key_ref):
    L = ki_ref.shape[1]
    qi = pl.program_id(1)
    ki = ki_ref[0].astype(_BF16)
    score = jnp.zeros((L, TQ), _F32)
    for hh in range(DSA_IDX_HEADS):
        s = jnp.dot(ki, qiT_ref[0, hh].astype(_BF16), preferred_element_type=_F32) * DSA_IDX_DIM ** -0.5
        score = score + wiT_ref[0, hh:hh + 1, :] * jnp.maximum(s, 0.0)
    idx = lax.broadcasted_iota(jnp.int32, (L, TQ), 0)
    q_pos = qi * TQ + lax.broadcasted_iota(jnp.int32, (L, TQ), 1)
    adm = idx <= q_pos
    score = jnp.where(adm, jnp.where(score == 0.0, 0.0, score), -jnp.inf)
    bits = pltpu.bitcast(score, jnp.int32)
    key_ref[...] = jnp.where(bits < 0, bits ^ jnp.int32(0x7FFFFFFF), bits)
    kf = jnp.float32(topk)
    int_min = jnp.int32(-2 ** 31)
    sub = lax.broadcasted_iota(jnp.int32, (TK, TQ), 0)

    def count(pred):
        def tile(kt, acc):
            rows = pl.ds(pl.multiple_of(kt * TK, TK), TK)
            return acc + _count_rows(pred(key_ref[rows, :], kt * TK + sub))
        return lax.fori_loop(0, qi + 1, tile, jnp.zeros((1, TQ), _F32))

    lo = jnp.where(count(lambda k, i: k >= 0) >= kf, jnp.int32(0), int_min)

    def vstep(i, lo):
        cand = lo + jnp.left_shift(jnp.int32(1), 30 - i)
        return jnp.where(count(lambda k, i: k >= cand) >= kf, cand, lo)

    thr = lax.fori_loop(0, 31, vstep, lo)
    need = kf - count(lambda k, i: k > thr)
    nbits = max(1, (L - 1).bit_length())

    def istep(i, lo):
        cand = lo + jnp.left_shift(jnp.int32(1), nbits - 1 - i)
        c = count(lambda k, ix: (k == thr) & (ix < cand))
        return jnp.where(c < need, cand, lo)

    last_tie = lax.fori_loop(0, nbits, istep, jnp.zeros((1, TQ), jnp.int32))
    key = key_ref[...]
    keep = ((key > thr) | ((key == thr) & (idx <= last_tie))) & adm
    mask_ref[0] = keep.astype(mask_ref.dtype)


def dsa_mask(ki, qiT, wiT, topk):
    B, L, _ = ki.shape
    T = qiT.shape[3]
    return pl.pallas_call(
        functools.partial(_dsa_mask_kernel, topk),
        grid=(B, T // TQ),
        in_specs=[pl.BlockSpec((1, L, DSA_IDX_DIM), lambda b, i: (b, 0, 0)),
                  pl.BlockSpec((1, DSA_IDX_HEADS, DSA_IDX_DIM, TQ), lambda b, i: (b, 0, 0, i)),
                  pl.BlockSpec((1, DSA_IDX_HEADS, TQ), lambda b, i: (b, 0, i))],
        out_specs=pl.BlockSpec((1, L, TQ), lambda b, i: (b, 0, i)),
        out_shape=jax.ShapeDtypeStruct((B, L, T), _BF16),
        scratch_shapes=[pltpu.VMEM((L, TQ), jnp.int32)],
        compiler_params=pltpu.CompilerParams(dimension_semantics=("arbitrary", "arbitrary"), vmem_limit_bytes=VMEM_LIMIT),
        name="dsa_mask",
    )(ki, qiT, wiT)


PEER_TN = 256
PEER_EB = 1024
PEER_IB = PEER_EB // PEER_NKEYS
RANK_OUT = 4096.0


def _extract_top(work, n):
    rows = work.shape[0]
    iota = lax.broadcasted_iota(jnp.int32, work.shape, 0)
    order = jnp.full(work.shape, RANK_OUT, _F32)
    vals = []
    idx = None
    for a in range(n):
        m = jnp.max(work, axis=0, keepdims=True)
        idx = jnp.min(jnp.where(work == m, iota, rows), axis=0, keepdims=True)
        hit = iota == idx
        vals.append(m)
        order = jnp.where(hit, jnp.float32(a), order)
        work = jnp.where(hit, -jnp.inf, work)
    return vals, order, idx


def _peer_select_kernel(x_ref, g_ref, wqT_ref, keys_ref, hx_ref, c1_ref, e1n_ref, r2_ref, e2_ref):
    x = x_ref[...]
    y = x * lax.rsqrt(jnp.mean(x * x, axis=1, keepdims=True) + NORM_EPS) * g_ref[...]
    hx = y.astype(_BF16)
    hx_ref[...] = hx
    qT = lax.dot_general(wqT_ref[...], hx, (((1,), (1,)), ((), ())), preferred_element_type=_F32).astype(_BF16)
    half = PEER_QDIM // 2
    for h in range(PEER_HEADS):
        sc = []
        for c in range(2):
            r0 = (h * 2 + c) * half
            sc.append(jnp.dot(keys_ref[h * 2 + c], qT[r0:r0 + half, :], preferred_element_type=_F32))
        v1, o1, _ = _extract_top(sc[0], PEER_TOPK)
        v2, o2, _ = _extract_top(sc[1], PEER_TOPK)
        v1s = jnp.concatenate(v1, axis=0)
        v2s = jnp.concatenate(v2, axis=0)
        ex1s = jnp.exp(v1s - v1[0])
        ex2s = jnp.exp(v2s - v2[0])
        lo_a, hi_a = PEER_TOPK // 2, PEER_TOPK
        cand = jnp.concatenate([v1[0] + v2s] + [v1[a] + v2s[:lo_a] for a in range(1, lo_a)] + [v1s[lo_a:hi_a] + v2[0]], axis=0)
        ecand = jnp.concatenate([ex1s[0:1] * ex2s] + [ex1s[a:a + 1] * ex2s[:lo_a] for a in range(1, lo_a)]
                                + [ex1s[lo_a:hi_a] * ex2s[0:1]], axis=0)
        _, corder, _ = _extract_top(cand, PEER_TOPK)
        chosen = (corder < RANK_OUT).astype(_F32)
        z = jnp.sum(chosen * ecand, axis=0, keepdims=True)
        tail0 = PEER_TOPK + (lo_a - 1) * lo_a
        c1 = jnp.zeros(sc[0].shape, _F32)
        for a in range(PEER_TOPK):
            if a == 0:
                grp = chosen[0:PEER_TOPK]
            elif a < lo_a:
                grp = chosen[PEER_TOPK + (a - 1) * lo_a:PEER_TOPK + a * lo_a]
            else:
                grp = chosen[tail0 + a - lo_a:tail0 + a - lo_a + 1]
            cnt = jnp.sum(grp, axis=0, keepdims=True)
            c1 = jnp.where(o1 == jnp.float32(a), cnt, c1)
        c1_ref[h] = c1
        e1n_ref[h] = jnp.exp(sc[0] - v1[0]) / z
        r2_ref[h] = o2.astype(_BF16)
        e2_ref[h] = jnp.exp(sc[1] - v2[0]).astype(_BF16)


def peer_select(x, g, wqT, keys2):
    Np, D = x.shape
    tn = PEER_TN
    row = lambda: pl.BlockSpec((PEER_HEADS, PEER_NKEYS, tn), lambda t: (0, 0, t))
    rshape = jax.ShapeDtypeStruct((PEER_HEADS, PEER_NKEYS, Np), _F32)
    return pl.pallas_call(
        _peer_select_kernel,
        grid=(Np // tn,),
        in_specs=[pl.BlockSpec((tn, D), lambda t: (t, 0)),
                  pl.BlockSpec((1, D), lambda t: (0, 0)),
                  pl.BlockSpec((PEER_HEADS * PEER_QDIM, D), lambda t: (0, 0)),
                  pl.BlockSpec((2 * PEER_HEADS, PEER_NKEYS, PEER_QDIM // 2), lambda t: (0, 0, 0))],
        out_specs=[pl.BlockSpec((tn, D), lambda t: (t, 0)), row(), row(), row(), row()],
        out_shape=[jax.ShapeDtypeStruct((Np, D), _BF16), rshape, rshape,
                   jax.ShapeDtypeStruct(rshape.shape, _BF16), jax.ShapeDtypeStruct(rshape.shape, _BF16)],
        compiler_params=pltpu.CompilerParams(dimension_semantics=("arbitrary",), vmem_limit_bytes=VMEM_LIMIT),
        name="peer_select",
    )(x, g, wqT, keys2)


def _gelu_tanh(x):
    return 0.5 * x * (1.0 + jnp.tanh(math.sqrt(2.0 / math.pi) * (x + 0.044715 * (x * x * x))))


def _peer_dense_kernel(x_ref, hx_ref, u_ref, vT_ref, c1_ref, e1n_ref, r2_ref, e2_ref, o_ref, p_ref, acc_ref):
    e = pl.program_id(1)
    tn, D = x_ref.shape

    @pl.when(e == 0)
    def _():
        acc_ref[...] = jnp.zeros_like(acc_ref)

    nt = (((1,), (1,)), ((), ()))
    half = PEER_EB // 2
    for s in range(2):
        act = _gelu_tanh(lax.dot_general(u_ref[s * half:(s + 1) * half, :], hx_ref[...], nt, preferred_element_type=_F32))
        for ib in range(s * half // PEER_NKEYS, (s + 1) * half // PEER_NKEYS):
            w = jnp.zeros((PEER_NKEYS, tn), _BF16)
            for h in range(PEER_HEADS):
                c1b = c1_ref[h, ib:ib + 1, :].astype(_BF16)
                e1b = e1n_ref[h, ib:ib + 1, :].astype(_BF16)
                w = w + jnp.where(r2_ref[h] < c1b, e1b * e2_ref[h], jnp.zeros((), _BF16))
            r0 = ib * PEER_NKEYS - s * half
            p_ref[ib * PEER_NKEYS:(ib + 1) * PEER_NKEYS, :] = w * act[r0:r0 + PEER_NKEYS, :].astype(_BF16)
    for s in range(2):
        rows = slice(s * (D // 2), (s + 1) * (D // 2))
        acc_ref[rows, :] += jnp.dot(vT_ref[rows, :], p_ref[...], preferred_element_type=_F32)

    @pl.when(e == pl.num_programs(1) - 1)
    def _():
        o_ref[...] = x_ref[...] + acc_ref[...].T


def peer_dense(x, hx, u, vT, c1, e1n, r2, e2):
    Np, D = x.shape
    E = u.shape[0]
    tn = PEER_TN
    rowi = lambda: pl.BlockSpec((PEER_HEADS, PEER_IB, tn), lambda t, e: (0, e, t))
    rowj = lambda: pl.BlockSpec((PEER_HEADS, PEER_NKEYS, tn), lambda t, e: (0, 0, t))
    return pl.pallas_call(
        _peer_dense_kernel,
        grid=(Np // tn, E // PEER_EB),
        in_specs=[pl.BlockSpec((tn, D), lambda t, e: (t, 0)),
                  pl.BlockSpec((tn, D), lambda t, e: (t, 0)),
                  pl.BlockSpec((PEER_EB, D), lambda t, e: (e, 0)),
                  pl.BlockSpec((D, PEER_EB), lambda t, e: (0, e)),
                  rowi(), rowi(), rowj(), rowj()],
        out_specs=pl.BlockSpec((tn, D), lambda t, e: (t, 0)),
        out_shape=jax.ShapeDtypeStruct((Np, D), _F32),
        scratch_shapes=[pltpu.VMEM((PEER_EB, tn), _BF16), pltpu.VMEM((D, tn), _F32)],
        compiler_params=pltpu.CompilerParams(dimension_semantics=("arbitrary", "arbitrary"), vmem_limit_bytes=VMEM_LIMIT),
        name="peer_dense",
    )(x, hx, u, vT, c1, e1n, r2, e2)


def peer_ffn_tokens(x_tokens, g, wq, keys, U, V):
    N, D = x_tokens.shape
    Np = -(-N // PEER_TN) * PEER_TN
    x = jnp.pad(x_tokens, ((0, Np - N), (0, 0)))
    keys2 = keys.reshape(2 * PEER_HEADS, PEER_NKEYS, PEER_QDIM // 2).astype(_BF16)
    hx, c1, e1n, r2, e2 = peer_select(x, g.reshape(1, D), wq.T.astype(_BF16), keys2)
    return peer_dense(x, hx, U.astype(_BF16), V.T.astype(_BF16), c1, e1n, r2, e2)[:N]


def bias_lookup(dist, tab):
    onehot = (rel_bucket(dist)[..., None] == jnp.arange(REL_BUCKETS)).astype(_F32)
    return jnp.einsum('...b,bh->h...', onehot, tab.astype(_F32), precision=lax.Precision.HIGHEST)


def rel_bias_tiles(tab):
    kk = jnp.arange(TK)[:, None]
    qq = jnp.arange(TQ)[None, :]
    dist = jnp.stack([d * TQ + qq - kk for d in range(3)])
    return bias_lookup(dist, tab)


ROWS_T = 8
DEC_ROWS = N_HEADS * ROWS_T
NEAR_TILES = 3


PP = 8


def _page_specs(block, n_pages):
    def spec(i):
        return pl.BlockSpec(block, lambda b, p, pt: (pt[b, jnp.minimum(PP * p + i, n_pages - 1)], 0, 0))
    return [spec(i) for i in range(PP)]


def _moba_decode_gate_kernel(pt_ref, q_ref, *rest):
    caches, gs_ref = rest[:PP], rest[PP]
    p = pl.program_id(1)
    per = MOBA_BLOCK // PAGE_SIZE

    @pl.when(p == 0)
    def _():
        gs_ref[...] = jnp.zeros_like(gs_ref)

    lane = lax.broadcasted_iota(jnp.int32, gs_ref.shape[1:], 1)
    for g in range(PP // per):
        ksum = sum(jnp.sum(caches[g * per + i][0], axis=0, keepdims=True) for i in range(per))
        km = (ksum * (1.0 / MOBA_BLOCK)).astype(_BF16).astype(_F32)
        col = jnp.sum(q_ref[0].astype(_F32) * km, axis=1, keepdims=True)
        gs_ref[0] = jnp.where(lane == p * (PP // per) + g, col, gs_ref[0])


def moba_decode_gate(pt, q_rows, cache_rows):
    B, n_pages = pt.shape
    W = N_HEADS * HD
    assert n_pages % PP == 0 and PP % (MOBA_BLOCK // PAGE_SIZE) == 0
    return pl.pallas_call(
        _moba_decode_gate_kernel,
        grid_spec=pltpu.PrefetchScalarGridSpec(
            num_scalar_prefetch=1,
            grid=(B, n_pages // PP),
            in_specs=[pl.BlockSpec((1, DEC_ROWS, W), lambda b, p, pt: (b, 0, 0))] + _page_specs((1, PAGE_SIZE, W), n_pages),
            out_specs=pl.BlockSpec((1, DEC_ROWS, 128), lambda b, p, pt: (b, 0, 0))),
        out_shape=jax.ShapeDtypeStruct((B, DEC_ROWS, 128), _F32),
        compiler_params=pltpu.CompilerParams(dimension_semantics=("arbitrary", "arbitrary"), vmem_limit_bytes=VMEM_LIMIT),
        name="moba_decode_gate",
    )(pt, q_rows, *([cache_rows] * PP))


def _paged_attn_kernel(mode, nblk, pt_ref, q_ref, *rest):
    caches = rest[:PP]
    new_ref, far_ref, near_ref, m_in_ref, o_ref, m_sc, l_sc, acc_sc, sel_sc = rest[PP:]
    p = pl.program_id(1)
    last = pl.num_programs(1) - 1
    n_pages = last * PP
    W = N_HEADS * HD
    lane = lax.broadcasted_iota(jnp.int32, (DEC_ROWS, PAGE_SIZE), 1)
    row_t = lax.broadcasted_iota(jnp.int32, (DEC_ROWS, PAGE_SIZE), 0) % ROWS_T

    @pl.when(p == 0)
    def _():
        m_sc[...] = jnp.full_like(m_sc, NEG_BIG)
        l_sc[...] = jnp.zeros_like(l_sc)
        acc_sc[...] = jnp.zeros_like(acc_sc)
        if mode == 'moba':
            valid = lane < nblk
            gs = jnp.where(valid, m_in_ref[0], -jnp.inf)
            rank = jnp.zeros(gs.shape, jnp.int32)
            for jp in range(nblk):
                col = gs[:, jp:jp + 1]
                rank = rank + ((col > gs) | ((col == gs) & (jp < lane))).astype(jnp.int32)
            sel_sc[...] = (valid & (rank < MOBA_TOPK)).astype(_F32)

    def tile_scores(blk, g, i, is_new):
        k = blk[:, :W].astype(_BF16)
        s = lax.dot_general(q_ref[0], k, (((1,), (1,)), ((), ())), preferred_element_type=_F32)
        near = near_ref[jnp.clip(g - (n_pages + 1 - NEAR_TILES), 0, NEAR_TILES - 1)]
        s = s + jnp.where(g >= n_pages + 1 - NEAR_TILES, near, far_ref[...])
        if mode == 'moba':
            if is_new:
                ok = lane <= row_t
            else:
                in_blk = jnp.sum(jnp.where(lane == g // (MOBA_BLOCK // PAGE_SIZE), sel_sc[...], 0.0), axis=1, keepdims=True)
                ok = jnp.broadcast_to(in_blk, s.shape) > 0.5
        else:
            ok = jnp.concatenate([m_in_ref[0, i]] * N_HEADS, axis=0) > 0.5
        return jnp.where(ok, s, NEG_BIG), ok.astype(_F32)

    def update(s, ok, v):
        m_new = jnp.maximum(m_sc[...], jnp.max(s, axis=1, keepdims=True))
        pr = jnp.where(ok > 0.5, jnp.exp(s - m_new), 0.0)
        alpha = jnp.exp(m_sc[...] - m_new)
        l_sc[...] = alpha * l_sc[...] + jnp.sum(pr, axis=1, keepdims=True)
        acc_sc[...] = alpha * acc_sc[...] + jnp.dot(pr.astype(_BF16), v, preferred_element_type=_F32)
        m_sc[...] = m_new

    @pl.when(p < last)
    def _():
        parts = [tile_scores(caches[i][0], PP * p + i, i, False) for i in range(PP)]
        v_all = jnp.concatenate([caches[i][0][:, W:].astype(_BF16) for i in range(PP)], axis=0)
        update(jnp.concatenate([s for s, _ in parts], axis=1), jnp.concatenate([ok for _, ok in parts], axis=1), v_all)

    @pl.when(p == last)
    def _():
        s, ok = tile_scores(new_ref[0], n_pages, 0, True)
        update(s, ok, new_ref[0][:, W:].astype(_BF16))
        o_ref[0] = acc_sc[...] / jnp.maximum(l_sc[...], 1e-30)


def paged_attention(mode, pt, q_rows, cache_rows, new_rows, far, near, m_in):
    B, n_pages = pt.shape
    W = N_HEADS * HD
    assert n_pages % PP == 0
    if mode == 'moba':
        m_spec = pl.BlockSpec((1, DEC_ROWS, 128), lambda b, p, pt: (b, 0, 0))
    else:
        m_spec = pl.BlockSpec((1, PP, ROWS_T, PAGE_SIZE), lambda b, p, pt: (b, p, 0, 0))
    return pl.pallas_call(
        functools.partial(_paged_attn_kernel, mode, n_pages // (MOBA_BLOCK // PAGE_SIZE)),
        grid_spec=pltpu.PrefetchScalarGridSpec(
            num_scalar_prefetch=1,
            grid=(B, n_pages // PP + 1),
            in_specs=[pl.BlockSpec((1, DEC_ROWS, W), lambda b, p, pt: (b, 0, 0))] + _page_specs((1, PAGE_SIZE, 2 * W), n_pages) + [
                      pl.BlockSpec((1, PAGE_SIZE, 2 * W), lambda b, p, pt: (b, 0, 0)),
                      pl.BlockSpec((DEC_ROWS, 1), lambda b, p, pt: (0, 0)),
                      pl.BlockSpec((NEAR_TILES, DEC_ROWS, PAGE_SIZE), lambda b, p, pt: (0, 0, 0)),
                      m_spec],
            out_specs=pl.BlockSpec((1, DEC_ROWS, W), lambda b, p, pt: (b, 0, 0)),
            scratch_shapes=[pltpu.VMEM((DEC_ROWS, 1), _F32), pltpu.VMEM((DEC_ROWS, 1), _F32),
                            pltpu.VMEM((DEC_ROWS, W), _F32), pltpu.VMEM((DEC_ROWS, 128), _F32)]),
        out_shape=jax.ShapeDtypeStruct((B, DEC_ROWS, W), _F32),
        compiler_params=pltpu.CompilerParams(dimension_semantics=("arbitrary", "arbitrary"), vmem_limit_bytes=VMEM_LIMIT),
        name="paged_attn_" + mode,
    )(pt, q_rows, *([cache_rows] * PP), new_rows, far, near, m_in)


def _dsa_decode_score_kernel(pt_ref, qi_ref, wi_ref, *rest):
    caches, new_ref, score_ref = rest[:PP], rest[PP], rest[PP + 1]
    p = pl.program_id(1)
    last = pl.num_programs(1) - 1

    def tile_scores(ki):
        ki = ki.astype(_BF16)
        acc = jnp.zeros((ROWS_T, PAGE_SIZE), _F32)
        for hh in range(DSA_IDX_HEADS):
            s = lax.dot_general(qi_ref[0, hh], ki, (((1,), (1,)), ((), ())), preferred_element_type=_F32) * DSA_IDX_DIM ** -0.5
            acc = acc + wi_ref[0, hh] * jnp.maximum(s, 0.0)
        return acc

    @pl.when(p < last)
    def _():
        for i in range(PP):
            score_ref[0, :, i * PAGE_SIZE:(i + 1) * PAGE_SIZE] = tile_scores(caches[i][0])

    @pl.when(p == last)
    def _():
        score_ref[0] = jnp.zeros(score_ref.shape[1:], _F32)
        score_ref[0, :, 0:PAGE_SIZE] = tile_scores(new_ref[0])


def dsa_decode_scores(pt, qi_rows, wi_rows, cache_rows, new_rows):
    B, n_pages = pt.shape
    assert n_pages % PP == 0
    return pl.pallas_call(
        _dsa_decode_score_kernel,
        grid_spec=pltpu.PrefetchScalarGridSpec(
            num_scalar_prefetch=1,
            grid=(B, n_pages // PP + 1),
            in_specs=[pl.BlockSpec((1, DSA_IDX_HEADS, ROWS_T, DSA_IDX_DIM), lambda b, p, pt: (b, 0, 0, 0)),
                      pl.BlockSpec((1, DSA_IDX_HEADS, ROWS_T, 1), lambda b, p, pt: (b, 0, 0, 0))]
            + _page_specs((1, PAGE_SIZE, DSA_IDX_DIM), n_pages)
            + [pl.BlockSpec((1, PAGE_SIZE, DSA_IDX_DIM), lambda b, p, pt: (b, 0, 0))],
            out_specs=pl.BlockSpec((1, ROWS_T, PP * PAGE_SIZE), lambda b, p, pt: (b, 0, p))),
        out_shape=jax.ShapeDtypeStruct((B, ROWS_T, (n_pages + PP) * PAGE_SIZE), _F32),
        compiler_params=pltpu.CompilerParams(dimension_semantics=("arbitrary", "arbitrary"), vmem_limit_bytes=VMEM_LIMIT),
        name="dsa_decode_scores",
    )(pt, qi_rows, wi_rows, *([cache_rows] * PP), new_rows)


def _count_lanes(pred):
    return jnp.sum(pred.astype(_F32), axis=1, keepdims=True)


def _dsa_decode_select_kernel(topk, past_len, score_ref, keep_ref, key_ref):
    Lp = score_ref.shape[2]
    idx = lax.broadcasted_iota(jnp.int32, (ROWS_T, Lp), 1)
    q_pos = past_len + lax.broadcasted_iota(jnp.int32, (ROWS_T, Lp), 0)
    adm = idx <= q_pos
    score = score_ref[0]
    score = jnp.where(adm, jnp.where(score == 0.0, 0.0, score), -jnp.inf)
    bits = pltpu.bitcast(score, jnp.int32)
    key_ref[...] = jnp.where(bits < 0, bits ^ jnp.int32(0x7FFFFFFF), bits)
    kf = jnp.float32(topk)
    lo = jnp.where(_count_lanes(key_ref[...] >= 0) >= kf, jnp.int32(0), jnp.int32(-2 ** 31))

    def vstep(i, lo):
        cand = lo + jnp.left_shift(jnp.int32(1), 30 - i)
        return jnp.where(_count_lanes(key_ref[...] >= cand) >= kf, cand, lo)

    thr = lax.fori_loop(0, 31, vstep, lo)
    need = kf - _count_lanes(key_ref[...] > thr)
    nbits = max(1, (Lp - 1).bit_length())

    def istep(i, lo):
        cand = lo + jnp.left_shift(jnp.int32(1), nbits - 1 - i)
        c = _count_lanes((key_ref[...] == thr) & (idx < cand))
        return jnp.where(c < need, cand, lo)

    last_tie = lax.fori_loop(0, nbits, istep, jnp.zeros((ROWS_T, 1), jnp.int32))
    key = key_ref[...]
    keep = ((key > thr) | ((key == thr) & (idx <= last_tie))) & adm
    keep_ref[0] = keep.astype(_F32)


def dsa_decode_select(score, topk, past_len):
    B, _, Lp = score.shape
    return pl.pallas_call(
        functools.partial(_dsa_decode_select_kernel, topk, past_len),
        grid=(B,),
        in_specs=[pl.BlockSpec((1, ROWS_T, Lp), lambda b: (b, 0, 0))],
        out_specs=pl.BlockSpec((1, ROWS_T, Lp), lambda b: (b, 0, 0)),
        out_shape=jax.ShapeDtypeStruct((B, ROWS_T, Lp), _F32),
        scratch_shapes=[pltpu.VMEM((ROWS_T, Lp), jnp.int32)],
        compiler_params=pltpu.CompilerParams(dimension_semantics=("arbitrary",), vmem_limit_bytes=VMEM_LIMIT),
        name="dsa_decode_select",
    )(score)


def decode_rows(q):
    B, T, H, _ = q.shape
    qp = jnp.pad(jnp.transpose(q, (0, 2, 1, 3)), ((0, 0), (0, 0), (0, ROWS_T - T), (0, 0)))
    rows = qp[:, :, :, None, :] * jnp.eye(H, dtype=q.dtype)[None, :, None, :, None]
    return rows.reshape(B, H * ROWS_T, H * HD)


def from_decode_rows(o, T):
    B = o.shape[0]
    o5 = o.reshape(B, N_HEADS, ROWS_T, N_HEADS, HD)
    own = jnp.sum(o5 * jnp.eye(N_HEADS, dtype=o.dtype)[None, :, None, :, None], axis=3)
    return jnp.transpose(own[:, :, :T], (0, 2, 1, 3)).reshape(B, T, N_HEADS * HD)


def decode_bias(tab, past_len, n_pages):
    t = jnp.arange(ROWS_T)[None, :, None]
    tile = (n_pages + 1 - NEAR_TILES + jnp.arange(NEAR_TILES))[:, None, None]
    dist = past_len + t - (tile * PAGE_SIZE + jnp.arange(PAGE_SIZE)[None, None, :])
    near = jnp.transpose(bias_lookup(dist, tab), (1, 0, 2, 3)).reshape(NEAR_TILES, DEC_ROWS, PAGE_SIZE)
    far = jnp.repeat(tab[REL_BUCKETS - 1].astype(_F32), ROWS_T).reshape(DEC_ROWS, 1)
    return far, near


def pad_rows(a, n):
    return jnp.pad(a, ((0, 0), (0, n - a.shape[1])) + ((0, 0),) * (a.ndim - 2))


def heads_T(z, nh):
    B, T, _ = z.shape
    return jnp.transpose(z.reshape(B, T, nh, HD), (0, 2, 3, 1))


def heads_K(z, nh):
    B, L, _ = z.shape
    return jnp.transpose(z.reshape(B, L, nh, HD), (0, 2, 1, 3))


def heads_VT(z, nh):
    B, L, _ = z.shape
    return jnp.transpose(z.reshape(B, L // TK, TK, nh, HD), (0, 3, 1, 4, 2))


def from_heads_T(oT):
    B, H, _, T = oT.shape
    return jnp.transpose(oT, (0, 3, 1, 2)).reshape(B, T, H * HD)


def nsa_attention_prompt(q_flat, gates_flat, nsa_new, q_pos, w1, w2, pe, tab, bias_tiles):
    B, T, _ = q_flat.shape
    G, H, L = NSA_KV_HEADS, N_HEADS, T
    scale = HD ** -0.5
    q = q_flat.reshape(B, T, H, HD)
    kc, vc = nsa_new[:, :, 0], nsa_new[:, :, 1]
    nch = L // NSA_CMP_STRIDE
    ncmp = nch - 1

    def compress(x, i):
        c = x[:, :nch * NSA_CMP_STRIDE].reshape(B, nch, NSA_CMP_STRIDE, G, HD)
        blk = jnp.concatenate([c[:, :-1], c[:, 1:]], axis=2) + pe[i][None, None, :, None, :]
        blk = jnp.transpose(blk, (0, 1, 3, 2, 4)).reshape(B, ncmp, G, NSA_CMP_LEN * HD)
        return jax.nn.gelu(blk @ w1[i]) @ w2[i]

    kcmp = compress(kc, 0)
    vcmp = compress(vc, 1)
    cmp_end = jnp.arange(ncmp) * NSA_CMP_STRIDE + NSA_CMP_LEN - 1
    qg = q.reshape(B, T, G, NSA_GROUP, HD)
    dist_c = q_pos[:, None] - cmp_end[None, :]
    lg_c = jnp.einsum('btgrd,bngd->btgrn', qg, kcmp).astype(jnp.float32).reshape(B, T, H, ncmp) * scale
    lg_c = lg_c + rel_bias(dist_c[None, :, None, :], tab)
    p_c = masked_softmax(lg_c, (dist_c >= 0)[None, :, None, :])
    p_cg = p_c.reshape(B, T, G, NSA_GROUP, ncmp)
    o_cmp = jnp.einsum('btgrn,bngd->btgrd', p_cg.astype(vcmp.dtype), vcmp).reshape(B, T, H, HD)
    nsel = -(-L // NSA_SEL_BLOCK)
    ii = np.arange(ncmp)[:, None]
    jj_np = np.arange(nsel)[None, :]
    overlap = ((ii * NSA_CMP_STRIDE < (jj_np + 1) * NSA_SEL_BLOCK) & (ii * NSA_CMP_STRIDE + NSA_CMP_LEN > jj_np * NSA_SEL_BLOCK)).astype(np.float32)
    imp = jnp.einsum('btgrn,nj->btgj', p_cg, jnp.asarray(overlap))
    cur = q_pos // NSA_SEL_BLOCK
    jj = jnp.arange(nsel)[None, :]
    causal_blk = jj <= cur[:, None]
    forced = (jj == 0) | (jj == cur[:, None]) | (jj == cur[:, None] - 1)
    imp = jnp.where(forced[None, :, None, :], jnp.inf, jnp.where(causal_blk[None, :, None, :], imp, -jnp.inf))
    _, sel = lax.top_k(imp, min(NSA_TOPN, nsel))
    chosen = jnp.any(sel[..., None] == jnp.arange(nsel), axis=-2)
    selT = jnp.transpose(chosen, (0, 2, 3, 1)).astype(_F32)
    qT = heads_T(q_flat, H) * scale
    flat = lambda a: a.reshape(B, L, G * HD)
    o_sel = from_heads_T(attention_T('sel', qT, heads_K(flat(nsa_new[:, :, 2]), G), heads_VT(flat(nsa_new[:, :, 3]), G), bias_tiles, selT))
    o_win = from_heads_T(attention_T('win', qT, heads_K(flat(nsa_new[:, :, 4]), G), heads_VT(flat(nsa_new[:, :, 5]), G), bias_tiles))
    g = jax.nn.sigmoid(gates_flat.reshape(B, T, H, 3).astype(jnp.float32))
    out = g[..., 0:1] * o_cmp + g[..., 1:2] * o_sel.reshape(B, T, H, HD) + g[..., 2:3] * o_win.reshape(B, T, H, HD)
    return out.reshape(B, T, H * HD)


def token_mix(z, past, lp, tab):
    B, T, _ = z.shape
    P = 0 if past is None else past['nsa'].shape[1]
    q_pos = P + jnp.arange(T, dtype=jnp.int32)
    qa, ka, va, q_nsa, nsa_kv, nsa_g, qc, kc, vc, qi, ki, wi, u, vg = split_last(z, IN_SPLITS)
    moba_new = jnp.stack([ka, va], axis=2).reshape(B, T, 2, N_HEADS, HD)
    nsa_new = nsa_kv.reshape(B, T, 6, NSA_KV_HEADS, HD)
    nsa_main_new = nsa_new[:, :, :4]
    win_new = nsa_new[:, :, 4:]
    dsa_new = jnp.stack([kc, vc], axis=2).reshape(B, T, 2, N_HEADS, HD)
    if past is None:
        nsa_all, win_all = nsa_main_new, win_new
    else:
        nsa_all = jnp.concatenate([past['nsa'], nsa_main_new], axis=1)
        win_all = jnp.concatenate([past['win'], win_new], axis=1)
    kw_pos0 = P + T - win_all.shape[1]
    scale = HD ** -0.5
    if past is None:
        bt = lp['bias_tiles']
        qaT = heads_T(qa, N_HEADS) * scale
        y_a = from_heads_T(attention_T('moba', qaT, heads_K(ka, N_HEADS), heads_VT(va, N_HEADS), bt[0], moba_gate(qaT, heads_K(ka, N_HEADS))))
        y_b = nsa_attention_prompt(q_nsa, nsa_g, nsa_new, q_pos, lp['cmp_w1'], lp['cmp_w2'], lp['cmp_pe'], tab[:, N_HEADS:2 * N_HEADS], bt[1])
        qiT = jnp.transpose(qi.reshape(B, T, DSA_IDX_HEADS, DSA_IDX_DIM), (0, 2, 3, 1))
        keep = dsa_mask(ki, qiT, jnp.transpose(wi, (0, 2, 1)), min(DSA_TOPK, T // 4))
        y_c = from_heads_T(attention_T('dsa', heads_T(qc, N_HEADS) * scale, heads_K(kc, N_HEADS), heads_VT(vc, N_HEADS), bt[2], keep))
    else:
        assert P % MOBA_BLOCK == 0 and T <= ROWS_T
        pt = past['pt']
        n_pages = pt.shape[1]
        far_a, near_a = decode_bias(tab[:, :N_HEADS], P, n_pages)
        qa_rows = (decode_rows(qa.reshape(B, T, N_HEADS, HD)) * scale).astype(_BF16)
        new_a = pad_rows(jnp.concatenate([ka, va], axis=-1), PAGE_SIZE)
        gs = moba_decode_gate(pt, qa_rows, past['moba_rows'])
        y_a = from_decode_rows(paged_attention('moba', pt, qa_rows, past['moba_rows'], new_a, far_a, near_a, gs), T)
        y_b = nsa_attention(q_nsa.reshape(B, T, N_HEADS, HD), nsa_g.reshape(B, T, N_HEADS, 3), nsa_all[:, :, 0], nsa_all[:, :, 1], nsa_all[:, :, 2], nsa_all[:, :, 3], win_all[:, :, 0], win_all[:, :, 1], kw_pos0, q_pos, lp['cmp_w1'], lp['cmp_w2'], lp['cmp_pe'], tab[:, N_HEADS:2 * N_HEADS])
        qi_rows = pad_rows(jnp.transpose(qi.reshape(B, T, DSA_IDX_HEADS, DSA_IDX_DIM), (0, 2, 1, 3)).reshape(B * DSA_IDX_HEADS, T, DSA_IDX_DIM), ROWS_T)
        qi_rows = qi_rows.reshape(B, DSA_IDX_HEADS, ROWS_T, DSA_IDX_DIM).astype(_BF16)
        wi_rows = pad_rows(jnp.transpose(wi, (0, 2, 1)).reshape(B * DSA_IDX_HEADS, T), ROWS_T).reshape(B, DSA_IDX_HEADS, ROWS_T, 1)
        score = dsa_decode_scores(pt, qi_rows, wi_rows, past['idx_rows'], pad_rows(ki, PAGE_SIZE))
        keep = dsa_decode_select(score, min(DSA_TOPK, (P + T) // 4), P)
        keep = jnp.transpose(keep.reshape(B, ROWS_T, n_pages + PP, PAGE_SIZE), (0, 2, 1, 3))
        far_c, near_c = decode_bias(tab[:, 2 * N_HEADS:], P, n_pages)
        qc_rows = (decode_rows(qc.reshape(B, T, N_HEADS, HD)) * scale).astype(_BF16)
        new_c = pad_rows(jnp.concatenate([kc, vc], axis=-1), PAGE_SIZE)
        y_c = from_decode_rows(paged_attention('dsa', pt, qc_rows, past['dsa_rows'], new_c, far_c, near_c, keep), T)
    y_d, v_rows = gmlp_sgu(u, vg, lp['gm_ws'], lp['gm_b'], lp['gm_norm'])
    branches = [y.reshape(B * T, MIX_W) for y in (y_a, y_b, y_c, y_d)]
    win_keep = win_all[:, -min(NSA_WINDOW, P + T):]
    return branches, (moba_new, nsa_main_new, win_keep, dsa_new, ki, v_rows)


def gated_branch_mix(branches, gz, lp):
    mixed = 0.0
    for n in range(N_BRANCH):
        gate = jax.nn.sigmoid(gz[:, n * D_MODEL:(n + 1) * D_MODEL] + lp['b_gate'][n])
        mixed = mixed + gate * (branches[n] @ lp['w_branch'][n])
    return mixed @ lp['w_out']


def mem_attention_q(q, mkv, wo):
    B, T, _ = q.shape
    q = q.reshape(B, T, MEM_HEADS, MEM_HD)
    lg = jnp.einsum('bthd,bmhd->bthm', q, mkv[:, :, 0]).astype(jnp.float32) * MEM_HD ** -0.5
    p = jax.nn.softmax(lg, axis=-1)
    o = jnp.einsum('bthm,bmhd->bthd', p.astype(mkv.dtype), mkv[:, :, 1]).reshape(B, T, MEM_W)
    return o @ wo


def layer_pair(rows, shape_p, shape_s, past, mkv_p, mkv_s, lp, tab):
    n_p = shape_p[0] * shape_p[1]
    split = lambda a: (a[:n_p].reshape(shape_p[0], shape_p[1], -1), a[n_p:].reshape(shape_s[0], shape_s[1], -1))
    z_p, z_s = split(norm_matmul(rows, lp['norm_mix'], lp['w_in']))
    gz = norm_matmul(rows, lp['norm_mix'], lp['w_gate'])
    br_p, rp = token_mix(z_p, None, lp, tab)
    br_s, rs = token_mix(z_s, past, lp, tab)
    branches = [jnp.concatenate([bp, bs], axis=0) for bp, bs in zip(br_p, br_s)]
    rows = rows + gated_branch_mix(branches, gz, lp)
    q_p, q_s = split(norm_matmul(rows, lp['norm_mem'], lp['w_mem_q']))
    mem_p = mem_attention_q(q_p, mkv_p, lp['w_mem_o'])
    mem_s = mem_attention_q(q_s, mkv_s, lp['w_mem_o'])
    rows = rows + jnp.concatenate([mem_p.reshape(-1, D_MODEL), mem_s.reshape(-1, D_MODEL)], axis=0)
    rows = peer_ffn_tokens(rows, lp['norm_ffn'], lp['peer_wq'], lp['peer_keys'], lp['peer_u'], lp['peer_v'])
    return rows, rp, rs


def kernel(x_prompt, x_sample, cache_moba_kv, cache_nsa_kv, cache_dsa_kv, cache_dsa_idx, state_nsa_win, cache_mem_kv, page_table, mem_prompt, rel_bias_table, w_in, nsa_cmp_w1, nsa_cmp_w2, nsa_cmp_pe, gm_ws, gm_b, gm_norm, w_branch, w_gate, b_gate, w_out, w_mem_q, w_mem_kv, w_mem_o, peer_wq, peer_keys, peer_u, peer_v, norm_mix, norm_mem, norm_ffn, norm_final):
    n_p = x_prompt.shape[0] * x_prompt.shape[1]
    rows = jnp.concatenate([x_prompt.reshape(-1, D_MODEL), x_sample.reshape(-1, D_MODEL)], axis=0)
    rows_p = []
    rows_s = []
    mem_rows = []
    Bp = x_prompt.shape[0]
    n_pool = cache_moba_kv.shape[1]
    bias_tiles =[rel_bias_tiles(rel_bias_table[:, n * N_HEADS:(n + 1) * N_HEADS]) for n in range(3)]
    for l in range(DEPTH):
        lp = {
            'w_in': w_in[l], 'cmp_w1': nsa_cmp_w1[l], 'cmp_w2': nsa_cmp_w2[l], 'cmp_pe': nsa_cmp_pe[l],
            'gm_ws': gm_ws[l], 'gm_b': gm_b[l], 'gm_norm': gm_norm[l],
            'w_branch': w_branch[l], 'w_gate': w_gate[l], 'b_gate': b_gate[l], 'w_out': w_out[l],
            'w_mem_q': w_mem_q[l], 'w_mem_o': w_mem_o[l],
            'peer_wq': peer_wq[l], 'peer_keys': peer_keys[l], 'peer_u': peer_u[l], 'peer_v': peer_v[l],
            'norm_mix': norm_mix[l], 'norm_mem': norm_mem[l], 'norm_ffn': norm_ffn[l],
        }
        mkv_p = (mem_prompt @ w_mem_kv[l]).reshape(Bp, MEM_TOKENS, 2, MEM_HEADS, MEM_HD)
        past = {
            'pt': page_table + l * n_pool,
            'moba_rows': cache_moba_kv.reshape(DEPTH * n_pool, PAGE_SIZE, 2 * MIX_W),
            'dsa_rows': cache_dsa_kv.reshape(DEPTH * n_pool, PAGE_SIZE, 2 * MIX_W),
            'idx_rows': cache_dsa_idx.reshape(DEPTH * n_pool, PAGE_SIZE, DSA_IDX_DIM),
            'nsa': gather_pages(cache_nsa_kv.reshape((DEPTH * n_pool,) + cache_nsa_kv.shape[2:]), page_table + l * n_pool),
            'win': state_nsa_win[l],
        }
        lp['bias_tiles'] = bias_tiles
        rows, rp, rs = layer_pair(rows, x_prompt.shape, x_sample.shape, past, mkv_p, cache_mem_kv[l], lp, rel_bias_table)
        rows_p.append(rp)
        rows_s.append(rs)
        mem_rows.append(mkv_p)
    y_rows = rmsnorm(rows, norm_final)
    y_prompt = y_rows[:n_p].reshape(x_prompt.shape)
    y_sample = y_rows[n_p:].reshape(x_sample.shape)
    new_moba_kv_prompt = jnp.stack([r[0] for r in rows_p])
    new_moba_kv_sample = jnp.stack([r[0] for r in rows_s])
    new_nsa_kv_prompt = jnp.stack([r[1] for r in rows_p])
    new_nsa_kv_sample = jnp.stack([r[1] for r in rows_s])
    new_nsa_win_prompt = jnp.stack([r[2] for r in rows_p])
    new_nsa_win_sample = jnp.stack([r[2] for r in rows_s])
    new_dsa_kv_prompt = jnp.stack([r[3] for r in rows_p])
    new_dsa_kv_sample = jnp.stack([r[3] for r in rows_s])
    new_dsa_idx_prompt = jnp.stack([r[4] for r in rows_p])
    new_dsa_idx_sample = jnp.stack([r[4] for r in rows_s])
    new_mem_kv_prompt = jnp.stack(mem_rows)
    new_gmlp_v_sample = jnp.stack([r[5] for r in rows_s])
    return (y_prompt, y_sample, new_moba_kv_prompt, new_moba_kv_sample, new_nsa_kv_prompt, new_nsa_kv_sample, new_nsa_win_prompt, new_nsa_win_sample, new_dsa_kv_prompt, new_dsa_kv_sample, new_dsa_idx_prompt, new_dsa_idx_sample, new_mem_kv_prompt, new_gmlp_v_sample)
```
